```python
import math
import jax, jax.numpy as jnp
from jax import lax
import numpy as np

D_MODEL = 1024
BATCH = 8
SEQ = 2048
DEPTH = 2

CHUNK = 64
Q_BLOCK = 128
MEM_LEN = 256

GLA_HEADS = 4
GLA_DK = 64
GLA_DV = 64
GLA_GATE_RANK = 16
GLA_TAU = 16.0
MLSTM_HEADS = 4
MLSTM_DK = 64
MLSTM_DV = 64
MLSTM_CONV = 4
MLA_HEADS = 8
MLA_NOPE = 64
MLA_ROPE = 32
MLA_V = 64
MLA_Q_RANK = 256
MLA_KV_RANK = 128
ROPE_BASE = 10000.0
XA_HEADS = 4
XA_DH = D_MODEL // XA_HEADS
N_GROUPS = 4
EXPERTS_PER_GROUP = 8
N_EXPERTS = N_GROUPS * EXPERTS_PER_GROUP
EXPERT_TOP_K = 2
D_EXPERT = 256

ALPHA = (2 * DEPTH) ** 0.25
BETA = (8 * DEPTH) ** -0.25
LN_EPS = 1e-5

GLA_QK = GLA_HEADS * GLA_DK
GLA_VW = GLA_HEADS * GLA_DV
ML_QK = MLSTM_HEADS * MLSTM_DK
ML_VW = MLSTM_HEADS * MLSTM_DV
MLA_OUT = MLA_HEADS * MLA_V
MIX_WIDTH = GLA_VW + ML_VW + MLA_OUT
IN_SIZES = (GLA_QK, GLA_QK, GLA_VW, GLA_VW, GLA_GATE_RANK,
            2 * ML_QK, ML_VW, ML_VW, MLSTM_HEADS, MLSTM_HEADS,
            MLA_Q_RANK, MLA_KV_RANK, MLA_ROPE)
IN_WIDTH = sum(IN_SIZES)

kernel_name = "hybrid_gla_mlstm_mla_hmoe_block"


def layer_norm(x, g, b):
    xf = x.astype(jnp.float32)
    mu = jnp.mean(xf, -1, keepdims=True)
    var = jnp.mean(jnp.square(xf - mu), -1, keepdims=True)
    return ((xf - mu) * lax.rsqrt(var + LN_EPS) * g + b).astype(x.dtype)


def rms_norm(x, g):
    xf = x.astype(jnp.float32)
    return (xf * lax.rsqrt(jnp.mean(jnp.square(xf), -1, keepdims=True) + LN_EPS) * g).astype(x.dtype)


def head_norm(x, g):
    mu = jnp.mean(x, -1, keepdims=True)
    var = jnp.mean(jnp.square(x - mu), -1, keepdims=True)
    return (x - mu) * lax.rsqrt(var + LN_EPS) * g.reshape(x.shape[-2:])


def rope(x, pos):
    half = x.shape[-1] // 2
    inv = ROPE_BASE ** (-jnp.arange(half, dtype=jnp.float32) / half)
    ang = pos.astype(jnp.float32)[..., None] * inv
    cos = jnp.cos(ang)[:, :, None, :]
    sin = jnp.sin(ang)[:, :, None, :]
    xf = x.astype(jnp.float32)
    x1, x2 = xf[..., :half], xf[..., half:]
    return jnp.concatenate([x1 * cos - x2 * sin, x2 * cos + x1 * sin], -1).astype(x.dtype)


def causal_depthwise_conv(x, w):
    k_w, c = w.shape
    return lax.conv_general_dilated(x, w[:, None, :], window_strides=(1,), padding=((k_w - 1, 0),),
                                    dimension_numbers=('NWC', 'WIO', 'NWC'), feature_group_count=c)


def gla_mixer(q, k, v, r, a_lr, w_a2, b_a, norm_g):
    B, S, _ = q.shape
    nc = S // CHUNK
    f32 = jnp.float32
    log_a = jax.nn.log_sigmoid((a_lr @ w_a2 + b_a).astype(f32)) / GLA_TAU

    def chunks(t, d):
        return t.astype(f32).reshape(B, nc, CHUNK, GLA_HEADS, d).transpose(1, 0, 3, 2, 4)

    causal = jnp.tril(jnp.ones((CHUNK, CHUNK), dtype=bool))[:, :, None]

    def step(state, inp):
        qb, kb, vb, la = inp
        b = jnp.cumsum(la, axis=2)
        b_end = b[:, :, -1:, :]
        o_inter = jnp.einsum('bhtd,bhde->bhte', qb * jnp.exp(b), state)
        decay = jnp.exp(jnp.where(causal, b[:, :, :, None, :] - b[:, :, None, :, :], -jnp.inf))
        scores = jnp.einsum('bhtd,bhtsd,bhsd->bhts', qb, decay, kb)
        o = o_inter + jnp.einsum('bhts,bhse->bhte', scores, vb)
        state = (jnp.exp(b_end[:, :, 0, :, None]) * state
                 + jnp.einsum('bhsd,bhse->bhde', kb * jnp.exp(b_end - b), vb))
        return state, o

    state0 = jnp.zeros((B, GLA_HEADS, GLA_DK, GLA_DV), f32)
    _, o = lax.scan(step, state0, (chunks(q, GLA_DK) * GLA_DK ** -0.5, chunks(k, GLA_DK),
                                   chunks(v, GLA_DV), chunks(log_a, GLA_DK)))
    o = o.transpose(1, 0, 3, 2, 4).reshape(B, S, GLA_HEADS, GLA_DV)
    o = head_norm(o, norm_g).reshape(B, S, GLA_VW) * jax.nn.silu(r.astype(f32))
    return o.astype(q.dtype)


def mlstm_mixer(q, k, v, o_pre, i_pre, f_pre, norm_g):
    B, S, _ = q.shape
    nc = S // CHUNK
    f32 = jnp.float32

    def chunks(t, d):
        return t.astype(f32).reshape(B, nc, CHUNK, MLSTM_HEADS, d).transpose(1, 0, 3, 2, 4)

    def gate_chunks(t):
        return t.reshape(B, nc, CHUNK, MLSTM_HEADS).transpose(1, 0, 3, 2)

    log_i = i_pre.astype(f32)
    log_f = jax.nn.log_sigmoid(f_pre.astype(f32))
    causal = jnp.tril(jnp.ones((CHUNK, CHUNK), dtype=bool))

    def step(carry, inp):
        c_st, n_st, m_st = carry
        qb, kb, vb, ig, lf = inp
        fcum = jnp.cumsum(lf, axis=-1)
        d_log = jnp.where(causal, fcum[..., :, None] - fcum[..., None, :] + ig[..., None, :], -jnp.inf)
        inter_log = fcum + m_st[..., None]
        m_t = jnp.maximum(inter_log, jnp.max(d_log, -1))
        w_intra = jnp.exp(d_log - m_t[..., None]) * jnp.einsum('bhtd,bhsd->bhts', qb, kb)
        w_inter = jnp.exp(inter_log - m_t)
        num = (w_inter[..., None] * jnp.einsum('bhtd,bhde->bhte', qb, c_st)
               + jnp.einsum('bhts,bhse->bhte', w_intra, vb))
        den = w_inter * jnp.einsum('bhtd,bhd->bht', qb, n_st) + jnp.sum(w_intra, -1)
        h = num / jnp.maximum(jnp.abs(den), jnp.exp(-m_t))[..., None]
        end_log = fcum[..., -1:] - fcum + ig
        m_new = jnp.maximum(fcum[..., -1] + m_st, jnp.max(end_log, -1))
        w_s = jnp.exp(end_log - m_new[..., None])
        decay = jnp.exp(fcum[..., -1] + m_st - m_new)
        c_st = decay[..., None, None] * c_st + jnp.einsum('bhs,bhsd,bhse->bhde', w_s, kb, vb)
        n_st = decay[..., None] * n_st + jnp.einsum('bhs,bhsd->bhd', w_s, kb)
        return (c_st, n_st, m_new), h

    carry0 = (jnp.zeros((B, MLSTM_HEADS, MLSTM_DK, MLSTM_DV), f32),
              jnp.zeros((B, MLSTM_HEADS, MLSTM_DK), f32),
              jnp.zeros((B, MLSTM_HEADS), f32))
    _, h = lax.scan(step, carry0, (chunks(q, MLSTM_DK) * MLSTM_DK ** -0.5, chunks(k, MLSTM_DK),
                                   chunks(v, MLSTM_DV), gate_chunks(log_i), gate_chunks(log_f)))
    h = h.transpose(1, 0, 3, 2, 4).reshape(B, S, MLSTM_HEADS, MLSTM_DV)
    h = head_norm(h, norm_g).reshape(B, S, ML_VW) * jax.nn.sigmoid(o_pre.astype(f32))
    return h.astype(q.dtype)


def mla_mixer(c_q, c_kv, k_r, pos, q_norm_g, w_uq, kv_norm_g, w_ukv):
    B, S, _ = c_q.shape
    q = (rms_norm(c_q, q_norm_g) @ w_uq).reshape(B, S, MLA_HEADS, MLA_NOPE + MLA_ROPE)
    q_nope = q[..., :MLA_NOPE]
    q_rope = rope(q[..., MLA_NOPE:], pos)
    kv = (rms_norm(c_kv, kv_norm_g) @ w_ukv).reshape(B, S, MLA_HEADS, MLA_NOPE + MLA_V)
    k_nope, v = kv[..., :MLA_NOPE], kv[..., MLA_NOPE:]
    k_rope = rope(k_r[:, :, None, :], pos)[:, :, 0, :]
    scale = (MLA_NOPE + MLA_ROPE) ** -0.5
    nb = S // Q_BLOCK
    qn_b = q_nope.reshape(B, nb, Q_BLOCK, MLA_HEADS, MLA_NOPE).swapaxes(0, 1)
    qr_b = q_rope.reshape(B, nb, Q_BLOCK, MLA_HEADS, MLA_ROPE).swapaxes(0, 1)
    key_chunk = jnp.arange(S) // CHUNK

    def attend(args):
        qn, qr, blk = args
        s = (jnp.einsum('bqhd,bkhd->bhqk', qn, k_nope)
             + jnp.einsum('bqhr,bkr->bhqk', qr, k_rope)).astype(jnp.float32) * scale
        q_chunk = (blk * Q_BLOCK + jnp.arange(Q_BLOCK)) // CHUNK
        mask = key_chunk[None, :] <= q_chunk[:, None]
        p = jax.nn.softmax(jnp.where(mask, s, -jnp.inf), axis=-1).astype(v.dtype)
        return jnp.einsum('bhqk,bkhd->bqhd', p, v)

    o = lax.map(attend, (qn_b, qr_b, jnp.arange(nb)))
    return o.swapaxes(0, 1).reshape(B, S, MLA_OUT)


def memory_cross_attention(x, mem, w_q, w_kv, w_o):
    B, S, D = x.shape
    q = (x @ w_q).reshape(B, S, XA_HEADS, XA_DH)
    kv = (mem @ w_kv).reshape(B, mem.shape[1], 2, XA_HEADS, XA_DH)
    k, v = kv[:, :, 0], kv[:, :, 1]
    s = jnp.einsum('bqhd,bkhd->bhqk', q, k).astype(jnp.float32) * XA_DH ** -0.5
    p = jax.nn.softmax(s, axis=-1).astype(x.dtype)
    o = jnp.einsum('bhqk,bkhd->bqhd', p, v).reshape(B, S, D)
    return o @ w_o


def hierarchical_moe(x, w_group, b_group, w_router, b_router, w_gate, w_up, w_down):
    B, S, D = x.shape
    t = x.reshape(-1, D)
    n_tok = t.shape[0]
    g_prob = jax.nn.softmax((t @ w_group + b_group).astype(jnp.float32), axis=-1)
    g_p, g_idx = lax.top_k(g_prob, 1)
    e_logits = (t @ w_router + b_router).astype(jnp.float32).reshape(n_tok, N_GROUPS, EXPERTS_PER_GROUP)
    e_sel = e_logits[jnp.arange(n_tok), g_idx[:, 0]]
    e_p, e_idx = lax.top_k(jax.nn.softmax(e_sel, axis=-1), EXPERT_TOP_K)
    weights = g_p * (e_p / jnp.sum(e_p, -1, keepdims=True))
    expert_id = g_idx * EXPERTS_PER_GROUP + e_idx
    gate = jnp.sum(jax.nn.one_hot(expert_id, N_EXPERTS, dtype=jnp.float32) * weights[..., None], axis=1)
    gate = gate.astype(t.dtype)
    out = jnp.zeros_like(t)
    for e in range(N_EXPERTS):
        h = jax.nn.silu(t @ w_gate[e]) * (t @ w_up[e])
        out = out + (h @ w_down[e]) * gate[:, e:e + 1]
    return out.reshape(B, S, D)


def hybrid_layer(x, mem, positions, w_in, w_out, gla_w_a2, gla_b_a, gla_norm_g, ml_conv_w, ml_b_i, ml_b_f,
                 ml_norm_g, mla_q_norm_g, mla_w_uq, mla_kv_norm_g, mla_w_ukv, xa_w_q, xa_w_kv, xa_w_o,
                 moe_w_group, moe_b_group, moe_w_router, moe_b_router, moe_w_gate, moe_w_up, moe_w_down,
                 ln1_g, ln1_b, ln2_g, ln2_b, ln3_g, ln3_b):
    y = x @ w_in
    split_at = np.cumsum(IN_SIZES)[:-1].tolist()
    (g_q, g_k, g_v, g_r, g_a, m_qk, m_v, m_o, m_i, m_f, c_q, c_kv, k_r) = jnp.split(y, split_at, axis=-1)
    o_gla = gla_mixer(g_q, g_k, g_v, g_r, g_a, gla_w_a2, gla_b_a, gla_norm_g)
    m_qk = jax.nn.silu(causal_depthwise_conv(m_qk, ml_conv_w))
    o_ml = mlstm_mixer(m_qk[..., :ML_QK], m_qk[..., ML_QK:], m_v, m_o, m_i + ml_b_i, m_f + ml_b_f, ml_norm_g)
    o_mla = mla_mixer(c_q, c_kv, k_r, positions, mla_q_norm_g, mla_w_uq, mla_kv_norm_g, mla_w_ukv)
    mix = jnp.concatenate([o_gla, o_ml, o_mla], axis=-1) @ w_out
    x = layer_norm(ALPHA * x + mix, ln1_g, ln1_b)
    x = layer_norm(ALPHA * x + memory_cross_attention(x, mem, xa_w_q, xa_w_kv, xa_w_o), ln2_g, ln2_b)
    moe = hierarchical_moe(x, moe_w_group, moe_b_group, moe_w_router, moe_b_router, moe_w_gate, moe_w_up, moe_w_down)
    return layer_norm(ALPHA * x + moe, ln3_g, ln3_b)


def setup_inputs(seed: int = 0) -> dict:
    key = jax.random.key(seed)
    ks = iter(jax.random.split(key, 40))
    f32 = jnp.float32

    def nrm(shape, scale):
        return jax.random.normal(next(ks), shape, f32) * scale

    def gain(shape):
        return 1.0 + 0.02 * jax.random.normal(next(ks), shape, f32)

    L = DEPTH
    x = jax.random.normal(next(ks), (BATCH, SEQ, D_MODEL), f32)
    mem = jax.random.normal(next(ks), (BATCH, MEM_LEN, D_MODEL), f32)
    offset = jax.random.randint(next(ks), (BATCH, 1), 0, 64, dtype=jnp.int32) * CHUNK
    positions = (offset + jnp.arange(SEQ, dtype=jnp.int32)[None, :]).astype(jnp.int32)
    f_bias = jnp.linspace(3.0, 6.0, MLSTM_HEADS, dtype=f32)[None, :] + 0.1 * jax.random.normal(next(ks), (L, MLSTM_HEADS), f32)
    return {
        "x": x,
        "mem": mem,
        "positions": positions,
        "w_in": nrm((L, D_MODEL, IN_WIDTH), D_MODEL ** -0.5),
        "w_out": nrm((L, MIX_WIDTH, D_MODEL), MIX_WIDTH ** -0.5 * BETA),
        "gla_w_a2": nrm((L, GLA_GATE_RANK, GLA_QK), GLA_GATE_RANK ** -0.5),
        "gla_b_a": nrm((L, GLA_QK), 0.1),
        "gla_norm_g": gain((L, GLA_VW)),
        "ml_conv_w": nrm((L, MLSTM_CONV, 2 * ML_QK), MLSTM_CONV ** -0.5),
        "ml_b_i": nrm((L, MLSTM_HEADS), 0.1),
        "ml_b_f": f_bias,
        "ml_norm_g": gain((L, ML_VW)),
        "mla_q_norm_g": gain((L, MLA_Q_RANK)),
        "mla_w_uq": nrm((L, MLA_Q_RANK, MLA_HEADS * (MLA_NOPE + MLA_ROPE)), MLA_Q_RANK ** -0.5),
        "mla_kv_norm_g": gain((L, MLA_KV_RANK)),
        "mla_w_ukv": nrm((L, MLA_KV_RANK, MLA_HEADS * (MLA_NOPE + MLA_V)), MLA_KV_RANK ** -0.5),
        "xa_w_q": nrm((L, D_MODEL, D_MODEL), D_MODEL ** -0.5),
        "xa_w_kv": nrm((L, D_MODEL, 2 * D_MODEL), D_MODEL ** -0.5),
        "xa_w_o": nrm((L, D_MODEL, D_MODEL), D_MODEL ** -0.5 * BETA),
        "moe_w_group": nrm((L, D_MODEL, N_GROUPS), D_MODEL ** -0.5),
        "moe_b_group": nrm((L, N_GROUPS), 0.01),
        "moe_w_router": nrm((L, D_MODEL, N_EXPERTS), D_MODEL ** -0.5),
        "moe_b_router": nrm((L, N_EXPERTS), 0.01),
        "moe_w_gate": nrm((L, N_EXPERTS, D_MODEL, D_EXPERT), D_MODEL ** -0.5),
        "moe_w_up": nrm((L, N_EXPERTS, D_MODEL, D_EXPERT), D_MODEL ** -0.5),
        "moe_w_down": nrm((L, N_EXPERTS, D_EXPERT, D_MODEL), D_EXPERT ** -0.5 * BETA),
        "ln1_g": gain((L, D_MODEL)),
        "ln1_b": nrm((L, D_MODEL), 0.01),
        "ln2_g": gain((L, D_MODEL)),
        "ln2_b": nrm((L, D_MODEL), 0.01),
        "ln3_g": gain((L, D_MODEL)),
        "ln3_b": nrm((L, D_MODEL), 0.01),
    }


def reference(x, mem, positions, w_in, w_out, gla_w_a2, gla_b_a, gla_norm_g, ml_conv_w, ml_b_i, ml_b_f,
              ml_norm_g, mla_q_norm_g, mla_w_uq, mla_kv_norm_g, mla_w_ukv, xa_w_q, xa_w_kv, xa_w_o,
              moe_w_group, moe_b_group, moe_w_router, moe_b_router, moe_w_gate, moe_w_up, moe_w_down,
              ln1_g, ln1_b, ln2_g, ln2_b, ln3_g, ln3_b):
    for l in range(DEPTH):
        x = hybrid_layer(x, mem, positions, w_in[l], w_out[l], gla_w_a2[l], gla_b_a[l], gla_norm_g[l],
                         ml_conv_w[l], ml_b_i[l], ml_b_f[l], ml_norm_g[l], mla_q_norm_g[l], mla_w_uq[l],
                         mla_kv_norm_g[l], mla_w_ukv[l], xa_w_q[l], xa_w_kv[l], xa_w_o[l],
                         moe_w_group[l], moe_b_group[l], moe_w_router[l], moe_b_router[l],
                         moe_w_gate[l], moe_w_up[l], moe_w_down[l],
                         ln1_g[l], ln1_b[l], ln2_g[l], ln2_b[l], ln3_g[l], ln3_b[l])
    return x
```

```python
import functools

import numpy as np
import jax
import jax.numpy as jnp
from jax import lax
from jax.experimental import pallas as pl
from jax.experimental.pallas import tpu as pltpu

F32 = jnp.float32
BF16 = jnp.bfloat16
HIGHEST = lax.Precision.HIGHEST

D_MODEL = 1024
CHUNK = 64
HEAD_D = 64
MIX_HEADS = 4
MIX_W = MIX_HEADS * HEAD_D
GLA_GATE_RANK = 16
GLA_TAU = 16.0
MLSTM_CONV = 4
MLA_HEADS = 8
MLA_NOPE = 64
MLA_ROPE = 32
MLA_V = 64
MLA_Q_RANK = 256
MLA_KV_RANK = 128
MLA_QK_PAD = 128
ROPE_BASE = 10000.0
XA_HEADS = 4
XA_DH = D_MODEL // XA_HEADS
N_GROUPS = 4
EXPERTS_PER_GROUP = 8
N_EXPERTS = N_GROUPS * EXPERTS_PER_GROUP
D_EXPERT = 256
DEPTH = 2
ALPHA = (2 * DEPTH) ** 0.25
LN_EPS = 1e-5
LANES = 128
SEG_W = 1152
MLA_SEG_W = 640
GLA_LEVELS = (32, 16, 8, 4, 2, 1)
VMEM_LIMIT = 56 * 1024 * 1024

_IN_SIZES = (256, 256, 256, 256, GLA_GATE_RANK, 512, 256, 256, 4, 4, MLA_Q_RANK, MLA_KV_RANK, MLA_ROPE)
_IN_OFF = np.concatenate([[0], np.cumsum(_IN_SIZES)]).tolist()


def _params(sem):
    return pltpu.CompilerParams(dimension_semantics=sem, vmem_limit_bytes=VMEM_LIMIT)


def _full(shape):
    return pl.BlockSpec(shape, lambda *_: (0,) * len(shape))


def _layer_norm(x, g, b):
    mu = jnp.mean(x, -1, keepdims=True)
    xc = x - mu
    var = jnp.mean(xc * xc, -1, keepdims=True)
    return xc * lax.rsqrt(var + LN_EPS) * g + b


def _log_sigmoid(z):
    return jnp.minimum(z, 0.0) - jnp.log1p(jnp.exp(-jnp.abs(z)))


def _dot_nt(a, b):
    return lax.dot_general(a, b, (((1,), (1,)), ((), ())), preferred_element_type=F32)


def _dot_tn(a, b):
    return lax.dot_general(a, b, (((0,), (0,)), ((), ())), preferred_element_type=F32)


def _dot(a, b):
    return jnp.dot(a, b, preferred_element_type=F32)


def _head_norm(o, g):
    mu = jnp.mean(o, -1, keepdims=True)
    oc = o - mu
    var = jnp.mean(oc * oc, -1, keepdims=True)
    return oc * lax.rsqrt(var + LN_EPS) * g


def _in_proj_kernel(x_ref, wg_ref, wm_ref, wc_ref, wift_ref, yg_ref, ym_ref, yc_ref, yift_ref):
    xb = x_ref[...].astype(BF16)
    yg_ref[...] = _dot(xb, wg_ref[...])
    ym_ref[...] = _dot(xb, wm_ref[...])
    yc_ref[...] = _dot(xb, wc_ref[...])
    yift_ref[...] = _dot_nt(wift_ref[...], xb)


def _in_proj(x2d, wg, wm, wc, wift, tm):
    n = x2d.shape[0]
    row = lambda w: pl.BlockSpec((tm, w), lambda i: (i, 0))
    return pl.pallas_call(
        _in_proj_kernel,
        grid=(n // tm,),
        in_specs=[row(D_MODEL), _full(wg.shape), _full(wm.shape), _full(wc.shape), _full(wift.shape)],
        out_specs=[row(SEG_W), row(SEG_W), row(MLA_SEG_W), pl.BlockSpec((8, tm), lambda i: (0, i))],
        out_shape=[jax.ShapeDtypeStruct((n, SEG_W), F32), jax.ShapeDtypeStruct((n, SEG_W), F32),
                   jax.ShapeDtypeStruct((n, MLA_SEG_W), F32), jax.ShapeDtypeStruct((8, n), F32)],
        compiler_params=_params(("parallel",)),
        name="in_proj",
    )(x2d, wg, wm, wc, wift)


def _gla_constants():
    t = np.arange(CHUNK)
    n_lv = len(GLA_LEVELS)
    mc = np.zeros((n_lv + 2, CHUNK, CHUNK), np.float32)
    masks = np.zeros((n_lv + 1, CHUNK, CHUNK), np.float32)
    right = np.zeros((n_lv, CHUNK, 1), np.float32)
    for li, n in enumerate(GLA_LEVELS):
        blk = t // (2 * n)
        is_right = (t % (2 * n)) >= n
        r = blk * 2 * n + n - 1
        for i in range(CHUNK):
            if is_right[i]:
                mc[li, i, r[i] + 1:i + 1] = 1.0
            else:
                mc[li, i, i + 1:r[i] + 1] = 1.0
        masks[li] = ((blk[:, None] == blk[None, :]) & is_right[:, None] & ~is_right[None, :])
        right[li, :, 0] = is_right
    mc[n_lv] = np.tril(np.ones((CHUNK, CHUNK)))
    mc[n_lv + 1] = np.triu(np.ones((CHUNK, CHUNK)), 1)
    masks[n_lv] = np.eye(CHUNK)
    return mc.reshape(-1, CHUNK), masks, right


def _gla_kernel(y_ref, wa2_ref, ba_ref, g_ref, mc_ref, mask_ref, right_ref, o_ref, st_ref, *, ts):
    n_lv = len(GLA_LEVELS)

    @pl.when(pl.program_id(1) == 0)
    def _():
        st_ref[...] = jnp.zeros_like(st_ref)

    def chunk(c, carry):
        rows = pl.ds(pl.multiple_of(c * CHUNK, CHUNK), CHUNK)
        q = y_ref[rows, 0:256] * (HEAD_D ** -0.5)
        k = y_ref[rows, 256:512]
        v = y_ref[rows, 512:768].astype(BF16)
        r_gate = y_ref[rows, 768:1024]
        a_lr = y_ref[rows, 1024:1152]
        z = jnp.dot(a_lr, wa2_ref[...], precision=HIGHEST, preferred_element_type=F32) + ba_ref[...]
        log_a = _log_sigmoid(z) * (1.0 / GLA_TAU)
        decay = jnp.exp(jnp.dot(mc_ref[...], log_a, precision=HIGHEST, preferred_element_type=F32))
        scores = [jnp.zeros((CHUNK, CHUNK), F32) for _ in range(MIX_HEADS)]
        for li in range(n_lv):
            x = (jnp.where(right_ref[li] > 0.0, q, k) * decay[li * CHUNK:(li + 1) * CHUNK]).astype(BF16)
            for h in range(MIX_HEADS):
                xh = x[:, h * HEAD_D:(h + 1) * HEAD_D]
                scores[h] = scores[h] + _dot_nt(xh, xh) * mask_ref[li]
        qb = q.astype(BF16)
        kb = k.astype(BF16)
        dec_in = decay[n_lv * CHUNK:(n_lv + 1) * CHUNK]
        q_in = (q * dec_in).astype(BF16)
        k_out = (k * decay[(n_lv + 1) * CHUNK:(n_lv + 2) * CHUNK]).astype(BF16)
        dec_end = dec_in[CHUNK - 1:CHUNK, :]
        outs = []
        for h in range(MIX_HEADS):
            sl = slice(h * HEAD_D, (h + 1) * HEAD_D)
            a = scores[h] + _dot_nt(qb[:, sl], kb[:, sl]) * mask_ref[n_lv]
            st = st_ref[h]
            o = _dot(a.astype(BF16), v[:, sl]) + _dot_nt(q_in[:, sl], st.astype(BF16))
            st_ref[h] = st * dec_end[:, sl] + _dot_tn(v[:, sl], k_out[:, sl])
            outs.append(_head_norm(o, g_ref[:, sl]))
        o_all = jnp.concatenate(outs, axis=-1)
        o_ref[rows, :] = (o_all * (r_gate * jax.nn.sigmoid(r_gate))).astype(o_ref.dtype)
        return carry

    lax.fori_loop(0, ts // CHUNK, chunk, 0)


def _gla(yg, wa2, ba, g, batch, seq, ts):
    n = yg.shape[0]
    nt = seq // ts
    mc, masks, right = _gla_constants()
    return pl.pallas_call(
        functools.partial(_gla_kernel, ts=ts),
        grid=(batch, nt),
        in_specs=[pl.BlockSpec((ts, SEG_W), lambda b, i: (b * nt + i, 0)),
                  _full(wa2.shape), _full(ba.shape), _full(g.shape),
                  _full(mc.shape), _full(masks.shape), _full(right.shape)],
        out_specs=pl.BlockSpec((ts, MIX_W), lambda b, i: (b * nt + i, 0)),
        out_shape=jax.ShapeDtypeStruct((n, MIX_W), BF16),
        scratch_shapes=[pltpu.VMEM((MIX_HEADS, HEAD_D, HEAD_D), F32)],
        compiler_params=_params(("parallel", "arbitrary")),
        name="gla_mixer",
    )(yg, wa2, ba, g, jnp.asarray(mc), jnp.asarray(masks), jnp.asarray(right))


def _mlstm_kernel(y_ref, gr_ref, cw_ref, bcol_ref, brow_ref, g_ref, tri_ref, o_ref,
                  xe_ref, qk_ref, c_ref, n_ref, m_ref, *, ts):
    first = pl.program_id(1) == 0

    @pl.when(first)
    def _():
        xe_ref[0:8, :] = jnp.zeros((8, 2 * MIX_W), F32)
        c_ref[...] = jnp.zeros_like(c_ref)
        n_ref[...] = jnp.zeros_like(n_ref)
        m_ref[...] = jnp.zeros_like(m_ref)

    @pl.when(jnp.logical_not(first))
    def _():
        xe_ref[0:8, :] = xe_ref[ts:ts + 8, :]

    xe_ref[8:ts + 8, :] = y_ref[:, 0:2 * MIX_W]
    conv = cw_ref[MLSTM_CONV - 1:MLSTM_CONV, :] * xe_ref[8:ts + 8, :]
    for j in range(MLSTM_CONV - 1):
        conv = conv + cw_ref[j:j + 1, :] * xe_ref[pl.ds(8 - (MLSTM_CONV - 1) + j, ts), :]
    qk_ref[...] = conv * jax.nn.sigmoid(conv)

    tri = tri_ref[...]
    t_idx = lax.broadcasted_iota(jnp.int32, (CHUNK, CHUNK), 0)
    s_idx = lax.broadcasted_iota(jnp.int32, (CHUNK, CHUNK), 1)
    causal = s_idx <= t_idx

    def chunk(c, carry):
        rows = pl.ds(pl.multiple_of(c * CHUNK, CHUNK), CHUNK)
        q = (qk_ref[rows, 0:MIX_W] * (HEAD_D ** -0.5)).astype(BF16)
        k_f = qk_ref[rows, MIX_W:2 * MIX_W]
        k = k_f.astype(BF16)
        v = y_ref[rows, 512:768].astype(BF16)
        o_gate = y_ref[rows, 768:1024]
        g_col = y_ref[rows, 1024:1152] + bcol_ref[...]
        fcum_col = jnp.dot(tri, _log_sigmoid(g_col), precision=HIGHEST, preferred_element_type=F32)
        g_row = gr_ref[c] + brow_ref[...]
        fcum_row = lax.dot_general(_log_sigmoid(g_row), tri, (((1,), (1,)), ((), ())),
                                   precision=HIGHEST, preferred_element_type=F32)
        outs = []
        for h in range(MIX_HEADS):
            sl = slice(h * HEAD_D, (h + 1) * HEAD_D)
            fc_c = fcum_col[:, MIX_HEADS + h:MIX_HEADS + h + 1]
            ig_c = g_col[:, h:h + 1]
            fc_r = fcum_row[MIX_HEADS + h:MIX_HEADS + h + 1, :]
            ig_r = g_row[h:h + 1, :]
            m_st = m_ref[h][0:1, 0:1]
            f_end = fc_c[CHUNK - 1:CHUNK, :]
            d_log = jnp.where(causal, fc_c - fc_r + ig_r, -jnp.inf)
            inter_log = fc_c + m_st
            m_t = jnp.maximum(inter_log, jnp.max(d_log, -1, keepdims=True))
            w_intra = jnp.exp(d_log - m_t) * _dot_nt(q[:, sl], k[:, sl])
            w_inter = jnp.exp(inter_log - m_t)
            c_st = c_ref[h]
            n_st = n_ref[h]
            num = w_inter * _dot_nt(q[:, sl], c_st.astype(BF16)) + _dot(w_intra.astype(BF16), v[:, sl])
            den = (w_inter * jnp.sum(q[:, sl].astype(F32) * n_st, -1, keepdims=True)
                   + jnp.sum(w_intra, -1, keepdims=True))
            hid = num / jnp.maximum(jnp.abs(den), jnp.exp(-m_t))
            end_c = f_end - fc_c + ig_c
            end_r = fc_r[:, CHUNK - 1:CHUNK] - fc_r + ig_r
            m_new = jnp.maximum(f_end + m_st, jnp.max(end_r, -1, keepdims=True))
            w_s = jnp.exp(end_c - m_new)
            dec = jnp.exp(f_end + m_st - m_new)
            k_w = k_f[:, sl] * w_s
            c_ref[h] = dec * c_st + _dot_tn(v[:, sl], k_w.astype(BF16))
            n_ref[h] = dec * n_st + jnp.sum(k_w, 0, keepdims=True)
            m_ref[h] = jnp.broadcast_to(m_new, m_ref.shape[1:])
            outs.append(_head_norm(hid, g_ref[:, sl]))
        h_all = jnp.concatenate(outs, axis=-1)
        o_ref[rows, :] = (h_all * jax.nn.sigmoid(o_gate)).astype(o_ref.dtype)
        return carry

    lax.fori_loop(0, ts // CHUNK, chunk, 0)


def _mlstm(ym, gates_row, conv_w, b_col, b_row, g, batch, seq, ts):
    n = ym.shape[0]
    nt = seq // ts
    nck = ts // CHUNK
    tri = jnp.asarray(np.tril(np.ones((CHUNK, CHUNK), np.float32)))
    return pl.pallas_call(
        functools.partial(_mlstm_kernel, ts=ts),
        grid=(batch, nt),
        in_specs=[pl.BlockSpec((ts, SEG_W), lambda b, i: (b * nt + i, 0)),
                  pl.BlockSpec((nck, 8, CHUNK), lambda b, i: (b * nt + i, 0, 0)),
                  _full(conv_w.shape), _full(b_col.shape), _full(b_row.shape), _full(g.shape),
                  _full(tri.shape)],
        out_specs=pl.BlockSpec((ts, MIX_W), lambda b, i: (b * nt + i, 0)),
        out_shape=jax.ShapeDtypeStruct((n, MIX_W), BF16),
        scratch_shapes=[pltpu.VMEM((ts + 8, 2 * MIX_W), F32),
                        pltpu.VMEM((ts, 2 * MIX_W), F32),
                        pltpu.VMEM((MIX_HEADS, HEAD_D, HEAD_D), F32),
                        pltpu.VMEM((MIX_HEADS, 1, HEAD_D), F32),
                        pltpu.VMEM((MIX_HEADS, 8, LANES), F32)],
        compiler_params=_params(("parallel", "arbitrary")),
        name="mlstm_mixer",
    )(ym, gates_row, conv_w, b_col, b_row, g, tri)


def _rope_table_kernel(pos_ref, inv_ref, cos_ref, sin_ref):
    ang = pos_ref[...].astype(F32) * inv_ref[...]
    lane = lax.broadcasted_iota(jnp.int32, ang.shape, 1)
    rot = (lane >= MLA_NOPE) & (lane < MLA_NOPE + MLA_ROPE)
    first_half = lane < MLA_NOPE + MLA_ROPE // 2
    cos_ref[...] = jnp.where(lane < MLA_NOPE, 1.0, jnp.where(rot, jnp.cos(ang), 0.0))
    s = jnp.sin(ang)
    sin_ref[...] = jnp.where(rot, jnp.where(first_half, -s, s), 0.0)


def _rope_tables(pos_col, tm):
    n = pos_col.shape[0]
    half = MLA_ROPE // 2
    inv = ROPE_BASE ** (-np.arange(half, dtype=np.float32) / half)
    inv_row = np.zeros((1, LANES), np.float32)
    inv_row[0, MLA_NOPE:MLA_NOPE + half] = inv
    inv_row[0, MLA_NOPE + half:MLA_NOPE + MLA_ROPE] = inv
    return pl.pallas_call(
        _rope_table_kernel,
        grid=(n // tm,),
        in_specs=[pl.BlockSpec((tm, 1), lambda i: (i, 0)), _full((1, LANES))],
        out_specs=[pl.BlockSpec((tm, LANES), lambda i: (i, 0))] * 2,
        out_shape=[jax.ShapeDtypeStruct((n, LANES), F32)] * 2,
        compiler_params=_params(("parallel",)),
        name="rope_tables",
    )(pos_col, jnp.asarray(inv_row))


def _mla_prep_kernel(y_ref, cos_ref, sin_ref, gq_ref, gkv_ref, wqa_ref, wqb_ref, wkn_ref, wv_ref,
                     q_ref, k_ref, v_ref):
    def rms(x, g):
        return x * lax.rsqrt(jnp.mean(x * x, -1, keepdims=True) + LN_EPS) * g

    cos = cos_ref[...]
    sin = sin_ref[...]
    cq = rms(y_ref[:, 0:MLA_Q_RANK], gq_ref[...]).astype(BF16)
    ckv = rms(y_ref[:, MLA_Q_RANK:MLA_Q_RANK + MLA_KV_RANK], gkv_ref[...]).astype(BF16)
    k_rope = y_ref[:, 384:512] * cos + y_ref[:, 512:640] * sin
    qa = _dot(cq, wqa_ref[...])
    qb = _dot(cq, wqb_ref[...])
    kn = _dot(ckv, wkn_ref[...])
    scale = (MLA_NOPE + MLA_ROPE) ** -0.5
    for h in range(MLA_HEADS):
        sl = slice(h * MLA_QK_PAD, (h + 1) * MLA_QK_PAD)
        q_ref[:, sl] = ((qa[:, sl] * cos + qb[:, sl] * sin) * scale).astype(q_ref.dtype)
        k_ref[:, sl] = (kn[:, sl] + k_rope).astype(k_ref.dtype)
    v_ref[...] = _dot(ckv, wv_ref[...]).astype(v_ref.dtype)


def _mla_prep(yc, cos_t, sin_t, gq, gkv, wqa, wqb, wkn, wv, tm):
    n = yc.shape[0]
    row = lambda w: pl.BlockSpec((tm, w), lambda i: (i, 0))
    qk_w = MLA_HEADS * MLA_QK_PAD
    return pl.pallas_call(
        _mla_prep_kernel,
        grid=(n // tm,),
        in_specs=[row(MLA_SEG_W), row(LANES), row(LANES), _full(gq.shape), _full(gkv.shape),
                  _full(wqa.shape), _full(wqb.shape), _full(wkn.shape), _full(wv.shape)],
        out_specs=[row(qk_w), row(qk_w), row(MLA_HEADS * MLA_V)],
        out_shape=[jax.ShapeDtypeStruct((n, qk_w), BF16), jax.ShapeDtypeStruct((n, qk_w), BF16),
                   jax.ShapeDtypeStruct((n, MLA_HEADS * MLA_V), BF16)],
        compiler_params=_params(("parallel",)),
        name="mla_prep",
    )(yc, cos_t, sin_t, gq, gkv, wqa, wqb, wkn, wv)


def _mla_attn_kernel(q_ref, k_ref, v_ref, o_ref, *, tq):
    i = pl.program_id(1)
    row_chunk = (i * tq + lax.broadcasted_iota(jnp.int32, (tq, 1), 0)) // CHUNK
    outs = []
    for h in range(MLA_HEADS):
        q = q_ref[:, h * MLA_QK_PAD:(h + 1) * MLA_QK_PAD]

        def block(j, carry, masked, h=h, q=q):
            m, l, acc = carry
            ks = pl.multiple_of(j * tq, tq)
            k = k_ref[0, pl.ds(ks, tq), h * MLA_QK_PAD:(h + 1) * MLA_QK_PAD]
            v = v_ref[0, pl.ds(ks, tq), h * MLA_V:(h + 1) * MLA_V]
            s = _dot_nt(q, k)
            if masked:
                col_chunk = (ks + lax.broadcasted_iota(jnp.int32, (1, tq), 1)) // CHUNK
                s = jnp.where(col_chunk <= row_chunk, s, -jnp.inf)
            m_new = jnp.maximum(m, jnp.max(s, -1, keepdims=True))
            alpha = jnp.exp(m - m_new)
            p = jnp.exp(s - m_new)
            l = alpha * l + jnp.sum(p, -1, keepdims=True)
            acc = alpha * acc + _dot(p.astype(BF16), v)
            return m_new, l, acc

        init = (jnp.full((tq, 1), -jnp.inf, F32), jnp.zeros((tq, 1), F32), jnp.zeros((tq, MLA_V), F32))
        carry = lax.fori_loop(0, i, functools.partial(block, masked=False), init)
        _, l, acc = block(i, carry, True)
        outs.append(acc / l)
    o_ref[...] = jnp.concatenate(outs, axis=-1).astype(o_ref.dtype)


def _mla_attn(q, k, v, batch, seq, tq):
    n = q.shape[0]
    nt = seq // tq
    qk_w = MLA_HEADS * MLA_QK_PAD
    v_w = MLA_HEADS * MLA_V
    return pl.pallas_call(
        functools.partial(_mla_attn_kernel, tq=tq),
        grid=(batch, nt),
        in_specs=[pl.BlockSpec((tq, qk_w), lambda b, i: (b * nt + i, 0)),
                  pl.BlockSpec((1, seq, qk_w), lambda b, i: (b, 0, 0)),
                  pl.BlockSpec((1, seq, v_w), lambda b, i: (b, 0, 0))],
        out_specs=pl.BlockSpec((tq, v_w), lambda b, i: (b * nt + i, 0)),
        out_shape=jax.ShapeDtypeStruct((n, v_w), BF16),
        compiler_params=_params(("parallel", "arbitrary")),
        name="mla_attention",
    )(q, k.reshape(batch, seq, qk_w), v.reshape(batch, seq, v_w))


def _out_proj_kernel(og_ref, om_ref, oc_ref, x_ref, wg_ref, wm_ref, wc_ref, g_ref, b_ref, o_ref):
    mix = _dot(og_ref[...], wg_ref[...]) + _dot(om_ref[...], wm_ref[...]) + _dot(oc_ref[...], wc_ref[...])
    o_ref[...] = _layer_norm(ALPHA * x_ref[...] + mix, g_ref[...], b_ref[...])


def _out_proj(og, om, oc, x2d, wg, wm, wc, g, b, tm):
    n = x2d.shape[0]
    row = lambda w: pl.BlockSpec((tm, w), lambda i: (i, 0))
    return pl.pallas_call(
        _out_proj_kernel,
        grid=(n // tm,),
        in_specs=[row(MIX_W), row(MIX_W), row(MLA_HEADS * MLA_V), row(D_MODEL),
                  _full(wg.shape), _full(wm.shape), _full(wc.shape), _full(g.shape), _full(b.shape)],
        out_specs=row(D_MODEL),
        out_shape=jax.ShapeDtypeStruct((n, D_MODEL), F32),
        compiler_params=_params(("parallel",)),
        name="out_proj_ln1",
    )(og, om, oc, x2d, wg, wm, wc, g, b)


def _xa_kv_kernel(mem_ref, w_ref, k_ref, v_ref):
    kv = _dot(mem_ref[...].astype(BF16), w_ref[...])
    k_ref[...] = kv[:, 0:D_MODEL].astype(k_ref.dtype)
    v_ref[...] = kv[:, D_MODEL:2 * D_MODEL].astype(v_ref.dtype)


def _xa_kv(mem2d, w_kv, mem_len):
    n = mem2d.shape[0]
    row = pl.BlockSpec((mem_len, D_MODEL), lambda i: (i, 0))
    return pl.pallas_call(
        _xa_kv_kernel,
        grid=(n // mem_len,),
        in_specs=[row, _full(w_kv.shape)],
        out_specs=[row, row],
        out_shape=[jax.ShapeDtypeStruct((n, D_MODEL), BF16)] * 2,
        compiler_params=_params(("parallel",)),
        name="xattn_kv",
    )(mem2d, w_kv)


def _xattn_kernel(x_ref, k_ref, v_ref, wq_ref, wo_ref, g_ref, b_ref, o_ref):
    x = x_ref[...]
    q = (_dot(x.astype(BF16), wq_ref[...]) * (XA_DH ** -0.5)).astype(BF16)
    out = jnp.zeros(x.shape, F32)
    for h in range(XA_HEADS):
        sl = slice(h * XA_DH, (h + 1) * XA_DH)
        s = _dot_nt(q[:, sl], k_ref[:, sl])
        p = jnp.exp(s - jnp.max(s, -1, keepdims=True))
        p = p / jnp.sum(p, -1, keepdims=True)
        o = _dot(p.astype(BF16), v_ref[:, sl])
        out = out + _dot(o.astype(BF16), wo_ref[sl, :])
    o_ref[...] = _layer_norm(ALPHA * x + out, g_ref[...], b_ref[...])


def _xattn(x1, xk, xv, wq, wo, g, b, batch, seq, mem_len, tm):
    n = x1.shape[0]
    nt = seq // tm
    row = pl.BlockSpec((tm, D_MODEL), lambda bi, i: (bi * nt + i, 0))
    mem = pl.BlockSpec((mem_len, D_MODEL), lambda bi, i: (bi, 0))
    return pl.pallas_call(
        _xattn_kernel,
        grid=(batch, nt),
        in_specs=[row, mem, mem, _full(wq.shape), _full(wo.shape), _full(g.shape), _full(b.shape)],
        out_specs=row,
        out_shape=jax.ShapeDtypeStruct((n, D_MODEL), F32),
        compiler_params=_params(("parallel", "parallel")),
        name="xattn_ln2",
    )(x1, xk, xv, wq, wo, g, b)


def _route_kernel(x_ref, w_ref, b_ref, gate_ref):
    logits = jnp.dot(x_ref[...], w_ref[...], precision=HIGHEST, preferred_element_type=F32) + b_ref[...]
    lane = lax.broadcasted_iota(jnp.int32, logits.shape, 1).astype(F32)
    is_group = (lane >= N_EXPERTS) & (lane < N_EXPERTS + N_GROUPS)
    g_max = jnp.max(jnp.where(is_group, logits, -jnp.inf), -1, keepdims=True)
    g_sum = jnp.sum(jnp.where(is_group, jnp.exp(logits - g_max), 0.0), -1, keepdims=True)
    g_p = 1.0 / g_sum
    g_idx = jnp.min(jnp.where(is_group & (logits == g_max), lane - N_EXPERTS, float(LANES)), -1, keepdims=True)
    in_group = (lane < N_EXPERTS) & (jnp.floor(lane * (1.0 / EXPERTS_PER_GROUP)) == g_idx)
    e_max = jnp.max(jnp.where(in_group, logits, -jnp.inf), -1, keepdims=True)
    e_exp = jnp.where(in_group, jnp.exp(logits - e_max), 0.0)
    prob = e_exp / jnp.sum(e_exp, -1, keepdims=True)
    cand = jnp.where(in_group, prob, -1.0)
    p1 = jnp.max(cand, -1, keepdims=True)
    i1 = jnp.min(jnp.where(cand == p1, lane, float(LANES)), -1, keepdims=True)
    cand2 = jnp.where(lane == i1, -1.0, cand)
    p2 = jnp.max(cand2, -1, keepdims=True)
    i2 = jnp.min(jnp.where(cand2 == p2, lane, float(LANES)), -1, keepdims=True)
    p_sum = p1 + p2
    gate_ref[...] = (jnp.where(lane == i1, g_p * (p1 / p_sum), 0.0)
                     + jnp.where(lane == i2, g_p * (p2 / p_sum), 0.0))


def _route(x2, w_route, b_route, tm):
    n = x2.shape[0]
    return pl.pallas_call(
        _route_kernel,
        grid=(n // tm,),
        in_specs=[pl.BlockSpec((tm, D_MODEL), lambda i: (i, 0)), _full(w_route.shape), _full(b_route.shape)],
        out_specs=pl.BlockSpec((tm, LANES), lambda i: (i, 0)),
        out_shape=jax.ShapeDtypeStruct((n, LANES), F32),
        compiler_params=_params(("parallel",)),
        name="moe_route",
    )(x2, w_route, b_route)


def _moe_kernel(x_ref, gate_ref, wg_ref, wu_ref, wd_ref, g_ref, b_ref, o_ref, xb_ref, acc_ref):
    e = pl.program_id(1)

    @pl.when(e == 0)
    def _():
        xb_ref[...] = x_ref[...].astype(BF16)
        acc_ref[...] = jnp.zeros_like(acc_ref)

    xb = xb_ref[...]
    lane = lax.broadcasted_iota(jnp.int32, gate_ref.shape, 1)
    w_tok = jnp.sum(jnp.where(lane == e, gate_ref[...], 0.0), -1, keepdims=True)
    hg = _dot(xb, wg_ref[0].astype(BF16))
    hu = _dot(xb, wu_ref[0].astype(BF16))
    hid = hg * jax.nn.sigmoid(hg) * hu
    acc_ref[...] += _dot(hid.astype(BF16), wd_ref[0].astype(BF16)) * w_tok

    @pl.when(e == pl.num_programs(1) - 1)
    def _():
        o_ref[...] = _layer_norm(ALPHA * x_ref[...] + acc_ref[...], g_ref[...], b_ref[...])


def _moe(x2, gate, w_gate, w_up, w_down, g, b, tm):
    n = x2.shape[0]
    row = pl.BlockSpec((tm, D_MODEL), lambda i, e: (i, 0))
    return pl.pallas_call(
        _moe_kernel,
        grid=(n // tm, N_EXPERTS),
        in_specs=[row, pl.BlockSpec((tm, LANES), lambda i, e: (i, 0)),
                  pl.BlockSpec((1, D_MODEL, D_EXPERT), lambda i, e: (e, 0, 0)),
                  pl.BlockSpec((1, D_MODEL, D_EXPERT), lambda i, e: (e, 0, 0)),
                  pl.BlockSpec((1, D_EXPERT, D_MODEL), lambda i, e: (e, 0, 0)),
                  _full(g.shape), _full(b.shape)],
        out_specs=row,
        out_shape=jax.ShapeDtypeStruct((n, D_MODEL), F32),
        scratch_shapes=[pltpu.VMEM((tm, D_MODEL), BF16), pltpu.VMEM((tm, D_MODEL), F32)],
        compiler_params=_params(("parallel", "arbitrary")),
        name="moe_experts_ln3",
    )(x2, gate, w_gate, w_up, w_down, g, b)


def _pad_cols(w, width):
    return jnp.pad(w, ((0, 0), (0, width - w.shape[1])))


def _layer_weights(w_in, w_out, gla_w_a2, ml_b_i, ml_b_f, mla_w_uq, mla_w_ukv, moe_w_group, moe_b_group,
                   moe_w_router, moe_b_router):
    o = _IN_OFF
    half = MLA_ROPE // 2
    wg = _pad_cols(w_in[:, o[0]:o[5]], SEG_W).astype(BF16)
    wm = _pad_cols(w_in[:, o[5]:o[10]], SEG_W).astype(BF16)
    kr = w_in[:, o[12]:o[13]]
    zeros = lambda w: jnp.zeros((D_MODEL, w), F32)
    kra = jnp.concatenate([zeros(MLA_NOPE), kr, zeros(LANES - MLA_NOPE - MLA_ROPE)], 1)
    krb = jnp.concatenate([zeros(MLA_NOPE), kr[:, half:], kr[:, :half], zeros(LANES - MLA_NOPE - MLA_ROPE)], 1)
    wc = jnp.concatenate([w_in[:, o[10]:o[12]], kra, krb], 1).astype(BF16)
    wift = w_in[:, o[8]:o[10]].T.astype(BF16)
    wa2 = jnp.pad(gla_w_a2, ((0, LANES - GLA_GATE_RANK), (0, 0)))
    b_gate = jnp.concatenate([ml_b_i, ml_b_f])
    b_col = jnp.pad(b_gate, (0, LANES - 2 * MIX_HEADS)).reshape(1, LANES)
    b_row = b_gate.reshape(2 * MIX_HEADS, 1)
    uq = mla_w_uq.reshape(MLA_Q_RANK, MLA_HEADS, MLA_NOPE + MLA_ROPE)
    zq = jnp.zeros((MLA_Q_RANK, MLA_HEADS, LANES - MLA_NOPE - MLA_ROPE), F32)
    wqa = jnp.concatenate([uq, zq], -1).reshape(MLA_Q_RANK, -1).astype(BF16)
    wqb = jnp.concatenate([jnp.zeros((MLA_Q_RANK, MLA_HEADS, MLA_NOPE), F32), uq[..., MLA_NOPE + half:],
                           uq[..., MLA_NOPE:MLA_NOPE + half], zq], -1).reshape(MLA_Q_RANK, -1).astype(BF16)
    ukv = mla_w_ukv.reshape(MLA_KV_RANK, MLA_HEADS, MLA_NOPE + MLA_V)
    wkn = jnp.concatenate([ukv[..., :MLA_NOPE], jnp.zeros((MLA_KV_RANK, MLA_HEADS, LANES - MLA_NOPE), F32)],
                          -1).reshape(MLA_KV_RANK, -1).astype(BF16)
    wv = ukv[..., MLA_NOPE:].reshape(MLA_KV_RANK, -1).astype(BF16)
    wo = w_out.astype(BF16)
    w_route = _pad_cols(jnp.concatenate([moe_w_router, moe_w_group], 1), LANES)
    b_route = jnp.pad(jnp.concatenate([moe_b_router, moe_b_group]), (0, LANES - N_EXPERTS - N_GROUPS)).reshape(1, LANES)
    return dict(wg=wg, wm=wm, wc=wc, wift=wift, wa2=wa2, b_col=b_col, b_row=b_row, wqa=wqa, wqb=wqb, wkn=wkn,
                wv=wv, wo_g=wo[0:MIX_W], wo_m=wo[MIX_W:2 * MIX_W], wo_c=wo[2 * MIX_W:], w_route=w_route,
                b_route=b_route)


def _tile(total, want):
    t = min(total, want)
    assert total % t == 0
    return t


def kernel(x, mem, positions, w_in, w_out, gla_w_a2, gla_b_a, gla_norm_g, ml_conv_w, ml_b_i, ml_b_f, ml_norm_g, mla_q_norm_g, mla_w_uq, mla_kv_norm_g, mla_w_ukv, xa_w_q, xa_w_kv, xa_w_o, moe_w_group, moe_b_group, moe_w_router, moe_b_router, moe_w_gate, moe_w_up, moe_w_down, ln1_g, ln1_b, ln2_g, ln2_b, ln3_g, ln3_b):
    batch, seq, _ = x.shape
    mem_len = mem.shape[1]
    n = batch * seq
    depth = w_in.shape[0]
    assert seq % CHUNK == 0
    tm = _tile(n, 512)
    ts = _tile(seq, 512)
    tq = _tile(seq, 256)
    t_moe = _tile(n, 1024)
    row = lambda a: a.reshape(1, -1)

    cos_t, sin_t = _rope_tables(positions.reshape(n, 1), _tile(n, 2048))
    mem2d = mem.reshape(batch * mem_len, D_MODEL)
    h = x.reshape(n, D_MODEL)
    for l in range(depth):
        w = _layer_weights(w_in[l], w_out[l], gla_w_a2[l], ml_b_i[l], ml_b_f[l], mla_w_uq[l], mla_w_ukv[l],
                           moe_w_group[l], moe_b_group[l], moe_w_router[l], moe_b_router[l])
        yg, ym, yc, yift = _in_proj(h, w["wg"], w["wm"], w["wc"], w["wift"], tm)
        gates_row = yift.reshape(2 * MIX_HEADS, n // CHUNK, CHUNK).transpose(1, 0, 2)
        og = _gla(yg, w["wa2"], row(gla_b_a[l]), row(gla_norm_g[l]), batch, seq, ts)
        om = _mlstm(ym, gates_row, ml_conv_w[l], w["b_col"], w["b_row"], row(ml_norm_g[l]), batch, seq, ts)
        q, k, v = _mla_prep(yc, cos_t, sin_t, row(mla_q_norm_g[l]), row(mla_kv_norm_g[l]),
                            w["wqa"], w["wqb"], w["wkn"], w["wv"], tm)
        oc = _mla_attn(q, k, v, batch, seq, tq)
        x1 = _out_proj(og, om, oc, h, w["wo_g"], w["wo_m"], w["wo_c"], row(ln1_g[l]), row(ln1_b[l]), tm)
        xk, xv = _xa_kv(mem2d, xa_w_kv[l].astype(BF16), mem_len)
        x2 = _xattn(x1, xk, xv, xa_w_q[l].astype(BF16), xa_w_o[l].astype(BF16), row(ln2_g[l]), row(ln2_b[l]),
                    batch, seq, mem_len, tm)
        gate = _route(x2, w["w_route"], w["b_route"], tm)
        h = _moe(x2, gate, moe_w_gate[l], moe_w_up[l], moe_w_down[l], row(ln3_g[l]), row(ln3_b[l]), t_moe)
    return h.reshape(batch, seq, D_MODEL)
```

```python
import functools

import numpy as np
import jax
import jax.numpy as jnp
from jax import lax
from jax.experimental import pallas as pl
from jax.experimental.pallas import tpu as pltpu

F32 = jnp.float32
BF16 = jnp.bfloat16
HIGHEST = lax.Precision.HIGHEST

D_MODEL = 1024
CHUNK = 64
HEAD_D = 64
MIX_HEADS = 4
MIX_W = MIX_HEADS * HEAD_D
GLA_GATE_RANK = 16
GLA_TAU = 16.0
MLSTM_CONV = 4
MLA_HEADS = 8
MLA_NOPE = 64
MLA_ROPE = 32
MLA_V = 64
MLA_Q_RANK = 256
MLA_KV_RANK = 128
MLA_QK_PAD = 128
ROPE_BASE = 10000.0
LOG2_E = 1.4426950408889634
MLA_HEAD_PAIR = 2
XA_HEADS = 4
XA_DH = D_MODEL // XA_HEADS
N_GROUPS = 4
EXPERTS_PER_GROUP = 8
N_EXPERTS = N_GROUPS * EXPERTS_PER_GROUP
D_EXPERT = 256
DEPTH = 2
ALPHA = (2 * DEPTH) ** 0.25
LN_EPS = 1e-5
LANES = 128
SUBLANES = 8
MIXER_UNROLL = 2
SEG_W = 1152
MLA_SEG_W = 640
GLA_LEVELS = (32, 16, 8, 4, 2, 1)
VMEM_LIMIT = 56 * 1024 * 1024

_IN_SIZES = (256, 256, 256, 256, GLA_GATE_RANK, 512, 256, 256, 4, 4, MLA_Q_RANK, MLA_KV_RANK, MLA_ROPE)
_IN_OFF = np.concatenate([[0], np.cumsum(_IN_SIZES)]).tolist()


def _params(sem):
    return pltpu.CompilerParams(dimension_semantics=sem, vmem_limit_bytes=VMEM_LIMIT)


def _full(shape):
    return pl.BlockSpec(shape, lambda *_: (0,) * len(shape))


def _layer_norm(x, g, b):
    mu = jnp.mean(x, -1, keepdims=True)
    xc = x - mu
    var = jnp.mean(xc * xc, -1, keepdims=True)
    return xc * lax.rsqrt(var + LN_EPS) * g + b


def _log_sigmoid(z):
    return jnp.minimum(z, 0.0) - jnp.log1p(jnp.exp(-jnp.abs(z)))


def _dot_nt(a, b):
    return lax.dot_general(a, b, (((1,), (1,)), ((), ())), preferred_element_type=F32)


def _dot_tn(a, b):
    return lax.dot_general(a, b, (((0,), (0,)), ((), ())), preferred_element_type=F32)


def _dot(a, b):
    return jnp.dot(a, b, preferred_element_type=F32)


def _head_norm(o, g):
    mu = jnp.mean(o, -1, keepdims=True)
    oc = o - mu
    var = jnp.mean(oc * oc, -1, keepdims=True)
    return oc * lax.rsqrt(var + LN_EPS) * g


def _in_proj_kernel(x_ref, wg_ref, wm_ref, wc_ref, wift_ref, yg_ref, ym_ref, yc_ref, yift_ref):
    xb = x_ref[...].astype(BF16)
    yg_ref[...] = _dot(xb, wg_ref[...])
    ym_ref[...] = _dot(xb, wm_ref[...])
    yc_ref[...] = _dot(xb, wc_ref[...])
    yift_ref[...] = _dot_nt(wift_ref[...], xb)


def _in_proj(x2d, wg, wm, wc, wift, tm):
    n = x2d.shape[0]
    row = lambda w: pl.BlockSpec((tm, w), lambda i: (i, 0))
    return pl.pallas_call(
        _in_proj_kernel,
        grid=(n // tm,),
        in_specs=[row(D_MODEL), _full(wg.shape), _full(wm.shape), _full(wc.shape), _full(wift.shape)],
        out_specs=[row(SEG_W), row(SEG_W), row(MLA_SEG_W), pl.BlockSpec((8, tm), lambda i: (0, i))],
        out_shape=[jax.ShapeDtypeStruct((n, SEG_W), F32), jax.ShapeDtypeStruct((n, SEG_W), F32),
                   jax.ShapeDtypeStruct((n, MLA_SEG_W), F32), jax.ShapeDtypeStruct((8, n), F32)],
        compiler_params=_params(("parallel",)),
        name="in_proj",
    )(x2d, wg, wm, wc, wift)


def _split3(x):
    hi = x.astype(BF16)
    r1 = x - hi.astype(F32)
    mid = r1.astype(BF16)
    lo = (r1 - mid.astype(F32)).astype(BF16)
    return hi, mid, lo


def _cumsum_rows(tri, x):
    hi, mid, lo = _split3(x)
    return _dot(tri, hi) + _dot(tri, mid) + _dot(tri, lo)


def _gla_constants():
    t = np.arange(CHUNK)
    n_lv = len(GLA_LEVELS)
    masks = np.zeros((n_lv + 1, CHUNK, CHUNK), np.float32)
    right = np.zeros((n_lv, CHUNK, 1), np.float32)
    for li, n in enumerate(GLA_LEVELS):
        blk = t // (2 * n)
        is_right = (t % (2 * n)) >= n
        masks[li] = ((blk[:, None] == blk[None, :]) & is_right[:, None] & ~is_right[None, :])
        right[li, :, 0] = is_right
    masks[n_lv] = np.eye(CHUNK)
    return masks, right


def _gla_level_exponents(b, log_a, right_ref):
    row = lax.broadcasted_iota(jnp.int32, (CHUNK, 1), 0)
    exps = []
    for li, n in enumerate(GLA_LEVELS):
        if n >= SUBLANES // 2:
            per_blk = max(2 * n // SUBLANES, 1)
            b4 = b.reshape(CHUNK // (per_blk * SUBLANES), per_blk, SUBLANES, MIX_W)
            r_reg, r_sub = divmod(n - 1, SUBLANES)
            src = b4[:, r_reg:r_reg + 1, r_sub:r_sub + 1, :]
            b_r = jnp.broadcast_to(src, b4.shape).reshape(CHUNK, MIX_W)
            exps.append(jnp.where(right_ref[li] > 0.0, b - b_r, b_r - b))
        elif n == 2:
            pos = row % 4
            nxt = pltpu.roll(log_a, CHUNK - 1, 0)
            prv = pltpu.roll(log_a, 1, 0)
            exps.append(jnp.where(pos == 0, nxt, jnp.where(pos == 1, 0.0,
                                                            jnp.where(pos == 2, log_a, log_a + prv))))
        else:
            exps.append(jnp.where(right_ref[li] > 0.0, log_a, 0.0))
    return exps


def _gla_kernel(y_ref, wa2_ref, ba_ref, g_ref, tri_ref, mask_ref, right_ref, o_ref, st_ref, *, ts, unroll):
    n_lv = len(GLA_LEVELS)

    @pl.when(pl.program_id(1) == 0)
    def _():
        st_ref[...] = jnp.zeros_like(st_ref)

    def chunk(c, carry):
        rows = pl.ds(pl.multiple_of(c * CHUNK, CHUNK), CHUNK)
        q = y_ref[rows, 0:256] * (HEAD_D ** -0.5)
        k = y_ref[rows, 256:512]
        v = y_ref[rows, 512:768].astype(BF16)
        r_gate = y_ref[rows, 768:1024]
        a_lr = y_ref[rows, 1024:1152]
        z = _dot(a_lr.astype(BF16), wa2_ref[...]) + ba_ref[...]
        log_a = _log_sigmoid(z) * (1.0 / GLA_TAU)
        b = _cumsum_rows(tri_ref[...], log_a)
        b_end = b[CHUNK - 1:CHUNK, :]
        scores = [jnp.zeros((CHUNK, CHUNK), F32) for _ in range(MIX_HEADS)]
        for li, e in enumerate(_gla_level_exponents(b, log_a, right_ref)):
            x = (jnp.where(right_ref[li] > 0.0, q, k) * jnp.exp(e)).astype(BF16)
            for h in range(MIX_HEADS):
                xh = x[:, h * HEAD_D:(h + 1) * HEAD_D]
                scores[h] = scores[h] + _dot_nt(xh, xh) * mask_ref[li]
        qb = q.astype(BF16)
        kb = k.astype(BF16)
        q_in = (q * jnp.exp(b)).astype(BF16)
        k_out = (k * jnp.exp(b_end - b)).astype(BF16)
        dec_end = jnp.exp(b_end)
        outs = []
        for h in range(MIX_HEADS):
            sl = slice(h * HEAD_D, (h + 1) * HEAD_D)
            a = scores[h] + _dot_nt(qb[:, sl], kb[:, sl]) * mask_ref[n_lv]
            st = st_ref[h]
            o = _dot(a.astype(BF16), v[:, sl]) + _dot_nt(q_in[:, sl], st.astype(BF16))
            st_ref[h] = st * dec_end[:, sl] + _dot_tn(v[:, sl], k_out[:, sl])
            outs.append(_head_norm(o, g_ref[:, sl]))
        o_all = jnp.concatenate(outs, axis=-1)
        o_ref[rows, :] = (o_all * (r_gate * jax.nn.sigmoid(r_gate))).astype(o_ref.dtype)
        return carry

    lax.fori_loop(0, ts // CHUNK, chunk, 0, unroll=unroll)


def _gla(yg, wa2, ba, g, batch, seq, ts):
    n = yg.shape[0]
    nt = seq // ts
    masks, right = _gla_constants()
    tri = jnp.asarray(np.tril(np.ones((CHUNK, CHUNK), np.float32)), BF16)
    return pl.pallas_call(
        functools.partial(_gla_kernel, ts=ts, unroll=MIXER_UNROLL),
        grid=(batch, nt),
        in_specs=[pl.BlockSpec((ts, SEG_W), lambda b, i: (b * nt + i, 0)),
                  _full(wa2.shape), _full(ba.shape), _full(g.shape),
                  _full(tri.shape), _full(masks.shape), _full(right.shape)],
        out_specs=pl.BlockSpec((ts, MIX_W), lambda b, i: (b * nt + i, 0)),
        out_shape=jax.ShapeDtypeStruct((n, MIX_W), BF16),
        scratch_shapes=[pltpu.VMEM((MIX_HEADS, HEAD_D, HEAD_D), F32)],
        compiler_params=_params(("parallel", "arbitrary")),
        name="gla_mixer",
    )(yg, wa2, ba, g, tri, jnp.asarray(masks), jnp.asarray(right))


def _mlstm_kernel(y_ref, gr_ref, cw_ref, bcol_ref, brow_ref, g_ref, tri_ref, o_ref,
                  xe_ref, qk_ref, c_ref, n_ref, m_ref, *, ts):
    first = pl.program_id(1) == 0

    @pl.when(first)
    def _():
        xe_ref[0:8, :] = jnp.zeros((8, 2 * MIX_W), F32)
        c_ref[...] = jnp.zeros_like(c_ref)
        n_ref[...] = jnp.zeros_like(n_ref)
        m_ref[...] = jnp.zeros_like(m_ref)

    @pl.when(jnp.logical_not(first))
    def _():
        xe_ref[0:8, :] = xe_ref[ts:ts + 8, :]

    xe_ref[8:ts + 8, :] = y_ref[:, 0:2 * MIX_W]
    conv = cw_ref[MLSTM_CONV - 1:MLSTM_CONV, :] * xe_ref[8:ts + 8, :]
    for j in range(MLSTM_CONV - 1):
        conv = conv + cw_ref[j:j + 1, :] * xe_ref[pl.ds(8 - (MLSTM_CONV - 1) + j, ts), :]
    qk_ref[...] = conv * jax.nn.sigmoid(conv)

    tri = tri_ref[...]
    t_idx = lax.broadcasted_iota(jnp.int32, (CHUNK, CHUNK), 0)
    s_idx = lax.broadcasted_iota(jnp.int32, (CHUNK, CHUNK), 1)
    causal = s_idx <= t_idx

    def chunk(c, carry):
        rows = pl.ds(pl.multiple_of(c * CHUNK, CHUNK), CHUNK)
        q = (qk_ref[rows, 0:MIX_W] * (HEAD_D ** -0.5)).astype(BF16)
        k_f = qk_ref[rows, MIX_W:2 * MIX_W]
        k = k_f.astype(BF16)
        v = y_ref[rows, 512:768].astype(BF16)
        o_gate = y_ref[rows, 768:1024]
        g_col = y_ref[rows, 1024:1152] + bcol_ref[...]
        fcum_col = _cumsum_rows(tri, _log_sigmoid(g_col))
        g_row = gr_ref[c] + brow_ref[...]
        fcum_row = sum(_dot_nt(part, tri) for part in _split3(_log_sigmoid(g_row)))
        outs = []
        for h in range(MIX_HEADS):
            sl = slice(h * HEAD_D, (h + 1) * HEAD_D)
            fc_c = fcum_col[:, MIX_HEADS + h:MIX_HEADS + h + 1]
            ig_c = g_col[:, h:h + 1]
            fc_r = fcum_row[MIX_HEADS + h:MIX_HEADS + h + 1, :]
            ig_r = g_row[h:h + 1, :]
            m_st = m_ref[h][0:1, 0:1]
            f_end = fc_c[CHUNK - 1:CHUNK, :]
            d_log = jnp.where(causal, fc_c - fc_r + ig_r, -jnp.inf)
            inter_log = fc_c + m_st
            m_t = jnp.maximum(inter_log, jnp.max(d_log, -1, keepdims=True))
            w_intra = jnp.exp(d_log - m_t) * _dot_nt(q[:, sl], k[:, sl])
            w_inter = jnp.exp(inter_log - m_t)
            c_st = c_ref[h]
            n_st = n_ref[h]
            num = w_inter * _dot_nt(q[:, sl], c_st.astype(BF16)) + _dot(w_intra.astype(BF16), v[:, sl])
            den = (w_inter * jnp.sum(q[:, sl].astype(F32) * n_st, -1, keepdims=True)
                   + jnp.sum(w_intra, -1, keepdims=True))
            hid = num / jnp.maximum(jnp.abs(den), jnp.exp(-m_t))
            end_c = f_end - fc_c + ig_c
            end_r = fc_r[:, CHUNK - 1:CHUNK] - fc_r + ig_r
            m_new = jnp.maximum(f_end + m_st, jnp.max(end_r, -1, keepdims=True))
            w_s = jnp.exp(end_c - m_new)
            dec = jnp.exp(f_end + m_st - m_new)
            k_w = k_f[:, sl] * w_s
            c_ref[h] = dec * c_st + _dot_tn(v[:, sl], k_w.astype(BF16))
            n_ref[h] = dec * n_st + jnp.sum(k_w, 0, keepdims=True)
            m_ref[h] = jnp.broadcast_to(m_new, m_ref.shape[1:])
            outs.append(_head_norm(hid, g_ref[:, sl]))
        h_all = jnp.concatenate(outs, axis=-1)
        o_ref[rows, :] = (h_all * jax.nn.sigmoid(o_gate)).astype(o_ref.dtype)
        return carry

    lax.fori_loop(0, ts // CHUNK, chunk, 0, unroll=MIXER_UNROLL)


def _mlstm(ym, gates_row, conv_w, b_col, b_row, g, batch, seq, ts):
    n = ym.shape[0]
    nt = seq // ts
    nck = ts // CHUNK
    tri = jnp.asarray(np.tril(np.ones((CHUNK, CHUNK), np.float32)), BF16)
    return pl.pallas_call(
        functools.partial(_mlstm_kernel, ts=ts),
        grid=(batch, nt),
        in_specs=[pl.BlockSpec((ts, SEG_W), lambda b, i: (b * nt + i, 0)),
                  pl.BlockSpec((nck, 8, CHUNK), lambda b, i: (b * nt + i, 0, 0)),
                  _full(conv_w.shape), _full(b_col.shape), _full(b_row.shape), _full(g.shape),
                  _full(tri.shape)],
        out_specs=pl.BlockSpec((ts, MIX_W), lambda b, i: (b * nt + i, 0)),
        out_shape=jax.ShapeDtypeStruct((n, MIX_W), BF16),
        scratch_shapes=[pltpu.VMEM((ts + 8, 2 * MIX_W), F32),
                        pltpu.VMEM((ts, 2 * MIX_W), F32),
                        pltpu.VMEM((MIX_HEADS, HEAD_D, HEAD_D), F32),
                        pltpu.VMEM((MIX_HEADS, 1, HEAD_D), F32),
                        pltpu.VMEM((MIX_HEADS, 8, LANES), F32)],
        compiler_params=_params(("parallel", "arbitrary")),
        name="mlstm_mixer",
    )(ym, gates_row, conv_w, b_col, b_row, g, tri)


def _rope_table_kernel(pos_ref, inv_ref, cos_ref, sin_ref):
    ang = pos_ref[...].astype(F32) * inv_ref[...]
    lane = lax.broadcasted_iota(jnp.int32, ang.shape, 1)
    rot = (lane >= MLA_NOPE) & (lane < MLA_NOPE + MLA_ROPE)
    first_half = lane < MLA_NOPE + MLA_ROPE // 2
    cos_ref[...] = jnp.where(lane < MLA_NOPE, 1.0, jnp.where(rot, jnp.cos(ang), 0.0))
    s = jnp.sin(ang)
    sin_ref[...] = jnp.where(rot, jnp.where(first_half, -s, s), 0.0)


def _rope_tables(pos_col, tm):
    n = pos_col.shape[0]
    half = MLA_ROPE // 2
    inv = ROPE_BASE ** (-np.arange(half, dtype=np.float32) / half)
    inv_row = np.zeros((1, LANES), np.float32)
    inv_row[0, MLA_NOPE:MLA_NOPE + half] = inv
    inv_row[0, MLA_NOPE + half:MLA_NOPE + MLA_ROPE] = inv
    return pl.pallas_call(
        _rope_table_kernel,
        grid=(n // tm,),
        in_specs=[pl.BlockSpec((tm, 1), lambda i: (i, 0)), _full((1, LANES))],
        out_specs=[pl.BlockSpec((tm, LANES), lambda i: (i, 0))] * 2,
        out_shape=[jax.ShapeDtypeStruct((n, LANES), F32)] * 2,
        compiler_params=_params(("parallel",)),
        name="rope_tables",
    )(pos_col, jnp.asarray(inv_row))


def _mla_prep_kernel(y_ref, cos_ref, sin_ref, gq_ref, gkv_ref, wqa_ref, wqb_ref, wkn_ref, wv_ref,
                     q_ref, k_ref, v_ref):
    def rms(x, g):
        return x * lax.rsqrt(jnp.mean(x * x, -1, keepdims=True) + LN_EPS) * g

    cos = cos_ref[...]
    sin = sin_ref[...]
    cq = rms(y_ref[:, 0:MLA_Q_RANK], gq_ref[...]).astype(BF16)
    ckv = rms(y_ref[:, MLA_Q_RANK:MLA_Q_RANK + MLA_KV_RANK], gkv_ref[...]).astype(BF16)
    k_rope = y_ref[:, 384:512] * cos + y_ref[:, 512:640] * sin
    qa = _dot(cq, wqa_ref[...])
    qb = _dot(cq, wqb_ref[...])
    kn = _dot(ckv, wkn_ref[...])
    scale = (MLA_NOPE + MLA_ROPE) ** -0.5 * LOG2_E
    for h in range(MLA_HEADS):
        sl = slice(h * MLA_QK_PAD, (h + 1) * MLA_QK_PAD)
        q_ref[:, sl] = ((qa[:, sl] * cos + qb[:, sl] * sin) * scale).astype(q_ref.dtype)
        k_ref[:, sl] = (kn[:, sl] + k_rope).astype(k_ref.dtype)
    v_ref[...] = _dot(ckv, wv_ref[...]).astype(v_ref.dtype)


def _mla_prep(yc, cos_t, sin_t, gq, gkv, wqa, wqb, wkn, wv, tm):
    n = yc.shape[0]
    row = lambda w: pl.BlockSpec((tm, w), lambda i: (i, 0))
    qk_w = MLA_HEADS * MLA_QK_PAD
    return pl.pallas_call(
        _mla_prep_kernel,
        grid=(n // tm,),
        in_specs=[row(MLA_SEG_W), row(LANES), row(LANES), _full(gq.shape), _full(gkv.shape),
                  _full(wqa.shape), _full(wqb.shape), _full(wkn.shape), _full(wv.shape)],
        out_specs=[row(qk_w), row(qk_w), row(MLA_HEADS * MLA_V)],
        out_shape=[jax.ShapeDtypeStruct((n, qk_w), BF16), jax.ShapeDtypeStruct((n, qk_w), BF16),
                   jax.ShapeDtypeStruct((n, MLA_HEADS * MLA_V), BF16)],
        compiler_params=_params(("parallel",)),
        name="mla_prep",
    )(yc, cos_t, sin_t, gq, gkv, wqa, wqb, wkn, wv)


def _mla_attn_kernel(q_ref, k_ref, v_ref, o_ref, *, seq, tq):
    t_chunk = lax.broadcasted_iota(jnp.int32, (tq, tq), 0) // CHUNK
    s_chunk = lax.broadcasted_iota(jnp.int32, (tq, tq), 1) // CHUNK
    diag_mask = s_chunk <= t_chunk
    for hh in range(MLA_HEAD_PAIR):
        ql = slice(hh * MLA_QK_PAD, (hh + 1) * MLA_QK_PAD)
        vl = slice(hh * MLA_V, (hh + 1) * MLA_V)
        for i in range(seq // tq):
            rows = slice(i * tq, (i + 1) * tq)
            q = q_ref[0, rows, ql]
            s_d = jnp.where(diag_mask, _dot_nt(q, k_ref[0, rows, ql]), -jnp.inf)
            m = jnp.max(s_d, -1, keepdims=True)
            if i > 0:
                s_o = _dot_nt(q, k_ref[0, 0:i * tq, ql])
                m = jnp.maximum(m, jnp.max(s_o, -1, keepdims=True))
            p_d = jnp.exp2(s_d - m)
            l = jnp.sum(p_d, -1, keepdims=True)
            o = _dot(p_d.astype(BF16), v_ref[0, rows, vl])
            if i > 0:
                p_o = jnp.exp2(s_o - m)
                l = l + jnp.sum(p_o, -1, keepdims=True)
                o = o + _dot(p_o.astype(BF16), v_ref[0, 0:i * tq, vl])
            o_ref[0, rows, vl] = (o / l).astype(o_ref.dtype)


def _mla_attn(q, k, v, batch, seq, tq):
    n = q.shape[0]
    qk_w = MLA_HEADS * MLA_QK_PAD
    v_w = MLA_HEADS * MLA_V
    pair = lambda w: pl.BlockSpec((1, seq, MLA_HEAD_PAIR * w), lambda b, h: (b, 0, h))
    out = pl.pallas_call(
        functools.partial(_mla_attn_kernel, seq=seq, tq=tq),
        grid=(batch, MLA_HEADS // MLA_HEAD_PAIR),
        in_specs=[pair(MLA_QK_PAD), pair(MLA_QK_PAD), pair(MLA_V)],
        out_specs=pair(MLA_V),
        out_shape=jax.ShapeDtypeStruct((batch, seq, v_w), BF16),
        compiler_params=_params(("parallel", "parallel")),
        name="mla_attention",
    )(q.reshape(batch, seq, qk_w), k.reshape(batch, seq, qk_w), v.reshape(batch, seq, v_w))
    return out.reshape(n, v_w)


def _out_proj_kernel(og_ref, om_ref, oc_ref, x_ref, wg_ref, wm_ref, wc_ref, g_ref, b_ref, o_ref):
    mix = _dot(og_ref[...], wg_ref[...]) + _dot(om_ref[...], wm_ref[...]) + _dot(oc_ref[...], wc_ref[...])
    o_ref[...] = _layer_norm(ALPHA * x_ref[...] + mix, g_ref[...], b_ref[...])


def _out_proj(og, om, oc, x2d, wg, wm, wc, g, b, tm):
    n = x2d.shape[0]
    row = lambda w: pl.BlockSpec((tm, w), lambda i: (i, 0))
    return pl.pallas_call(
        _out_proj_kernel,
        grid=(n // tm,),
        in_specs=[row(MIX_W), row(MIX_W), row(MLA_HEADS * MLA_V), row(D_MODEL),
                  _full(wg.shape), _full(wm.shape), _full(wc.shape), _full(g.shape), _full(b.shape)],
        out_specs=row(D_MODEL),
        out_shape=jax.ShapeDtypeStruct((n, D_MODEL), F32),
        compiler_params=_params(("parallel",)),
        name="out_proj_ln1",
    )(og, om, oc, x2d, wg, wm, wc, g, b)


def _xa_kv_kernel(mem_ref, w_ref, k_ref, v_ref):
    kv = _dot(mem_ref[...].astype(BF16), w_ref[...])
    k_ref[...] = kv[:, 0:D_MODEL].astype(k_ref.dtype)
    v_ref[...] = kv[:, D_MODEL:2 * D_MODEL].astype(v_ref.dtype)


def _xa_kv(mem2d, w_kv, mem_len):
    n = mem2d.shape[0]
    row = pl.BlockSpec((mem_len, D_MODEL), lambda i: (i, 0))
    return pl.pallas_call(
        _xa_kv_kernel,
        grid=(n // mem_len,),
        in_specs=[row, _full(w_kv.shape)],
        out_specs=[row, row],
        out_shape=[jax.ShapeDtypeStruct((n, D_MODEL), BF16)] * 2,
        compiler_params=_params(("parallel",)),
        name="xattn_kv",
    )(mem2d, w_kv)


def _xattn_kernel(x_ref, k_ref, v_ref, wq_ref, wo_ref, g_ref, b_ref, o_ref):
    x = x_ref[...]
    q = (_dot(x.astype(BF16), wq_ref[...]) * (XA_DH ** -0.5)).astype(BF16)
    out = jnp.zeros(x.shape, F32)
    for h in range(XA_HEADS):
        sl = slice(h * XA_DH, (h + 1) * XA_DH)
        s = _dot_nt(q[:, sl], k_ref[:, sl])
        p = jnp.exp(s - jnp.max(s, -1, keepdims=True))
        p = p / jnp.sum(p, -1, keepdims=True)
        o = _dot(p.astype(BF16), v_ref[:, sl])
        out = out + _dot(o.astype(BF16), wo_ref[sl, :])
    o_ref[...] = _layer_norm(ALPHA * x + out, g_ref[...], b_ref[...])


def _xattn(x1, xk, xv, wq, wo, g, b, batch, seq, mem_len, tm):
    n = x1.shape[0]
    nt = seq // tm
    row = pl.BlockSpec((tm, D_MODEL), lambda bi, i: (bi * nt + i, 0))
    mem = pl.BlockSpec((mem_len, D_MODEL), lambda bi, i: (bi, 0))
    return pl.pallas_call(
        _xattn_kernel,
        grid=(batch, nt),
        in_specs=[row, mem, mem, _full(wq.shape), _full(wo.shape), _full(g.shape), _full(b.shape)],
        out_specs=row,
        out_shape=jax.ShapeDtypeStruct((n, D_MODEL), F32),
        compiler_params=_params(("parallel", "parallel")),
        name="xattn_ln2",
    )(x1, xk, xv, wq, wo, g, b)


def _route_kernel(x_ref, w_ref, b_ref, gate_ref):
    logits = jnp.dot(x_ref[...], w_ref[...], precision=HIGHEST, preferred_element_type=F32) + b_ref[...]
    lane = lax.broadcasted_iota(jnp.int32, logits.shape, 1).astype(F32)
    is_group = (lane >= N_EXPERTS) & (lane < N_EXPERTS + N_GROUPS)
    g_max = jnp.max(jnp.where(is_group, logits, -jnp.inf), -1, keepdims=True)
    g_sum = jnp.sum(jnp.where(is_group, jnp.exp(logits - g_max), 0.0), -1, keepdims=True)
    g_p = 1.0 / g_sum
    g_idx = jnp.min(jnp.where(is_group & (logits == g_max), lane - N_EXPERTS, float(LANES)), -1, keepdims=True)
    in_group = (lane < N_EXPERTS) & (jnp.floor(lane * (1.0 / EXPERTS_PER_GROUP)) == g_idx)
    e_max = jnp.max(jnp.where(in_group, logits, -jnp.inf), -1, keepdims=True)
    e_exp = jnp.where(in_group, jnp.exp(logits - e_max), 0.0)
    prob = e_exp / jnp.sum(e_exp, -1, keepdims=True)
    cand = jnp.where(in_group, prob, -1.0)
    p1 = jnp.max(cand, -1, keepdims=True)
    i1 = jnp.min(jnp.where(cand == p1, lane, float(LANES)), -1, keepdims=True)
    cand2 = jnp.where(lane == i1, -1.0, cand)
    p2 = jnp.max(cand2, -1, keepdims=True)
    i2 = jnp.min(jnp.where(cand2 == p2, lane, float(LANES)), -1, keepdims=True)
    p_sum = p1 + p2
    gate_ref[...] = (jnp.where(lane == i1, g_p * (p1 / p_sum), 0.0)
                     + jnp.where(lane == i2, g_p * (p2 / p_sum), 0.0))


def _route(x2, w_route, b_route, tm):
    n = x2.shape[0]
    return pl.pallas_call(
        _route_kernel,
        grid=(n // tm,),
        in_specs=[pl.BlockSpec((tm, D_MODEL), lambda i: (i, 0)), _full(w_route.shape), _full(b_route.shape)],
        out_specs=pl.BlockSpec((tm, LANES), lambda i: (i, 0)),
        out_shape=jax.ShapeDtypeStruct((n, LANES), F32),
        compiler_params=_params(("parallel",)),
        name="moe_route",
    )(x2, w_route, b_route)


def _moe_kernel(x_ref, gate_ref, wg_ref, wu_ref, wd_ref, g_ref, b_ref, o_ref, xb_ref, acc_ref):
    e = pl.program_id(1)

    @pl.when(e == 0)
    def _():
        xb_ref[...] = x_ref[...].astype(BF16)
        acc_ref[...] = jnp.zeros_like(acc_ref)

    xb = xb_ref[...]
    lane = lax.broadcasted_iota(jnp.int32, gate_ref.shape, 1)
    w_tok = jnp.sum(jnp.where(lane == e, gate_ref[...], 0.0), -1, keepdims=True)
    hg = _dot(xb, wg_ref[0].astype(BF16))
    hu = _dot(xb, wu_ref[0].astype(BF16))
    hid = hg * jax.nn.sigmoid(hg) * hu
    acc_ref[...] += _dot(hid.astype(BF16), wd_ref[0].astype(BF16)) * w_tok

    @pl.when(e == pl.num_programs(1) - 1)
    def _():
        o_ref[...] = _layer_norm(ALPHA * x_ref[...] + acc_ref[...], g_ref[...], b_ref[...])


def _moe(x2, gate, w_gate, w_up, w_down, g, b, tm):
    n = x2.shape[0]
    row = pl.BlockSpec((tm, D_MODEL), lambda i, e: (i, 0))
    return pl.pallas_call(
        _moe_kernel,
        grid=(n // tm, N_EXPERTS),
        in_specs=[row, pl.BlockSpec((tm, LANES), lambda i, e: (i, 0)),
                  pl.BlockSpec((1, D_MODEL, D_EXPERT), lambda i, e: (e, 0, 0)),
                  pl.BlockSpec((1, D_MODEL, D_EXPERT), lambda i, e: (e, 0, 0)),
                  pl.BlockSpec((1, D_EXPERT, D_MODEL), lambda i, e: (e, 0, 0)),
                  _full(g.shape), _full(b.shape)],
        out_specs=row,
        out_shape=jax.ShapeDtypeStruct((n, D_MODEL), F32),
        scratch_shapes=[pltpu.VMEM((tm, D_MODEL), BF16), pltpu.VMEM((tm, D_MODEL), F32)],
        compiler_params=_params(("parallel", "arbitrary")),
        name="moe_experts_ln3",
    )(x2, gate, w_gate, w_up, w_down, g, b)


def _pad_cols(w, width):
    return jnp.pad(w, ((0, 0), (0, width - w.shape[1])))


def _layer_weights(w_in, w_out, gla_w_a2, ml_b_i, ml_b_f, mla_w_uq, mla_w_ukv, moe_w_group, moe_b_group,
                   moe_w_router, moe_b_router):
    o = _IN_OFF
    half = MLA_ROPE // 2
    wg = _pad_cols(w_in[:, o[0]:o[5]], SEG_W).astype(BF16)
    wm = _pad_cols(w_in[:, o[5]:o[10]], SEG_W).astype(BF16)
    kr = w_in[:, o[12]:o[13]]
    zeros = lambda w: jnp.zeros((D_MODEL, w), F32)
    kra = jnp.concatenate([zeros(MLA_NOPE), kr, zeros(LANES - MLA_NOPE - MLA_ROPE)], 1)
    krb = jnp.concatenate([zeros(MLA_NOPE), kr[:, half:], kr[:, :half], zeros(LANES - MLA_NOPE - MLA_ROPE)], 1)
    wc = jnp.concatenate([w_in[:, o[10]:o[12]], kra, krb], 1).astype(BF16)
    wift = w_in[:, o[8]:o[10]].T.astype(BF16)
    wa2 = jnp.pad(gla_w_a2, ((0, LANES - GLA_GATE_RANK), (0, 0))).astype(BF16)
    b_gate = jnp.concatenate([ml_b_i, ml_b_f])
    b_col = jnp.pad(b_gate, (0, LANES - 2 * MIX_HEADS)).reshape(1, LANES)
    b_row = b_gate.reshape(2 * MIX_HEADS, 1)
    uq = mla_w_uq.reshape(MLA_Q_RANK, MLA_HEADS, MLA_NOPE + MLA_ROPE)
    zq = jnp.zeros((MLA_Q_RANK, MLA_HEADS, LANES - MLA_NOPE - MLA_ROPE), F32)
    wqa = jnp.concatenate([uq, zq], -1).reshape(MLA_Q_RANK, -1).astype(BF16)
    wqb = jnp.concatenate([jnp.zeros((MLA_Q_RANK, MLA_HEADS, MLA_NOPE), F32), uq[..., MLA_NOPE + half:],
                           uq[..., MLA_NOPE:MLA_NOPE + half], zq], -1).reshape(MLA_Q_RANK, -1).astype(BF16)
    ukv = mla_w_ukv.reshape(MLA_KV_RANK, MLA_HEADS, MLA_NOPE + MLA_V)
    wkn = jnp.concatenate([ukv[..., :MLA_NOPE], jnp.zeros((MLA_KV_RANK, MLA_HEADS, LANES - MLA_NOPE), F32)],
                          -1).reshape(MLA_KV_RANK, -1).astype(BF16)
    wv = ukv[..., MLA_NOPE:].reshape(MLA_KV_RANK, -1).astype(BF16)
    wo = w_out.astype(BF16)
    w_route = _pad_cols(jnp.concatenate([moe_w_router, moe_w_group], 1), LANES)
    b_route = jnp.pad(jnp.concatenate([moe_b_router, moe_b_group]), (0, LANES - N_EXPERTS - N_GROUPS)).reshape(1, LANES)
    return dict(wg=wg, wm=wm, wc=wc, wift=wift, wa2=wa2, b_col=b_col, b_row=b_row, wqa=wqa, wqb=wqb, wkn=wkn,
                wv=wv, wo_g=wo[0:MIX_W], wo_m=wo[MIX_W:2 * MIX_W], wo_c=wo[2 * MIX_W:], w_route=w_route,
                b_route=b_route)


def _tile(total, want):
    t = min(total, want)
    assert total % t == 0
    return t


def kernel(x, mem, positions, w_in, w_out, gla_w_a2, gla_b_a, gla_norm_g, ml_conv_w, ml_b_i, ml_b_f, ml_norm_g, mla_q_norm_g, mla_w_uq, mla_kv_norm_g, mla_w_ukv, xa_w_q, xa_w_kv, xa_w_o, moe_w_group, moe_b_group, moe_w_router, moe_b_router, moe_w_gate, moe_w_up, moe_w_down, ln1_g, ln1_b, ln2_g, ln2_b, ln3_g, ln3_b):
    batch, seq, _ = x.shape
    mem_len = mem.shape[1]
    n = batch * seq
    depth = w_in.shape[0]
    assert seq % CHUNK == 0
    tm = _tile(n, 512)
    ts = _tile(seq, 512)
    tq = _tile(seq, 256)
    t_moe = _tile(n, 1024)
    row = lambda a: a.reshape(1, -1)

    cos_t, sin_t = _rope_tables(positions.reshape(n, 1), _tile(n, 2048))
    mem2d = mem.reshape(batch * mem_len, D_MODEL)
    h = x.reshape(n, D_MODEL)
    for l in range(depth):
        w = _layer_weights(w_in[l], w_out[l], gla_w_a2[l], ml_b_i[l], ml_b_f[l], mla_w_uq[l], mla_w_ukv[l],
                           moe_w_group[l], moe_b_group[l], moe_w_router[l], moe_b_router[l])
        yg, ym, yc, yift = _in_proj(h, w["wg"], w["wm"], w["wc"], w["wift"], tm)
        gates_row = yift.reshape(2 * MIX_HEADS, n // CHUNK, CHUNK).transpose(1, 0, 2)
        og = _gla(yg, w["wa2"], row(gla_b_a[l]), row(gla_norm_g[l]), batch, seq, ts)
        om = _mlstm(ym, gates_row, ml_conv_w[l], w["b_col"], w["b_row"], row(ml_norm_g[l]), batch, seq, ts)
        q, k, v = _mla_prep(yc, cos_t, sin_t, row(mla_q_norm_g[l]), row(mla_kv_norm_g[l]),
                            w["wqa"], w["wqb"], w["wkn"], w["wv"], tm)
        oc = _mla_attn(q, k, v, batch, seq, tq)
        x1 = _out_proj(og, om, oc, h, w["wo_g"], w["wo_m"], w["wo_c"], row(ln1_g[l]), row(ln1_b[l]), tm)
        xk, xv = _xa_kv(mem2d, xa_w_kv[l].astype(BF16), mem_len)
        x2 = _xattn(x1, xk, xv, xa_w_q[l].astype(BF16), xa_w_o[l].astype(BF16), row(ln2_g[l]), row(ln2_b[l]),
                    batch, seq, mem_len, tm)
        gate = _route(x2, w["w_route"], w["b_route"], tm)
        h = _moe(x2, gate, moe_w_gate[l], moe_w_up[l], moe_w_down[l], row(ln3_g[l]), row(ln3_b[l]), t_moe)
    return h.reshape(batch, seq, D_MODEL)
```

```python
import functools

import numpy as np
import jax
import jax.numpy as jnp
from jax import lax
from jax.experimental import pallas as pl
from jax.experimental.pallas import tpu as pltpu

F32 = jnp.float32
BF16 = jnp.bfloat16
HIGHEST = lax.Precision.HIGHEST

D_MODEL = 1024
CHUNK = 64
HEAD_D = 64
MIX_HEADS = 4
MIX_W = MIX_HEADS * HEAD_D
GLA_GATE_RANK = 16
GLA_TAU = 16.0
MLSTM_CONV = 4
MLA_HEADS = 8
MLA_NOPE = 64
MLA_ROPE = 32
MLA_V = 64
MLA_Q_RANK = 256
MLA_KV_RANK = 128
MLA_QK_PAD = 128
ROPE_BASE = 10000.0
LOG2_E = 1.4426950408889634
MLA_HEAD_PAIR = 2
XA_HEADS = 4
XA_DH = D_MODEL // XA_HEADS
N_GROUPS = 4
EXPERTS_PER_GROUP = 8
N_EXPERTS = N_GROUPS * EXPERTS_PER_GROUP
D_EXPERT = 256
DEPTH = 2
ALPHA = (2 * DEPTH) ** 0.25
LN_EPS = 1e-5
LANES = 128
SUBLANES = 8
MIXER_UNROLL = 2
SEG_W = 1152
MLA_SEG_W = 640
XG_W = D_MODEL + LANES
DMA_UNROLL = 8
GLA_LEVELS = (32, 16, 8, 4, 2, 1)
VMEM_LIMIT = 56 * 1024 * 1024

_IN_SIZES = (256, 256, 256, 256, GLA_GATE_RANK, 512, 256, 256, 4, 4, MLA_Q_RANK, MLA_KV_RANK, MLA_ROPE)
_IN_OFF = np.concatenate([[0], np.cumsum(_IN_SIZES)]).tolist()


def _params(sem):
    return pltpu.CompilerParams(dimension_semantics=sem, vmem_limit_bytes=VMEM_LIMIT)


def _full(shape):
    return pl.BlockSpec(shape, lambda *_: (0,) * len(shape))


def _layer_norm(x, g, b):
    mu = jnp.mean(x, -1, keepdims=True)
    xc = x - mu
    var = jnp.mean(xc * xc, -1, keepdims=True)
    return xc * lax.rsqrt(var + LN_EPS) * g + b


def _log_sigmoid(z):
    return jnp.minimum(z, 0.0) - jnp.log1p(jnp.exp(-jnp.abs(z)))


def _dot_nt(a, b):
    return lax.dot_general(a, b, (((1,), (1,)), ((), ())), preferred_element_type=F32)


def _dot_tn(a, b):
    return lax.dot_general(a, b, (((0,), (0,)), ((), ())), preferred_element_type=F32)


def _dot(a, b):
    return jnp.dot(a, b, preferred_element_type=F32)


def _head_norm(o, g):
    mu = jnp.mean(o, -1, keepdims=True)
    oc = o - mu
    var = jnp.mean(oc * oc, -1, keepdims=True)
    return oc * lax.rsqrt(var + LN_EPS) * g


def _in_proj_kernel(x_ref, wg_ref, wm_ref, wc_ref, wift_ref, yg_ref, ym_ref, yc_ref, yift_ref):
    xb = x_ref[...].astype(BF16)
    yg_ref[...] = _dot(xb, wg_ref[...])
    ym_ref[...] = _dot(xb, wm_ref[...])
    yc_ref[...] = _dot(xb, wc_ref[...])
    yift_ref[...] = _dot_nt(wift_ref[...], xb)


def _in_proj(x2d, wg, wm, wc, wift, tm):
    n = x2d.shape[0]
    row = lambda w: pl.BlockSpec((tm, w), lambda i: (i, 0))
    return pl.pallas_call(
        _in_proj_kernel,
        grid=(n // tm,),
        in_specs=[row(D_MODEL), _full(wg.shape), _full(wm.shape), _full(wc.shape), _full(wift.shape)],
        out_specs=[row(SEG_W), row(SEG_W), row(MLA_SEG_W), pl.BlockSpec((8, tm), lambda i: (0, i))],
        out_shape=[jax.ShapeDtypeStruct((n, SEG_W), F32), jax.ShapeDtypeStruct((n, SEG_W), F32),
                   jax.ShapeDtypeStruct((n, MLA_SEG_W), F32), jax.ShapeDtypeStruct((8, n), F32)],
        compiler_params=_params(("parallel",)),
        name="in_proj",
    )(x2d, wg, wm, wc, wift)


def _split3(x):
    hi = x.astype(BF16)
    r1 = x - hi.astype(F32)
    mid = r1.astype(BF16)
    lo = (r1 - mid.astype(F32)).astype(BF16)
    return hi, mid, lo


def _cumsum_rows(tri, x):
    hi, mid, lo = _split3(x)
    return _dot(tri, hi) + _dot(tri, mid) + _dot(tri, lo)


def _gla_constants():
    t = np.arange(CHUNK)
    n_lv = len(GLA_LEVELS)
    masks = np.zeros((n_lv + 1, CHUNK, CHUNK), np.float32)
    right = np.zeros((n_lv, CHUNK, 1), np.float32)
    for li, n in enumerate(GLA_LEVELS):
        blk = t // (2 * n)
        is_right = (t % (2 * n)) >= n
        masks[li] = ((blk[:, None] == blk[None, :]) & is_right[:, None] & ~is_right[None, :])
        right[li, :, 0] = is_right
    masks[n_lv] = np.eye(CHUNK)
    return masks, right


def _gla_level_exponents(b, log_a, right_ref):
    row = lax.broadcasted_iota(jnp.int32, (CHUNK, 1), 0)
    exps = []
    for li, n in enumerate(GLA_LEVELS):
        if n >= SUBLANES // 2:
            per_blk = max(2 * n // SUBLANES, 1)
            b4 = b.reshape(CHUNK // (per_blk * SUBLANES), per_blk, SUBLANES, MIX_W)
            r_reg, r_sub = divmod(n - 1, SUBLANES)
            src = b4[:, r_reg:r_reg + 1, r_sub:r_sub + 1, :]
            b_r = jnp.broadcast_to(src, b4.shape).reshape(CHUNK, MIX_W)
            exps.append(jnp.where(right_ref[li] > 0.0, b - b_r, b_r - b))
        elif n == 2:
            pos = row % 4
            nxt = pltpu.roll(log_a, CHUNK - 1, 0)
            prv = pltpu.roll(log_a, 1, 0)
            exps.append(jnp.where(pos == 0, nxt, jnp.where(pos == 1, 0.0,
                                                            jnp.where(pos == 2, log_a, log_a + prv))))
        else:
            exps.append(jnp.where(right_ref[li] > 0.0, log_a, 0.0))
    return exps


def _gla_kernel(y_ref, wa2_ref, ba_ref, g_ref, tri_ref, mask_ref, right_ref, o_ref, st_ref, *, ts, unroll):
    n_lv = len(GLA_LEVELS)

    @pl.when(pl.program_id(1) == 0)
    def _():
        st_ref[...] = jnp.zeros_like(st_ref)

    def chunk(c, carry):
        rows = pl.ds(pl.multiple_of(c * CHUNK, CHUNK), CHUNK)
        q = y_ref[rows, 0:256] * (HEAD_D ** -0.5)
        k = y_ref[rows, 256:512]
        v = y_ref[rows, 512:768].astype(BF16)
        r_gate = y_ref[rows, 768:1024]
        a_lr = y_ref[rows, 1024:1152]
        z = _dot(a_lr.astype(BF16), wa2_ref[...]) + ba_ref[...]
        log_a = _log_sigmoid(z) * (1.0 / GLA_TAU)
        b = _cumsum_rows(tri_ref[...], log_a)
        b_end = b[CHUNK - 1:CHUNK, :]
        scores = [jnp.zeros((CHUNK, CHUNK), F32) for _ in range(MIX_HEADS)]
        for li, e in enumerate(_gla_level_exponents(b, log_a, right_ref)):
            x = (jnp.where(right_ref[li] > 0.0, q, k) * jnp.exp(e)).astype(BF16)
            for h in range(MIX_HEADS):
                xh = x[:, h * HEAD_D:(h + 1) * HEAD_D]
                scores[h] = scores[h] + _dot_nt(xh, xh) * mask_ref[li]
        qb = q.astype(BF16)
        kb = k.astype(BF16)
        q_in = (q * jnp.exp(b)).astype(BF16)
        k_out = (k * jnp.exp(b_end - b)).astype(BF16)
        dec_end = jnp.exp(b_end)
        outs = []
        for h in range(MIX_HEADS):
            sl = slice(h * HEAD_D, (h + 1) * HEAD_D)
            a = scores[h] + _dot_nt(qb[:, sl], kb[:, sl]) * mask_ref[n_lv]
            st = st_ref[h]
            o = _dot(a.astype(BF16), v[:, sl]) + _dot_nt(q_in[:, sl], st.astype(BF16))
            st_ref[h] = st * dec_end[:, sl] + _dot_tn(v[:, sl], k_out[:, sl])
            outs.append(_head_norm(o, g_ref[:, sl]))
        o_all = jnp.concatenate(outs, axis=-1)
        o_ref[rows, :] = (o_all * (r_gate * jax.nn.sigmoid(r_gate))).astype(o_ref.dtype)
        return carry

    lax.fori_loop(0, ts // CHUNK, chunk, 0, unroll=unroll)


def _gla(yg, wa2, ba, g, batch, seq, ts):
    n = yg.shape[0]
    nt = seq // ts
    masks, right = _gla_constants()
    tri = jnp.asarray(np.tril(np.ones((CHUNK, CHUNK), np.float32)), BF16)
    return pl.pallas_call(
        functools.partial(_gla_kernel, ts=ts, unroll=MIXER_UNROLL),
        grid=(batch, nt),
        in_specs=[pl.BlockSpec((ts, SEG_W), lambda b, i: (b * nt + i, 0)),
                  _full(wa2.shape), _full(ba.shape), _full(g.shape),
                  _full(tri.shape), _full(masks.shape), _full(right.shape)],
        out_specs=pl.BlockSpec((ts, MIX_W), lambda b, i: (b * nt + i, 0)),
        out_shape=jax.ShapeDtypeStruct((n, MIX_W), BF16),
        scratch_shapes=[pltpu.VMEM((MIX_HEADS, HEAD_D, HEAD_D), F32)],
        compiler_params=_params(("parallel", "arbitrary")),
        name="gla_mixer",
    )(yg, wa2, ba, g, tri, jnp.asarray(masks), jnp.asarray(right))


def _mlstm_kernel(y_ref, gr_ref, cw_ref, bcol_ref, brow_ref, g_ref, tri_ref, o_ref,
                  xe_ref, qk_ref, c_ref, n_ref, m_ref, *, ts):
    first = pl.program_id(1) == 0

    @pl.when(first)
    def _():
        xe_ref[0:8, :] = jnp.zeros((8, 2 * MIX_W), F32)
        c_ref[...] = jnp.zeros_like(c_ref)
        n_ref[...] = jnp.zeros_like(n_ref)
        m_ref[...] = jnp.zeros_like(m_ref)

    @pl.when(jnp.logical_not(first))
    def _():
        xe_ref[0:8, :] = xe_ref[ts:ts + 8, :]

    xe_ref[8:ts + 8, :] = y_ref[:, 0:2 * MIX_W]
    conv = cw_ref[MLSTM_CONV - 1:MLSTM_CONV, :] * xe_ref[8:ts + 8, :]
    for j in range(MLSTM_CONV - 1):
        conv = conv + cw_ref[j:j + 1, :] * xe_ref[pl.ds(8 - (MLSTM_CONV - 1) + j, ts), :]
    qk_ref[...] = conv * jax.nn.sigmoid(conv)

    tri = tri_ref[...]
    t_idx = lax.broadcasted_iota(jnp.int32, (CHUNK, CHUNK), 0)
    s_idx = lax.broadcasted_iota(jnp.int32, (CHUNK, CHUNK), 1)
    causal = s_idx <= t_idx

    def chunk(c, carry):
        rows = pl.ds(pl.multiple_of(c * CHUNK, CHUNK), CHUNK)
        q = (qk_ref[rows, 0:MIX_W] * (HEAD_D ** -0.5)).astype(BF16)
        k_f = qk_ref[rows, MIX_W:2 * MIX_W]
        k = k_f.astype(BF16)
        v = y_ref[rows, 512:768].astype(BF16)
        o_gate = y_ref[rows, 768:1024]
        g_col = y_ref[rows, 1024:1152] + bcol_ref[...]
        fcum_col = _cumsum_rows(tri, _log_sigmoid(g_col))
        g_row = gr_ref[c] + brow_ref[...]
        fcum_row = sum(_dot_nt(part, tri) for part in _split3(_log_sigmoid(g_row)))
        outs = []
        for h in range(MIX_HEADS):
            sl = slice(h * HEAD_D, (h + 1) * HEAD_D)
            fc_c = fcum_col[:, MIX_HEADS + h:MIX_HEADS + h + 1]
            ig_c = g_col[:, h:h + 1]
            fc_r = fcum_row[MIX_HEADS + h:MIX_HEADS + h + 1, :]
            ig_r = g_row[h:h + 1, :]
            m_st = m_ref[h][0:1, 0:1]
            f_end = fc_c[CHUNK - 1:CHUNK, :]
            d_log = jnp.where(causal, fc_c - fc_r + ig_r, -jnp.inf)
            inter_log = fc_c + m_st
            m_t = jnp.maximum(inter_log, jnp.max(d_log, -1, keepdims=True))
            w_intra = jnp.exp(d_log - m_t) * _dot_nt(q[:, sl], k[:, sl])
            w_inter = jnp.exp(inter_log - m_t)
            c_st = c_ref[h]
            n_st = n_ref[h]
            num = w_inter * _dot_nt(q[:, sl], c_st.astype(BF16)) + _dot(w_intra.astype(BF16), v[:, sl])
            den = (w_inter * jnp.sum(q[:, sl].astype(F32) * n_st, -1, keepdims=True)
                   + jnp.sum(w_intra, -1, keepdims=True))
            hid = num / jnp.maximum(jnp.abs(den), jnp.exp(-m_t))
            end_c = f_end - fc_c + ig_c
            end_r = fc_r[:, CHUNK - 1:CHUNK] - fc_r + ig_r
            m_new = jnp.maximum(f_end + m_st, jnp.max(end_r, -1, keepdims=True))
            w_s = jnp.exp(end_c - m_new)
            dec = jnp.exp(f_end + m_st - m_new)
            k_w = k_f[:, sl] * w_s
            c_ref[h] = dec * c_st + _dot_tn(v[:, sl], k_w.astype(BF16))
            n_ref[h] = dec * n_st + jnp.sum(k_w, 0, keepdims=True)
            m_ref[h] = jnp.broadcast_to(m_new, m_ref.shape[1:])
            outs.append(_head_norm(hid, g_ref[:, sl]))
        h_all = jnp.concatenate(outs, axis=-1)
        o_ref[rows, :] = (h_all * jax.nn.sigmoid(o_gate)).astype(o_ref.dtype)
        return carry

    lax.fori_loop(0, ts // CHUNK, chunk, 0, unroll=MIXER_UNROLL)


def _mlstm(ym, gates_row, conv_w, b_col, b_row, g, batch, seq, ts):
    n = ym.shape[0]
    nt = seq // ts
    nck = ts // CHUNK
    tri = jnp.asarray(np.tril(np.ones((CHUNK, CHUNK), np.float32)), BF16)
    return pl.pallas_call(
        functools.partial(_mlstm_kernel, ts=ts),
        grid=(batch, nt),
        in_specs=[pl.BlockSpec((ts, SEG_W), lambda b, i: (b * nt + i, 0)),
                  pl.BlockSpec((nck, 8, CHUNK), lambda b, i: (b * nt + i, 0, 0)),
                  _full(conv_w.shape), _full(b_col.shape), _full(b_row.shape), _full(g.shape),
                  _full(tri.shape)],
        out_specs=pl.BlockSpec((ts, MIX_W), lambda b, i: (b * nt + i, 0)),
        out_shape=jax.ShapeDtypeStruct((n, MIX_W), BF16),
        scratch_shapes=[pltpu.VMEM((ts + 8, 2 * MIX_W), F32),
                        pltpu.VMEM((ts, 2 * MIX_W), F32),
                        pltpu.VMEM((MIX_HEADS, HEAD_D, HEAD_D), F32),
                        pltpu.VMEM((MIX_HEADS, 1, HEAD_D), F32),
                        pltpu.VMEM((MIX_HEADS, 8, LANES), F32)],
        compiler_params=_params(("parallel", "arbitrary")),
        name="mlstm_mixer",
    )(ym, gates_row, conv_w, b_col, b_row, g, tri)


def _rope_table_kernel(pos_ref, inv_ref, cos_ref, sin_ref):
    ang = pos_ref[...].astype(F32) * inv_ref[...]
    lane = lax.broadcasted_iota(jnp.int32, ang.shape, 1)
    rot = (lane >= MLA_NOPE) & (lane < MLA_NOPE + MLA_ROPE)
    first_half = lane < MLA_NOPE + MLA_ROPE // 2
    cos_ref[...] = jnp.where(lane < MLA_NOPE, 1.0, jnp.where(rot, jnp.cos(ang), 0.0))
    s = jnp.sin(ang)
    sin_ref[...] = jnp.where(rot, jnp.where(first_half, -s, s), 0.0)


def _rope_tables(pos_col, tm):
    n = pos_col.shape[0]
    half = MLA_ROPE // 2
    inv = ROPE_BASE ** (-np.arange(half, dtype=np.float32) / half)
    inv_row = np.zeros((1, LANES), np.float32)
    inv_row[0, MLA_NOPE:MLA_NOPE + half] = inv
    inv_row[0, MLA_NOPE + half:MLA_NOPE + MLA_ROPE] = inv
    return pl.pallas_call(
        _rope_table_kernel,
        grid=(n // tm,),
        in_specs=[pl.BlockSpec((tm, 1), lambda i: (i, 0)), _full((1, LANES))],
        out_specs=[pl.BlockSpec((tm, LANES), lambda i: (i, 0))] * 2,
        out_shape=[jax.ShapeDtypeStruct((n, LANES), F32)] * 2,
        compiler_params=_params(("parallel",)),
        name="rope_tables",
    )(pos_col, jnp.asarray(inv_row))


def _mla_prep_kernel(y_ref, cos_ref, sin_ref, gq_ref, gkv_ref, wqa_ref, wqb_ref, wkn_ref, wv_ref,
                     q_ref, k_ref, v_ref):
    def rms(x, g):
        return x * lax.rsqrt(jnp.mean(x * x, -1, keepdims=True) + LN_EPS) * g

    cos = cos_ref[...]
    sin = sin_ref[...]
    cq = rms(y_ref[:, 0:MLA_Q_RANK], gq_ref[...]).astype(BF16)
    ckv = rms(y_ref[:, MLA_Q_RANK:MLA_Q_RANK + MLA_KV_RANK], gkv_ref[...]).astype(BF16)
    k_rope = y_ref[:, 384:512] * cos + y_ref[:, 512:640] * sin
    qa = _dot(cq, wqa_ref[...])
    qb = _dot(cq, wqb_ref[...])
    kn = _dot(ckv, wkn_ref[...])
    scale = (MLA_NOPE + MLA_ROPE) ** -0.5 * LOG2_E
    for h in range(MLA_HEADS):
        sl = slice(h * MLA_QK_PAD, (h + 1) * MLA_QK_PAD)
        q_ref[:, sl] = ((qa[:, sl] * cos + qb[:, sl] * sin) * scale).astype(q_ref.dtype)
        k_ref[:, sl] = (kn[:, sl] + k_rope).astype(k_ref.dtype)
    v_ref[...] = _dot(ckv, wv_ref[...]).astype(v_ref.dtype)


def _mla_prep(yc, cos_t, sin_t, gq, gkv, wqa, wqb, wkn, wv, tm):
    n = yc.shape[0]
    row = lambda w: pl.BlockSpec((tm, w), lambda i: (i, 0))
    qk_w = MLA_HEADS * MLA_QK_PAD
    return pl.pallas_call(
        _mla_prep_kernel,
        grid=(n // tm,),
        in_specs=[row(MLA_SEG_W), row(LANES), row(LANES), _full(gq.shape), _full(gkv.shape),
                  _full(wqa.shape), _full(wqb.shape), _full(wkn.shape), _full(wv.shape)],
        out_specs=[row(qk_w), row(qk_w), row(MLA_HEADS * MLA_V)],
        out_shape=[jax.ShapeDtypeStruct((n, qk_w), BF16), jax.ShapeDtypeStruct((n, qk_w), BF16),
                   jax.ShapeDtypeStruct((n, MLA_HEADS * MLA_V), BF16)],
        compiler_params=_params(("parallel",)),
        name="mla_prep",
    )(yc, cos_t, sin_t, gq, gkv, wqa, wqb, wkn, wv)


def _mla_attn_kernel(q_ref, k_ref, v_ref, o_ref, *, seq, tq):
    t_chunk = lax.broadcasted_iota(jnp.int32, (tq, tq), 0) // CHUNK
    s_chunk = lax.broadcasted_iota(jnp.int32, (tq, tq), 1) // CHUNK
    diag_mask = s_chunk <= t_chunk
    for hh in range(MLA_HEAD_PAIR):
        ql = slice(hh * MLA_QK_PAD, (hh + 1) * MLA_QK_PAD)
        vl = slice(hh * MLA_V, (hh + 1) * MLA_V)
        for i in range(seq // tq):
            rows = slice(i * tq, (i + 1) * tq)
            q = q_ref[0, rows, ql]
            s_d = jnp.where(diag_mask, _dot_nt(q, k_ref[0, rows, ql]), -jnp.inf)
            m = jnp.max(s_d, -1, keepdims=True)
            if i > 0:
                s_o = _dot_nt(q, k_ref[0, 0:i * tq, ql])
                m = jnp.maximum(m, jnp.max(s_o, -1, keepdims=True))
            p_d = jnp.exp2(s_d - m)
            l = jnp.sum(p_d, -1, keepdims=True)
            o = _dot(p_d.astype(BF16), v_ref[0, rows, vl])
            if i > 0:
                p_o = jnp.exp2(s_o - m)
                l = l + jnp.sum(p_o, -1, keepdims=True)
                o = o + _dot(p_o.astype(BF16), v_ref[0, 0:i * tq, vl])
            o_ref[0, rows, vl] = (o / l).astype(o_ref.dtype)


def _mla_attn(q, k, v, batch, seq, tq):
    n = q.shape[0]
    qk_w = MLA_HEADS * MLA_QK_PAD
    v_w = MLA_HEADS * MLA_V
    pair = lambda w: pl.BlockSpec((1, seq, MLA_HEAD_PAIR * w), lambda b, h: (b, 0, h))
    out = pl.pallas_call(
        functools.partial(_mla_attn_kernel, seq=seq, tq=tq),
        grid=(batch, MLA_HEADS // MLA_HEAD_PAIR),
        in_specs=[pair(MLA_QK_PAD), pair(MLA_QK_PAD), pair(MLA_V)],
        out_specs=pair(MLA_V),
        out_shape=jax.ShapeDtypeStruct((batch, seq, v_w), BF16),
        compiler_params=_params(("parallel", "parallel")),
        name="mla_attention",
    )(q.reshape(batch, seq, qk_w), k.reshape(batch, seq, qk_w), v.reshape(batch, seq, v_w))
    return out.reshape(n, v_w)


def _out_proj_kernel(og_ref, om_ref, oc_ref, x_ref, wg_ref, wm_ref, wc_ref, g_ref, b_ref, o_ref):
    mix = _dot(og_ref[...], wg_ref[...]) + _dot(om_ref[...], wm_ref[...]) + _dot(oc_ref[...], wc_ref[...])
    o_ref[...] = _layer_norm(ALPHA * x_ref[...] + mix, g_ref[...], b_ref[...])


def _out_proj(og, om, oc, x2d, wg, wm, wc, g, b, tm):
    n = x2d.shape[0]
    row = lambda w: pl.BlockSpec((tm, w), lambda i: (i, 0))
    return pl.pallas_call(
        _out_proj_kernel,
        grid=(n // tm,),
        in_specs=[row(MIX_W), row(MIX_W), row(MLA_HEADS * MLA_V), row(D_MODEL),
                  _full(wg.shape), _full(wm.shape), _full(wc.shape), _full(g.shape), _full(b.shape)],
        out_specs=row(D_MODEL),
        out_shape=jax.ShapeDtypeStruct((n, D_MODEL), F32),
        compiler_params=_params(("parallel",)),
        name="out_proj_ln1",
    )(og, om, oc, x2d, wg, wm, wc, g, b)


def _xa_kv_kernel(mem_ref, w_ref, k_ref, v_ref):
    kv = _dot(mem_ref[...].astype(BF16), w_ref[...])
    k_ref[...] = kv[:, 0:D_MODEL].astype(k_ref.dtype)
    v_ref[...] = kv[:, D_MODEL:2 * D_MODEL].astype(v_ref.dtype)


def _xa_kv(mem2d, w_kv, mem_len):
    n = mem2d.shape[0]
    row = pl.BlockSpec((mem_len, D_MODEL), lambda i: (i, 0))
    return pl.pallas_call(
        _xa_kv_kernel,
        grid=(n // mem_len,),
        in_specs=[row, _full(w_kv.shape)],
        out_specs=[row, row],
        out_shape=[jax.ShapeDtypeStruct((n, D_MODEL), BF16)] * 2,
        compiler_params=_params(("parallel",)),
        name="xattn_kv",
    )(mem2d, w_kv)


def _xattn_kernel(x_ref, k_ref, v_ref, wq_ref, wo_ref, g_ref, b_ref, o_ref):
    x = x_ref[...]
    q = (_dot(x.astype(BF16), wq_ref[...]) * (XA_DH ** -0.5)).astype(BF16)
    out = jnp.zeros(x.shape, F32)
    for h in range(XA_HEADS):
        sl = slice(h * XA_DH, (h + 1) * XA_DH)
        s = _dot_nt(q[:, sl], k_ref[:, sl])
        p = jnp.exp(s - jnp.max(s, -1, keepdims=True))
        p = p / jnp.sum(p, -1, keepdims=True)
        o = _dot(p.astype(BF16), v_ref[:, sl])
        out = out + _dot(o.astype(BF16), wo_ref[sl, :])
    o_ref[...] = _layer_norm(ALPHA * x + out, g_ref[...], b_ref[...])


def _xattn(x1, xk, xv, wq, wo, g, b, batch, seq, mem_len, tm):
    n = x1.shape[0]
    nt = seq // tm
    row = pl.BlockSpec((tm, D_MODEL), lambda bi, i: (bi * nt + i, 0))
    mem = pl.BlockSpec((mem_len, D_MODEL), lambda bi, i: (bi, 0))
    return pl.pallas_call(
        _xattn_kernel,
        grid=(batch, nt),
        in_specs=[row, mem, mem, _full(wq.shape), _full(wo.shape), _full(g.shape), _full(b.shape)],
        out_specs=row,
        out_shape=jax.ShapeDtypeStruct((n, D_MODEL), F32),
        compiler_params=_params(("parallel", "parallel")),
        name="xattn_ln2",
    )(x1, xk, xv, wq, wo, g, b)


def _route_kernel(x_ref, w_ref, b_ref, tri_ref, xg_ref, rank_ref, grp_ref, cnt_out_ref, cnt_ref):
    @pl.when(pl.program_id(0) == 0)
    def _():
        cnt_ref[...] = jnp.zeros_like(cnt_ref)

    x = x_ref[...]
    logits = jnp.dot(x, w_ref[...], precision=HIGHEST, preferred_element_type=F32) + b_ref[...]
    lane = lax.broadcasted_iota(jnp.int32, logits.shape, 1).astype(F32)
    is_group = (lane >= N_EXPERTS) & (lane < N_EXPERTS + N_GROUPS)
    g_max = jnp.max(jnp.where(is_group, logits, -jnp.inf), -1, keepdims=True)
    g_sum = jnp.sum(jnp.where(is_group, jnp.exp(logits - g_max), 0.0), -1, keepdims=True)
    g_p = 1.0 / g_sum
    g_idx = jnp.min(jnp.where(is_group & (logits == g_max), lane - N_EXPERTS, float(LANES)), -1, keepdims=True)
    in_group = (lane < N_EXPERTS) & (jnp.floor(lane * (1.0 / EXPERTS_PER_GROUP)) == g_idx)
    e_max = jnp.max(jnp.where(in_group, logits, -jnp.inf), -1, keepdims=True)
    e_exp = jnp.where(in_group, jnp.exp(logits - e_max), 0.0)
    prob = e_exp / jnp.sum(e_exp, -1, keepdims=True)
    cand = jnp.where(in_group, prob, -1.0)
    p1 = jnp.max(cand, -1, keepdims=True)
    i1 = jnp.min(jnp.where(cand == p1, lane, float(LANES)), -1, keepdims=True)
    cand2 = jnp.where(lane == i1, -1.0, cand)
    p2 = jnp.max(cand2, -1, keepdims=True)
    i2 = jnp.min(jnp.where(cand2 == p2, lane, float(LANES)), -1, keepdims=True)
    p_sum = p1 + p2
    xg_ref[:, 0:D_MODEL] = x
    xg_ref[:, D_MODEL:] = (jnp.where(lane == i1, g_p * (p1 / p_sum), 0.0)
                           + jnp.where(lane == i2, g_p * (p2 / p_sum), 0.0))
    onehot = jnp.where(lane == g_idx, 1.0, 0.0)
    before = _dot(tri_ref[...], onehot.astype(BF16)) + cnt_ref[...]
    rank_ref[...] = jnp.sum(onehot * before, -1, keepdims=True).astype(jnp.int32)
    grp_ref[...] = g_idx.astype(jnp.int32)
    cnt_ref[...] += jnp.sum(onehot, 0, keepdims=True)
    cnt_out_ref[...] = cnt_ref[...]


def _route(x2, w_route, b_route, tm):
    n = x2.shape[0]
    tri = jnp.asarray(np.tril(np.ones((tm, tm), np.float32), -1), BF16)
    col = pl.BlockSpec((tm, 1), lambda i: (i, 0))
    return pl.pallas_call(
        _route_kernel,
        grid=(n // tm,),
        in_specs=[pl.BlockSpec((tm, D_MODEL), lambda i: (i, 0)), _full(w_route.shape), _full(b_route.shape),
                  _full(tri.shape)],
        out_specs=[pl.BlockSpec((tm, XG_W), lambda i: (i, 0)), col, col, _full((1, LANES))],
        out_shape=[jax.ShapeDtypeStruct((n, XG_W), F32), jax.ShapeDtypeStruct((n, 1), jnp.int32),
                   jax.ShapeDtypeStruct((n, 1), jnp.int32), jax.ShapeDtypeStruct((1, LANES), F32)],
        scratch_shapes=[pltpu.VMEM((1, LANES), F32)],
        compiler_params=_params(("arbitrary",)),
        name="moe_route",
    )(x2, w_route, b_route, tri)


def _dispatch_kernel(pos_ref, x_hbm, xs_hbm, sem, *, tm):
    base = pl.program_id(0) * tm

    def issue(t, carry):
        pltpu.make_async_copy(x_hbm.at[pl.ds(base + t, 1)], xs_hbm.at[pl.ds(pos_ref[t], 1)], sem).start()
        return carry

    lax.fori_loop(0, tm, issue, 0, unroll=DMA_UNROLL)
    pltpu.make_async_copy(x_hbm.at[pl.ds(0, tm)], xs_hbm.at[pl.ds(0, tm)], sem).wait()


def _dispatch(xg, pos, tm):
    n = xg.shape[0]
    return pl.pallas_call(
        functools.partial(_dispatch_kernel, tm=tm),
        grid=(n // tm,),
        in_specs=[pl.BlockSpec((tm,), lambda i: (i,), memory_space=pltpu.SMEM),
                  pl.BlockSpec(memory_space=pl.ANY)],
        out_specs=pl.BlockSpec(memory_space=pl.ANY),
        out_shape=jax.ShapeDtypeStruct(xg.shape, xg.dtype),
        scratch_shapes=[pltpu.SemaphoreType.DMA(())],
        compiler_params=_params(("arbitrary",)),
        name="moe_dispatch",
    )(pos, xg)


def _moe_ffn_kernel(blk_ref, grp_ref, first_ref, valid_ref, xs_ref, wg_ref, wu_ref, wd_ref, o_ref, xb_ref):
    w = pl.program_id(0)
    e = pl.program_id(1)

    @pl.when((first_ref[w] == 1) & (e == 0))
    def _():
        xb_ref[...] = xs_ref[:, 0:D_MODEL].astype(BF16)
        o_ref[...] = jnp.zeros_like(o_ref)

    @pl.when(valid_ref[w] == 1)
    def _():
        xb = xb_ref[...]
        gates = xs_ref[:, D_MODEL:]
        lane = lax.broadcasted_iota(jnp.int32, gates.shape, 1)
        w_tok = jnp.sum(jnp.where(lane == grp_ref[w] * EXPERTS_PER_GROUP + e, gates, 0.0), -1, keepdims=True)
        hg = _dot(xb, wg_ref[0].astype(BF16))
        hu = _dot(xb, wu_ref[0].astype(BF16))
        hid = hg * jax.nn.sigmoid(hg) * hu * w_tok
        o_ref[...] += _dot(hid.astype(BF16), wd_ref[0].astype(BF16))


def _moe_ffn(xs, items, w_gate, w_up, w_down, rb):
    n = xs.shape[0]
    blk, grp, first, valid = items

    def expert(w, e, blk, grp, first, valid):
        return (grp[w] * EXPERTS_PER_GROUP + jnp.where(valid[w] == 1, e, EXPERTS_PER_GROUP - 1), 0, 0)

    grid_spec = pltpu.PrefetchScalarGridSpec(
        num_scalar_prefetch=4,
        grid=(blk.shape[0], EXPERTS_PER_GROUP),
        in_specs=[pl.BlockSpec((rb, XG_W), lambda w, e, blk, grp, first, valid: (blk[w], 0)),
                  pl.BlockSpec((1, D_MODEL, D_EXPERT), expert),
                  pl.BlockSpec((1, D_MODEL, D_EXPERT), expert),
                  pl.BlockSpec((1, D_EXPERT, D_MODEL), expert)],
        out_specs=pl.BlockSpec((rb, D_MODEL), lambda w, e, blk, grp, first, valid: (blk[w], 0)),
        scratch_shapes=[pltpu.VMEM((rb, D_MODEL), BF16)],
    )
    return pl.pallas_call(
        _moe_ffn_kernel,
        grid_spec=grid_spec,
        out_shape=jax.ShapeDtypeStruct((n, D_MODEL), F32),
        compiler_params=_params(("arbitrary", "arbitrary")),
        name="moe_experts",
    )(blk, grp, first, valid, xs, w_gate, w_up, w_down)


def _combine_kernel(pos_ref, xg_ref, ys_hbm, g_ref, b_ref, o_ref, buf_ref, sem, *, tm):
    def issue(t, carry):
        pltpu.make_async_copy(ys_hbm.at[pl.ds(pos_ref[t], 1)], buf_ref.at[pl.ds(t, 1)], sem).start()
        return carry

    lax.fori_loop(0, tm, issue, 0, unroll=DMA_UNROLL)
    pltpu.make_async_copy(ys_hbm.at[pl.ds(0, tm)], buf_ref, sem).wait()
    o_ref[...] = _layer_norm(ALPHA * xg_ref[:, 0:D_MODEL] + buf_ref[...], g_ref[...], b_ref[...])


def _combine(pos, xg, ys, g, b, tm):
    n = xg.shape[0]
    return pl.pallas_call(
        functools.partial(_combine_kernel, tm=tm),
        grid=(n // tm,),
        in_specs=[pl.BlockSpec((tm,), lambda i: (i,), memory_space=pltpu.SMEM),
                  pl.BlockSpec((tm, XG_W), lambda i: (i, 0)),
                  pl.BlockSpec(memory_space=pl.ANY), _full(g.shape), _full(b.shape)],
        out_specs=pl.BlockSpec((tm, D_MODEL), lambda i: (i, 0)),
        out_shape=jax.ShapeDtypeStruct((n, D_MODEL), F32),
        scratch_shapes=[pltpu.VMEM((tm, D_MODEL), F32), pltpu.SemaphoreType.DMA(())],
        compiler_params=_params(("arbitrary",)),
        name="moe_combine_ln3",
    )(pos, xg, ys, g, b)


def _moe_work_items(counts, n, rb):
    nb = n // rb
    n_items = nb + N_GROUPS - 1
    ends = jnp.cumsum(counts)
    start = jnp.arange(nb, dtype=jnp.int32) * rb
    g_lo = jnp.sum(ends[None, :] <= start[:, None], axis=1).astype(jnp.int32)
    g_hi = jnp.sum(ends[None, :] <= (start + rb - 1)[:, None], axis=1).astype(jnp.int32)
    per_blk = g_hi - g_lo + 1
    item0 = jnp.cumsum(per_blk) - per_blk
    w = jnp.arange(n_items, dtype=jnp.int32)
    valid = w < jnp.sum(per_blk)
    blk = jnp.clip(jnp.sum(item0[None, :] <= w[:, None], axis=1) - 1, 0, nb - 1).astype(jnp.int32)
    grp = jnp.where(valid, g_lo[blk] + (w - item0[blk]), g_hi[nb - 1]).astype(jnp.int32)
    first = (valid & (w == item0[blk])).astype(jnp.int32)
    return blk, grp, first, valid.astype(jnp.int32)


def _moe(x2, w_route, b_route, w_gate, w_up, w_down, g, b, tm, t_dma, rb):
    n = x2.shape[0]
    xg, rank, grp, counts = _route(x2, w_route, b_route, tm)
    counts = counts[0, :N_GROUPS].astype(jnp.int32)
    offsets = jnp.cumsum(counts) - counts
    pos = (offsets[grp[:, 0]] + rank[:, 0]).astype(jnp.int32)
    xs = _dispatch(xg, pos, t_dma)
    ys = _moe_ffn(xs, _moe_work_items(counts, n, rb), w_gate, w_up, w_down, rb)
    return _combine(pos, xg, ys, g, b, t_dma)


def _pad_cols(w, width):
    return jnp.pad(w, ((0, 0), (0, width - w.shape[1])))


def _layer_weights(w_in, w_out, gla_w_a2, ml_b_i, ml_b_f, mla_w_uq, mla_w_ukv, moe_w_group, moe_b_group,
                   moe_w_router, moe_b_router):
    o = _IN_OFF
    half = MLA_ROPE // 2
    wg = _pad_cols(w_in[:, o[0]:o[5]], SEG_W).astype(BF16)
    wm = _pad_cols(w_in[:, o[5]:o[10]], SEG_W).astype(BF16)
    kr = w_in[:, o[12]:o[13]]
    zeros = lambda w: jnp.zeros((D_MODEL, w), F32)
    kra = jnp.concatenate([zeros(MLA_NOPE), kr, zeros(LANES - MLA_NOPE - MLA_ROPE)], 1)
    krb = jnp.concatenate([zeros(MLA_NOPE), kr[:, half:], kr[:, :half], zeros(LANES - MLA_NOPE - MLA_ROPE)], 1)
    wc = jnp.concatenate([w_in[:, o[10]:o[12]], kra, krb], 1).astype(BF16)
    wift = w_in[:, o[8]:o[10]].T.astype(BF16)
    wa2 = jnp.pad(gla_w_a2, ((0, LANES - GLA_GATE_RANK), (0, 0))).astype(BF16)
    b_gate = jnp.concatenate([ml_b_i, ml_b_f])
    b_col = jnp.pad(b_gate, (0, LANES - 2 * MIX_HEADS)).reshape(1, LANES)
    b_row = b_gate.reshape(2 * MIX_HEADS, 1)
    uq = mla_w_uq.reshape(MLA_Q_RANK, MLA_HEADS, MLA_NOPE + MLA_ROPE)
    zq = jnp.zeros((MLA_Q_RANK, MLA_HEADS, LANES - MLA_NOPE - MLA_ROPE), F32)
    wqa = jnp.concatenate([uq, zq], -1).reshape(MLA_Q_RANK, -1).astype(BF16)
    wqb = jnp.concatenate([jnp.zeros((MLA_Q_RANK, MLA_HEADS, MLA_NOPE), F32), uq[..., MLA_NOPE + half:],
                           uq[..., MLA_NOPE:MLA_NOPE + half], zq], -1).reshape(MLA_Q_RANK, -1).astype(BF16)
    ukv = mla_w_ukv.reshape(MLA_KV_RANK, MLA_HEADS, MLA_NOPE + MLA_V)
    wkn = jnp.concatenate([ukv[..., :MLA_NOPE], jnp.zeros((MLA_KV_RANK, MLA_HEADS, LANES - MLA_NOPE), F32)],
                          -1).reshape(MLA_KV_RANK, -1).astype(BF16)
    wv = ukv[..., MLA_NOPE:].reshape(MLA_KV_RANK, -1).astype(BF16)
    wo = w_out.astype(BF16)
    w_route = _pad_cols(jnp.concatenate([moe_w_router, moe_w_group], 1), LANES)
    b_route = jnp.pad(jnp.concatenate([moe_b_router, moe_b_group]), (0, LANES - N_EXPERTS - N_GROUPS)).reshape(1, LANES)
    return dict(wg=wg, wm=wm, wc=wc, wift=wift, wa2=wa2, b_col=b_col, b_row=b_row, wqa=wqa, wqb=wqb, wkn=wkn,
                wv=wv, wo_g=wo[0:MIX_W], wo_m=wo[MIX_W:2 * MIX_W], wo_c=wo[2 * MIX_W:], w_route=w_route,
                b_route=b_route)


def _tile(total, want):
    t = min(total, want)
    assert total % t == 0
    return t


def kernel(x, mem, positions, w_in, w_out, gla_w_a2, gla_b_a, gla_norm_g, ml_conv_w, ml_b_i, ml_b_f, ml_norm_g, mla_q_norm_g, mla_w_uq, mla_kv_norm_g, mla_w_ukv, xa_w_q, xa_w_kv, xa_w_o, moe_w_group, moe_b_group, moe_w_router, moe_b_router, moe_w_gate, moe_w_up, moe_w_down, ln1_g, ln1_b, ln2_g, ln2_b, ln3_g, ln3_b):
    batch, seq, _ = x.shape
    mem_len = mem.shape[1]
    n = batch * seq
    depth = w_in.shape[0]
    assert seq % CHUNK == 0
    tm = _tile(n, 512)
    ts = _tile(seq, 512)
    tq = _tile(seq, 256)
    t_moe = _tile(n, 1024)
    row = lambda a: a.reshape(1, -1)

    cos_t, sin_t = _rope_tables(positions.reshape(n, 1), _tile(n, 2048))
    mem2d = mem.reshape(batch * mem_len, D_MODEL)
    h = x.reshape(n, D_MODEL)
    for l in range(depth):
        w = _layer_weights(w_in[l], w_out[l], gla_w_a2[l], ml_b_i[l], ml_b_f[l], mla_w_uq[l], mla_w_ukv[l],
                           moe_w_group[l], moe_b_group[l], moe_w_router[l], moe_b_router[l])
        yg, ym, yc, yift = _in_proj(h, w["wg"], w["wm"], w["wc"], w["wift"], tm)
        gates_row = yift.reshape(2 * MIX_HEADS, n // CHUNK, CHUNK).transpose(1, 0, 2)
        og = _gla(yg, w["wa2"], row(gla_b_a[l]), row(gla_norm_g[l]), batch, seq, ts)
        om = _mlstm(ym, gates_row, ml_conv_w[l], w["b_col"], w["b_row"], row(ml_norm_g[l]), batch, seq, ts)
        q, k, v = _mla_prep(yc, cos_t, sin_t, row(mla_q_norm_g[l]), row(mla_kv_norm_g[l]),
                            w["wqa"], w["wqb"], w["wkn"], w["wv"], tm)
        oc = _mla_attn(q, k, v, batch, seq, tq)
        x1 = _out_proj(og, om, oc, h, w["wo_g"], w["wo_m"], w["wo_c"], row(ln1_g[l]), row(ln1_b[l]), tm)
        xk, xv = _xa_kv(mem2d, xa_w_kv[l].astype(BF16), mem_len)
        x2 = _xattn(x1, xk, xv, xa_w_q[l].astype(BF16), xa_w_o[l].astype(BF16), row(ln2_g[l]), row(ln2_b[l]),
                    batch, seq, mem_len, tm)
        h = _moe(x2, w["w_route"], w["b_route"], moe_w_gate[l], moe_w_up[l], moe_w_down[l],
                 row(ln3_g[l]), row(ln3_b[l]), tm, t_moe, t_moe)
    return h.reshape(batch, seq, D_MODEL)
```

```python
import functools

import numpy as np
import jax
import jax.numpy as jnp
from jax import lax
from jax.experimental import pallas as pl
from jax.experimental.pallas import tpu as pltpu

F32 = jnp.float32
BF16 = jnp.bfloat16
HIGHEST = lax.Precision.HIGHEST

D_MODEL = 1024
CHUNK = 64
HEAD_D = 64
MIX_HEADS = 4
MIX_W = MIX_HEADS * HEAD_D
GLA_GATE_RANK = 16
GLA_TAU = 16.0
MLSTM_CONV = 4
MLA_HEADS = 8
MLA_NOPE = 64
MLA_ROPE = 32
MLA_V = 64
MLA_Q_RANK = 256
MLA_KV_RANK = 128
MLA_QK_PAD = 128
ROPE_BASE = 10000.0
LOG2_E = 1.4426950408889634
MLA_HEAD_PAIR = 2
XA_HEADS = 4
XA_DH = D_MODEL // XA_HEADS
N_GROUPS = 4
EXPERTS_PER_GROUP = 8
N_EXPERTS = N_GROUPS * EXPERTS_PER_GROUP
D_EXPERT = 256
DEPTH = 2
ALPHA = (2 * DEPTH) ** 0.25
LN_EPS = 1e-5
LANES = 128
SUBLANES = 8
MIXER_UNROLL = 2
SEG_W = 1152
MLA_SEG_W = 640
XG_W = D_MODEL + LANES
DMA_UNROLL = 8
GLA_LEVELS = (32, 16, 8, 4, 2, 1)
VMEM_LIMIT = 56 * 1024 * 1024

_IN_SIZES = (256, 256, 256, 256, GLA_GATE_RANK, 512, 256, 256, 4, 4, MLA_Q_RANK, MLA_KV_RANK, MLA_ROPE)
_IN_OFF = np.concatenate([[0], np.cumsum(_IN_SIZES)]).tolist()


def _params(sem):
    return pltpu.CompilerParams(dimension_semantics=sem, vmem_limit_bytes=VMEM_LIMIT)


def _full(shape):
    return pl.BlockSpec(shape, lambda *_: (0,) * len(shape))


def _layer_norm(x, g, b):
    mu = jnp.mean(x, -1, keepdims=True)
    xc = x - mu
    var = jnp.mean(xc * xc, -1, keepdims=True)
    return xc * lax.rsqrt(var + LN_EPS) * g + b


def _log_sigmoid(z):
    return jnp.minimum(z, 0.0) - jnp.log1p(jnp.exp(-jnp.abs(z)))


def _dot_nt(a, b):
    return lax.dot_general(a, b, (((1,), (1,)), ((), ())), preferred_element_type=F32)


def _dot_tn(a, b):
    return lax.dot_general(a, b, (((0,), (0,)), ((), ())), preferred_element_type=F32)


def _dot(a, b):
    return jnp.dot(a, b, preferred_element_type=F32)


def _head_norm(o, g):
    mu = jnp.mean(o, -1, keepdims=True)
    oc = o - mu
    var = jnp.mean(oc * oc, -1, keepdims=True)
    return oc * lax.rsqrt(var + LN_EPS) * g


def _in_proj_kernel(x_ref, wg_ref, wm_ref, wc_ref, wift_ref, yg_ref, ym_ref, yc_ref, yift_ref):
    xb = x_ref[...].astype(BF16)
    yg_ref[...] = _dot(xb, wg_ref[...])
    ym_ref[...] = _dot(xb, wm_ref[...])
    yc_ref[...] = _dot(xb, wc_ref[...])
    yift_ref[...] = _dot_nt(wift_ref[...], xb)


def _in_proj(x2d, wg, wm, wc, wift, tm):
    n = x2d.shape[0]
    row = lambda w: pl.BlockSpec((tm, w), lambda i: (i, 0))
    return pl.pallas_call(
        _in_proj_kernel,
        grid=(n // tm,),
        in_specs=[row(D_MODEL), _full(wg.shape), _full(wm.shape), _full(wc.shape), _full(wift.shape)],
        out_specs=[row(SEG_W), row(SEG_W), row(MLA_SEG_W), pl.BlockSpec((8, tm), lambda i: (0, i))],
        out_shape=[jax.ShapeDtypeStruct((n, SEG_W), F32), jax.ShapeDtypeStruct((n, SEG_W), F32),
                   jax.ShapeDtypeStruct((n, MLA_SEG_W), F32), jax.ShapeDtypeStruct((8, n), F32)],
        compiler_params=_params(("parallel",)),
        name="in_proj",
    )(x2d, wg, wm, wc, wift)


def _split3(x):
    hi = x.astype(BF16)
    r1 = x - hi.astype(F32)
    mid = r1.astype(BF16)
    lo = (r1 - mid.astype(F32)).astype(BF16)
    return hi, mid, lo


def _cumsum_rows(tri, x):
    hi, mid, lo = _split3(x)
    return _dot(tri, hi) + _dot(tri, mid) + _dot(tri, lo)


def _gla_constants():
    t = np.arange(CHUNK)
    n_lv = len(GLA_LEVELS)
    masks = np.zeros((n_lv + 1, CHUNK, CHUNK), np.float32)
    right = np.zeros((n_lv, CHUNK, 1), np.float32)
    for li, n in enumerate(GLA_LEVELS):
        blk = t // (2 * n)
        is_right = (t % (2 * n)) >= n
        masks[li] = ((blk[:, None] == blk[None, :]) & is_right[:, None] & ~is_right[None, :])
        right[li, :, 0] = is_right
    masks[n_lv] = np.eye(CHUNK)
    return masks, right


def _gla_level_exponents(b, log_a, right_ref):
    row = lax.broadcasted_iota(jnp.int32, (CHUNK, 1), 0)
    exps = []
    for li, n in enumerate(GLA_LEVELS):
        if n >= SUBLANES // 2:
            per_blk = max(2 * n // SUBLANES, 1)
            b4 = b.reshape(CHUNK // (per_blk * SUBLANES), per_blk, SUBLANES, MIX_W)
            r_reg, r_sub = divmod(n - 1, SUBLANES)
            src = b4[:, r_reg:r_reg + 1, r_sub:r_sub + 1, :]
            b_r = jnp.broadcast_to(src, b4.shape).reshape(CHUNK, MIX_W)
            exps.append(jnp.where(right_ref[li] > 0.0, b - b_r, b_r - b))
        elif n == 2:
            pos = row % 4
            nxt = pltpu.roll(log_a, CHUNK - 1, 0)
            prv = pltpu.roll(log_a, 1, 0)
            exps.append(jnp.where(pos == 0, nxt, jnp.where(pos == 1, 0.0,
                                                            jnp.where(pos == 2, log_a, log_a + prv))))
        else:
            exps.append(jnp.where(right_ref[li] > 0.0, log_a, 0.0))
    return exps


def _gla_kernel(y_ref, wa2_ref, ba_ref, g_ref, tri_ref, mask_ref, right_ref, o_ref, st_ref, *, ts, unroll):
    n_lv = len(GLA_LEVELS)

    @pl.when(pl.program_id(1) == 0)
    def _():
        st_ref[...] = jnp.zeros_like(st_ref)

    def chunk(c, carry):
        rows = pl.ds(pl.multiple_of(c * CHUNK, CHUNK), CHUNK)
        q = y_ref[rows, 0:256] * (HEAD_D ** -0.5)
        k = y_ref[rows, 256:512]
        v = y_ref[rows, 512:768].astype(BF16)
        r_gate = y_ref[rows, 768:1024]
        a_lr = y_ref[rows, 1024:1152]
        z = _dot(a_lr.astype(BF16), wa2_ref[...]) + ba_ref[...]
        log_a = _log_sigmoid(z) * (1.0 / GLA_TAU)
        b = _cumsum_rows(tri_ref[...], log_a)
        b_end = b[CHUNK - 1:CHUNK, :]
        scores = [jnp.zeros((CHUNK, CHUNK), F32) for _ in range(MIX_HEADS)]
        for li, e in enumerate(_gla_level_exponents(b, log_a, right_ref)):
            x = (jnp.where(right_ref[li] > 0.0, q, k) * jnp.exp(e)).astype(BF16)
            for h in range(MIX_HEADS):
                xh = x[:, h * HEAD_D:(h + 1) * HEAD_D]
                scores[h] = scores[h] + _dot_nt(xh, xh) * mask_ref[li]
        qb = q.astype(BF16)
        kb = k.astype(BF16)
        q_in = (q * jnp.exp(b)).astype(BF16)
        k_out = (k * jnp.exp(b_end - b)).astype(BF16)
        dec_end = jnp.exp(b_end)
        outs = []
        for h in range(MIX_HEADS):
            sl = slice(h * HEAD_D, (h + 1) * HEAD_D)
            a = scores[h] + _dot_nt(qb[:, sl], kb[:, sl]) * mask_ref[n_lv]
            st = st_ref[h]
            o = _dot(a.astype(BF16), v[:, sl]) + _dot_nt(q_in[:, sl], st.astype(BF16))
            st_ref[h] = st * dec_end[:, sl] + _dot_tn(v[:, sl], k_out[:, sl])
            outs.append(_head_norm(o, g_ref[:, sl]))
        o_all = jnp.concatenate(outs, axis=-1)
        o_ref[rows, :] = (o_all * (r_gate * jax.nn.sigmoid(r_gate))).astype(o_ref.dtype)
        return carry

    lax.fori_loop(0, ts // CHUNK, chunk, 0, unroll=unroll)


def _gla(yg, wa2, ba, g, batch, seq, ts):
    n = yg.shape[0]
    nt = seq // ts
    masks, right = _gla_constants()
    tri = jnp.asarray(np.tril(np.ones((CHUNK, CHUNK), np.float32)), BF16)
    return pl.pallas_call(
        functools.partial(_gla_kernel, ts=ts, unroll=MIXER_UNROLL),
        grid=(batch, nt),
        in_specs=[pl.BlockSpec((ts, SEG_W), lambda b, i: (b * nt + i, 0)),
                  _full(wa2.shape), _full(ba.shape), _full(g.shape),
                  _full(tri.shape), _full(masks.shape), _full(right.shape)],
        out_specs=pl.BlockSpec((ts, MIX_W), lambda b, i: (b * nt + i, 0)),
        out_shape=jax.ShapeDtypeStruct((n, MIX_W), BF16),
        scratch_shapes=[pltpu.VMEM((MIX_HEADS, HEAD_D, HEAD_D), F32)],
        compiler_params=_params(("parallel", "arbitrary")),
        name="gla_mixer",
    )(yg, wa2, ba, g, tri, jnp.asarray(masks), jnp.asarray(right))


def _mlstm_kernel(y_ref, gr_ref, cw_ref, bcol_ref, brow_ref, g_ref, tri_ref, o_ref,
                  xe_ref, qk_ref, vx_ref, fcc_ref, fcr_ref, c_ref, m_ref, *, ts):
    first = pl.program_id(1) == 0

    @pl.when(first)
    def _():
        xe_ref[0:8, :] = jnp.zeros((8, 2 * MIX_W), F32)
        c_ref[...] = jnp.zeros_like(c_ref)
        m_ref[...] = jnp.zeros_like(m_ref)

    @pl.when(jnp.logical_not(first))
    def _():
        xe_ref[0:8, :] = xe_ref[ts:ts + 8, :]

    xe_ref[8:ts + 8, :] = y_ref[:, 0:2 * MIX_W]
    conv = cw_ref[MLSTM_CONV - 1:MLSTM_CONV, :] * xe_ref[8:ts + 8, :]
    for j in range(MLSTM_CONV - 1):
        conv = conv + cw_ref[j:j + 1, :] * xe_ref[pl.ds(8 - (MLSTM_CONV - 1) + j, ts), :]
    qk_ref[...] = conv * jax.nn.sigmoid(conv)

    ones_col = (lax.broadcasted_iota(jnp.int32, (ts, HEAD_D), 1) == 0).astype(BF16)
    for h in range(MIX_HEADS):
        vx_ref[:, h * LANES:h * LANES + HEAD_D] = y_ref[:, 512 + h * HEAD_D:512 + (h + 1) * HEAD_D].astype(BF16)
        vx_ref[:, h * LANES + HEAD_D:(h + 1) * LANES] = ones_col

    tri = tri_ref[...]
    nck = ts // CHUNK
    m_col = m_ref[nck][0:MIX_HEADS, 0:1]
    for c in range(nck):
        g_col = y_ref[c * CHUNK:(c + 1) * CHUNK, 1024:1152] + bcol_ref[...]
        fcc_ref[c] = _cumsum_rows(tri, _log_sigmoid(g_col))
        g_row = gr_ref[c] + brow_ref[...]
        fcum_row = sum(_dot_nt(part, tri) for part in _split3(_log_sigmoid(g_row)))
        fcr_ref[c] = fcum_row
        m_ref[c, 0:MIX_HEADS, :] = jnp.broadcast_to(m_col, (MIX_HEADS, LANES))
        f_row = fcum_row[MIX_HEADS:2 * MIX_HEADS, :]
        f_end = f_row[:, CHUNK - 1:CHUNK]
        end_log = f_end - f_row + g_row[0:MIX_HEADS, :]
        m_col = jnp.maximum(f_end + m_col, jnp.max(end_log, -1, keepdims=True))
    m_ref[nck, 0:MIX_HEADS, :] = jnp.broadcast_to(m_col, (MIX_HEADS, LANES))

    t_idx = lax.broadcasted_iota(jnp.int32, (CHUNK, CHUNK), 0)
    s_idx = lax.broadcasted_iota(jnp.int32, (CHUNK, CHUNK), 1)
    causal = s_idx <= t_idx

    def chunk(c, carry):
        rows = pl.ds(pl.multiple_of(c * CHUNK, CHUNK), CHUNK)
        q = (qk_ref[rows, 0:MIX_W] * (HEAD_D ** -0.5)).astype(BF16)
        k_f = qk_ref[rows, MIX_W:2 * MIX_W]
        k = k_f.astype(BF16)
        o_gate = y_ref[rows, 768:1024]
        g_col = y_ref[rows, 1024:1152] + bcol_ref[...]
        g_row = gr_ref[c] + brow_ref[...]
        fcum_col = fcc_ref[c]
        fcum_row = fcr_ref[c]
        m_now = m_ref[c]
        m_next = m_ref[c + 1]
        outs = []
        for h in range(MIX_HEADS):
            sl = slice(h * HEAD_D, (h + 1) * HEAD_D)
            vx = vx_ref[rows, h * LANES:(h + 1) * LANES]
            fc_c = fcum_col[:, MIX_HEADS + h:MIX_HEADS + h + 1]
            ig_c = g_col[:, h:h + 1]
            fc_r = fcum_row[MIX_HEADS + h:MIX_HEADS + h + 1, :]
            ig_r = g_row[h:h + 1, :]
            m_st = m_now[h:h + 1, 0:1]
            m_new = m_next[h:h + 1, 0:1]
            f_end = fc_c[CHUNK - 1:CHUNK, :]
            d_log = jnp.where(causal, fc_c - fc_r + ig_r, -jnp.inf)
            inter_log = fc_c + m_st
            m_t = jnp.maximum(inter_log, jnp.max(d_log, -1, keepdims=True))
            w_intra = jnp.exp(d_log - m_t) * _dot_nt(q[:, sl], k[:, sl])
            w_inter = jnp.exp(inter_log - m_t)
            c_st = c_ref[h]
            num = w_inter * _dot_nt(q[:, sl], c_st.astype(BF16)) + _dot(w_intra.astype(BF16), vx)
            den = num[:, HEAD_D:HEAD_D + 1]
            hid = num[:, 0:HEAD_D] / jnp.maximum(jnp.abs(den), jnp.exp(-m_t))
            w_s = jnp.exp(f_end - fc_c + ig_c - m_new)
            dec = jnp.exp(f_end + m_st - m_new)
            c_ref[h] = dec * c_st + _dot_tn(vx, (k_f[:, sl] * w_s).astype(BF16))
            outs.append(_head_norm(hid, g_ref[:, sl]))
        h_all = jnp.concatenate(outs, axis=-1)
        o_ref[rows, :] = (h_all * jax.nn.sigmoid(o_gate)).astype(o_ref.dtype)
        return carry

    lax.fori_loop(0, nck, chunk, 0, unroll=MIXER_UNROLL)


def _mlstm(ym, gates_row, conv_w, b_col, b_row, g, batch, seq, ts):
    n = ym.shape[0]
    nt = seq // ts
    nck = ts // CHUNK
    tri = jnp.asarray(np.tril(np.ones((CHUNK, CHUNK), np.float32)), BF16)
    return pl.pallas_call(
        functools.partial(_mlstm_kernel, ts=ts),
        grid=(batch, nt),
        in_specs=[pl.BlockSpec((ts, SEG_W), lambda b, i: (b * nt + i, 0)),
                  pl.BlockSpec((nck, 8, CHUNK), lambda b, i: (b * nt + i, 0, 0)),
                  _full(conv_w.shape), _full(b_col.shape), _full(b_row.shape), _full(g.shape),
                  _full(tri.shape)],
        out_specs=pl.BlockSpec((ts, MIX_W), lambda b, i: (b * nt + i, 0)),
        out_shape=jax.ShapeDtypeStruct((n, MIX_W), BF16),
        scratch_shapes=[pltpu.VMEM((ts + 8, 2 * MIX_W), F32),
                        pltpu.VMEM((ts, 2 * MIX_W), F32),
                        pltpu.VMEM((ts, MIX_HEADS * LANES), BF16),
                        pltpu.VMEM((nck, CHUNK, LANES), F32),
                        pltpu.VMEM((nck, SUBLANES, CHUNK), F32),
                        pltpu.VMEM((MIX_HEADS, LANES, HEAD_D), F32),
                        pltpu.VMEM((nck + 1, SUBLANES, LANES), F32)],
        compiler_params=_params(("parallel", "arbitrary")),
        name="mlstm_mixer",
    )(ym, gates_row, conv_w, b_col, b_row, g, tri)


def _rope_table_kernel(pos_ref, inv_ref, cos_ref, sin_ref):
    ang = pos_ref[...].astype(F32) * inv_ref[...]
    lane = lax.broadcasted_iota(jnp.int32, ang.shape, 1)
    rot = (lane >= MLA_NOPE) & (lane < MLA_NOPE + MLA_ROPE)
    first_half = lane < MLA_NOPE + MLA_ROPE // 2
    cos_ref[...] = jnp.where(lane < MLA_NOPE, 1.0, jnp.where(rot, jnp.cos(ang), 0.0))
    s = jnp.sin(ang)
    sin_ref[...] = jnp.where(rot, jnp.where(first_half, -s, s), 0.0)


def _rope_tables(pos_col, tm):
    n = pos_col.shape[0]
    half = MLA_ROPE // 2
    inv = ROPE_BASE ** (-np.arange(half, dtype=np.float32) / half)
    inv_row = np.zeros((1, LANES), np.float32)
    inv_row[0, MLA_NOPE:MLA_NOPE + half] = inv
    inv_row[0, MLA_NOPE + half:MLA_NOPE + MLA_ROPE] = inv
    return pl.pallas_call(
        _rope_table_kernel,
        grid=(n // tm,),
        in_specs=[pl.BlockSpec((tm, 1), lambda i: (i, 0)), _full((1, LANES))],
        out_specs=[pl.BlockSpec((tm, LANES), lambda i: (i, 0))] * 2,
        out_shape=[jax.ShapeDtypeStruct((n, LANES), F32)] * 2,
        compiler_params=_params(("parallel",)),
        name="rope_tables",
    )(pos_col, jnp.asarray(inv_row))


def _mla_prep_kernel(y_ref, cos_ref, sin_ref, gq_ref, gkv_ref, wqa_ref, wqb_ref, wkn_ref, wv_ref,
                     q_ref, k_ref, v_ref):
    def rms(x, g):
        return x * lax.rsqrt(jnp.mean(x * x, -1, keepdims=True) + LN_EPS) * g

    cos = cos_ref[...]
    sin = sin_ref[...]
    cq = rms(y_ref[:, 0:MLA_Q_RANK], gq_ref[...]).astype(BF16)
    ckv = rms(y_ref[:, MLA_Q_RANK:MLA_Q_RANK + MLA_KV_RANK], gkv_ref[...]).astype(BF16)
    k_rope = y_ref[:, 384:512] * cos + y_ref[:, 512:640] * sin
    qa = _dot(cq, wqa_ref[...])
    qb = _dot(cq, wqb_ref[...])
    kn = _dot(ckv, wkn_ref[...])
    scale = (MLA_NOPE + MLA_ROPE) ** -0.5 * LOG2_E
    for h in range(MLA_HEADS):
        sl = slice(h * MLA_QK_PAD, (h + 1) * MLA_QK_PAD)
        q_ref[:, sl] = ((qa[:, sl] * cos + qb[:, sl] * sin) * scale).astype(q_ref.dtype)
        k_ref[:, sl] = (kn[:, sl] + k_rope).astype(k_ref.dtype)
    v_ref[...] = _dot(ckv, wv_ref[...]).astype(v_ref.dtype)


def _mla_prep(yc, cos_t, sin_t, gq, gkv, wqa, wqb, wkn, wv, tm):
    n = yc.shape[0]
    row = lambda w: pl.BlockSpec((tm, w), lambda i: (i, 0))
    qk_w = MLA_HEADS * MLA_QK_PAD
    return pl.pallas_call(
        _mla_prep_kernel,
        grid=(n // tm,),
        in_specs=[row(MLA_SEG_W), row(LANES), row(LANES), _full(gq.shape), _full(gkv.shape),
                  _full(wqa.shape), _full(wqb.shape), _full(wkn.shape), _full(wv.shape)],
        out_specs=[row(qk_w), row(qk_w), row(MLA_HEADS * MLA_V)],
        out_shape=[jax.ShapeDtypeStruct((n, qk_w), BF16), jax.ShapeDtypeStruct((n, qk_w), BF16),
                   jax.ShapeDtypeStruct((n, MLA_HEADS * MLA_V), BF16)],
        compiler_params=_params(("parallel",)),
        name="mla_prep",
    )(yc, cos_t, sin_t, gq, gkv, wqa, wqb, wkn, wv)


def _mla_attn_kernel(q_ref, k_ref, v_ref, o_ref, *, seq, tq):
    t_chunk = lax.broadcasted_iota(jnp.int32, (tq, tq), 0) // CHUNK
    s_chunk = lax.broadcasted_iota(jnp.int32, (tq, tq), 1) // CHUNK
    diag_mask = s_chunk <= t_chunk
    for hh in range(MLA_HEAD_PAIR):
        ql = slice(hh * MLA_QK_PAD, (hh + 1) * MLA_QK_PAD)
        vl = slice(hh * MLA_V, (hh + 1) * MLA_V)
        for i in range(seq // tq):
            rows = slice(i * tq, (i + 1) * tq)
            q = q_ref[0, rows, ql]
            s_d = jnp.where(diag_mask, _dot_nt(q, k_ref[0, rows, ql]), -jnp.inf)
            m = jnp.max(s_d, -1, keepdims=True)
            if i > 0:
                s_o = _dot_nt(q, k_ref[0, 0:i * tq, ql])
                m = jnp.maximum(m, jnp.max(s_o, -1, keepdims=True))
            p_d = jnp.exp2(s_d - m)
            l = jnp.sum(p_d, -1, keepdims=True)
            o = _dot(p_d.astype(BF16), v_ref[0, rows, vl])
            if i > 0:
                p_o = jnp.exp2(s_o - m)
                l = l + jnp.sum(p_o, -1, keepdims=True)
                o = o + _dot(p_o.astype(BF16), v_ref[0, 0:i * tq, vl])
            o_ref[0, rows, vl] = (o / l).astype(o_ref.dtype)


def _mla_attn(q, k, v, batch, seq, tq):
    n = q.shape[0]
    qk_w = MLA_HEADS * MLA_QK_PAD
    v_w = MLA_HEADS * MLA_V
    pair = lambda w: pl.BlockSpec((1, seq, MLA_HEAD_PAIR * w), lambda b, h: (b, 0, h))
    out = pl.pallas_call(
        functools.partial(_mla_attn_kernel, seq=seq, tq=tq),
        grid=(batch, MLA_HEADS // MLA_HEAD_PAIR),
        in_specs=[pair(MLA_QK_PAD), pair(MLA_QK_PAD), pair(MLA_V)],
        out_specs=pair(MLA_V),
        out_shape=jax.ShapeDtypeStruct((batch, seq, v_w), BF16),
        compiler_params=_params(("parallel", "parallel")),
        name="mla_attention",
    )(q.reshape(batch, seq, qk_w), k.reshape(batch, seq, qk_w), v.reshape(batch, seq, v_w))
    return out.reshape(n, v_w)


def _out_proj_kernel(og_ref, om_ref, oc_ref, x_ref, wg_ref, wm_ref, wc_ref, g_ref, b_ref, o_ref):
    mix = _dot(og_ref[...], wg_ref[...]) + _dot(om_ref[...], wm_ref[...]) + _dot(oc_ref[...], wc_ref[...])
    o_ref[...] = _layer_norm(ALPHA * x_ref[...] + mix, g_ref[...], b_ref[...])


def _out_proj(og, om, oc, x2d, wg, wm, wc, g, b, tm):
    n = x2d.shape[0]
    row = lambda w: pl.BlockSpec((tm, w), lambda i: (i, 0))
    return pl.pallas_call(
        _out_proj_kernel,
        grid=(n // tm,),
        in_specs=[row(MIX_W), row(MIX_W), row(MLA_HEADS * MLA_V), row(D_MODEL),
                  _full(wg.shape), _full(wm.shape), _full(wc.shape), _full(g.shape), _full(b.shape)],
        out_specs=row(D_MODEL),
        out_shape=jax.ShapeDtypeStruct((n, D_MODEL), F32),
        compiler_params=_params(("parallel",)),
        name="out_proj_ln1",
    )(og, om, oc, x2d, wg, wm, wc, g, b)


def _xa_kv_kernel(mem_ref, w_ref, k_ref, v_ref):
    kv = _dot(mem_ref[...].astype(BF16), w_ref[...])
    k_ref[...] = kv[:, 0:D_MODEL].astype(k_ref.dtype)
    v_ref[...] = kv[:, D_MODEL:2 * D_MODEL].astype(v_ref.dtype)


def _xa_kv(mem2d, w_kv, mem_len):
    n = mem2d.shape[0]
    row = pl.BlockSpec((mem_len, D_MODEL), lambda i: (i, 0))
    return pl.pallas_call(
        _xa_kv_kernel,
        grid=(n // mem_len,),
        in_specs=[row, _full(w_kv.shape)],
        out_specs=[row, row],
        out_shape=[jax.ShapeDtypeStruct((n, D_MODEL), BF16)] * 2,
        compiler_params=_params(("parallel",)),
        name="xattn_kv",
    )(mem2d, w_kv)


def _xattn_kernel(x_ref, k_ref, v_ref, wq_ref, wo_ref, g_ref, b_ref, o_ref):
    x = x_ref[...]
    q = (_dot(x.astype(BF16), wq_ref[...]) * (XA_DH ** -0.5)).astype(BF16)
    out = jnp.zeros(x.shape, F32)
    for h in range(XA_HEADS):
        sl = slice(h * XA_DH, (h + 1) * XA_DH)
        s = _dot_nt(q[:, sl], k_ref[:, sl])
        p = jnp.exp(s - jnp.max(s, -1, keepdims=True))
        p = p / jnp.sum(p, -1, keepdims=True)
        o = _dot(p.astype(BF16), v_ref[:, sl])
        out = out + _dot(o.astype(BF16), wo_ref[sl, :])
    o_ref[...] = _layer_norm(ALPHA * x + out, g_ref[...], b_ref[...])


def _xattn(x1, xk, xv, wq, wo, g, b, batch, seq, mem_len, tm):
    n = x1.shape[0]
    nt = seq // tm
    row = pl.BlockSpec((tm, D_MODEL), lambda bi, i: (bi * nt + i, 0))
    mem = pl.BlockSpec((mem_len, D_MODEL), lambda bi, i: (bi, 0))
    return pl.pallas_call(
        _xattn_kernel,
        grid=(batch, nt),
        in_specs=[row, mem, mem, _full(wq.shape), _full(wo.shape), _full(g.shape), _full(b.shape)],
        out_specs=row,
        out_shape=jax.ShapeDtypeStruct((n, D_MODEL), F32),
        compiler_params=_params(("parallel", "parallel")),
        name="xattn_ln2",
    )(x1, xk, xv, wq, wo, g, b)


def _route_kernel(x_ref, w_ref, b_ref, tri_ref, xg_ref, rank_ref, grp_ref, cnt_out_ref, cnt_ref):
    @pl.when(pl.program_id(0) == 0)
    def _():
        cnt_ref[...] = jnp.zeros_like(cnt_ref)

    x = x_ref[...]
    logits = jnp.dot(x, w_ref[...], precision=HIGHEST, preferred_element_type=F32) + b_ref[...]
    lane = lax.broadcasted_iota(jnp.int32, logits.shape, 1).astype(F32)
    is_group = (lane >= N_EXPERTS) & (lane < N_EXPERTS + N_GROUPS)
    g_max = jnp.max(jnp.where(is_group, logits, -jnp.inf), -1, keepdims=True)
    g_sum = jnp.sum(jnp.where(is_group, jnp.exp(logits - g_max), 0.0), -1, keepdims=True)
    g_p = 1.0 / g_sum
    g_idx = jnp.min(jnp.where(is_group & (logits == g_max), lane - N_EXPERTS, float(LANES)), -1, keepdims=True)
    in_group = (lane < N_EXPERTS) & (jnp.floor(lane * (1.0 / EXPERTS_PER_GROUP)) == g_idx)
    e_max = jnp.max(jnp.where(in_group, logits, -jnp.inf), -1, keepdims=True)
    e_exp = jnp.where(in_group, jnp.exp(logits - e_max), 0.0)
    prob = e_exp / jnp.sum(e_exp, -1, keepdims=True)
    cand = jnp.where(in_group, prob, -1.0)
    p1 = jnp.max(cand, -1, keepdims=True)
    i1 = jnp.min(jnp.where(cand == p1, lane, float(LANES)), -1, keepdims=True)
    cand2 = jnp.where(lane == i1, -1.0, cand)
    p2 = jnp.max(cand2, -1, keepdims=True)
    i2 = jnp.min(jnp.where(cand2 == p2, lane, float(LANES)), -1, keepdims=True)
    p_sum = p1 + p2
    xg_ref[:, 0:D_MODEL] = x
    xg_ref[:, D_MODEL:] = (jnp.where(lane == i1, g_p * (p1 / p_sum), 0.0)
                           + jnp.where(lane == i2, g_p * (p2 / p_sum), 0.0))
    onehot = jnp.where(lane == g_idx, 1.0, 0.0)
    before = _dot(tri_ref[...], onehot.astype(BF16)) + cnt_ref[...]
    rank_ref[...] = jnp.sum(onehot * before, -1, keepdims=True).astype(jnp.int32)
    grp_ref[...] = g_idx.astype(jnp.int32)
    cnt_ref[...] += jnp.sum(onehot, 0, keepdims=True)
    cnt_out_ref[...] = cnt_ref[...]


def _route(x2, w_route, b_route, tm):
    n = x2.shape[0]
    tri = jnp.asarray(np.tril(np.ones((tm, tm), np.float32), -1), BF16)
    col = pl.BlockSpec((tm, 1), lambda i: (i, 0))
    return pl.pallas_call(
        _route_kernel,
        grid=(n // tm,),
        in_specs=[pl.BlockSpec((tm, D_MODEL), lambda i: (i, 0)), _full(w_route.shape), _full(b_route.shape),
                  _full(tri.shape)],
        out_specs=[pl.BlockSpec((tm, XG_W), lambda i: (i, 0)), col, col, _full((1, LANES))],
        out_shape=[jax.ShapeDtypeStruct((n, XG_W), F32), jax.ShapeDtypeStruct((n, 1), jnp.int32),
                   jax.ShapeDtypeStruct((n, 1), jnp.int32), jax.ShapeDtypeStruct((1, LANES), F32)],
        scratch_shapes=[pltpu.VMEM((1, LANES), F32)],
        compiler_params=_params(("arbitrary",)),
        name="moe_route",
    )(x2, w_route, b_route, tri)


def _dispatch_kernel(pos_ref, x_ref, xs_hbm, sem, *, tm):
    def issue(t, carry):
        pltpu.make_async_copy(x_ref.at[pl.ds(t, 1)], xs_hbm.at[pl.ds(pos_ref[t], 1)], sem).start()
        return carry

    lax.fori_loop(0, tm, issue, 0, unroll=DMA_UNROLL)
    pltpu.make_async_copy(x_ref, xs_hbm.at[pl.ds(0, tm)], sem).wait()


def _dispatch(xg, pos, tm):
    n = xg.shape[0]
    return pl.pallas_call(
        functools.partial(_dispatch_kernel, tm=tm),
        grid=(n // tm,),
        in_specs=[pl.BlockSpec((tm,), lambda i: (i,), memory_space=pltpu.SMEM),
                  pl.BlockSpec((tm, XG_W), lambda i: (i, 0))],
        out_specs=pl.BlockSpec(memory_space=pl.ANY),
        out_shape=jax.ShapeDtypeStruct(xg.shape, xg.dtype),
        scratch_shapes=[pltpu.SemaphoreType.DMA(())],
        compiler_params=_params(("arbitrary",)),
        name="moe_dispatch",
    )(pos, xg)


def _moe_ffn_kernel(blk_ref, grp_ref, first_ref, valid_ref, xs_ref, wg_ref, wu_ref, wd_ref, o_ref, xb_ref):
    w = pl.program_id(0)
    e = pl.program_id(1)

    @pl.when((first_ref[w] == 1) & (e == 0))
    def _():
        xb_ref[...] = xs_ref[:, 0:D_MODEL].astype(BF16)
        o_ref[...] = jnp.zeros_like(o_ref)

    @pl.when(valid_ref[w] == 1)
    def _():
        xb = xb_ref[...]
        gates = xs_ref[:, D_MODEL:]
        lane = lax.broadcasted_iota(jnp.int32, gates.shape, 1)
        w_tok = jnp.sum(jnp.where(lane == grp_ref[w] * EXPERTS_PER_GROUP + e, gates, 0.0), -1, keepdims=True)
        hg = _dot(xb, wg_ref[0].astype(BF16))
        hu = _dot(xb, wu_ref[0].astype(BF16))
        hid = hg * jax.nn.sigmoid(hg) * hu * w_tok
        o_ref[...] += _dot(hid.astype(BF16), wd_ref[0].astype(BF16))


def _moe_ffn(xs, items, w_gate, w_up, w_down, rb):
    n = xs.shape[0]
    blk, grp, first, valid = items

    def expert(w, e, blk, grp, first, valid):
        return (grp[w] * EXPERTS_PER_GROUP + jnp.where(valid[w] == 1, e, EXPERTS_PER_GROUP - 1), 0, 0)

    grid_spec = pltpu.PrefetchScalarGridSpec(
        num_scalar_prefetch=4,
        grid=(blk.shape[0], EXPERTS_PER_GROUP),
        in_specs=[pl.BlockSpec((rb, XG_W), lambda w, e, blk, grp, first, valid: (blk[w], 0)),
                  pl.BlockSpec((1, D_MODEL, D_EXPERT), expert),
                  pl.BlockSpec((1, D_MODEL, D_EXPERT), expert),
                  pl.BlockSpec((1, D_EXPERT, D_MODEL), expert)],
        out_specs=pl.BlockSpec((rb, D_MODEL), lambda w, e, blk, grp, first, valid: (blk[w], 0)),
        scratch_shapes=[pltpu.VMEM((rb, D_MODEL), BF16)],
    )
    return pl.pallas_call(
        _moe_ffn_kernel,
        grid_spec=grid_spec,
        out_shape=jax.ShapeDtypeStruct((n, D_MODEL), F32),
        compiler_params=_params(("arbitrary", "arbitrary")),
        name="moe_experts",
    )(blk, grp, first, valid, xs, w_gate, w_up, w_down)


def _combine_kernel(pos_ref, xg_ref, ys_hbm, g_ref, b_ref, o_ref, buf_ref, sem, *, tm):
    def issue(t, carry):
        pltpu.make_async_copy(ys_hbm.at[pl.ds(pos_ref[t], 1)], buf_ref.at[pl.ds(t, 1)], sem).start()
        return carry

    lax.fori_loop(0, tm, issue, 0, unroll=DMA_UNROLL)
    pltpu.make_async_copy(ys_hbm.at[pl.ds(0, tm)], buf_ref, sem).wait()
    o_ref[...] = _layer_norm(ALPHA * xg_ref[:, 0:D_MODEL] + buf_ref[...], g_ref[...], b_ref[...])


def _combine(pos, xg, ys, g, b, tm):
    n = xg.shape[0]
    return pl.pallas_call(
        functools.partial(_combine_kernel, tm=tm),
        grid=(n // tm,),
        in_specs=[pl.BlockSpec((tm,), lambda i: (i,), memory_space=pltpu.SMEM),
                  pl.BlockSpec((tm, XG_W), lambda i: (i, 0)),
                  pl.BlockSpec(memory_space=pl.ANY), _full(g.shape), _full(b.shape)],
        out_specs=pl.BlockSpec((tm, D_MODEL), lambda i: (i, 0)),
        out_shape=jax.ShapeDtypeStruct((n, D_MODEL), F32),
        scratch_shapes=[pltpu.VMEM((tm, D_MODEL), F32), pltpu.SemaphoreType.DMA(())],
        compiler_params=_params(("arbitrary",)),
        name="moe_combine_ln3",
    )(pos, xg, ys, g, b)


def _moe_work_items(counts, n, rb):
    nb = n // rb
    n_items = nb + N_GROUPS - 1
    ends = jnp.cumsum(counts)
    start = jnp.arange(nb, dtype=jnp.int32) * rb
    g_lo = jnp.sum(ends[None, :] <= start[:, None], axis=1).astype(jnp.int32)
    g_hi = jnp.sum(ends[None, :] <= (start + rb - 1)[:, None], axis=1).astype(jnp.int32)
    per_blk = g_hi - g_lo + 1
    item0 = jnp.cumsum(per_blk) - per_blk
    w = jnp.arange(n_items, dtype=jnp.int32)
    valid = w < jnp.sum(per_blk)
    blk = jnp.clip(jnp.sum(item0[None, :] <= w[:, None], axis=1) - 1, 0, nb - 1).astype(jnp.int32)
    grp = jnp.where(valid, g_lo[blk] + (w - item0[blk]), g_hi[nb - 1]).astype(jnp.int32)
    first = (valid & (w == item0[blk])).astype(jnp.int32)
    return blk, grp, first, valid.astype(jnp.int32)


def _moe(x2, w_route, b_route, w_gate, w_up, w_down, g, b, tm, t_dma, rb):
    n = x2.shape[0]
    xg, rank, grp, counts = _route(x2, w_route, b_route, tm)
    counts = counts[0, :N_GROUPS].astype(jnp.int32)
    offsets = jnp.cumsum(counts) - counts
    pos = (offsets[grp[:, 0]] + rank[:, 0]).astype(jnp.int32)
    xs = _dispatch(xg, pos, t_dma)
    ys = _moe_ffn(xs, _moe_work_items(counts, n, rb), w_gate, w_up, w_down, rb)
    return _combine(pos, xg, ys, g, b, t_dma)


def _pad_cols(w, width):
    return jnp.pad(w, ((0, 0), (0, width - w.shape[1])))


def _layer_weights(w_in, w_out, gla_w_a2, ml_b_i, ml_b_f, mla_w_uq, mla_w_ukv, moe_w_group, moe_b_group,
                   moe_w_router, moe_b_router):
    o = _IN_OFF
    half = MLA_ROPE // 2
    wg = _pad_cols(w_in[:, o[0]:o[5]], SEG_W).astype(BF16)
    wm = _pad_cols(w_in[:, o[5]:o[10]], SEG_W).astype(BF16)
    kr = w_in[:, o[12]:o[13]]
    zeros = lambda w: jnp.zeros((D_MODEL, w), F32)
    kra = jnp.concatenate([zeros(MLA_NOPE), kr, zeros(LANES - MLA_NOPE - MLA_ROPE)], 1)
    krb = jnp.concatenate([zeros(MLA_NOPE), kr[:, half:], kr[:, :half], zeros(LANES - MLA_NOPE - MLA_ROPE)], 1)
    wc = jnp.concatenate([w_in[:, o[10]:o[12]], kra, krb], 1).astype(BF16)
    wift = w_in[:, o[8]:o[10]].T.astype(BF16)
    wa2 = jnp.pad(gla_w_a2, ((0, LANES - GLA_GATE_RANK), (0, 0))).astype(BF16)
    b_gate = jnp.concatenate([ml_b_i, ml_b_f])
    b_col = jnp.pad(b_gate, (0, LANES - 2 * MIX_HEADS)).reshape(1, LANES)
    b_row = b_gate.reshape(2 * MIX_HEADS, 1)
    uq = mla_w_uq.reshape(MLA_Q_RANK, MLA_HEADS, MLA_NOPE + MLA_ROPE)
    zq = jnp.zeros((MLA_Q_RANK, MLA_HEADS, LANES - MLA_NOPE - MLA_ROPE), F32)
    wqa = jnp.concatenate([uq, zq], -1).reshape(MLA_Q_RANK, -1).astype(BF16)
    wqb = jnp.concatenate([jnp.zeros((MLA_Q_RANK, MLA_HEADS, MLA_NOPE), F32), uq[..., MLA_NOPE + half:],
                           uq[..., MLA_NOPE:MLA_NOPE + half], zq], -1).reshape(MLA_Q_RANK, -1).astype(BF16)
    ukv = mla_w_ukv.reshape(MLA_KV_RANK, MLA_HEADS, MLA_NOPE + MLA_V)
    wkn = jnp.concatenate([ukv[..., :MLA_NOPE], jnp.zeros((MLA_KV_RANK, MLA_HEADS, LANES - MLA_NOPE), F32)],
                          -1).reshape(MLA_KV_RANK, -1).astype(BF16)
    wv = ukv[..., MLA_NOPE:].reshape(MLA_KV_RANK, -1).astype(BF16)
    wo = w_out.astype(BF16)
    w_route = _pad_cols(jnp.concatenate([moe_w_router, moe_w_group], 1), LANES)
    b_route = jnp.pad(jnp.concatenate([moe_b_router, moe_b_group]), (0, LANES - N_EXPERTS - N_GROUPS)).reshape(1, LANES)
    return dict(wg=wg, wm=wm, wc=wc, wift=wift, wa2=wa2, b_col=b_col, b_row=b_row, wqa=wqa, wqb=wqb, wkn=wkn,
                wv=wv, wo_g=wo[0:MIX_W], wo_m=wo[MIX_W:2 * MIX_W], wo_c=wo[2 * MIX_W:], w_route=w_route,
                b_route=b_route)


def _tile(total, want):
    t = min(total, want)
    assert total % t == 0
    return t


def kernel(x, mem, positions, w_in, w_out, gla_w_a2, gla_b_a, gla_norm_g, ml_conv_w, ml_b_i, ml_b_f, ml_norm_g, mla_q_norm_g, mla_w_uq, mla_kv_norm_g, mla_w_ukv, xa_w_q, xa_w_kv, xa_w_o, moe_w_group, moe_b_group, moe_w_router, moe_b_router, moe_w_gate, moe_w_up, moe_w_down, ln1_g, ln1_b, ln2_g, ln2_b, ln3_g, ln3_b):
    batch, seq, _ = x.shape
    mem_len = mem.shape[1]
    n = batch * seq
    depth = w_in.shape[0]
    assert seq % CHUNK == 0
    tm = _tile(n, 512)
    ts = _tile(seq, 512)
    tq = _tile(seq, 256)
    t_moe = _tile(n, 1024)
    row = lambda a: a.reshape(1, -1)

    cos_t, sin_t = _rope_tables(positions.reshape(n, 1), _tile(n, 2048))
    mem2d = mem.reshape(batch * mem_len, D_MODEL)
    h = x.reshape(n, D_MODEL)
    for l in range(depth):
        w = _layer_weights(w_in[l], w_out[l], gla_w_a2[l], ml_b_i[l], ml_b_f[l], mla_w_uq[l], mla_w_ukv[l],
                           moe_w_group[l], moe_b_group[l], moe_w_router[l], moe_b_router[l])
        yg, ym, yc, yift = _in_proj(h, w["wg"], w["wm"], w["wc"], w["wift"], tm)
        gates_row = yift.reshape(2 * MIX_HEADS, n // CHUNK, CHUNK).transpose(1, 0, 2)
        og = _gla(yg, w["wa2"], row(gla_b_a[l]), row(gla_norm_g[l]), batch, seq, ts)
        om = _mlstm(ym, gates_row, ml_conv_w[l], w["b_col"], w["b_row"], row(ml_norm_g[l]), batch, seq, ts)
        q, k, v = _mla_prep(yc, cos_t, sin_t, row(mla_q_norm_g[l]), row(mla_kv_norm_g[l]),
                            w["wqa"], w["wqb"], w["wkn"], w["wv"], tm)
        oc = _mla_attn(q, k, v, batch, seq, tq)
        x1 = _out_proj(og, om, oc, h, w["wo_g"], w["wo_m"], w["wo_c"], row(ln1_g[l]), row(ln1_b[l]), tm)
        xk, xv = _xa_kv(mem2d, xa_w_kv[l].astype(BF16), mem_len)
        x2 = _xattn(x1, xk, xv, xa_w_q[l].astype(BF16), xa_w_o[l].astype(BF16), row(ln2_g[l]), row(ln2_b[l]),
                    batch, seq, mem_len, tm)
        h = _moe(x2, w["w_route"], w["b_route"], moe_w_gate[l], moe_w_up[l], moe_w_down[l],
                 row(ln3_g[l]), row(ln3_b[l]), tm, t_moe, t_moe)
    return h.reshape(batch, seq, D_MODEL)
```

```python
import functools

import numpy as np
import jax
import jax.numpy as jnp
from jax import lax
from jax.experimental import pallas as pl
from jax.experimental.pallas import tpu as pltpu

F32 = jnp.float32
BF16 = jnp.bfloat16
HIGHEST = lax.Precision.HIGHEST

D_MODEL = 1024
CHUNK = 64
HEAD_D = 64
MIX_HEADS = 4
MIX_W = MIX_HEADS * HEAD_D
GLA_GATE_RANK = 16
GLA_TAU = 16.0
MLSTM_CONV = 4
MLA_HEADS = 8
MLA_NOPE = 64
MLA_ROPE = 32
MLA_V = 64
MLA_Q_RANK = 256
MLA_KV_RANK = 128
MLA_QK_PAD = 128
ROPE_BASE = 10000.0
LOG2_E = 1.4426950408889634
MLA_HEAD_PAIR = 2
XA_HEADS = 4
XA_DH = D_MODEL // XA_HEADS
N_GROUPS = 4
EXPERTS_PER_GROUP = 8
N_EXPERTS = N_GROUPS * EXPERTS_PER_GROUP
D_EXPERT = 256
DEPTH = 2
ALPHA = (2 * DEPTH) ** 0.25
LN_EPS = 1e-5
LANES = 128
SUBLANES = 8
MIXER_UNROLL = 4
SEG_W = 1152
MLA_SEG_W = 640
XG_W = D_MODEL + LANES
DMA_UNROLL = 8
GLA_LEVELS = (32, 16, 8, 4, 2, 1)
VMEM_LIMIT = 56 * 1024 * 1024

_IN_SIZES = (256, 256, 256, 256, GLA_GATE_RANK, 512, 256, 256, 4, 4, MLA_Q_RANK, MLA_KV_RANK, MLA_ROPE)
_IN_OFF = np.concatenate([[0], np.cumsum(_IN_SIZES)]).tolist()


def _params(sem):
    return pltpu.CompilerParams(dimension_semantics=sem, vmem_limit_bytes=VMEM_LIMIT)


def _full(shape):
    return pl.BlockSpec(shape, lambda *_: (0,) * len(shape))


def _layer_norm(x, g, b):
    mu = jnp.mean(x, -1, keepdims=True)
    xc = x - mu
    var = jnp.mean(xc * xc, -1, keepdims=True)
    return xc * lax.rsqrt(var + LN_EPS) * g + b


def _log_sigmoid(z):
    return jnp.minimum(z, 0.0) - jnp.log1p(jnp.exp(-jnp.abs(z)))


def _dot_nt(a, b):
    return lax.dot_general(a, b, (((1,), (1,)), ((), ())), preferred_element_type=F32)


def _dot_tn(a, b):
    return lax.dot_general(a, b, (((0,), (0,)), ((), ())), preferred_element_type=F32)


def _dot(a, b):
    return jnp.dot(a, b, preferred_element_type=F32)


def _head_norm(o, g):
    mu = jnp.mean(o, -1, keepdims=True)
    oc = o - mu
    var = jnp.mean(oc * oc, -1, keepdims=True)
    return oc * lax.rsqrt(var + LN_EPS) * g


def _in_proj_kernel(x_ref, wg_ref, wm_ref, wc_ref, wift_ref, yg_ref, ym_ref, yc_ref, yift_ref):
    xb = x_ref[...].astype(BF16)
    yg_ref[...] = _dot(xb, wg_ref[...])
    ym_ref[...] = _dot(xb, wm_ref[...])
    yc_ref[...] = _dot(xb, wc_ref[...])
    yift_ref[...] = _dot_nt(wift_ref[...], xb)


def _in_proj(x2d, wg, wm, wc, wift, tm):
    n = x2d.shape[0]
    row = lambda w: pl.BlockSpec((tm, w), lambda i: (i, 0))
    return pl.pallas_call(
        _in_proj_kernel,
        grid=(n // tm,),
        in_specs=[row(D_MODEL), _full(wg.shape), _full(wm.shape), _full(wc.shape), _full(wift.shape)],
        out_specs=[row(SEG_W), row(SEG_W), row(MLA_SEG_W), pl.BlockSpec((8, tm), lambda i: (0, i))],
        out_shape=[jax.ShapeDtypeStruct((n, SEG_W), F32), jax.ShapeDtypeStruct((n, SEG_W), F32),
                   jax.ShapeDtypeStruct((n, MLA_SEG_W), F32), jax.ShapeDtypeStruct((8, n), F32)],
        compiler_params=_params(("parallel",)),
        name="in_proj",
    )(x2d, wg, wm, wc, wift)


def _split3(x):
    hi = x.astype(BF16)
    r1 = x - hi.astype(F32)
    mid = r1.astype(BF16)
    lo = (r1 - mid.astype(F32)).astype(BF16)
    return hi, mid, lo


def _cumsum_rows(tri, x):
    hi, mid, lo = _split3(x)
    return _dot(tri, hi) + _dot(tri, mid) + _dot(tri, lo)


def _gla_constants():
    t = np.arange(CHUNK)
    n_lv = len(GLA_LEVELS)
    masks = np.zeros((n_lv + 1, CHUNK, CHUNK), np.float32)
    right = np.zeros((n_lv, CHUNK, 1), np.float32)
    for li, n in enumerate(GLA_LEVELS):
        blk = t // (2 * n)
        is_right = (t % (2 * n)) >= n
        masks[li] = ((blk[:, None] == blk[None, :]) & is_right[:, None] & ~is_right[None, :])
        right[li, :, 0] = is_right
    masks[n_lv] = np.eye(CHUNK)
    return masks, right


def _gla_level_exponents(b, log_a, right_ref):
    row = lax.broadcasted_iota(jnp.int32, (CHUNK, 1), 0)
    exps = []
    for li, n in enumerate(GLA_LEVELS):
        if n >= SUBLANES // 2:
            per_blk = max(2 * n // SUBLANES, 1)
            b4 = b.reshape(CHUNK // (per_blk * SUBLANES), per_blk, SUBLANES, MIX_W)
            r_reg, r_sub = divmod(n - 1, SUBLANES)
            src = b4[:, r_reg:r_reg + 1, r_sub:r_sub + 1, :]
            b_r = jnp.broadcast_to(src, b4.shape).reshape(CHUNK, MIX_W)
            exps.append(jnp.where(right_ref[li] > 0.0, b - b_r, b_r - b))
        elif n == 2:
            pos = row % 4
            nxt = pltpu.roll(log_a, CHUNK - 1, 0)
            prv = pltpu.roll(log_a, 1, 0)
            exps.append(jnp.where(pos == 0, nxt, jnp.where(pos == 1, 0.0,
                                                            jnp.where(pos == 2, log_a, log_a + prv))))
        else:
            exps.append(jnp.where(right_ref[li] > 0.0, log_a, 0.0))
    return exps


def _gla_kernel(y_ref, wa2_ref, ba_ref, g_ref, tri_ref, mask_ref, right_ref, o_ref, st_ref, *, ts, unroll):
    n_lv = len(GLA_LEVELS)

    @pl.when(pl.program_id(1) == 0)
    def _():
        st_ref[...] = jnp.zeros_like(st_ref)

    def chunk(c, carry):
        rows = pl.ds(pl.multiple_of(c * CHUNK, CHUNK), CHUNK)
        q = y_ref[rows, 0:256] * (HEAD_D ** -0.5)
        k = y_ref[rows, 256:512]
        v = y_ref[rows, 512:768].astype(BF16)
        r_gate = y_ref[rows, 768:1024]
        a_lr = y_ref[rows, 1024:1152]
        z = _dot(a_lr.astype(BF16), wa2_ref[...]) + ba_ref[...]
        log_a = _log_sigmoid(z) * (1.0 / GLA_TAU)
        b = _cumsum_rows(tri_ref[...], log_a)
        b_end = b[CHUNK - 1:CHUNK, :]
        scores = [jnp.zeros((CHUNK, CHUNK), F32) for _ in range(MIX_HEADS)]
        for li, e in enumerate(_gla_level_exponents(b, log_a, right_ref)):
            x = (jnp.where(right_ref[li] > 0.0, q, k) * jnp.exp(e)).astype(BF16)
            for h in range(MIX_HEADS):
                xh = x[:, h * HEAD_D:(h + 1) * HEAD_D]
                scores[h] = scores[h] + _dot_nt(xh, xh) * mask_ref[li]
        qb = q.astype(BF16)
        kb = k.astype(BF16)
        q_in = (q * jnp.exp(b)).astype(BF16)
        k_out = (k * jnp.exp(b_end - b)).astype(BF16)
        dec_end = jnp.exp(b_end)
        outs = []
        for h in range(MIX_HEADS):
            sl = slice(h * HEAD_D, (h + 1) * HEAD_D)
            a = scores[h] + _dot_nt(qb[:, sl], kb[:, sl]) * mask_ref[n_lv]
            st = st_ref[h]
            o = _dot(a.astype(BF16), v[:, sl]) + _dot_nt(q_in[:, sl], st.astype(BF16))
            st_ref[h] = st * dec_end[:, sl] + _dot_tn(v[:, sl], k_out[:, sl])
            outs.append(_head_norm(o, g_ref[:, sl]))
        o_all = jnp.concatenate(outs, axis=-1)
        o_ref[rows, :] = (o_all * (r_gate * jax.nn.sigmoid(r_gate))).astype(o_ref.dtype)
        return carry

    lax.fori_loop(0, ts // CHUNK, chunk, 0, unroll=unroll)


def _gla(yg, wa2, ba, g, batch, seq, ts):
    n = yg.shape[0]
    nt = seq // ts
    masks, right = _gla_constants()
    tri = jnp.asarray(np.tril(np.ones((CHUNK, CHUNK), np.float32)), BF16)
    return pl.pallas_call(
        functools.partial(_gla_kernel, ts=ts, unroll=MIXER_UNROLL),
        grid=(batch, nt),
        in_specs=[pl.BlockSpec((ts, SEG_W), lambda b, i: (b * nt + i, 0)),
                  _full(wa2.shape), _full(ba.shape), _full(g.shape),
                  _full(tri.shape), _full(masks.shape), _full(right.shape)],
        out_specs=pl.BlockSpec((ts, MIX_W), lambda b, i: (b * nt + i, 0)),
        out_shape=jax.ShapeDtypeStruct((n, MIX_W), BF16),
        scratch_shapes=[pltpu.VMEM((MIX_HEADS, HEAD_D, HEAD_D), F32)],
        compiler_params=_params(("parallel", "arbitrary")),
        name="gla_mixer",
    )(yg, wa2, ba, g, tri, jnp.asarray(masks), jnp.asarray(right))


def _mlstm_kernel(y_ref, gr_ref, cw_ref, bcol_ref, brow_ref, g_ref, tri_ref, eb_ref, eye_ref, o_ref,
                  xe_ref, qk_ref, vx_ref, fcb_ref, wsb_ref, rv_ref, dec_ref, c_ref, m_ref, *, ts):
    first = pl.program_id(1) == 0

    @pl.when(first)
    def _():
        xe_ref[0:8, :] = jnp.zeros((8, 2 * MIX_W), F32)
        c_ref[...] = jnp.zeros_like(c_ref)
        m_ref[...] = jnp.zeros_like(m_ref)

    @pl.when(jnp.logical_not(first))
    def _():
        xe_ref[0:8, :] = xe_ref[ts:ts + 8, :]

    xe_ref[8:ts + 8, :] = y_ref[:, 0:2 * MIX_W]
    conv = cw_ref[MLSTM_CONV - 1:MLSTM_CONV, :] * xe_ref[8:ts + 8, :]
    for j in range(MLSTM_CONV - 1):
        conv = conv + cw_ref[j:j + 1, :] * xe_ref[pl.ds(8 - (MLSTM_CONV - 1) + j, ts), :]
    qk_ref[...] = conv * jax.nn.sigmoid(conv)

    ones_col = (lax.broadcasted_iota(jnp.int32, (ts, HEAD_D), 1) == 0).astype(BF16)
    for h in range(MIX_HEADS):
        v_at, one_at = (0, HEAD_D) if h % 2 == 0 else (HEAD_D, 0)
        vx_ref[:, h * LANES + v_at:h * LANES + v_at + HEAD_D] = (
            y_ref[:, 512 + h * HEAD_D:512 + (h + 1) * HEAD_D].astype(BF16))
        vx_ref[:, h * LANES + one_at:h * LANES + one_at + HEAD_D] = ones_col

    tri = tri_ref[...]
    expand = eb_ref[...]
    nck = ts // CHUNK
    lane = lax.broadcasted_iota(jnp.int32, (1, LANES), 1)
    f_lanes = (lane >= MIX_HEADS) & (lane < 2 * MIX_HEADS)
    head_lane = (lax.broadcasted_iota(jnp.int32, (MIX_HEADS, LANES), 1)
                 == lax.broadcasted_iota(jnp.int32, (MIX_HEADS, LANES), 0) + MIX_HEADS)

    def to_lanes(col):
        return jnp.sum(jnp.where(head_lane, col, 0.0), 0, keepdims=True)

    m_col = m_ref[0:MIX_HEADS, 0:1]
    for c in range(nck):
        g_col = y_ref[c * CHUNK:(c + 1) * CHUNK, 1024:1152] + bcol_ref[...]
        fcum_col = jnp.where(f_lanes, _cumsum_rows(tri, _log_sigmoid(g_col)), 0.0)
        fcb_ref[c] = sum(_dot(part, expand) for part in _split3(fcum_col))
        g_row = gr_ref[c] + brow_ref[...]
        fcum_row = sum(_dot_nt(part, tri) for part in _split3(_log_sigmoid(g_row)))
        f_row = fcum_row[MIX_HEADS:2 * MIX_HEADS, :]
        i_row = g_row[0:MIX_HEADS, :]
        f_end = f_row[:, CHUNK - 1:CHUNK]
        rv_ref[c, 0:MIX_HEADS, :] = jnp.concatenate(
            [i_row - f_row, jnp.broadcast_to(m_col, (MIX_HEADS, HEAD_D))], axis=1)
        m_new = jnp.maximum(f_end + m_col, jnp.max(f_end - f_row + i_row, -1, keepdims=True))
        m_new_l = to_lanes(m_new)
        f_end_l = fcum_col[CHUNK - 1:CHUNK, :]
        i_shift = pltpu.roll(g_col, MIX_HEADS, 1)
        w_s = jnp.where(f_lanes, jnp.exp(f_end_l - fcum_col + i_shift - m_new_l), 0.0)
        hi, mid, _ = _split3(w_s)
        wsb_ref[c] = _dot(hi, expand) + _dot(mid, expand)
        dec_ref[c] = jnp.broadcast_to(jnp.exp(f_end_l + to_lanes(m_col) - m_new_l), (SUBLANES, LANES))
        m_col = m_new
    m_ref[0:MIX_HEADS, :] = jnp.broadcast_to(m_col, (MIX_HEADS, LANES))

    t_idx = lax.broadcasted_iota(jnp.int32, (CHUNK, LANES), 0)
    s_idx = lax.broadcasted_iota(jnp.int32, (CHUNK, LANES), 1)
    keep = (s_idx <= t_idx) | (s_idx >= CHUNK)
    low_half = s_idx < HEAD_D

    def chunk(c, carry):
        rows = pl.ds(pl.multiple_of(c * CHUNK, CHUNK), CHUNK)
        fcb = fcb_ref[c]
        wsb = wsb_ref[c]
        rv = rv_ref[c, 0:MIX_HEADS, :]
        dec_l = dec_ref[c]
        for p in range(MIX_HEADS // 2):
            pl_ = slice(p * LANES, (p + 1) * LANES)
            q2 = (qk_ref[rows, pl_] * (HEAD_D ** -0.5)).astype(BF16)
            k2 = qk_ref[rows, MIX_W + p * LANES:MIX_W + (p + 1) * LANES]
            pair = jnp.zeros((CHUNK, LANES), F32)
            for h in (2 * p, 2 * p + 1):
                mine = low_half if h % 2 == 0 else jnp.logical_not(low_half)
                hb = slice(h * LANES, (h + 1) * LANES)
                vx = vx_ref[rows, hb]
                logw = jnp.where(keep, fcb[:, hb] + rv[h:h + 1, :], -jnp.inf)
                m_t = jnp.max(logw, -1, keepdims=True)
                k_h = jnp.where(mine, k2, 0.0)
                qk = _dot_nt(q2, jnp.concatenate([k_h.astype(BF16), eye_ref[h % 2]], axis=0))
                w = (jnp.exp(logw - m_t) * qk).astype(BF16)
                c_st = c_ref[h]
                num = _dot(w, jnp.concatenate([vx, c_st.astype(BF16)], axis=0))
                den = num[:, HEAD_D:HEAD_D + 1] if h % 2 == 0 else num[:, 0:1]
                r = 1.0 / jnp.maximum(jnp.abs(den), jnp.exp(-m_t))
                mu = jnp.sum(jnp.where(mine, num, 0.0), -1, keepdims=True) * (1.0 / HEAD_D)
                cen = jnp.where(mine, num - mu, 0.0)
                var = jnp.sum(cen * cen, -1, keepdims=True) * (1.0 / HEAD_D)
                pair = pair + cen * (r * lax.rsqrt(r * r * var + LN_EPS))
                upd = _dot_tn((k_h * wsb[:, hb]).astype(BF16), vx)
                off = (h % 2) * HEAD_D
                c_ref[h] = dec_l[0:1, MIX_HEADS + h:MIX_HEADS + h + 1] * c_st + upd[off:off + HEAD_D, :]
            o_gate = y_ref[rows, 768 + p * LANES:768 + (p + 1) * LANES]
            o_ref[rows, pl_] = (pair * g_ref[:, pl_] * jax.nn.sigmoid(o_gate)).astype(o_ref.dtype)
        return carry

    lax.fori_loop(0, nck, chunk, 0, unroll=MIXER_UNROLL)


def _mlstm(ym, gates_row, conv_w, b_col, b_row, g, batch, seq, ts):
    n = ym.shape[0]
    nt = seq // ts
    nck = ts // CHUNK
    tri = jnp.asarray(np.tril(np.ones((CHUNK, CHUNK), np.float32)), BF16)
    expand = np.zeros((LANES, MIX_HEADS * LANES), np.float32)
    for h in range(MIX_HEADS):
        expand[MIX_HEADS + h, h * LANES:(h + 1) * LANES] = 1.0
    eye = np.zeros((2, HEAD_D, LANES), np.float32)
    eye[0, :, 0:HEAD_D] = np.eye(HEAD_D)
    eye[1, :, HEAD_D:] = np.eye(HEAD_D)
    expand = jnp.asarray(expand, BF16)
    eye = jnp.asarray(eye, BF16)
    return pl.pallas_call(
        functools.partial(_mlstm_kernel, ts=ts),
        grid=(batch, nt),
        in_specs=[pl.BlockSpec((ts, SEG_W), lambda b, i: (b * nt + i, 0)),
                  pl.BlockSpec((nck, 8, CHUNK), lambda b, i: (b * nt + i, 0, 0)),
                  _full(conv_w.shape), _full(b_col.shape), _full(b_row.shape), _full(g.shape),
                  _full(tri.shape), _full(expand.shape), _full(eye.shape)],
        out_specs=pl.BlockSpec((ts, MIX_W), lambda b, i: (b * nt + i, 0)),
        out_shape=jax.ShapeDtypeStruct((n, MIX_W), BF16),
        scratch_shapes=[pltpu.VMEM((ts + 8, 2 * MIX_W), F32),
                        pltpu.VMEM((ts, 2 * MIX_W), F32),
                        pltpu.VMEM((ts, MIX_HEADS * LANES), BF16),
                        pltpu.VMEM((nck, CHUNK, MIX_HEADS * LANES), F32),
                        pltpu.VMEM((nck, CHUNK, MIX_HEADS * LANES), F32),
                        pltpu.VMEM((nck, SUBLANES, LANES), F32),
                        pltpu.VMEM((nck, SUBLANES, LANES), F32),
                        pltpu.VMEM((MIX_HEADS, HEAD_D, LANES), F32),
                        pltpu.VMEM((SUBLANES, LANES), F32)],
        compiler_params=_params(("parallel", "arbitrary")),
        name="mlstm_mixer",
    )(ym, gates_row, conv_w, b_col, b_row, g, tri, expand, eye)


def _rope_table_kernel(pos_ref, inv_ref, cos_ref, sin_ref):
    ang = pos_ref[...].astype(F32) * inv_ref[...]
    lane = lax.broadcasted_iota(jnp.int32, ang.shape, 1)
    rot = (lane >= MLA_NOPE) & (lane < MLA_NOPE + MLA_ROPE)
    first_half = lane < MLA_NOPE + MLA_ROPE // 2
    cos_ref[...] = jnp.where(lane < MLA_NOPE, 1.0, jnp.where(rot, jnp.cos(ang), 0.0))
    s = jnp.sin(ang)
    sin_ref[...] = jnp.where(rot, jnp.where(first_half, -s, s), 0.0)


def _rope_tables(pos_col, tm):
    n = pos_col.shape[0]
    half = MLA_ROPE // 2
    inv = ROPE_BASE ** (-np.arange(half, dtype=np.float32) / half)
    inv_row = np.zeros((1, LANES), np.float32)
    inv_row[0, MLA_NOPE:MLA_NOPE + half] = inv
    inv_row[0, MLA_NOPE + half:MLA_NOPE + MLA_ROPE] = inv
    return pl.pallas_call(
        _rope_table_kernel,
        grid=(n // tm,),
        in_specs=[pl.BlockSpec((tm, 1), lambda i: (i, 0)), _full((1, LANES))],
        out_specs=[pl.BlockSpec((tm, LANES), lambda i: (i, 0))] * 2,
        out_shape=[jax.ShapeDtypeStruct((n, LANES), F32)] * 2,
        compiler_params=_params(("parallel",)),
        name="rope_tables",
    )(pos_col, jnp.asarray(inv_row))


def _mla_prep_kernel(y_ref, cos_ref, sin_ref, gq_ref, gkv_ref, wqa_ref, wqb_ref, wkn_ref, wv_ref,
                     q_ref, k_ref, v_ref):
    def rms(x, g):
        return x * lax.rsqrt(jnp.mean(x * x, -1, keepdims=True) + LN_EPS) * g

    cos = cos_ref[...]
    sin = sin_ref[...]
    cq = rms(y_ref[:, 0:MLA_Q_RANK], gq_ref[...]).astype(BF16)
    ckv = rms(y_ref[:, MLA_Q_RANK:MLA_Q_RANK + MLA_KV_RANK], gkv_ref[...]).astype(BF16)
    k_rope = y_ref[:, 384:512] * cos + y_ref[:, 512:640] * sin
    qa = _dot(cq, wqa_ref[...])
    qb = _dot(cq, wqb_ref[...])
    kn = _dot(ckv, wkn_ref[...])
    scale = (MLA_NOPE + MLA_ROPE) ** -0.5 * LOG2_E
    for h in range(MLA_HEADS):
        sl = slice(h * MLA_QK_PAD, (h + 1) * MLA_QK_PAD)
        q_ref[:, sl] = ((qa[:, sl] * cos + qb[:, sl] * sin) * scale).astype(q_ref.dtype)
        k_ref[:, sl] = (kn[:, sl] + k_rope).astype(k_ref.dtype)
    v_ref[...] = _dot(ckv, wv_ref[...]).astype(v_ref.dtype)


def _mla_prep(yc, cos_t, sin_t, gq, gkv, wqa, wqb, wkn, wv, tm):
    n = yc.shape[0]
    row = lambda w: pl.BlockSpec((tm, w), lambda i: (i, 0))
    qk_w = MLA_HEADS * MLA_QK_PAD
    return pl.pallas_call(
        _mla_prep_kernel,
        grid=(n // tm,),
        in_specs=[row(MLA_SEG_W), row(LANES), row(LANES), _full(gq.shape), _full(gkv.shape),
                  _full(wqa.shape), _full(wqb.shape), _full(wkn.shape), _full(wv.shape)],
        out_specs=[row(qk_w), row(qk_w), row(MLA_HEADS * MLA_V)],
        out_shape=[jax.ShapeDtypeStruct((n, qk_w), BF16), jax.ShapeDtypeStruct((n, qk_w), BF16),
                   jax.ShapeDtypeStruct((n, MLA_HEADS * MLA_V), BF16)],
        compiler_params=_params(("parallel",)),
        name="mla_prep",
    )(yc, cos_t, sin_t, gq, gkv, wqa, wqb, wkn, wv)


def _mla_attn_kernel(q_ref, k_ref, v_ref, o_ref, *, seq, tq):
    t_chunk = lax.broadcasted_iota(jnp.int32, (tq, tq), 0) // CHUNK
    s_chunk = lax.broadcasted_iota(jnp.int32, (tq, tq), 1) // CHUNK
    diag_mask = s_chunk <= t_chunk
    for hh in range(MLA_HEAD_PAIR):
        ql = slice(hh * MLA_QK_PAD, (hh + 1) * MLA_QK_PAD)
        vl = slice(hh * MLA_V, (hh + 1) * MLA_V)
        for i in range(seq // tq):
            rows = slice(i * tq, (i + 1) * tq)
            q = q_ref[0, rows, ql]
            s_d = jnp.where(diag_mask, _dot_nt(q, k_ref[0, rows, ql]), -jnp.inf)
            m = jnp.max(s_d, -1, keepdims=True)
            if i > 0:
                s_o = _dot_nt(q, k_ref[0, 0:i * tq, ql])
                m = jnp.maximum(m, jnp.max(s_o, -1, keepdims=True))
            p_d = jnp.exp2(s_d - m)
            l = jnp.sum(p_d, -1, keepdims=True)
            o = _dot(p_d.astype(BF16), v_ref[0, rows, vl])
            if i > 0:
                p_o = jnp.exp2(s_o - m)
                l = l + jnp.sum(p_o, -1, keepdims=True)
                o = o + _dot(p_o.astype(BF16), v_ref[0, 0:i * tq, vl])
            o_ref[0, rows, vl] = (o / l).astype(o_ref.dtype)


def _mla_attn(q, k, v, batch, seq, tq):
    n = q.shape[0]
    qk_w = MLA_HEADS * MLA_QK_PAD
    v_w = MLA_HEADS * MLA_V
    pair = lambda w: pl.BlockSpec((1, seq, MLA_HEAD_PAIR * w), lambda b, h: (b, 0, h))
    out = pl.pallas_call(
        functools.partial(_mla_attn_kernel, seq=seq, tq=tq),
        grid=(batch, MLA_HEADS // MLA_HEAD_PAIR),
        in_specs=[pair(MLA_QK_PAD), pair(MLA_QK_PAD), pair(MLA_V)],
        out_specs=pair(MLA_V),
        out_shape=jax.ShapeDtypeStruct((batch, seq, v_w), BF16),
        compiler_params=_params(("parallel", "parallel")),
        name="mla_attention",
    )(q.reshape(batch, seq, qk_w), k.reshape(batch, seq, qk_w), v.reshape(batch, seq, v_w))
    return out.reshape(n, v_w)


def _out_proj_kernel(og_ref, om_ref, oc_ref, x_ref, wg_ref, wm_ref, wc_ref, g_ref, b_ref, o_ref):
    mix = _dot(og_ref[...], wg_ref[...]) + _dot(om_ref[...], wm_ref[...]) + _dot(oc_ref[...], wc_ref[...])
    o_ref[...] = _layer_norm(ALPHA * x_ref[...] + mix, g_ref[...], b_ref[...])


def _out_proj(og, om, oc, x2d, wg, wm, wc, g, b, tm):
    n = x2d.shape[0]
    row = lambda w: pl.BlockSpec((tm, w), lambda i: (i, 0))
    return pl.pallas_call(
        _out_proj_kernel,
        grid=(n // tm,),
        in_specs=[row(MIX_W), row(MIX_W), row(MLA_HEADS * MLA_V), row(D_MODEL),
                  _full(wg.shape), _full(wm.shape), _full(wc.shape), _full(g.shape), _full(b.shape)],
        out_specs=row(D_MODEL),
        out_shape=jax.ShapeDtypeStruct((n, D_MODEL), F32),
        compiler_params=_params(("parallel",)),
        name="out_proj_ln1",
    )(og, om, oc, x2d, wg, wm, wc, g, b)


def _xa_kv_kernel(mem_ref, w_ref, k_ref, v_ref):
    kv = _dot(mem_ref[...].astype(BF16), w_ref[...])
    k_ref[...] = kv[:, 0:D_MODEL].astype(k_ref.dtype)
    v_ref[...] = kv[:, D_MODEL:2 * D_MODEL].astype(v_ref.dtype)


def _xa_kv(mem2d, w_kv, mem_len):
    n = mem2d.shape[0]
    row = pl.BlockSpec((mem_len, D_MODEL), lambda i: (i, 0))
    return pl.pallas_call(
        _xa_kv_kernel,
        grid=(n // mem_len,),
        in_specs=[row, _full(w_kv.shape)],
        out_specs=[row, row],
        out_shape=[jax.ShapeDtypeStruct((n, D_MODEL), BF16)] * 2,
        compiler_params=_params(("parallel",)),
        name="xattn_kv",
    )(mem2d, w_kv)


def _xattn_kernel(x_ref, k_ref, v_ref, wq_ref, wo_ref, g_ref, b_ref, o_ref):
    x = x_ref[...]
    q = (_dot(x.astype(BF16), wq_ref[...]) * (XA_DH ** -0.5)).astype(BF16)
    out = jnp.zeros(x.shape, F32)
    for h in range(XA_HEADS):
        sl = slice(h * XA_DH, (h + 1) * XA_DH)
        s = _dot_nt(q[:, sl], k_ref[:, sl])
        p = jnp.exp(s - jnp.max(s, -1, keepdims=True))
        p = p / jnp.sum(p, -1, keepdims=True)
        o = _dot(p.astype(BF16), v_ref[:, sl])
        out = out + _dot(o.astype(BF16), wo_ref[sl, :])
    o_ref[...] = _layer_norm(ALPHA * x + out, g_ref[...], b_ref[...])


def _xattn(x1, xk, xv, wq, wo, g, b, batch, seq, mem_len, tm):
    n = x1.shape[0]
    nt = seq // tm
    row = pl.BlockSpec((tm, D_MODEL), lambda bi, i: (bi * nt + i, 0))
    mem = pl.BlockSpec((mem_len, D_MODEL), lambda bi, i: (bi, 0))
    return pl.pallas_call(
        _xattn_kernel,
        grid=(batch, nt),
        in_specs=[row, mem, mem, _full(wq.shape), _full(wo.shape), _full(g.shape), _full(b.shape)],
        out_specs=row,
        out_shape=jax.ShapeDtypeStruct((n, D_MODEL), F32),
        compiler_params=_params(("parallel", "parallel")),
        name="xattn_ln2",
    )(x1, xk, xv, wq, wo, g, b)


def _route_kernel(x_ref, w_ref, b_ref, tri_ref, xg_ref, rank_ref, grp_ref, cnt_out_ref, cnt_ref):
    @pl.when(pl.program_id(0) == 0)
    def _():
        cnt_ref[...] = jnp.zeros_like(cnt_ref)

    x = x_ref[...]
    logits = jnp.dot(x, w_ref[...], precision=HIGHEST, preferred_element_type=F32) + b_ref[...]
    lane = lax.broadcasted_iota(jnp.int32, logits.shape, 1).astype(F32)
    is_group = (lane >= N_EXPERTS) & (lane < N_EXPERTS + N_GROUPS)
    g_max = jnp.max(jnp.where(is_group, logits, -jnp.inf), -1, keepdims=True)
    g_sum = jnp.sum(jnp.where(is_group, jnp.exp(logits - g_max), 0.0), -1, keepdims=True)
    g_p = 1.0 / g_sum
    g_idx = jnp.min(jnp.where(is_group & (logits == g_max), lane - N_EXPERTS, float(LANES)), -1, keepdims=True)
    in_group = (lane < N_EXPERTS) & (jnp.floor(lane * (1.0 / EXPERTS_PER_GROUP)) == g_idx)
    e_max = jnp.max(jnp.where(in_group, logits, -jnp.inf), -1, keepdims=True)
    e_exp = jnp.where(in_group, jnp.exp(logits - e_max), 0.0)
    prob = e_exp / jnp.sum(e_exp, -1, keepdims=True)
    cand = jnp.where(in_group, prob, -1.0)
    p1 = jnp.max(cand, -1, keepdims=True)
    i1 = jnp.min(jnp.where(cand == p1, lane, float(LANES)), -1, keepdims=True)
    cand2 = jnp.where(lane == i1, -1.0, cand)
    p2 = jnp.max(cand2, -1, keepdims=True)
    i2 = jnp.min(jnp.where(cand2 == p2, lane, float(LANES)), -1, keepdims=True)
    p_sum = p1 + p2
    xg_ref[:, 0:D_MODEL] = x
    xg_ref[:, D_MODEL:] = (jnp.where(lane == i1, g_p * (p1 / p_sum), 0.0)
                           + jnp.where(lane == i2, g_p * (p2 / p_sum), 0.0))
    onehot = jnp.where(lane == g_idx, 1.0, 0.0)
    before = _dot(tri_ref[...], onehot.astype(BF16)) + cnt_ref[...]
    rank_ref[...] = jnp.sum(onehot * before, -1, keepdims=True).astype(jnp.int32)
    grp_ref[...] = g_idx.astype(jnp.int32)
    cnt_ref[...] += jnp.sum(onehot, 0, keepdims=True)
    cnt_out_ref[...] = cnt_ref[...]


def _route(x2, w_route, b_route, tm):
    n = x2.shape[0]
    tri = jnp.asarray(np.tril(np.ones((tm, tm), np.float32), -1), BF16)
    col = pl.BlockSpec((tm, 1), lambda i: (i, 0))
    return pl.pallas_call(
        _route_kernel,
        grid=(n // tm,),
        in_specs=[pl.BlockSpec((tm, D_MODEL), lambda i: (i, 0)), _full(w_route.shape), _full(b_route.shape),
                  _full(tri.shape)],
        out_specs=[pl.BlockSpec((tm, XG_W), lambda i: (i, 0)), col, col, _full((1, LANES))],
        out_shape=[jax.ShapeDtypeStruct((n, XG_W), F32), jax.ShapeDtypeStruct((n, 1), jnp.int32),
                   jax.ShapeDtypeStruct((n, 1), jnp.int32), jax.ShapeDtypeStruct((1, LANES), F32)],
        scratch_shapes=[pltpu.VMEM((1, LANES), F32)],
        compiler_params=_params(("arbitrary",)),
        name="moe_route",
    )(x2, w_route, b_route, tri)


def _dispatch_kernel(pos_ref, x_ref, xs_hbm, sem, *, tm):
    def issue(t, carry):
        pltpu.make_async_copy(x_ref.at[pl.ds(t, 1)], xs_hbm.at[pl.ds(pos_ref[t], 1)], sem).start()
        return carry

    lax.fori_loop(0, tm, issue, 0, unroll=DMA_UNROLL)
    pltpu.make_async_copy(x_ref, xs_hbm.at[pl.ds(0, tm)], sem).wait()


def _dispatch(xg, pos, tm):
    n = xg.shape[0]
    return pl.pallas_call(
        functools.partial(_dispatch_kernel, tm=tm),
        grid=(n // tm,),
        in_specs=[pl.BlockSpec((tm,), lambda i: (i,), memory_space=pltpu.SMEM),
                  pl.BlockSpec((tm, XG_W), lambda i: (i, 0))],
        out_specs=pl.BlockSpec(memory_space=pl.ANY),
        out_shape=jax.ShapeDtypeStruct(xg.shape, xg.dtype),
        scratch_shapes=[pltpu.SemaphoreType.DMA(())],
        compiler_params=_params(("arbitrary",)),
        name="moe_dispatch",
    )(pos, xg)


def _moe_ffn_kernel(blk_ref, grp_ref, first_ref, valid_ref, xs_ref, wg_ref, wu_ref, wd_ref, o_ref, xb_ref):
    w = pl.program_id(0)
    e = pl.program_id(1)

    @pl.when((first_ref[w] == 1) & (e == 0))
    def _():
        xb_ref[...] = xs_ref[:, 0:D_MODEL].astype(BF16)
        o_ref[...] = jnp.zeros_like(o_ref)

    @pl.when(valid_ref[w] == 1)
    def _():
        xb = xb_ref[...]
        gates = xs_ref[:, D_MODEL:]
        lane = lax.broadcasted_iota(jnp.int32, gates.shape, 1)
        w_tok = jnp.sum(jnp.where(lane == grp_ref[w] * EXPERTS_PER_GROUP + e, gates, 0.0), -1, keepdims=True)
        hg = _dot(xb, wg_ref[0].astype(BF16))
        hu = _dot(xb, wu_ref[0].astype(BF16))
        hid = hg * jax.nn.sigmoid(hg) * hu * w_tok
        o_ref[...] += _dot(hid.astype(BF16), wd_ref[0].astype(BF16))


def _moe_ffn(xs, items, w_gate, w_up, w_down, rb, e_base):
    n = xs.shape[0]
    blk, grp, first, valid = items

    def expert(w, e, blk, grp, first, valid):
        return (e_base + grp[w] * EXPERTS_PER_GROUP + jnp.where(valid[w] == 1, e, EXPERTS_PER_GROUP - 1), 0, 0)

    grid_spec = pltpu.PrefetchScalarGridSpec(
        num_scalar_prefetch=4,
        grid=(blk.shape[0], EXPERTS_PER_GROUP),
        in_specs=[pl.BlockSpec((rb, XG_W), lambda w, e, blk, grp, first, valid: (blk[w], 0)),
                  pl.BlockSpec((1, D_MODEL, D_EXPERT), expert),
                  pl.BlockSpec((1, D_MODEL, D_EXPERT), expert),
                  pl.BlockSpec((1, D_EXPERT, D_MODEL), expert)],
        out_specs=pl.BlockSpec((rb, D_MODEL), lambda w, e, blk, grp, first, valid: (blk[w], 0)),
        scratch_shapes=[pltpu.VMEM((rb, D_MODEL), BF16)],
    )
    return pl.pallas_call(
        _moe_ffn_kernel,
        grid_spec=grid_spec,
        out_shape=jax.ShapeDtypeStruct((n, D_MODEL), F32),
        compiler_params=_params(("arbitrary", "arbitrary")),
        name="moe_experts",
    )(blk, grp, first, valid, xs, w_gate, w_up, w_down)


def _combine_kernel(pos_ref, xg_ref, ys_hbm, g_ref, b_ref, o_ref, buf_ref, sem, *, tm):
    def issue(t, carry):
        pltpu.make_async_copy(ys_hbm.at[pl.ds(pos_ref[t], 1)], buf_ref.at[pl.ds(t, 1)], sem).start()
        return carry

    lax.fori_loop(0, tm, issue, 0, unroll=DMA_UNROLL)
    pltpu.make_async_copy(ys_hbm.at[pl.ds(0, tm)], buf_ref, sem).wait()
    o_ref[...] = _layer_norm(ALPHA * xg_ref[:, 0:D_MODEL] + buf_ref[...], g_ref[...], b_ref[...])


def _combine(pos, xg, ys, g, b, tm):
    n = xg.shape[0]
    return pl.pallas_call(
        functools.partial(_combine_kernel, tm=tm),
        grid=(n // tm,),
        in_specs=[pl.BlockSpec((tm,), lambda i: (i,), memory_space=pltpu.SMEM),
                  pl.BlockSpec((tm, XG_W), lambda i: (i, 0)),
                  pl.BlockSpec(memory_space=pl.ANY), _full(g.shape), _full(b.shape)],
        out_specs=pl.BlockSpec((tm, D_MODEL), lambda i: (i, 0)),
        out_shape=jax.ShapeDtypeStruct((n, D_MODEL), F32),
        scratch_shapes=[pltpu.VMEM((tm, D_MODEL), F32), pltpu.SemaphoreType.DMA(())],
        compiler_params=_params(("arbitrary",)),
        name="moe_combine_ln3",
    )(pos, xg, ys, g, b)


def _moe_work_items(counts, n, rb):
    nb = n // rb
    n_items = nb + N_GROUPS - 1
    ends = jnp.cumsum(counts)
    start = jnp.arange(nb, dtype=jnp.int32) * rb
    g_lo = jnp.sum(ends[None, :] <= start[:, None], axis=1).astype(jnp.int32)
    g_hi = jnp.sum(ends[None, :] <= (start + rb - 1)[:, None], axis=1).astype(jnp.int32)
    per_blk = g_hi - g_lo + 1
    item0 = jnp.cumsum(per_blk) - per_blk
    w = jnp.arange(n_items, dtype=jnp.int32)
    valid = w < jnp.sum(per_blk)
    blk = jnp.clip(jnp.sum(item0[None, :] <= w[:, None], axis=1) - 1, 0, nb - 1).astype(jnp.int32)
    grp = jnp.where(valid, g_lo[blk] + (w - item0[blk]), g_hi[nb - 1]).astype(jnp.int32)
    first = (valid & (w == item0[blk])).astype(jnp.int32)
    return blk, grp, first, valid.astype(jnp.int32)


def _moe(x2, w_route, b_route, w_gate, w_up, w_down, e_base, g, b, tm, t_dma, rb):
    n = x2.shape[0]
    xg, rank, grp, counts = _route(x2, w_route, b_route, tm)
    counts = counts[0, :N_GROUPS].astype(jnp.int32)
    offsets = jnp.cumsum(counts) - counts
    pos = (offsets[grp[:, 0]] + rank[:, 0]).astype(jnp.int32)
    xs = _dispatch(xg, pos, t_dma)
    ys = _moe_ffn(xs, _moe_work_items(counts, n, rb), w_gate, w_up, w_down, rb, e_base)
    return _combine(pos, xg, ys, g, b, t_dma)


def _pad_cols(w, width):
    return jnp.pad(w, ((0, 0), (0, width - w.shape[1])))


def _layer_weights(w_in, w_out, gla_w_a2, ml_b_i, ml_b_f, mla_w_uq, mla_w_ukv, moe_w_group, moe_b_group,
                   moe_w_router, moe_b_router):
    o = _IN_OFF
    half = MLA_ROPE // 2
    wg = _pad_cols(w_in[:, o[0]:o[5]], SEG_W).astype(BF16)
    wm = _pad_cols(w_in[:, o[5]:o[10]], SEG_W).astype(BF16)
    kr = w_in[:, o[12]:o[13]]
    zeros = lambda w: jnp.zeros((D_MODEL, w), F32)
    kra = jnp.concatenate([zeros(MLA_NOPE), kr, zeros(LANES - MLA_NOPE - MLA_ROPE)], 1)
    krb = jnp.concatenate([zeros(MLA_NOPE), kr[:, half:], kr[:, :half], zeros(LANES - MLA_NOPE - MLA_ROPE)], 1)
    wc = jnp.concatenate([w_in[:, o[10]:o[12]], kra, krb], 1).astype(BF16)
    wift = w_in[:, o[8]:o[10]].T.astype(BF16)
    wa2 = jnp.pad(gla_w_a2, ((0, LANES - GLA_GATE_RANK), (0, 0))).astype(BF16)
    b_gate = jnp.concatenate([ml_b_i, ml_b_f])
    b_col = jnp.pad(b_gate, (0, LANES - 2 * MIX_HEADS)).reshape(1, LANES)
    b_row = b_gate.reshape(2 * MIX_HEADS, 1)
    uq = mla_w_uq.reshape(MLA_Q_RANK, MLA_HEADS, MLA_NOPE + MLA_ROPE)
    zq = jnp.zeros((MLA_Q_RANK, MLA_HEADS, LANES - MLA_NOPE - MLA_ROPE), F32)
    wqa = jnp.concatenate([uq, zq], -1).reshape(MLA_Q_RANK, -1).astype(BF16)
    wqb = jnp.concatenate([jnp.zeros((MLA_Q_RANK, MLA_HEADS, MLA_NOPE), F32), uq[..., MLA_NOPE + half:],
                           uq[..., MLA_NOPE:MLA_NOPE + half], zq], -1).reshape(MLA_Q_RANK, -1).astype(BF16)
    ukv = mla_w_ukv.reshape(MLA_KV_RANK, MLA_HEADS, MLA_NOPE + MLA_V)
    wkn = jnp.concatenate([ukv[..., :MLA_NOPE], jnp.zeros((MLA_KV_RANK, MLA_HEADS, LANES - MLA_NOPE), F32)],
                          -1).reshape(MLA_KV_RANK, -1).astype(BF16)
    wv = ukv[..., MLA_NOPE:].reshape(MLA_KV_RANK, -1).astype(BF16)
    wo = w_out.astype(BF16)
    w_route = _pad_cols(jnp.concatenate([moe_w_router, moe_w_group], 1), LANES)
    b_route = jnp.pad(jnp.concatenate([moe_b_router, moe_b_group]), (0, LANES - N_EXPERTS - N_GROUPS)).reshape(1, LANES)
    return dict(wg=wg, wm=wm, wc=wc, wift=wift, wa2=wa2, b_col=b_col, b_row=b_row, wqa=wqa, wqb=wqb, wkn=wkn,
                wv=wv, wo_g=wo[0:MIX_W], wo_m=wo[MIX_W:2 * MIX_W], wo_c=wo[2 * MIX_W:], w_route=w_route,
                b_route=b_route)


def _tile(total, want):
    t = min(total, want)
    assert total % t == 0
    return t


def kernel(x, mem, positions, w_in, w_out, gla_w_a2, gla_b_a, gla_norm_g, ml_conv_w, ml_b_i, ml_b_f, ml_norm_g, mla_q_norm_g, mla_w_uq, mla_kv_norm_g, mla_w_ukv, xa_w_q, xa_w_kv, xa_w_o, moe_w_group, moe_b_group, moe_w_router, moe_b_router, moe_w_gate, moe_w_up, moe_w_down, ln1_g, ln1_b, ln2_g, ln2_b, ln3_g, ln3_b):
    batch, seq, _ = x.shape
    mem_len = mem.shape[1]
    n = batch * seq
    depth = w_in.shape[0]
    assert seq % CHUNK == 0
    tm = _tile(n, 512)
    ts = _tile(seq, 512)
    tq = _tile(seq, 256)
    t_moe = _tile(n, 1024)
    row = lambda a: a.reshape(1, -1)

    cos_t, sin_t = _rope_tables(positions.reshape(n, 1), _tile(n, 2048))
    mem2d = mem.reshape(batch * mem_len, D_MODEL)
    experts_gate = moe_w_gate.reshape(-1, D_MODEL, D_EXPERT)
    experts_up = moe_w_up.reshape(-1, D_MODEL, D_EXPERT)
    experts_down = moe_w_down.reshape(-1, D_EXPERT, D_MODEL)
    h = x.reshape(n, D_MODEL)
    for l in range(depth):
        w = _layer_weights(w_in[l], w_out[l], gla_w_a2[l], ml_b_i[l], ml_b_f[l], mla_w_uq[l], mla_w_ukv[l],
                           moe_w_group[l], moe_b_group[l], moe_w_router[l], moe_b_router[l])
        yg, ym, yc, yift = _in_proj(h, w["wg"], w["wm"], w["wc"], w["wift"], tm)
        gates_row = yift.reshape(2 * MIX_HEADS, n // CHUNK, CHUNK).transpose(1, 0, 2)
        og = _gla(yg, w["wa2"], row(gla_b_a[l]), row(gla_norm_g[l]), batch, seq, ts)
        om = _mlstm(ym, gates_row, ml_conv_w[l], w["b_col"], w["b_row"], row(ml_norm_g[l]), batch, seq, ts)
        q, k, v = _mla_prep(yc, cos_t, sin_t, row(mla_q_norm_g[l]), row(mla_kv_norm_g[l]),
                            w["wqa"], w["wqb"], w["wkn"], w["wv"], tm)
        oc = _mla_attn(q, k, v, batch, seq, tq)
        x1 = _out_proj(og, om, oc, h, w["wo_g"], w["wo_m"], w["wo_c"], row(ln1_g[l]), row(ln1_b[l]), tm)
        xk, xv = _xa_kv(mem2d, xa_w_kv[l].astype(BF16), mem_len)
        x2 = _xattn(x1, xk, xv, xa_w_q[l].astype(BF16), xa_w_o[l].astype(BF16), row(ln2_g[l]), row(ln2_b[l]),
                    batch, seq, mem_len, tm)
        h = _moe(x2, w["w_route"], w["b_route"], experts_gate, experts_up, experts_down, l * N_EXPERTS,
                 row(ln3_g[l]), row(ln3_b[l]), tm, t_moe, t_moe)
    return h.reshape(batch, seq, D_MODEL)
```

```python
import functools

import numpy as np
import jax
import jax.numpy as jnp
from jax import lax
from jax.experimental import pallas as pl
from jax.experimental.pallas import tpu as pltpu

F32 = jnp.float32
BF16 = jnp.bfloat16
HIGHEST = lax.Precision.HIGHEST

D_MODEL = 1024
CHUNK = 64
HEAD_D = 64
MIX_HEADS = 4
MIX_W = MIX_HEADS * HEAD_D
GLA_GATE_RANK = 16
GLA_TAU = 16.0
MLSTM_CONV = 4
MLA_HEADS = 8
MLA_NOPE = 64
MLA_ROPE = 32
MLA_V = 64
MLA_Q_RANK = 256
MLA_KV_RANK = 128
MLA_QK_PAD = 128
ROPE_BASE = 10000.0
LOG2_E = 1.4426950408889634
MLA_HEAD_PAIR = 2
XA_HEADS = 4
XA_DH = D_MODEL // XA_HEADS
N_GROUPS = 4
EXPERTS_PER_GROUP = 8
N_EXPERTS = N_GROUPS * EXPERTS_PER_GROUP
D_EXPERT = 256
DEPTH = 2
ALPHA = (2 * DEPTH) ** 0.25
LN_EPS = 1e-5
LANES = 128
SUBLANES = 8
MIXER_UNROLL = 4
SEG_W = 1152
MLA_SEG_W = 640
XG_W = D_MODEL + LANES
DMA_UNROLL = 8
GLA_LEVELS = (32, 16, 8, 4, 2, 1)
VMEM_LIMIT = 56 * 1024 * 1024

_IN_SIZES = (256, 256, 256, 256, GLA_GATE_RANK, 512, 256, 256, 4, 4, MLA_Q_RANK, MLA_KV_RANK, MLA_ROPE)
_IN_OFF = np.concatenate([[0], np.cumsum(_IN_SIZES)]).tolist()


def _params(sem):
    return pltpu.CompilerParams(dimension_semantics=sem, vmem_limit_bytes=VMEM_LIMIT)


def _full(shape):
    return pl.BlockSpec(shape, lambda *_: (0,) * len(shape))


def _layer_norm(x, g, b):
    mu = jnp.mean(x, -1, keepdims=True)
    xc = x - mu
    var = jnp.mean(xc * xc, -1, keepdims=True)
    return xc * lax.rsqrt(var + LN_EPS) * g + b


def _log_sigmoid(z):
    return jnp.minimum(z, 0.0) - jnp.log1p(jnp.exp(-jnp.abs(z)))


def _dot_nt(a, b):
    return lax.dot_general(a, b, (((1,), (1,)), ((), ())), preferred_element_type=F32)


def _dot_tn(a, b):
    return lax.dot_general(a, b, (((0,), (0,)), ((), ())), preferred_element_type=F32)


def _dot(a, b):
    return jnp.dot(a, b, preferred_element_type=F32)


def _head_norm(o, g):
    mu = jnp.mean(o, -1, keepdims=True)
    oc = o - mu
    var = jnp.mean(oc * oc, -1, keepdims=True)
    return oc * lax.rsqrt(var + LN_EPS) * g


def _in_proj_kernel(x_ref, wg_ref, wm_ref, wc_ref, wift_ref, yg_ref, ym_ref, yc_ref, yift_ref):
    xb = x_ref[...].astype(BF16)
    yg_ref[...] = _dot(xb, wg_ref[...])
    ym_ref[...] = _dot(xb, wm_ref[...])
    yc_ref[...] = _dot(xb, wc_ref[...])
    yift_ref[...] = _dot_nt(wift_ref[...], xb)


def _in_proj(x2d, wg, wm, wc, wift, tm):
    n = x2d.shape[0]
    row = lambda w: pl.BlockSpec((tm, w), lambda i: (i, 0))
    return pl.pallas_call(
        _in_proj_kernel,
        grid=(n // tm,),
        in_specs=[row(D_MODEL), _full(wg.shape), _full(wm.shape), _full(wc.shape), _full(wift.shape)],
        out_specs=[row(SEG_W), row(SEG_W), row(MLA_SEG_W), pl.BlockSpec((8, tm), lambda i: (0, i))],
        out_shape=[jax.ShapeDtypeStruct((n, SEG_W), F32), jax.ShapeDtypeStruct((n, SEG_W), F32),
                   jax.ShapeDtypeStruct((n, MLA_SEG_W), F32), jax.ShapeDtypeStruct((8, n), F32)],
        compiler_params=_params(("parallel",)),
        name="in_proj",
    )(x2d, wg, wm, wc, wift)


def _split3(x):
    hi = x.astype(BF16)
    r1 = x - hi.astype(F32)
    mid = r1.astype(BF16)
    lo = (r1 - mid.astype(F32)).astype(BF16)
    return hi, mid, lo


def _cumsum_rows(tri, x):
    hi, mid, lo = _split3(x)
    return _dot(tri, hi) + _dot(tri, mid) + _dot(tri, lo)


def _gla_constants():
    t = np.arange(CHUNK)
    n_lv = len(GLA_LEVELS)
    masks = np.zeros((n_lv + 1, CHUNK, CHUNK), np.float32)
    right = np.zeros((n_lv, CHUNK, 1), np.float32)
    for li, n in enumerate(GLA_LEVELS):
        blk = t // (2 * n)
        is_right = (t % (2 * n)) >= n
        masks[li] = ((blk[:, None] == blk[None, :]) & is_right[:, None] & ~is_right[None, :])
        right[li, :, 0] = is_right
    masks[n_lv] = np.eye(CHUNK)
    return masks, right


def _gla_level_exponents(b, log_a, right_ref):
    row = lax.broadcasted_iota(jnp.int32, (CHUNK, 1), 0)
    exps = []
    for li, n in enumerate(GLA_LEVELS):
        if n >= SUBLANES // 2:
            per_blk = max(2 * n // SUBLANES, 1)
            b4 = b.reshape(CHUNK // (per_blk * SUBLANES), per_blk, SUBLANES, MIX_W)
            r_reg, r_sub = divmod(n - 1, SUBLANES)
            src = b4[:, r_reg:r_reg + 1, r_sub:r_sub + 1, :]
            b_r = jnp.broadcast_to(src, b4.shape).reshape(CHUNK, MIX_W)
            exps.append(jnp.where(right_ref[li] > 0.0, b - b_r, b_r - b))
        elif n == 2:
            pos = row % 4
            nxt = pltpu.roll(log_a, CHUNK - 1, 0)
            prv = pltpu.roll(log_a, 1, 0)
            exps.append(jnp.where(pos == 0, nxt, jnp.where(pos == 1, 0.0,
                                                            jnp.where(pos == 2, log_a, log_a + prv))))
        else:
            exps.append(jnp.where(right_ref[li] > 0.0, log_a, 0.0))
    return exps


def _gla_kernel(y_ref, wa2_ref, ba_ref, g_ref, tri_ref, mask_ref, right_ref, o_ref, st_ref, *, ts, unroll):
    n_lv = len(GLA_LEVELS)

    @pl.when(pl.program_id(1) == 0)
    def _():
        st_ref[...] = jnp.zeros_like(st_ref)

    def chunk(c, carry):
        rows = pl.ds(pl.multiple_of(c * CHUNK, CHUNK), CHUNK)
        q = y_ref[rows, 0:256] * (HEAD_D ** -0.5)
        k = y_ref[rows, 256:512]
        v = y_ref[rows, 512:768].astype(BF16)
        r_gate = y_ref[rows, 768:1024]
        a_lr = y_ref[rows, 1024:1152]
        z = _dot(a_lr.astype(BF16), wa2_ref[...]) + ba_ref[...]
        log_a = _log_sigmoid(z) * (1.0 / GLA_TAU)
        b = _cumsum_rows(tri_ref[...], log_a)
        b_end = b[CHUNK - 1:CHUNK, :]
        scores = [jnp.zeros((CHUNK, CHUNK), F32) for _ in range(MIX_HEADS)]
        for li, e in enumerate(_gla_level_exponents(b, log_a, right_ref)):
            x = (jnp.where(right_ref[li] > 0.0, q, k) * jnp.exp(e)).astype(BF16)
            for h in range(MIX_HEADS):
                xh = x[:, h * HEAD_D:(h + 1) * HEAD_D]
                scores[h] = scores[h] + _dot_nt(xh, xh) * mask_ref[li]
        qb = q.astype(BF16)
        kb = k.astype(BF16)
        q_in = (q * jnp.exp(b)).astype(BF16)
        k_out = (k * jnp.exp(b_end - b)).astype(BF16)
        dec_end = jnp.exp(b_end)
        outs = []
        for h in range(MIX_HEADS):
            sl = slice(h * HEAD_D, (h + 1) * HEAD_D)
            a = scores[h] + _dot_nt(qb[:, sl], kb[:, sl]) * mask_ref[n_lv]
            st = st_ref[h]
            o = _dot(a.astype(BF16), v[:, sl]) + _dot_nt(q_in[:, sl], st.astype(BF16))
            st_ref[h] = st * dec_end[:, sl] + _dot_tn(v[:, sl], k_out[:, sl])
            outs.append(_head_norm(o, g_ref[:, sl]))
        o_all = jnp.concatenate(outs, axis=-1)
        o_ref[rows, :] = (o_all * (r_gate * jax.nn.sigmoid(r_gate))).astype(o_ref.dtype)
        return carry

    lax.fori_loop(0, ts // CHUNK, chunk, 0, unroll=unroll)


def _gla(yg, wa2, ba, g, batch, seq, ts):
    n = yg.shape[0]
    nt = seq // ts
    masks, right = _gla_constants()
    tri = jnp.asarray(np.tril(np.ones((CHUNK, CHUNK), np.float32)), BF16)
    return pl.pallas_call(
        functools.partial(_gla_kernel, ts=ts, unroll=MIXER_UNROLL),
        grid=(batch, nt),
        in_specs=[pl.BlockSpec((ts, SEG_W), lambda b, i: (b * nt + i, 0)),
                  _full(wa2.shape), _full(ba.shape), _full(g.shape),
                  _full(tri.shape), _full(masks.shape), _full(right.shape)],
        out_specs=pl.BlockSpec((ts, MIX_W), lambda b, i: (b * nt + i, 0)),
        out_shape=jax.ShapeDtypeStruct((n, MIX_W), BF16),
        scratch_shapes=[pltpu.VMEM((MIX_HEADS, HEAD_D, HEAD_D), F32)],
        compiler_params=_params(("parallel", "arbitrary")),
        name="gla_mixer",
    )(yg, wa2, ba, g, tri, jnp.asarray(masks), jnp.asarray(right))


def _mlstm_kernel(y_ref, gr_ref, cw_ref, bcol_ref, brow_ref, g_ref, tri_ref, eb_ref, eye_ref, o_ref,
                  xe_ref, qk_ref, vx_ref, fcb_ref, wsb_ref, rv_ref, dec_ref, c_ref, m_ref, *, ts):
    first = pl.program_id(1) == 0

    @pl.when(first)
    def _():
        xe_ref[0:8, :] = jnp.zeros((8, 2 * MIX_W), F32)
        c_ref[...] = jnp.zeros_like(c_ref)
        m_ref[...] = jnp.zeros_like(m_ref)

    @pl.when(jnp.logical_not(first))
    def _():
        xe_ref[0:8, :] = xe_ref[ts:ts + 8, :]

    xe_ref[8:ts + 8, :] = y_ref[:, 0:2 * MIX_W]
    conv = cw_ref[MLSTM_CONV - 1:MLSTM_CONV, :] * xe_ref[8:ts + 8, :]
    for j in range(MLSTM_CONV - 1):
        conv = conv + cw_ref[j:j + 1, :] * xe_ref[pl.ds(8 - (MLSTM_CONV - 1) + j, ts), :]
    qk_ref[...] = conv * jax.nn.sigmoid(conv)

    ones_col = (lax.broadcasted_iota(jnp.int32, (ts, HEAD_D), 1) == 0).astype(BF16)
    for h in range(MIX_HEADS):
        v_at, one_at = (0, HEAD_D) if h % 2 == 0 else (HEAD_D, 0)
        vx_ref[:, h * LANES + v_at:h * LANES + v_at + HEAD_D] = (
            y_ref[:, 512 + h * HEAD_D:512 + (h + 1) * HEAD_D].astype(BF16))
        vx_ref[:, h * LANES + one_at:h * LANES + one_at + HEAD_D] = ones_col

    tri = tri_ref[...]
    expand = eb_ref[...]
    nck = ts // CHUNK
    lane = lax.broadcasted_iota(jnp.int32, (1, LANES), 1)
    f_lanes = (lane >= MIX_HEADS) & (lane < 2 * MIX_HEADS)
    head_lane = (lax.broadcasted_iota(jnp.int32, (MIX_HEADS, LANES), 1)
                 == lax.broadcasted_iota(jnp.int32, (MIX_HEADS, LANES), 0) + MIX_HEADS)

    def to_lanes(col):
        return jnp.sum(jnp.where(head_lane, col, 0.0), 0, keepdims=True)

    m_col = m_ref[0:MIX_HEADS, 0:1]
    for c in range(nck):
        g_col = y_ref[c * CHUNK:(c + 1) * CHUNK, 1024:1152] + bcol_ref[...]
        fcum_col = jnp.where(f_lanes, _cumsum_rows(tri, _log_sigmoid(g_col)), 0.0)
        fcb_ref[c] = sum(_dot(part, expand) for part in _split3(fcum_col))
        g_row = gr_ref[c] + brow_ref[...]
        fcum_row = sum(_dot_nt(part, tri) for part in _split3(_log_sigmoid(g_row)))
        f_row = fcum_row[MIX_HEADS:2 * MIX_HEADS, :]
        i_row = g_row[0:MIX_HEADS, :]
        f_end = f_row[:, CHUNK - 1:CHUNK]
        rv_ref[c, 0:MIX_HEADS, :] = jnp.concatenate(
            [i_row - f_row, jnp.broadcast_to(m_col, (MIX_HEADS, HEAD_D))], axis=1)
        m_new = jnp.maximum(f_end + m_col, jnp.max(f_end - f_row + i_row, -1, keepdims=True))
        m_new_l = to_lanes(m_new)
        f_end_l = fcum_col[CHUNK - 1:CHUNK, :]
        i_shift = pltpu.roll(g_col, MIX_HEADS, 1)
        w_s = jnp.where(f_lanes, jnp.exp(f_end_l - fcum_col + i_shift - m_new_l), 0.0)
        hi, mid, _ = _split3(w_s)
        wsb_ref[c] = _dot(hi, expand) + _dot(mid, expand)
        dec_ref[c] = jnp.broadcast_to(jnp.exp(f_end_l + to_lanes(m_col) - m_new_l), (SUBLANES, LANES))
        m_col = m_new
    m_ref[0:MIX_HEADS, :] = jnp.broadcast_to(m_col, (MIX_HEADS, LANES))

    t_idx = lax.broadcasted_iota(jnp.int32, (CHUNK, LANES), 0)
    s_idx = lax.broadcasted_iota(jnp.int32, (CHUNK, LANES), 1)
    keep = (s_idx <= t_idx) | (s_idx >= CHUNK)
    low_half = s_idx < HEAD_D

    def chunk(c, carry):
        rows = pl.ds(pl.multiple_of(c * CHUNK, CHUNK), CHUNK)
        fcb = fcb_ref[c]
        wsb = wsb_ref[c]
        rv = rv_ref[c, 0:MIX_HEADS, :]
        dec_l = dec_ref[c]
        for p in range(MIX_HEADS // 2):
            pl_ = slice(p * LANES, (p + 1) * LANES)
            q2 = (qk_ref[rows, pl_] * (HEAD_D ** -0.5)).astype(BF16)
            k2 = qk_ref[rows, MIX_W + p * LANES:MIX_W + (p + 1) * LANES]
            pair = jnp.zeros((CHUNK, LANES), F32)
            for h in (2 * p, 2 * p + 1):
                mine = low_half if h % 2 == 0 else jnp.logical_not(low_half)
                hb = slice(h * LANES, (h + 1) * LANES)
                vx = vx_ref[rows, hb]
                logw = jnp.where(keep, fcb[:, hb] + rv[h:h + 1, :], -jnp.inf)
                m_t = jnp.max(logw, -1, keepdims=True)
                k_h = jnp.where(mine, k2, 0.0)
                qk = _dot_nt(q2, jnp.concatenate([k_h.astype(BF16), eye_ref[h % 2]], axis=0))
                w = (jnp.exp(logw - m_t) * qk).astype(BF16)
                c_st = c_ref[h]
                num = _dot(w, jnp.concatenate([vx, c_st.astype(BF16)], axis=0))
                den = num[:, HEAD_D:HEAD_D + 1] if h % 2 == 0 else num[:, 0:1]
                r = 1.0 / jnp.maximum(jnp.abs(den), jnp.exp(-m_t))
                mu = jnp.sum(jnp.where(mine, num, 0.0), -1, keepdims=True) * (1.0 / HEAD_D)
                cen = jnp.where(mine, num - mu, 0.0)
                var = jnp.sum(cen * cen, -1, keepdims=True) * (1.0 / HEAD_D)
                pair = pair + cen * (r * lax.rsqrt(r * r * var + LN_EPS))
                upd = _dot_tn((k_h * wsb[:, hb]).astype(BF16), vx)
                off = (h % 2) * HEAD_D
                c_ref[h] = dec_l[0:1, MIX_HEADS + h:MIX_HEADS + h + 1] * c_st + upd[off:off + HEAD_D, :]
            o_gate = y_ref[rows, 768 + p * LANES:768 + (p + 1) * LANES]
            o_ref[rows, pl_] = (pair * g_ref[:, pl_] * jax.nn.sigmoid(o_gate)).astype(o_ref.dtype)
        return carry

    lax.fori_loop(0, nck, chunk, 0, unroll=MIXER_UNROLL)


def _mlstm(ym, gates_row, conv_w, b_col, b_row, g, batch, seq, ts):
    n = ym.shape[0]
    nt = seq // ts
    nck = ts // CHUNK
    tri = jnp.asarray(np.tril(np.ones((CHUNK, CHUNK), np.float32)), BF16)
    expand = np.zeros((LANES, MIX_HEADS * LANES), np.float32)
    for h in range(MIX_HEADS):
        expand[MIX_HEADS + h, h * LANES:(h + 1) * LANES] = 1.0
    eye = np.zeros((2, HEAD_D, LANES), np.float32)
    eye[0, :, 0:HEAD_D] = np.eye(HEAD_D)
    eye[1, :, HEAD_D:] = np.eye(HEAD_D)
    expand = jnp.asarray(expand, BF16)
    eye = jnp.asarray(eye, BF16)
    return pl.pallas_call(
        functools.partial(_mlstm_kernel, ts=ts),
        grid=(batch, nt),
        in_specs=[pl.BlockSpec((ts, SEG_W), lambda b, i: (b * nt + i, 0)),
                  pl.BlockSpec((nck, 8, CHUNK), lambda b, i: (b * nt + i, 0, 0)),
                  _full(conv_w.shape), _full(b_col.shape), _full(b_row.shape), _full(g.shape),
                  _full(tri.shape), _full(expand.shape), _full(eye.shape)],
        out_specs=pl.BlockSpec((ts, MIX_W), lambda b, i: (b * nt + i, 0)),
        out_shape=jax.ShapeDtypeStruct((n, MIX_W), BF16),
        scratch_shapes=[pltpu.VMEM((ts + 8, 2 * MIX_W), F32),
                        pltpu.VMEM((ts, 2 * MIX_W), F32),
                        pltpu.VMEM((ts, MIX_HEADS * LANES), BF16),
                        pltpu.VMEM((nck, CHUNK, MIX_HEADS * LANES), F32),
                        pltpu.VMEM((nck, CHUNK, MIX_HEADS * LANES), F32),
                        pltpu.VMEM((nck, SUBLANES, LANES), F32),
                        pltpu.VMEM((nck, SUBLANES, LANES), F32),
                        pltpu.VMEM((MIX_HEADS, HEAD_D, LANES), F32),
                        pltpu.VMEM((SUBLANES, LANES), F32)],
        compiler_params=_params(("parallel", "arbitrary")),
        name="mlstm_mixer",
    )(ym, gates_row, conv_w, b_col, b_row, g, tri, expand, eye)


def _rope_table_kernel(pos_ref, inv_ref, cos_ref, sin_ref):
    ang = pos_ref[...].astype(F32) * inv_ref[...]
    lane = lax.broadcasted_iota(jnp.int32, ang.shape, 1)
    rot = (lane >= MLA_NOPE) & (lane < MLA_NOPE + MLA_ROPE)
    first_half = lane < MLA_NOPE + MLA_ROPE // 2
    cos_ref[...] = jnp.where(lane < MLA_NOPE, 1.0, jnp.where(rot, jnp.cos(ang), 0.0))
    s = jnp.sin(ang)
    sin_ref[...] = jnp.where(rot, jnp.where(first_half, -s, s), 0.0)


def _rope_tables(pos_col, tm):
    n = pos_col.shape[0]
    half = MLA_ROPE // 2
    inv = ROPE_BASE ** (-np.arange(half, dtype=np.float32) / half)
    inv_row = np.zeros((1, LANES), np.float32)
    inv_row[0, MLA_NOPE:MLA_NOPE + half] = inv
    inv_row[0, MLA_NOPE + half:MLA_NOPE + MLA_ROPE] = inv
    return pl.pallas_call(
        _rope_table_kernel,
        grid=(n // tm,),
        in_specs=[pl.BlockSpec((tm, 1), lambda i: (i, 0)), _full((1, LANES))],
        out_specs=[pl.BlockSpec((tm, LANES), lambda i: (i, 0))] * 2,
        out_shape=[jax.ShapeDtypeStruct((n, LANES), F32)] * 2,
        compiler_params=_params(("parallel",)),
        name="rope_tables",
    )(pos_col, jnp.asarray(inv_row))


def _mla_prep_kernel(y_ref, cos_ref, sin_ref, gq_ref, gkv_ref, wqa_ref, wqb_ref, wkn_ref, wv_ref, ones_ref,
                     q_ref, k_ref, v_ref):
    def rms(x, g):
        return x * lax.rsqrt(jnp.mean(x * x, -1, keepdims=True) + LN_EPS) * g

    cos = cos_ref[...]
    sin = sin_ref[...]
    cq = rms(y_ref[:, 0:MLA_Q_RANK], gq_ref[...]).astype(BF16)
    ckv = rms(y_ref[:, MLA_Q_RANK:MLA_Q_RANK + MLA_KV_RANK], gkv_ref[...]).astype(BF16)
    k_rope = y_ref[:, 384:512] * cos + y_ref[:, 512:640] * sin
    qa = _dot(cq, wqa_ref[...])
    qb = _dot(cq, wqb_ref[...])
    kn = _dot(ckv, wkn_ref[...])
    scale = (MLA_NOPE + MLA_ROPE) ** -0.5 * LOG2_E
    for h in range(MLA_HEADS):
        sl = slice(h * MLA_QK_PAD, (h + 1) * MLA_QK_PAD)
        q_ref[:, sl] = ((qa[:, sl] * cos + qb[:, sl] * sin) * scale).astype(q_ref.dtype)
        k_ref[:, sl] = (kn[:, sl] + k_rope).astype(k_ref.dtype)
    v_ref[...] = (_dot(ckv, wv_ref[...]) + ones_ref[...]).astype(v_ref.dtype)


def _mla_prep(yc, cos_t, sin_t, gq, gkv, wqa, wqb, wkn, wv, tm):
    ones_row = np.zeros((1, MLA_HEADS * LANES), np.float32)
    for h in range(MLA_HEADS):
        ones_row[0, h * LANES + (MLA_V if h % 2 == 0 else 0)] = 1.0
    ones_row = jnp.asarray(ones_row)
    n = yc.shape[0]
    row = lambda w: pl.BlockSpec((tm, w), lambda i: (i, 0))
    qk_w = MLA_HEADS * MLA_QK_PAD
    return pl.pallas_call(
        _mla_prep_kernel,
        grid=(n // tm,),
        in_specs=[row(MLA_SEG_W), row(LANES), row(LANES), _full(gq.shape), _full(gkv.shape),
                  _full(wqa.shape), _full(wqb.shape), _full(wkn.shape), _full(wv.shape), _full(ones_row.shape)],
        out_specs=[row(qk_w), row(qk_w), row(MLA_HEADS * LANES)],
        out_shape=[jax.ShapeDtypeStruct((n, qk_w), BF16), jax.ShapeDtypeStruct((n, qk_w), BF16),
                   jax.ShapeDtypeStruct((n, MLA_HEADS * LANES), BF16)],
        compiler_params=_params(("parallel",)),
        name="mla_prep",
    )(yc, cos_t, sin_t, gq, gkv, wqa, wqb, wkn, wv, ones_row)


def _mla_attn_kernel(q_ref, k_ref, v_ref, o_ref, *, seq, tq):
    t_chunk = lax.broadcasted_iota(jnp.int32, (tq, tq), 0) // CHUNK
    s_chunk = lax.broadcasted_iota(jnp.int32, (tq, tq), 1) // CHUNK
    diag_mask = s_chunk <= t_chunk
    low_half = lax.broadcasted_iota(jnp.int32, (tq, LANES), 1) < MLA_V
    for i in range(seq // tq):
        rows = slice(i * tq, (i + 1) * tq)
        res = []
        for hh in range(MLA_HEAD_PAIR):
            ql = slice(hh * MLA_QK_PAD, (hh + 1) * MLA_QK_PAD)
            vl = slice(hh * LANES, (hh + 1) * LANES)
            q = q_ref[0, rows, ql]
            s_d = jnp.where(diag_mask, _dot_nt(q, k_ref[0, rows, ql]), -jnp.inf)
            m = jnp.max(s_d, -1, keepdims=True)
            if i > 0:
                s_o = _dot_nt(q, k_ref[0, 0:i * tq, ql])
                m = jnp.maximum(m, jnp.max(s_o, -1, keepdims=True))
            o = _dot(jnp.exp2(s_d - m).astype(BF16), v_ref[0, rows, vl])
            if i > 0:
                o = o + _dot(jnp.exp2(s_o - m).astype(BF16), v_ref[0, 0:i * tq, vl])
            l = o[:, MLA_V:MLA_V + 1] if hh == 0 else o[:, 0:1]
            res.append(o / l)
        o_ref[0, rows, :] = jnp.where(low_half, res[0], res[1]).astype(o_ref.dtype)


def _mla_attn(q, k, v, batch, seq, tq):
    n = q.shape[0]
    qk_w = MLA_HEADS * MLA_QK_PAD
    v_w = MLA_HEADS * MLA_V
    pair = lambda w: pl.BlockSpec((1, seq, MLA_HEAD_PAIR * w), lambda b, h: (b, 0, h))
    out = pl.pallas_call(
        functools.partial(_mla_attn_kernel, seq=seq, tq=tq),
        grid=(batch, MLA_HEADS // MLA_HEAD_PAIR),
        in_specs=[pair(MLA_QK_PAD), pair(MLA_QK_PAD), pair(LANES)],
        out_specs=pair(MLA_V),
        out_shape=jax.ShapeDtypeStruct((batch, seq, v_w), BF16),
        compiler_params=_params(("parallel", "parallel")),
        name="mla_attention",
    )(q.reshape(batch, seq, qk_w), k.reshape(batch, seq, qk_w), v.reshape(batch, seq, MLA_HEADS * LANES))
    return out.reshape(n, v_w)


def _out_proj_kernel(og_ref, om_ref, oc_ref, x_ref, wg_ref, wm_ref, wc_ref, g_ref, b_ref, o_ref):
    mix = _dot(og_ref[...], wg_ref[...]) + _dot(om_ref[...], wm_ref[...]) + _dot(oc_ref[...], wc_ref[...])
    o_ref[...] = _layer_norm(ALPHA * x_ref[...] + mix, g_ref[...], b_ref[...])


def _out_proj(og, om, oc, x2d, wg, wm, wc, g, b, tm):
    n = x2d.shape[0]
    row = lambda w: pl.BlockSpec((tm, w), lambda i: (i, 0))
    return pl.pallas_call(
        _out_proj_kernel,
        grid=(n // tm,),
        in_specs=[row(MIX_W), row(MIX_W), row(MLA_HEADS * MLA_V), row(D_MODEL),
                  _full(wg.shape), _full(wm.shape), _full(wc.shape), _full(g.shape), _full(b.shape)],
        out_specs=row(D_MODEL),
        out_shape=jax.ShapeDtypeStruct((n, D_MODEL), F32),
        compiler_params=_params(("parallel",)),
        name="out_proj_ln1",
    )(og, om, oc, x2d, wg, wm, wc, g, b)


def _xa_kv_kernel(mem_ref, w_ref, k_ref, v_ref):
    kv = _dot(mem_ref[...].astype(BF16), w_ref[...])
    k_ref[...] = kv[:, 0:D_MODEL].astype(k_ref.dtype)
    v_ref[...] = kv[:, D_MODEL:2 * D_MODEL].astype(v_ref.dtype)


def _xa_kv(mem2d, w_kv, mem_len):
    n = mem2d.shape[0]
    row = pl.BlockSpec((mem_len, D_MODEL), lambda i: (i, 0))
    return pl.pallas_call(
        _xa_kv_kernel,
        grid=(n // mem_len,),
        in_specs=[row, _full(w_kv.shape)],
        out_specs=[row, row],
        out_shape=[jax.ShapeDtypeStruct((n, D_MODEL), BF16)] * 2,
        compiler_params=_params(("parallel",)),
        name="xattn_kv",
    )(mem2d, w_kv)


def _xattn_kernel(x_ref, k_ref, v_ref, wq_ref, wo_ref, g_ref, b_ref, o_ref):
    x = x_ref[...]
    q = (_dot(x.astype(BF16), wq_ref[...]) * (XA_DH ** -0.5)).astype(BF16)
    out = jnp.zeros(x.shape, F32)
    for h in range(XA_HEADS):
        sl = slice(h * XA_DH, (h + 1) * XA_DH)
        s = _dot_nt(q[:, sl], k_ref[:, sl])
        p = jnp.exp(s - jnp.max(s, -1, keepdims=True))
        p = p / jnp.sum(p, -1, keepdims=True)
        o = _dot(p.astype(BF16), v_ref[:, sl])
        out = out + _dot(o.astype(BF16), wo_ref[sl, :])
    o_ref[...] = _layer_norm(ALPHA * x + out, g_ref[...], b_ref[...])


def _xattn(x1, xk, xv, wq, wo, g, b, batch, seq, mem_len, tm):
    n = x1.shape[0]
    nt = seq // tm
    row = pl.BlockSpec((tm, D_MODEL), lambda bi, i: (bi * nt + i, 0))
    mem = pl.BlockSpec((mem_len, D_MODEL), lambda bi, i: (bi, 0))
    return pl.pallas_call(
        _xattn_kernel,
        grid=(batch, nt),
        in_specs=[row, mem, mem, _full(wq.shape), _full(wo.shape), _full(g.shape), _full(b.shape)],
        out_specs=row,
        out_shape=jax.ShapeDtypeStruct((n, D_MODEL), F32),
        compiler_params=_params(("parallel", "parallel")),
        name="xattn_ln2",
    )(x1, xk, xv, wq, wo, g, b)


def _route_kernel(x_ref, w_ref, b_ref, tri_ref, xg_ref, rank_ref, grp_ref, cnt_out_ref, cnt_ref):
    @pl.when(pl.program_id(0) == 0)
    def _():
        cnt_ref[...] = jnp.zeros_like(cnt_ref)

    x = x_ref[...]
    logits = jnp.dot(x, w_ref[...], precision=HIGHEST, preferred_element_type=F32) + b_ref[...]
    lane = lax.broadcasted_iota(jnp.int32, logits.shape, 1).astype(F32)
    is_group = (lane >= N_EXPERTS) & (lane < N_EXPERTS + N_GROUPS)
    g_max = jnp.max(jnp.where(is_group, logits, -jnp.inf), -1, keepdims=True)
    g_sum = jnp.sum(jnp.where(is_group, jnp.exp(logits - g_max), 0.0), -1, keepdims=True)
    g_p = 1.0 / g_sum
    g_idx = jnp.min(jnp.where(is_group & (logits == g_max), lane - N_EXPERTS, float(LANES)), -1, keepdims=True)
    in_group = (lane < N_EXPERTS) & (jnp.floor(lane * (1.0 / EXPERTS_PER_GROUP)) == g_idx)
    e_max = jnp.max(jnp.where(in_group, logits, -jnp.inf), -1, keepdims=True)
    e_exp = jnp.where(in_group, jnp.exp(logits - e_max), 0.0)
    prob = e_exp / jnp.sum(e_exp, -1, keepdims=True)
    cand = jnp.where(in_group, prob, -1.0)
    p1 = jnp.max(cand, -1, keepdims=True)
    i1 = jnp.min(jnp.where(cand == p1, lane, float(LANES)), -1, keepdims=True)
    cand2 = jnp.where(lane == i1, -1.0, cand)
    p2 = jnp.max(cand2, -1, keepdims=True)
    i2 = jnp.min(jnp.where(cand2 == p2, lane, float(LANES)), -1, keepdims=True)
    p_sum = p1 + p2
    xg_ref[:, 0:D_MODEL] = x
    xg_ref[:, D_MODEL:] = (jnp.where(lane == i1, g_p * (p1 / p_sum), 0.0)
                           + jnp.where(lane == i2, g_p * (p2 / p_sum), 0.0))
    onehot = jnp.where(lane == g_idx, 1.0, 0.0)
    before = _dot(tri_ref[...], onehot.astype(BF16)) + cnt_ref[...]
    rank_ref[...] = jnp.sum(onehot * before, -1, keepdims=True).astype(jnp.int32)
    grp_ref[...] = g_idx.astype(jnp.int32)
    cnt_ref[...] += jnp.sum(onehot, 0, keepdims=True)
    cnt_out_ref[...] = cnt_ref[...]


def _route(x2, w_route, b_route, tm):
    n = x2.shape[0]
    tri = jnp.asarray(np.tril(np.ones((tm, tm), np.float32), -1), BF16)
    col = pl.BlockSpec((tm, 1), lambda i: (i, 0))
    return pl.pallas_call(
        _route_kernel,
        grid=(n // tm,),
        in_specs=[pl.BlockSpec((tm, D_MODEL), lambda i: (i, 0)), _full(w_route.shape), _full(b_route.shape),
                  _full(tri.shape)],
        out_specs=[pl.BlockSpec((tm, XG_W), lambda i: (i, 0)), col, col, _full((1, LANES))],
        out_shape=[jax.ShapeDtypeStruct((n, XG_W), F32), jax.ShapeDtypeStruct((n, 1), jnp.int32),
                   jax.ShapeDtypeStruct((n, 1), jnp.int32), jax.ShapeDtypeStruct((1, LANES), F32)],
        scratch_shapes=[pltpu.VMEM((1, LANES), F32)],
        compiler_params=_params(("arbitrary",)),
        name="moe_route",
    )(x2, w_route, b_route, tri)


def _dispatch_kernel(pos_ref, x_ref, xs_hbm, sem, *, tm):
    def issue(t, carry):
        pltpu.make_async_copy(x_ref.at[pl.ds(t, 1)], xs_hbm.at[pl.ds(pos_ref[t], 1)], sem).start()
        return carry

    lax.fori_loop(0, tm, issue, 0, unroll=DMA_UNROLL)
    pltpu.make_async_copy(x_ref, xs_hbm.at[pl.ds(0, tm)], sem).wait()


def _dispatch(xg, pos, tm):
    n = xg.shape[0]
    return pl.pallas_call(
        functools.partial(_dispatch_kernel, tm=tm),
        grid=(n // tm,),
        in_specs=[pl.BlockSpec((tm,), lambda i: (i,), memory_space=pltpu.SMEM),
                  pl.BlockSpec((tm, XG_W), lambda i: (i, 0))],
        out_specs=pl.BlockSpec(memory_space=pl.ANY),
        out_shape=jax.ShapeDtypeStruct(xg.shape, xg.dtype),
        scratch_shapes=[pltpu.SemaphoreType.DMA(())],
        compiler_params=_params(("arbitrary",)),
        name="moe_dispatch",
    )(pos, xg)


def _moe_ffn_kernel(blk_ref, grp_ref, first_ref, valid_ref, xs_ref, wg_ref, wu_ref, wd_ref, o_ref, xb_ref):
    w = pl.program_id(0)
    e = pl.program_id(1)

    @pl.when((first_ref[w] == 1) & (e == 0))
    def _():
        xb_ref[...] = xs_ref[:, 0:D_MODEL].astype(BF16)
        o_ref[...] = jnp.zeros_like(o_ref)

    @pl.when(valid_ref[w] == 1)
    def _():
        xb = xb_ref[...]
        gates = xs_ref[:, D_MODEL:]
        lane = lax.broadcasted_iota(jnp.int32, gates.shape, 1)
        w_tok = jnp.sum(jnp.where(lane == grp_ref[w] * EXPERTS_PER_GROUP + e, gates, 0.0), -1, keepdims=True)
        hg = _dot(xb, wg_ref[0].astype(BF16))
        hu = _dot(xb, wu_ref[0].astype(BF16))
        hid = hg * jax.nn.sigmoid(hg) * hu * w_tok
        o_ref[...] += _dot(hid.astype(BF16), wd_ref[0].astype(BF16))


def _moe_ffn(xs, items, w_gate, w_up, w_down, rb, e_base):
    n = xs.shape[0]
    blk, grp, first, valid = items

    def expert(w, e, blk, grp, first, valid):
        return (e_base + grp[w] * EXPERTS_PER_GROUP + jnp.where(valid[w] == 1, e, EXPERTS_PER_GROUP - 1), 0, 0)

    grid_spec = pltpu.PrefetchScalarGridSpec(
        num_scalar_prefetch=4,
        grid=(blk.shape[0], EXPERTS_PER_GROUP),
        in_specs=[pl.BlockSpec((rb, XG_W), lambda w, e, blk, grp, first, valid: (blk[w], 0)),
                  pl.BlockSpec((1, D_MODEL, D_EXPERT), expert),
                  pl.BlockSpec((1, D_MODEL, D_EXPERT), expert),
                  pl.BlockSpec((1, D_EXPERT, D_MODEL), expert)],
        out_specs=pl.BlockSpec((rb, D_MODEL), lambda w, e, blk, grp, first, valid: (blk[w], 0)),
        scratch_shapes=[pltpu.VMEM((rb, D_MODEL), BF16)],
    )
    return pl.pallas_call(
        _moe_ffn_kernel,
        grid_spec=grid_spec,
        out_shape=jax.ShapeDtypeStruct((n, D_MODEL), F32),
        compiler_params=_params(("arbitrary", "arbitrary")),
        name="moe_experts",
    )(blk, grp, first, valid, xs, w_gate, w_up, w_down)


def _combine_kernel(pos_ref, xg_ref, ys_hbm, g_ref, b_ref, o_ref, buf_ref, sem, *, tm):
    def issue(t, carry):
        pltpu.make_async_copy(ys_hbm.at[pl.ds(pos_ref[t], 1)], buf_ref.at[pl.ds(t, 1)], sem).start()
        return carry

    lax.fori_loop(0, tm, issue, 0, unroll=DMA_UNROLL)
    pltpu.make_async_copy(ys_hbm.at[pl.ds(0, tm)], buf_ref, sem).wait()
    o_ref[...] = _layer_norm(ALPHA * xg_ref[:, 0:D_MODEL] + buf_ref[...], g_ref[...], b_ref[...])


def _combine(pos, xg, ys, g, b, tm):
    n = xg.shape[0]
    return pl.pallas_call(
        functools.partial(_combine_kernel, tm=tm),
        grid=(n // tm,),
        in_specs=[pl.BlockSpec((tm,), lambda i: (i,), memory_space=pltpu.SMEM),
                  pl.BlockSpec((tm, XG_W), lambda i: (i, 0)),
                  pl.BlockSpec(memory_space=pl.ANY), _full(g.shape), _full(b.shape)],
        out_specs=pl.BlockSpec((tm, D_MODEL), lambda i: (i, 0)),
        out_shape=jax.ShapeDtypeStruct((n, D_MODEL), F32),
        scratch_shapes=[pltpu.VMEM((tm, D_MODEL), F32), pltpu.SemaphoreType.DMA(())],
        compiler_params=_params(("arbitrary",)),
        name="moe_combine_ln3",
    )(pos, xg, ys, g, b)


def _moe_work_items(counts, n, rb):
    nb = n // rb
    n_items = nb + N_GROUPS - 1
    ends = jnp.cumsum(counts)
    start = jnp.arange(nb, dtype=jnp.int32) * rb
    g_lo = jnp.sum(ends[None, :] <= start[:, None], axis=1).astype(jnp.int32)
    g_hi = jnp.sum(ends[None, :] <= (start + rb - 1)[:, None], axis=1).astype(jnp.int32)
    per_blk = g_hi - g_lo + 1
    item0 = jnp.cumsum(per_blk) - per_blk
    w = jnp.arange(n_items, dtype=jnp.int32)
    valid = w < jnp.sum(per_blk)
    blk = jnp.clip(jnp.sum(item0[None, :] <= w[:, None], axis=1) - 1, 0, nb - 1).astype(jnp.int32)
    grp = jnp.where(valid, g_lo[blk] + (w - item0[blk]), g_hi[nb - 1]).astype(jnp.int32)
    first = (valid & (w == item0[blk])).astype(jnp.int32)
    return blk, grp, first, valid.astype(jnp.int32)


def _moe(x2, w_route, b_route, w_gate, w_up, w_down, e_base, g, b, tm, t_dma, rb):
    n = x2.shape[0]
    xg, rank, grp, counts = _route(x2, w_route, b_route, tm)
    counts = counts[0, :N_GROUPS].astype(jnp.int32)
    offsets = jnp.cumsum(counts) - counts
    pos = (offsets[grp[:, 0]] + rank[:, 0]).astype(jnp.int32)
    xs = _dispatch(xg, pos, t_dma)
    ys = _moe_ffn(xs, _moe_work_items(counts, n, rb), w_gate, w_up, w_down, rb, e_base)
    return _combine(pos, xg, ys, g, b, t_dma)


def _pad_cols(w, width):
    return jnp.pad(w, ((0, 0), (0, width - w.shape[1])))


def _layer_weights(w_in, w_out, gla_w_a2, ml_b_i, ml_b_f, mla_w_uq, mla_w_ukv, moe_w_group, moe_b_group,
                   moe_w_router, moe_b_router):
    o = _IN_OFF
    half = MLA_ROPE // 2
    wg = _pad_cols(w_in[:, o[0]:o[5]], SEG_W).astype(BF16)
    wm = _pad_cols(w_in[:, o[5]:o[10]], SEG_W).astype(BF16)
    kr = w_in[:, o[12]:o[13]]
    zeros = lambda w: jnp.zeros((D_MODEL, w), F32)
    kra = jnp.concatenate([zeros(MLA_NOPE), kr, zeros(LANES - MLA_NOPE - MLA_ROPE)], 1)
    krb = jnp.concatenate([zeros(MLA_NOPE), kr[:, half:], kr[:, :half], zeros(LANES - MLA_NOPE - MLA_ROPE)], 1)
    wc = jnp.concatenate([w_in[:, o[10]:o[12]], kra, krb], 1).astype(BF16)
    wift = w_in[:, o[8]:o[10]].T.astype(BF16)
    wa2 = jnp.pad(gla_w_a2, ((0, LANES - GLA_GATE_RANK), (0, 0))).astype(BF16)
    b_gate = jnp.concatenate([ml_b_i, ml_b_f])
    b_col = jnp.pad(b_gate, (0, LANES - 2 * MIX_HEADS)).reshape(1, LANES)
    b_row = b_gate.reshape(2 * MIX_HEADS, 1)
    uq = mla_w_uq.reshape(MLA_Q_RANK, MLA_HEADS, MLA_NOPE + MLA_ROPE)
    zq = jnp.zeros((MLA_Q_RANK, MLA_HEADS, LANES - MLA_NOPE - MLA_ROPE), F32)
    wqa = jnp.concatenate([uq, zq], -1).reshape(MLA_Q_RANK, -1).astype(BF16)
    wqb = jnp.concatenate([jnp.zeros((MLA_Q_RANK, MLA_HEADS, MLA_NOPE), F32), uq[..., MLA_NOPE + half:],
                           uq[..., MLA_NOPE:MLA_NOPE + half], zq], -1).reshape(MLA_Q_RANK, -1).astype(BF16)
    ukv = mla_w_ukv.reshape(MLA_KV_RANK, MLA_HEADS, MLA_NOPE + MLA_V)
    wkn = jnp.concatenate([ukv[..., :MLA_NOPE], jnp.zeros((MLA_KV_RANK, MLA_HEADS, LANES - MLA_NOPE), F32)],
                          -1).reshape(MLA_KV_RANK, -1).astype(BF16)
    uv = ukv[..., MLA_NOPE:]
    zv = jnp.zeros_like(uv)
    odd = (jnp.arange(MLA_HEADS) % 2 == 1)[None, :, None]
    wv = jnp.concatenate([jnp.where(odd, zv, uv), jnp.where(odd, uv, zv)], -1).reshape(MLA_KV_RANK, -1).astype(BF16)
    wo = w_out.astype(BF16)
    w_route = _pad_cols(jnp.concatenate([moe_w_router, moe_w_group], 1), LANES)
    b_route = jnp.pad(jnp.concatenate([moe_b_router, moe_b_group]), (0, LANES - N_EXPERTS - N_GROUPS)).reshape(1, LANES)
    return dict(wg=wg, wm=wm, wc=wc, wift=wift, wa2=wa2, b_col=b_col, b_row=b_row, wqa=wqa, wqb=wqb, wkn=wkn,
                wv=wv, wo_g=wo[0:MIX_W], wo_m=wo[MIX_W:2 * MIX_W], wo_c=wo[2 * MIX_W:], w_route=w_route,
                b_route=b_route)


def _tile(total, want):
    t = min(total, want)
    assert total % t == 0
    return t


def kernel(x, mem, positions, w_in, w_out, gla_w_a2, gla_b_a, gla_norm_g, ml_conv_w, ml_b_i, ml_b_f, ml_norm_g, mla_q_norm_g, mla_w_uq, mla_kv_norm_g, mla_w_ukv, xa_w_q, xa_w_kv, xa_w_o, moe_w_group, moe_b_group, moe_w_router, moe_b_router, moe_w_gate, moe_w_up, moe_w_down, ln1_g, ln1_b, ln2_g, ln2_b, ln3_g, ln3_b):
    batch, seq, _ = x.shape
    mem_len = mem.shape[1]
    n = batch * seq
    depth = w_in.shape[0]
    assert seq % CHUNK == 0
    tm = _tile(n, 512)
    ts = _tile(seq, 512)
    tq = _tile(seq, 512)
    t_moe = _tile(n, 1024)
    row = lambda a: a.reshape(1, -1)

    cos_t, sin_t = _rope_tables(positions.reshape(n, 1), _tile(n, 2048))
    mem2d = mem.reshape(batch * mem_len, D_MODEL)
    experts_gate = moe_w_gate.reshape(-1, D_MODEL, D_EXPERT)
    experts_up = moe_w_up.reshape(-1, D_MODEL, D_EXPERT)
    experts_down = moe_w_down.reshape(-1, D_EXPERT, D_MODEL)
    h = x.reshape(n, D_MODEL)
    for l in range(depth):
        w = _layer_weights(w_in[l], w_out[l], gla_w_a2[l], ml_b_i[l], ml_b_f[l], mla_w_uq[l], mla_w_ukv[l],
                           moe_w_group[l], moe_b_group[l], moe_w_router[l], moe_b_router[l])
        yg, ym, yc, yift = _in_proj(h, w["wg"], w["wm"], w["wc"], w["wift"], tm)
        gates_row = yift.reshape(2 * MIX_HEADS, n // CHUNK, CHUNK).transpose(1, 0, 2)
        og = _gla(yg, w["wa2"], row(gla_b_a[l]), row(gla_norm_g[l]), batch, seq, ts)
        om = _mlstm(ym, gates_row, ml_conv_w[l], w["b_col"], w["b_row"], row(ml_norm_g[l]), batch, seq, ts)
        q, k, v = _mla_prep(yc, cos_t, sin_t, row(mla_q_norm_g[l]), row(mla_kv_norm_g[l]),
                            w["wqa"], w["wqb"], w["wkn"], w["wv"], tm)
        oc = _mla_attn(q, k, v, batch, seq, tq)
        x1 = _out_proj(og, om, oc, h, w["wo_g"], w["wo_m"], w["wo_c"], row(ln1_g[l]), row(ln1_b[l]), tm)
        xk, xv = _xa_kv(mem2d, xa_w_kv[l].astype(BF16), mem_len)
        x2 = _xattn(x1, xk, xv, xa_w_q[l].astype(BF16), xa_w_o[l].astype(BF16), row(ln2_g[l]), row(ln2_b[l]),
                    batch, seq, mem_len, tm)
        h = _moe(x2, w["w_route"], w["b_route"], experts_gate, experts_up, experts_down, l * N_EXPERTS,
                 row(ln3_g[l]), row(ln3_b[l]), tm, t_moe, t_moe)
    return h.reshape(batch, seq, D_MODEL)
```

```python
import functools

import numpy as np
import jax
import jax.numpy as jnp
from jax import lax
from jax.experimental import pallas as pl
from jax.experimental.pallas import tpu as pltpu

F32 = jnp.float32
BF16 = jnp.bfloat16
HIGHEST = lax.Precision.HIGHEST

D_MODEL = 1024
CHUNK = 64
HEAD_D = 64
MIX_HEADS = 4
MIX_W = MIX_HEADS * HEAD_D
GLA_GATE_RANK = 16
GLA_TAU = 16.0
MLSTM_CONV = 4
MLA_HEADS = 8
MLA_NOPE = 64
MLA_ROPE = 32
MLA_V = 64
MLA_Q_RANK = 256
MLA_KV_RANK = 128
MLA_QK_PAD = 128
ROPE_BASE = 10000.0
LOG2_E = 1.4426950408889634
MLA_HEAD_PAIR = 2
XA_HEADS = 4
XA_DH = D_MODEL // XA_HEADS
N_GROUPS = 4
EXPERTS_PER_GROUP = 8
N_EXPERTS = N_GROUPS * EXPERTS_PER_GROUP
D_EXPERT = 256
DEPTH = 2
ALPHA = (2 * DEPTH) ** 0.25
LN_EPS = 1e-5
LANES = 128
SUBLANES = 8
MIXER_UNROLL = 4
SEG_W = 1152
MLA_SEG_W = 640
XG_W = D_MODEL + LANES
DMA_UNROLL = 8
GLA_LEVELS = (32, 16, 8, 4, 2, 1)
VMEM_LIMIT = 56 * 1024 * 1024

_IN_SIZES = (256, 256, 256, 256, GLA_GATE_RANK, 512, 256, 256, 4, 4, MLA_Q_RANK, MLA_KV_RANK, MLA_ROPE)
_IN_OFF = np.concatenate([[0], np.cumsum(_IN_SIZES)]).tolist()


def _params(sem):
    return pltpu.CompilerParams(dimension_semantics=sem, vmem_limit_bytes=VMEM_LIMIT)


def _full(shape):
    return pl.BlockSpec(shape, lambda *_: (0,) * len(shape))


def _layer_norm(x, g, b):
    mu = jnp.mean(x, -1, keepdims=True)
    xc = x - mu
    var = jnp.mean(xc * xc, -1, keepdims=True)
    return xc * lax.rsqrt(var + LN_EPS) * g + b


def _log_sigmoid(z):
    return jnp.minimum(z, 0.0) - jnp.log1p(jnp.exp(-jnp.abs(z)))


def _dot_nt(a, b):
    return lax.dot_general(a, b, (((1,), (1,)), ((), ())), preferred_element_type=F32)


def _dot_tn(a, b):
    return lax.dot_general(a, b, (((0,), (0,)), ((), ())), preferred_element_type=F32)


def _dot(a, b):
    return jnp.dot(a, b, preferred_element_type=F32)


def _head_norm(o, g):
    mu = jnp.mean(o, -1, keepdims=True)
    oc = o - mu
    var = jnp.mean(oc * oc, -1, keepdims=True)
    return oc * lax.rsqrt(var + LN_EPS) * g


def _in_proj_kernel(x_ref, wg_ref, wm_ref, wc_ref, wift_ref, yg_ref, ym_ref, yc_ref, yift_ref):
    xb = x_ref[...].astype(BF16)
    yg_ref[...] = _dot(xb, wg_ref[...])
    ym_ref[...] = _dot(xb, wm_ref[...])
    yc_ref[...] = _dot(xb, wc_ref[...])
    yift_ref[...] = _dot_nt(wift_ref[...], xb)


def _in_proj(x2d, wg, wm, wc, wift, tm):
    n = x2d.shape[0]
    row = lambda w: pl.BlockSpec((tm, w), lambda i: (i, 0))
    return pl.pallas_call(
        _in_proj_kernel,
        grid=(n // tm,),
        in_specs=[row(D_MODEL), _full(wg.shape), _full(wm.shape), _full(wc.shape), _full(wift.shape)],
        out_specs=[row(SEG_W), row(SEG_W), row(MLA_SEG_W), pl.BlockSpec((8, tm), lambda i: (0, i))],
        out_shape=[jax.ShapeDtypeStruct((n, SEG_W), F32), jax.ShapeDtypeStruct((n, SEG_W), F32),
                   jax.ShapeDtypeStruct((n, MLA_SEG_W), F32), jax.ShapeDtypeStruct((8, n), F32)],
        compiler_params=_params(("parallel",)),
        name="in_proj",
    )(x2d, wg, wm, wc, wift)


def _split3(x):
    hi = x.astype(BF16)
    r1 = x - hi.astype(F32)
    mid = r1.astype(BF16)
    lo = (r1 - mid.astype(F32)).astype(BF16)
    return hi, mid, lo


def _cumsum_rows(tri, x):
    hi, mid, lo = _split3(x)
    return _dot(tri, hi) + _dot(tri, mid) + _dot(tri, lo)


def _gla_constants():
    t = np.arange(CHUNK)
    n_lv = len(GLA_LEVELS)
    masks = np.zeros((n_lv + 1, CHUNK, CHUNK), np.float32)
    right = np.zeros((n_lv, CHUNK, 1), np.float32)
    for li, n in enumerate(GLA_LEVELS):
        blk = t // (2 * n)
        is_right = (t % (2 * n)) >= n
        masks[li] = ((blk[:, None] == blk[None, :]) & is_right[:, None] & ~is_right[None, :])
        right[li, :, 0] = is_right
    masks[n_lv] = np.eye(CHUNK)
    return masks, right


def _gla_level_exponents(b, log_a, right_ref):
    row = lax.broadcasted_iota(jnp.int32, (CHUNK, 1), 0)
    exps = []
    for li, n in enumerate(GLA_LEVELS):
        if n >= SUBLANES // 2:
            per_blk = max(2 * n // SUBLANES, 1)
            b4 = b.reshape(CHUNK // (per_blk * SUBLANES), per_blk, SUBLANES, MIX_W)
            r_reg, r_sub = divmod(n - 1, SUBLANES)
            src = b4[:, r_reg:r_reg + 1, r_sub:r_sub + 1, :]
            b_r = jnp.broadcast_to(src, b4.shape).reshape(CHUNK, MIX_W)
            exps.append(jnp.where(right_ref[li] > 0.0, b - b_r, b_r - b))
        elif n == 2:
            pos = row % 4
            nxt = pltpu.roll(log_a, CHUNK - 1, 0)
            prv = pltpu.roll(log_a, 1, 0)
            exps.append(jnp.where(pos == 0, nxt, jnp.where(pos == 1, 0.0,
                                                            jnp.where(pos == 2, log_a, log_a + prv))))
        else:
            exps.append(jnp.where(right_ref[li] > 0.0, log_a, 0.0))
    return exps


def _gla_kernel(y_ref, wa2_ref, ba_ref, g_ref, tri_ref, mask_ref, right_ref, o_ref, st_ref, *, ts, unroll):
    n_lv = len(GLA_LEVELS)

    @pl.when(pl.program_id(1) == 0)
    def _():
        st_ref[...] = jnp.zeros_like(st_ref)

    def chunk(c, carry):
        rows = pl.ds(pl.multiple_of(c * CHUNK, CHUNK), CHUNK)
        q = y_ref[rows, 0:256] * (HEAD_D ** -0.5)
        k = y_ref[rows, 256:512]
        v = y_ref[rows, 512:768].astype(BF16)
        r_gate = y_ref[rows, 768:1024]
        a_lr = y_ref[rows, 1024:1152]
        z = _dot(a_lr.astype(BF16), wa2_ref[...]) + ba_ref[...]
        log_a = _log_sigmoid(z) * (1.0 / GLA_TAU)
        b = _cumsum_rows(tri_ref[...], log_a)
        b_end = b[CHUNK - 1:CHUNK, :]
        scores = [jnp.zeros((CHUNK, CHUNK), F32) for _ in range(MIX_HEADS)]
        for li, e in enumerate(_gla_level_exponents(b, log_a, right_ref)):
            x = (jnp.where(right_ref[li] > 0.0, q, k) * jnp.exp(e)).astype(BF16)
            for h in range(MIX_HEADS):
                xh = x[:, h * HEAD_D:(h + 1) * HEAD_D]
                scores[h] = scores[h] + _dot_nt(xh, xh) * mask_ref[li]
        qb = q.astype(BF16)
        kb = k.astype(BF16)
        q_in = (q * jnp.exp(b)).astype(BF16)
        k_out = (k * jnp.exp(b_end - b)).astype(BF16)
        dec_end = jnp.exp(b_end)
        outs = []
        for h in range(MIX_HEADS):
            sl = slice(h * HEAD_D, (h + 1) * HEAD_D)
            a = scores[h] + _dot_nt(qb[:, sl], kb[:, sl]) * mask_ref[n_lv]
            st = st_ref[h]
            o = _dot(a.astype(BF16), v[:, sl]) + _dot_nt(q_in[:, sl], st.astype(BF16))
            st_ref[h] = st * dec_end[:, sl] + _dot_tn(v[:, sl], k_out[:, sl])
            outs.append(_head_norm(o, g_ref[:, sl]))
        o_all = jnp.concatenate(outs, axis=-1)
        o_ref[rows, :] = (o_all * (r_gate * jax.nn.sigmoid(r_gate))).astype(o_ref.dtype)
        return carry

    lax.fori_loop(0, ts // CHUNK, chunk, 0, unroll=unroll)


def _gla(yg, wa2, ba, g, batch, seq, ts):
    n = yg.shape[0]
    nt = seq // ts
    masks, right = _gla_constants()
    tri = jnp.asarray(np.tril(np.ones((CHUNK, CHUNK), np.float32)), BF16)
    return pl.pallas_call(
        functools.partial(_gla_kernel, ts=ts, unroll=MIXER_UNROLL),
        grid=(batch, nt),
        in_specs=[pl.BlockSpec((ts, SEG_W), lambda b, i: (b * nt + i, 0)),
                  _full(wa2.shape), _full(ba.shape), _full(g.shape),
                  _full(tri.shape), _full(masks.shape), _full(right.shape)],
        out_specs=pl.BlockSpec((ts, MIX_W), lambda b, i: (b * nt + i, 0)),
        out_shape=jax.ShapeDtypeStruct((n, MIX_W), BF16),
        scratch_shapes=[pltpu.VMEM((MIX_HEADS, HEAD_D, HEAD_D), F32)],
        compiler_params=_params(("parallel", "arbitrary")),
        name="gla_mixer",
    )(yg, wa2, ba, g, tri, jnp.asarray(masks), jnp.asarray(right))


def _mlstm_kernel(y_ref, gr_ref, cw_ref, bcol_ref, brow_ref, g_ref, tri_ref, eb_ref, eye_ref, o_ref,
                  xe_ref, qk_ref, vx_ref, fcl_ref, wsl_ref, fcb_ref, wsb_ref, rv_ref, dec_ref,
                  w_ref, em_ref, upd_ref, c_ref, m_ref, *, ts):
    first = pl.program_id(1) == 0

    @pl.when(first)
    def _():
        xe_ref[0:8, :] = jnp.zeros((8, 2 * MIX_W), F32)
        c_ref[...] = jnp.zeros_like(c_ref)
        m_ref[...] = jnp.zeros_like(m_ref)

    @pl.when(jnp.logical_not(first))
    def _():
        xe_ref[0:8, :] = xe_ref[ts:ts + 8, :]

    xe_ref[8:ts + 8, :] = y_ref[:, 0:2 * MIX_W]
    for c in range(ts // CHUNK):
        r0 = 8 + c * CHUNK
        conv = cw_ref[MLSTM_CONV - 1:MLSTM_CONV, :] * xe_ref[r0:r0 + CHUNK, :]
        for j in range(MLSTM_CONV - 1):
            lo = r0 - (MLSTM_CONV - 1) + j
            conv = conv + cw_ref[j:j + 1, :] * xe_ref[lo:lo + CHUNK, :]
        qk_ref[c * CHUNK:(c + 1) * CHUNK, :] = conv * jax.nn.sigmoid(conv)

    ones_col = (lax.broadcasted_iota(jnp.int32, (ts, HEAD_D), 1) == 0).astype(BF16)
    for h in range(MIX_HEADS):
        v_at, one_at = (0, HEAD_D) if h % 2 == 0 else (HEAD_D, 0)
        vx_ref[:, h * LANES + v_at:h * LANES + v_at + HEAD_D] = (
            y_ref[:, 512 + h * HEAD_D:512 + (h + 1) * HEAD_D].astype(BF16))
        vx_ref[:, h * LANES + one_at:h * LANES + one_at + HEAD_D] = ones_col

    tri = tri_ref[...]
    expand = eb_ref[...]
    nck = ts // CHUNK
    lane = lax.broadcasted_iota(jnp.int32, (1, LANES), 1)
    f_lanes = (lane >= MIX_HEADS) & (lane < 2 * MIX_HEADS)
    head_lane = (lax.broadcasted_iota(jnp.int32, (MIX_HEADS, LANES), 1)
                 == lax.broadcasted_iota(jnp.int32, (MIX_HEADS, LANES), 0) + MIX_HEADS)

    def to_lanes(col):
        return jnp.sum(jnp.where(head_lane, col, 0.0), 0, keepdims=True)

    m_col = m_ref[0:MIX_HEADS, 0:1]
    for c in range(nck):
        g_col = y_ref[c * CHUNK:(c + 1) * CHUNK, 1024:1152] + bcol_ref[...]
        fcum_col = jnp.where(f_lanes, _cumsum_rows(tri, _log_sigmoid(g_col)), 0.0)
        fcl_ref[c * CHUNK:(c + 1) * CHUNK, :] = fcum_col
        g_row = gr_ref[c] + brow_ref[...]
        fcum_row = sum(_dot_nt(part, tri) for part in _split3(_log_sigmoid(g_row)))
        f_row = fcum_row[MIX_HEADS:2 * MIX_HEADS, :]
        i_row = g_row[0:MIX_HEADS, :]
        f_end = f_row[:, CHUNK - 1:CHUNK]
        rv_ref[c, 0:MIX_HEADS, :] = jnp.concatenate(
            [i_row - f_row, jnp.broadcast_to(m_col, (MIX_HEADS, HEAD_D))], axis=1)
        m_new = jnp.maximum(f_end + m_col, jnp.max(f_end - f_row + i_row, -1, keepdims=True))
        m_new_l = to_lanes(m_new)
        f_end_l = fcum_col[CHUNK - 1:CHUNK, :]
        i_shift = pltpu.roll(g_col, MIX_HEADS, 1)
        wsl_ref[c * CHUNK:(c + 1) * CHUNK, :] = jnp.where(
            f_lanes, jnp.exp(f_end_l - fcum_col + i_shift - m_new_l), 0.0)
        dec_ref[c] = jnp.broadcast_to(jnp.exp(f_end_l + to_lanes(m_col) - m_new_l), (SUBLANES, LANES))
        m_col = m_new
    m_ref[0:MIX_HEADS, :] = jnp.broadcast_to(m_col, (MIX_HEADS, LANES))
    slab = 2 * CHUNK
    for i in range(ts // slab):
        sr = slice(i * slab, (i + 1) * slab)
        fcb_ref[sr, :] = sum(_dot(part, expand) for part in _split3(fcl_ref[sr, :]))
        ws_hi, ws_mid, _ = _split3(wsl_ref[sr, :])
        wsb_ref[sr, :] = _dot(ws_hi, expand) + _dot(ws_mid, expand)

    t_idx = lax.broadcasted_iota(jnp.int32, (CHUNK, LANES), 0)
    s_idx = lax.broadcasted_iota(jnp.int32, (CHUNK, LANES), 1)
    keep = (s_idx <= t_idx) | (s_idx >= CHUNK)
    low_half = s_idx < HEAD_D

    def weights(c, carry):
        rows = pl.ds(pl.multiple_of(c * CHUNK, CHUNK), CHUNK)
        rv = rv_ref[c, 0:MIX_HEADS, :]
        for p in range(MIX_HEADS // 2):
            q2 = (qk_ref[rows, p * LANES:(p + 1) * LANES] * (HEAD_D ** -0.5)).astype(BF16)
            k2 = qk_ref[rows, MIX_W + p * LANES:MIX_W + (p + 1) * LANES]
            for h in (2 * p, 2 * p + 1):
                mine = low_half if h % 2 == 0 else jnp.logical_not(low_half)
                hb = slice(h * LANES, (h + 1) * LANES)
                logw = jnp.where(keep, fcb_ref[rows, hb] + rv[h:h + 1, :], -jnp.inf)
                m_t = jnp.max(logw, -1, keepdims=True)
                k_h = jnp.where(mine, k2, 0.0)
                qk = _dot_nt(q2, jnp.concatenate([k_h.astype(BF16), eye_ref[h % 2]], axis=0))
                w_ref[c * MIX_HEADS + h] = (jnp.exp(logw - m_t) * qk).astype(BF16)
                em_ref[c * MIX_HEADS + h] = jnp.exp(-m_t)
                upd = _dot_tn((k_h * wsb_ref[rows, hb]).astype(BF16), vx_ref[rows, hb])
                off = (h % 2) * HEAD_D
                upd_ref[c * MIX_HEADS + h] = upd[off:off + HEAD_D, :]
        return carry

    lax.fori_loop(0, nck, weights, 0, unroll=2)

    def chunk(c, carry):
        rows = pl.ds(pl.multiple_of(c * CHUNK, CHUNK), CHUNK)
        dec_l = dec_ref[c]
        for p in range(MIX_HEADS // 2):
            pl_ = slice(p * LANES, (p + 1) * LANES)
            pair = jnp.zeros((CHUNK, LANES), F32)
            for h in (2 * p, 2 * p + 1):
                mine = low_half if h % 2 == 0 else jnp.logical_not(low_half)
                c_st = c_ref[h]
                num = _dot(w_ref[c * MIX_HEADS + h],
                           jnp.concatenate([vx_ref[rows, h * LANES:(h + 1) * LANES], c_st.astype(BF16)], axis=0))
                c_ref[h] = dec_l[0:1, MIX_HEADS + h:MIX_HEADS + h + 1] * c_st + upd_ref[c * MIX_HEADS + h]
                den = num[:, HEAD_D:HEAD_D + 1] if h % 2 == 0 else num[:, 0:1]
                r = 1.0 / jnp.maximum(jnp.abs(den), em_ref[c * MIX_HEADS + h])
                mu = jnp.sum(jnp.where(mine, num, 0.0), -1, keepdims=True) * (1.0 / HEAD_D)
                cen = jnp.where(mine, num - mu, 0.0)
                var = jnp.sum(cen * cen, -1, keepdims=True) * (1.0 / HEAD_D)
                pair = pair + cen * (r * lax.rsqrt(r * r * var + LN_EPS))
            o_gate = y_ref[rows, 768 + p * LANES:768 + (p + 1) * LANES]
            o_ref[rows, pl_] = (pair * g_ref[:, pl_] * jax.nn.sigmoid(o_gate)).astype(o_ref.dtype)
        return carry

    lax.fori_loop(0, nck, chunk, 0, unroll=2)


def _mlstm(ym, gates_row, conv_w, b_col, b_row, g, batch, seq, ts):
    n = ym.shape[0]
    nt = seq // ts
    nck = ts // CHUNK
    tri = jnp.asarray(np.tril(np.ones((CHUNK, CHUNK), np.float32)), BF16)
    expand = np.zeros((LANES, MIX_HEADS * LANES), np.float32)
    for h in range(MIX_HEADS):
        expand[MIX_HEADS + h, h * LANES:(h + 1) * LANES] = 1.0
    eye = np.zeros((2, HEAD_D, LANES), np.float32)
    eye[0, :, 0:HEAD_D] = np.eye(HEAD_D)
    eye[1, :, HEAD_D:] = np.eye(HEAD_D)
    expand = jnp.asarray(expand, BF16)
    eye = jnp.asarray(eye, BF16)
    return pl.pallas_call(
        functools.partial(_mlstm_kernel, ts=ts),
        grid=(batch, nt),
        in_specs=[pl.BlockSpec((ts, SEG_W), lambda b, i: (b * nt + i, 0)),
                  pl.BlockSpec((nck, 8, CHUNK), lambda b, i: (b * nt + i, 0, 0)),
                  _full(conv_w.shape), _full(b_col.shape), _full(b_row.shape), _full(g.shape),
                  _full(tri.shape), _full(expand.shape), _full(eye.shape)],
        out_specs=pl.BlockSpec((ts, MIX_W), lambda b, i: (b * nt + i, 0)),
        out_shape=jax.ShapeDtypeStruct((n, MIX_W), BF16),
        scratch_shapes=[pltpu.VMEM((ts + 8, 2 * MIX_W), F32),
                        pltpu.VMEM((ts, 2 * MIX_W), F32),
                        pltpu.VMEM((ts, MIX_HEADS * LANES), BF16),
                        pltpu.VMEM((ts, LANES), F32),
                        pltpu.VMEM((ts, LANES), F32),
                        pltpu.VMEM((ts, MIX_HEADS * LANES), F32),
                        pltpu.VMEM((ts, MIX_HEADS * LANES), F32),
                        pltpu.VMEM((nck, SUBLANES, LANES), F32),
                        pltpu.VMEM((nck, SUBLANES, LANES), F32),
                        pltpu.VMEM((nck * MIX_HEADS, CHUNK, LANES), BF16),
                        pltpu.VMEM((nck * MIX_HEADS, CHUNK, 1), F32),
                        pltpu.VMEM((nck * MIX_HEADS, HEAD_D, LANES), F32),
                        pltpu.VMEM((MIX_HEADS, HEAD_D, LANES), F32),
                        pltpu.VMEM((SUBLANES, LANES), F32)],
        compiler_params=_params(("parallel", "arbitrary")),
        name="mlstm_mixer",
    )(ym, gates_row, conv_w, b_col, b_row, g, tri, expand, eye)


def _rope_table_kernel(pos_ref, inv_ref, cos_ref, sin_ref):
    ang = pos_ref[...].astype(F32) * inv_ref[...]
    lane = lax.broadcasted_iota(jnp.int32, ang.shape, 1)
    rot = (lane >= MLA_NOPE) & (lane < MLA_NOPE + MLA_ROPE)
    first_half = lane < MLA_NOPE + MLA_ROPE // 2
    cos_ref[...] = jnp.where(lane < MLA_NOPE, 1.0, jnp.where(rot, jnp.cos(ang), 0.0))
    s = jnp.sin(ang)
    sin_ref[...] = jnp.where(rot, jnp.where(first_half, -s, s), 0.0)


def _rope_tables(pos_col, tm):
    n = pos_col.shape[0]
    half = MLA_ROPE // 2
    inv = ROPE_BASE ** (-np.arange(half, dtype=np.float32) / half)
    inv_row = np.zeros((1, LANES), np.float32)
    inv_row[0, MLA_NOPE:MLA_NOPE + half] = inv
    inv_row[0, MLA_NOPE + half:MLA_NOPE + MLA_ROPE] = inv
    return pl.pallas_call(
        _rope_table_kernel,
        grid=(n // tm,),
        in_specs=[pl.BlockSpec((tm, 1), lambda i: (i, 0)), _full((1, LANES))],
        out_specs=[pl.BlockSpec((tm, LANES), lambda i: (i, 0))] * 2,
        out_shape=[jax.ShapeDtypeStruct((n, LANES), F32)] * 2,
        compiler_params=_params(("parallel",)),
        name="rope_tables",
    )(pos_col, jnp.asarray(inv_row))


def _mla_prep_kernel(y_ref, cos_ref, sin_ref, gq_ref, gkv_ref, wqa_ref, wqb_ref, wkn_ref, wv_ref, ones_ref,
                     q_ref, k_ref, v_ref):
    def rms(x, g):
        return x * lax.rsqrt(jnp.mean(x * x, -1, keepdims=True) + LN_EPS) * g

    cos = cos_ref[...]
    sin = sin_ref[...]
    cq = rms(y_ref[:, 0:MLA_Q_RANK], gq_ref[...]).astype(BF16)
    ckv = rms(y_ref[:, MLA_Q_RANK:MLA_Q_RANK + MLA_KV_RANK], gkv_ref[...]).astype(BF16)
    k_rope = y_ref[:, 384:512] * cos + y_ref[:, 512:640] * sin
    qa = _dot(cq, wqa_ref[...])
    qb = _dot(cq, wqb_ref[...])
    kn = _dot(ckv, wkn_ref[...])
    scale = (MLA_NOPE + MLA_ROPE) ** -0.5 * LOG2_E
    for h in range(MLA_HEADS):
        sl = slice(h * MLA_QK_PAD, (h + 1) * MLA_QK_PAD)
        q_ref[:, sl] = ((qa[:, sl] * cos + qb[:, sl] * sin) * scale).astype(q_ref.dtype)
        k_ref[:, sl] = (kn[:, sl] + k_rope).astype(k_ref.dtype)
    v_ref[...] = (_dot(ckv, wv_ref[...]) + ones_ref[...]).astype(v_ref.dtype)


def _mla_prep(yc, cos_t, sin_t, gq, gkv, wqa, wqb, wkn, wv, tm):
    ones_row = np.zeros((1, MLA_HEADS * LANES), np.float32)
    for h in range(MLA_HEADS):
        ones_row[0, h * LANES + (MLA_V if h % 2 == 0 else 0)] = 1.0
    ones_row = jnp.asarray(ones_row)
    n = yc.shape[0]
    row = lambda w: pl.BlockSpec((tm, w), lambda i: (i, 0))
    qk_w = MLA_HEADS * MLA_QK_PAD
    return pl.pallas_call(
        _mla_prep_kernel,
        grid=(n // tm,),
        in_specs=[row(MLA_SEG_W), row(LANES), row(LANES), _full(gq.shape), _full(gkv.shape),
                  _full(wqa.shape), _full(wqb.shape), _full(wkn.shape), _full(wv.shape), _full(ones_row.shape)],
        out_specs=[row(qk_w), row(qk_w), row(MLA_HEADS * LANES)],
        out_shape=[jax.ShapeDtypeStruct((n, qk_w), BF16), jax.ShapeDtypeStruct((n, qk_w), BF16),
                   jax.ShapeDtypeStruct((n, MLA_HEADS * LANES), BF16)],
        compiler_params=_params(("parallel",)),
        name="mla_prep",
    )(yc, cos_t, sin_t, gq, gkv, wqa, wqb, wkn, wv, ones_row)


def _mla_attn_kernel(q_ref, k_ref, v_ref, o_ref, *, seq, tq):
    t_chunk = lax.broadcasted_iota(jnp.int32, (tq, tq), 0) // CHUNK
    s_chunk = lax.broadcasted_iota(jnp.int32, (tq, tq), 1) // CHUNK
    diag_mask = s_chunk <= t_chunk
    low_half = lax.broadcasted_iota(jnp.int32, (tq, LANES), 1) < MLA_V
    for i in range(seq // tq):
        rows = slice(i * tq, (i + 1) * tq)
        res = []
        for hh in range(MLA_HEAD_PAIR):
            ql = slice(hh * MLA_QK_PAD, (hh + 1) * MLA_QK_PAD)
            vl = slice(hh * LANES, (hh + 1) * LANES)
            q = q_ref[0, rows, ql]
            s_d = jnp.where(diag_mask, _dot_nt(q, k_ref[0, rows, ql]), -jnp.inf)
            m = jnp.max(s_d, -1, keepdims=True)
            if i > 0:
                s_o = _dot_nt(q, k_ref[0, 0:i * tq, ql])
                m = jnp.maximum(m, jnp.max(s_o, -1, keepdims=True))
            o = _dot(jnp.exp2(s_d - m).astype(BF16), v_ref[0, rows, vl])
            if i > 0:
                o = o + _dot(jnp.exp2(s_o - m).astype(BF16), v_ref[0, 0:i * tq, vl])
            l = o[:, MLA_V:MLA_V + 1] if hh == 0 else o[:, 0:1]
            res.append(o / l)
        o_ref[0, rows, :] = jnp.where(low_half, res[0], res[1]).astype(o_ref.dtype)


def _mla_attn(q, k, v, batch, seq, tq):
    n = q.shape[0]
    qk_w = MLA_HEADS * MLA_QK_PAD
    v_w = MLA_HEADS * MLA_V
    pair = lambda w: pl.BlockSpec((1, seq, MLA_HEAD_PAIR * w), lambda b, h: (b, 0, h))
    out = pl.pallas_call(
        functools.partial(_mla_attn_kernel, seq=seq, tq=tq),
        grid=(batch, MLA_HEADS // MLA_HEAD_PAIR),
        in_specs=[pair(MLA_QK_PAD), pair(MLA_QK_PAD), pair(LANES)],
        out_specs=pair(MLA_V),
        out_shape=jax.ShapeDtypeStruct((batch, seq, v_w), BF16),
        compiler_params=_params(("parallel", "parallel")),
        name="mla_attention",
    )(q.reshape(batch, seq, qk_w), k.reshape(batch, seq, qk_w), v.reshape(batch, seq, MLA_HEADS * LANES))
    return out.reshape(n, v_w)


def _out_proj_kernel(og_ref, om_ref, oc_ref, x_ref, wg_ref, wm_ref, wc_ref, g_ref, b_ref, o_ref):
    mix = _dot(og_ref[...], wg_ref[...]) + _dot(om_ref[...], wm_ref[...]) + _dot(oc_ref[...], wc_ref[...])
    o_ref[...] = _layer_norm(ALPHA * x_ref[...] + mix, g_ref[...], b_ref[...])


def _out_proj(og, om, oc, x2d, wg, wm, wc, g, b, tm):
    n = x2d.shape[0]
    row = lambda w: pl.BlockSpec((tm, w), lambda i: (i, 0))
    return pl.pallas_call(
        _out_proj_kernel,
        grid=(n // tm,),
        in_specs=[row(MIX_W), row(MIX_W), row(MLA_HEADS * MLA_V), row(D_MODEL),
                  _full(wg.shape), _full(wm.shape), _full(wc.shape), _full(g.shape), _full(b.shape)],
        out_specs=row(D_MODEL),
        out_shape=jax.ShapeDtypeStruct((n, D_MODEL), F32),
        compiler_params=_params(("parallel",)),
        name="out_proj_ln1",
    )(og, om, oc, x2d, wg, wm, wc, g, b)


def _xa_kv_kernel(mem_ref, w_ref, k_ref, v_ref):
    kv = _dot(mem_ref[...].astype(BF16), w_ref[...])
    k_ref[...] = kv[:, 0:D_MODEL].astype(k_ref.dtype)
    v_ref[...] = kv[:, D_MODEL:2 * D_MODEL].astype(v_ref.dtype)


def _xa_kv(mem2d, w_kv, mem_len):
    n = mem2d.shape[0]
    row = pl.BlockSpec((mem_len, D_MODEL), lambda i: (i, 0))
    return pl.pallas_call(
        _xa_kv_kernel,
        grid=(n // mem_len,),
        in_specs=[row, _full(w_kv.shape)],
        out_specs=[row, row],
        out_shape=[jax.ShapeDtypeStruct((n, D_MODEL), BF16)] * 2,
        compiler_params=_params(("parallel",)),
        name="xattn_kv",
    )(mem2d, w_kv)


def _xattn_kernel(x_ref, k_ref, v_ref, wq_ref, wo_ref, g_ref, b_ref, o_ref):
    x = x_ref[...]
    q = (_dot(x.astype(BF16), wq_ref[...]) * (XA_DH ** -0.5)).astype(BF16)
    out = jnp.zeros(x.shape, F32)
    for h in range(XA_HEADS):
        sl = slice(h * XA_DH, (h + 1) * XA_DH)
        s = _dot_nt(q[:, sl], k_ref[:, sl])
        p = jnp.exp(s - jnp.max(s, -1, keepdims=True))
        p = p / jnp.sum(p, -1, keepdims=True)
        o = _dot(p.astype(BF16), v_ref[:, sl])
        out = out + _dot(o.astype(BF16), wo_ref[sl, :])
    o_ref[...] = _layer_norm(ALPHA * x + out, g_ref[...], b_ref[...])


def _xattn(x1, xk, xv, wq, wo, g, b, batch, seq, mem_len, tm):
    n = x1.shape[0]
    nt = seq // tm
    row = pl.BlockSpec((tm, D_MODEL), lambda bi, i: (bi * nt + i, 0))
    mem = pl.BlockSpec((mem_len, D_MODEL), lambda bi, i: (bi, 0))
    return pl.pallas_call(
        _xattn_kernel,
        grid=(batch, nt),
        in_specs=[row, mem, mem, _full(wq.shape), _full(wo.shape), _full(g.shape), _full(b.shape)],
        out_specs=row,
        out_shape=jax.ShapeDtypeStruct((n, D_MODEL), F32),
        compiler_params=_params(("parallel", "parallel")),
        name="xattn_ln2",
    )(x1, xk, xv, wq, wo, g, b)


def _route_kernel(x_ref, w_ref, b_ref, tri_ref, xg_ref, rank_ref, grp_ref, cnt_out_ref, cnt_ref):
    @pl.when(pl.program_id(0) == 0)
    def _():
        cnt_ref[...] = jnp.zeros_like(cnt_ref)

    x = x_ref[...]
    logits = jnp.dot(x, w_ref[...], precision=HIGHEST, preferred_element_type=F32) + b_ref[...]
    lane = lax.broadcasted_iota(jnp.int32, logits.shape, 1).astype(F32)
    is_group = (lane >= N_EXPERTS) & (lane < N_EXPERTS + N_GROUPS)
    g_max = jnp.max(jnp.where(is_group, logits, -jnp.inf), -1, keepdims=True)
    g_sum = jnp.sum(jnp.where(is_group, jnp.exp(logits - g_max), 0.0), -1, keepdims=True)
    g_p = 1.0 / g_sum
    g_idx = jnp.min(jnp.where(is_group & (logits == g_max), lane - N_EXPERTS, float(LANES)), -1, keepdims=True)
    in_group = (lane < N_EXPERTS) & (jnp.floor(lane * (1.0 / EXPERTS_PER_GROUP)) == g_idx)
    e_max = jnp.max(jnp.where(in_group, logits, -jnp.inf), -1, keepdims=True)
    e_exp = jnp.where(in_group, jnp.exp(logits - e_max), 0.0)
    prob = e_exp / jnp.sum(e_exp, -1, keepdims=True)
    cand = jnp.where(in_group, prob, -1.0)
    p1 = jnp.max(cand, -1, keepdims=True)
    i1 = jnp.min(jnp.where(cand == p1, lane, float(LANES)), -1, keepdims=True)
    cand2 = jnp.where(lane == i1, -1.0, cand)
    p2 = jnp.max(cand2, -1, keepdims=True)
    i2 = jnp.min(jnp.where(cand2 == p2, lane, float(LANES)), -1, keepdims=True)
    p_sum = p1 + p2
    xg_ref[:, 0:D_MODEL] = x
    xg_ref[:, D_MODEL:] = (jnp.where(lane == i1, g_p * (p1 / p_sum), 0.0)
                           + jnp.where(lane == i2, g_p * (p2 / p_sum), 0.0))
    onehot = jnp.where(lane == g_idx, 1.0, 0.0)
    before = _dot(tri_ref[...], onehot.astype(BF16)) + cnt_ref[...]
    rank_ref[...] = jnp.sum(onehot * before, -1, keepdims=True).astype(jnp.int32)
    grp_ref[...] = g_idx.astype(jnp.int32)
    cnt_ref[...] += jnp.sum(onehot, 0, keepdims=True)
    cnt_out_ref[...] = cnt_ref[...]


def _route(x2, w_route, b_route, tm):
    n = x2.shape[0]
    tri = jnp.asarray(np.tril(np.ones((tm, tm), np.float32), -1), BF16)
    col = pl.BlockSpec((tm, 1), lambda i: (i, 0))
    return pl.pallas_call(
        _route_kernel,
        grid=(n // tm,),
        in_specs=[pl.BlockSpec((tm, D_MODEL), lambda i: (i, 0)), _full(w_route.shape), _full(b_route.shape),
                  _full(tri.shape)],
        out_specs=[pl.BlockSpec((tm, XG_W), lambda i: (i, 0)), col, col, _full((1, LANES))],
        out_shape=[jax.ShapeDtypeStruct((n, XG_W), F32), jax.ShapeDtypeStruct((n, 1), jnp.int32),
                   jax.ShapeDtypeStruct((n, 1), jnp.int32), jax.ShapeDtypeStruct((1, LANES), F32)],
        scratch_shapes=[pltpu.VMEM((1, LANES), F32)],
        compiler_params=_params(("arbitrary",)),
        name="moe_route",
    )(x2, w_route, b_route, tri)


def _dispatch_kernel(pos_ref, x_ref, xs_hbm, sem, *, tm):
    def issue(t, carry):
        pltpu.make_async_copy(x_ref.at[pl.ds(t, 1)], xs_hbm.at[pl.ds(pos_ref[t], 1)], sem).start()
        return carry

    lax.fori_loop(0, tm, issue, 0, unroll=DMA_UNROLL)
    pltpu.make_async_copy(x_ref, xs_hbm.at[pl.ds(0, tm)], sem).wait()


def _dispatch(xg, pos, tm):
    n = xg.shape[0]
    return pl.pallas_call(
        functools.partial(_dispatch_kernel, tm=tm),
        grid=(n // tm,),
        in_specs=[pl.BlockSpec((tm,), lambda i: (i,), memory_space=pltpu.SMEM),
                  pl.BlockSpec((tm, XG_W), lambda i: (i, 0))],
        out_specs=pl.BlockSpec(memory_space=pl.ANY),
        out_shape=jax.ShapeDtypeStruct(xg.shape, xg.dtype),
        scratch_shapes=[pltpu.SemaphoreType.DMA(())],
        compiler_params=_params(("arbitrary",)),
        name="moe_dispatch",
    )(pos, xg)


def _moe_ffn_kernel(blk_ref, grp_ref, first_ref, valid_ref, xs_ref, wg_ref, wu_ref, wd_ref, o_ref, xb_ref):
    w = pl.program_id(0)
    e = pl.program_id(1)

    @pl.when((first_ref[w] == 1) & (e == 0))
    def _():
        xb_ref[...] = xs_ref[:, 0:D_MODEL].astype(BF16)
        o_ref[...] = jnp.zeros_like(o_ref)

    @pl.when(valid_ref[w] == 1)
    def _():
        xb = xb_ref[...]
        gates = xs_ref[:, D_MODEL:]
        lane = lax.broadcasted_iota(jnp.int32, gates.shape, 1)
        w_tok = jnp.sum(jnp.where(lane == grp_ref[w] * EXPERTS_PER_GROUP + e, gates, 0.0), -1, keepdims=True)
        hg = _dot(xb, wg_ref[0].astype(BF16))
        hu = _dot(xb, wu_ref[0].astype(BF16))
        hid = hg * jax.nn.sigmoid(hg) * hu * w_tok
        o_ref[...] += _dot(hid.astype(BF16), wd_ref[0].astype(BF16))


def _moe_ffn(xs, items, w_gate, w_up, w_down, rb, e_base):
    n = xs.shape[0]
    blk, grp, first, valid = items

    def expert(w, e, blk, grp, first, valid):
        return (e_base + grp[w] * EXPERTS_PER_GROUP + jnp.where(valid[w] == 1, e, EXPERTS_PER_GROUP - 1), 0, 0)

    grid_spec = pltpu.PrefetchScalarGridSpec(
        num_scalar_prefetch=4,
        grid=(blk.shape[0], EXPERTS_PER_GROUP),
        in_specs=[pl.BlockSpec((rb, XG_W), lambda w, e, blk, grp, first, valid: (blk[w], 0)),
                  pl.BlockSpec((1, D_MODEL, D_EXPERT), expert),
                  pl.BlockSpec((1, D_MODEL, D_EXPERT), expert),
                  pl.BlockSpec((1, D_EXPERT, D_MODEL), expert)],
        out_specs=pl.BlockSpec((rb, D_MODEL), lambda w, e, blk, grp, first, valid: (blk[w], 0)),
        scratch_shapes=[pltpu.VMEM((rb, D_MODEL), BF16)],
    )
    return pl.pallas_call(
        _moe_ffn_kernel,
        grid_spec=grid_spec,
        out_shape=jax.ShapeDtypeStruct((n, D_MODEL), F32),
        compiler_params=_params(("arbitrary", "arbitrary")),
        name="moe_experts",
    )(blk, grp, first, valid, xs, w_gate, w_up, w_down)


def _combine_kernel(pos_ref, xg_ref, ys_hbm, g_ref, b_ref, o_ref, buf_ref, sem, *, tm):
    def issue(t, carry):
        pltpu.make_async_copy(ys_hbm.at[pl.ds(pos_ref[t], 1)], buf_ref.at[pl.ds(t, 1)], sem).start()
        return carry

    lax.fori_loop(0, tm, issue, 0, unroll=DMA_UNROLL)
    pltpu.make_async_copy(ys_hbm.at[pl.ds(0, tm)], buf_ref, sem).wait()
    o_ref[...] = _layer_norm(ALPHA * xg_ref[:, 0:D_MODEL] + buf_ref[...], g_ref[...], b_ref[...])


def _combine(pos, xg, ys, g, b, tm):
    n = xg.shape[0]
    return pl.pallas_call(
        functools.partial(_combine_kernel, tm=tm),
        grid=(n // tm,),
        in_specs=[pl.BlockSpec((tm,), lambda i: (i,), memory_space=pltpu.SMEM),
                  pl.BlockSpec((tm, XG_W), lambda i: (i, 0)),
                  pl.BlockSpec(memory_space=pl.ANY), _full(g.shape), _full(b.shape)],
        out_specs=pl.BlockSpec((tm, D_MODEL), lambda i: (i, 0)),
        out_shape=jax.ShapeDtypeStruct((n, D_MODEL), F32),
        scratch_shapes=[pltpu.VMEM((tm, D_MODEL), F32), pltpu.SemaphoreType.DMA(())],
        compiler_params=_params(("arbitrary",)),
        name="moe_combine_ln3",
    )(pos, xg, ys, g, b)


def _moe_work_items(counts, n, rb):
    nb = n // rb
    n_items = nb + N_GROUPS - 1
    ends = jnp.cumsum(counts)
    start = jnp.arange(nb, dtype=jnp.int32) * rb
    g_lo = jnp.sum(ends[None, :] <= start[:, None], axis=1).astype(jnp.int32)
    g_hi = jnp.sum(ends[None, :] <= (start + rb - 1)[:, None], axis=1).astype(jnp.int32)
    per_blk = g_hi - g_lo + 1
    item0 = jnp.cumsum(per_blk) - per_blk
    w = jnp.arange(n_items, dtype=jnp.int32)
    valid = w < jnp.sum(per_blk)
    blk = jnp.clip(jnp.sum(item0[None, :] <= w[:, None], axis=1) - 1, 0, nb - 1).astype(jnp.int32)
    grp = jnp.where(valid, g_lo[blk] + (w - item0[blk]), g_hi[nb - 1]).astype(jnp.int32)
    first = (valid & (w == item0[blk])).astype(jnp.int32)
    return blk, grp, first, valid.astype(jnp.int32)


def _moe(x2, w_route, b_route, w_gate, w_up, w_down, e_base, g, b, tm, t_dma, rb):
    n = x2.shape[0]
    xg, rank, grp, counts = _route(x2, w_route, b_route, tm)
    counts = counts[0, :N_GROUPS].astype(jnp.int32)
    offsets = jnp.cumsum(counts) - counts
    pos = (offsets[grp[:, 0]] + rank[:, 0]).astype(jnp.int32)
    xs = _dispatch(xg, pos, t_dma)
    ys = _moe_ffn(xs, _moe_work_items(counts, n, rb), w_gate, w_up, w_down, rb, e_base)
    return _combine(pos, xg, ys, g, b, t_dma)


def _pad_cols(w, width):
    return jnp.pad(w, ((0, 0), (0, width - w.shape[1])))


def _layer_weights(w_in, w_out, gla_w_a2, ml_b_i, ml_b_f, mla_w_uq, mla_w_ukv, moe_w_group, moe_b_group,
                   moe_w_router, moe_b_router):
    o = _IN_OFF
    half = MLA_ROPE // 2
    wg = _pad_cols(w_in[:, o[0]:o[5]], SEG_W).astype(BF16)
    wm = _pad_cols(w_in[:, o[5]:o[10]], SEG_W).astype(BF16)
    kr = w_in[:, o[12]:o[13]]
    zeros = lambda w: jnp.zeros((D_MODEL, w), F32)
    kra = jnp.concatenate([zeros(MLA_NOPE), kr, zeros(LANES - MLA_NOPE - MLA_ROPE)], 1)
    krb = jnp.concatenate([zeros(MLA_NOPE), kr[:, half:], kr[:, :half], zeros(LANES - MLA_NOPE - MLA_ROPE)], 1)
    wc = jnp.concatenate([w_in[:, o[10]:o[12]], kra, krb], 1).astype(BF16)
    wift = w_in[:, o[8]:o[10]].T.astype(BF16)
    wa2 = jnp.pad(gla_w_a2, ((0, LANES - GLA_GATE_RANK), (0, 0))).astype(BF16)
    b_gate = jnp.concatenate([ml_b_i, ml_b_f])
    b_col = jnp.pad(b_gate, (0, LANES - 2 * MIX_HEADS)).reshape(1, LANES)
    b_row = b_gate.reshape(2 * MIX_HEADS, 1)
    uq = mla_w_uq.reshape(MLA_Q_RANK, MLA_HEADS, MLA_NOPE + MLA_ROPE)
    zq = jnp.zeros((MLA_Q_RANK, MLA_HEADS, LANES - MLA_NOPE - MLA_ROPE), F32)
    wqa = jnp.concatenate([uq, zq], -1).reshape(MLA_Q_RANK, -1).astype(BF16)
    wqb = jnp.concatenate([jnp.zeros((MLA_Q_RANK, MLA_HEADS, MLA_NOPE), F32), uq[..., MLA_NOPE + half:],
                           uq[..., MLA_NOPE:MLA_NOPE + half], zq], -1).reshape(MLA_Q_RANK, -1).astype(BF16)
    ukv = mla_w_ukv.reshape(MLA_KV_RANK, MLA_HEADS, MLA_NOPE + MLA_V)
    wkn = jnp.concatenate([ukv[..., :MLA_NOPE], jnp.zeros((MLA_KV_RANK, MLA_HEADS, LANES - MLA_NOPE), F32)],
                          -1).reshape(MLA_KV_RANK, -1).astype(BF16)
    uv = ukv[..., MLA_NOPE:]
    zv = jnp.zeros_like(uv)
    odd = (jnp.arange(MLA_HEADS) % 2 == 1)[None, :, None]
    wv = jnp.concatenate([jnp.where(odd, zv, uv), jnp.where(odd, uv, zv)], -1).reshape(MLA_KV_RANK, -1).astype(BF16)
    wo = w_out.astype(BF16)
    w_route = _pad_cols(jnp.concatenate([moe_w_router, moe_w_group], 1), LANES)
    b_route = jnp.pad(jnp.concatenate([moe_b_router, moe_b_group]), (0, LANES - N_EXPERTS - N_GROUPS)).reshape(1, LANES)
    return dict(wg=wg, wm=wm, wc=wc, wift=wift, wa2=wa2, b_col=b_col, b_row=b_row, wqa=wqa, wqb=wqb, wkn=wkn,
                wv=wv, wo_g=wo[0:MIX_W], wo_m=wo[MIX_W:2 * MIX_W], wo_c=wo[2 * MIX_W:], w_route=w_route,
                b_route=b_route)


def _tile(total, want):
    t = min(total, want)
    assert total % t == 0
    return t


def kernel(x, mem, positions, w_in, w_out, gla_w_a2, gla_b_a, gla_norm_g, ml_conv_w, ml_b_i, ml_b_f, ml_norm_g, mla_q_norm_g, mla_w_uq, mla_kv_norm_g, mla_w_ukv, xa_w_q, xa_w_kv, xa_w_o, moe_w_group, moe_b_group, moe_w_router, moe_b_router, moe_w_gate, moe_w_up, moe_w_down, ln1_g, ln1_b, ln2_g, ln2_b, ln3_g, ln3_b):
    batch, seq, _ = x.shape
    mem_len = mem.shape[1]
    n = batch * seq
    depth = w_in.shape[0]
    assert seq % CHUNK == 0
    tm = _tile(n, 512)
    ts = _tile(seq, 512)
    tq = _tile(seq, 512)
    t_moe = _tile(n, 1024)
    row = lambda a: a.reshape(1, -1)

    cos_t, sin_t = _rope_tables(positions.reshape(n, 1), _tile(n, 2048))
    mem2d = mem.reshape(batch * mem_len, D_MODEL)
    experts_gate = moe_w_gate.reshape(-1, D_MODEL, D_EXPERT)
    experts_up = moe_w_up.reshape(-1, D_MODEL, D_EXPERT)
    experts_down = moe_w_down.reshape(-1, D_EXPERT, D_MODEL)
    h = x.reshape(n, D_MODEL)
    for l in range(depth):
        w = _layer_weights(w_in[l], w_out[l], gla_w_a2[l], ml_b_i[l], ml_b_f[l], mla_w_uq[l], mla_w_ukv[l],
                           moe_w_group[l], moe_b_group[l], moe_w_router[l], moe_b_router[l])
        yg, ym, yc, yift = _in_proj(h, w["wg"], w["wm"], w["wc"], w["wift"], tm)
        gates_row = yift.reshape(2 * MIX_HEADS, n // CHUNK, CHUNK).transpose(1, 0, 2)
        og = _gla(yg, w["wa2"], row(gla_b_a[l]), row(gla_norm_g[l]), batch, seq, ts)
        om = _mlstm(ym, gates_row, ml_conv_w[l], w["b_col"], w["b_row"], row(ml_norm_g[l]), batch, seq, ts)
        q, k, v = _mla_prep(yc, cos_t, sin_t, row(mla_q_norm_g[l]), row(mla_kv_norm_g[l]),
                            w["wqa"], w["wqb"], w["wkn"], w["wv"], tm)
        oc = _mla_attn(q, k, v, batch, seq, tq)
        x1 = _out_proj(og, om, oc, h, w["wo_g"], w["wo_m"], w["wo_c"], row(ln1_g[l]), row(ln1_b[l]), tm)
        xk, xv = _xa_kv(mem2d, xa_w_kv[l].astype(BF16), mem_len)
        x2 = _xattn(x1, xk, xv, xa_w_q[l].astype(BF16), xa_w_o[l].astype(BF16), row(ln2_g[l]), row(ln2_b[l]),
                    batch, seq, mem_len, tm)
        h = _moe(x2, w["w_route"], w["b_route"], experts_gate, experts_up, experts_down, l * N_EXPERTS,
                 row(ln3_g[l]), row(ln3_b[l]), tm, t_moe, t_moe)
    return h.reshape(batch, seq, D_MODEL)
```

```python
import functools

import numpy as np
import jax
import jax.numpy as jnp
from jax import lax
from jax.experimental import pallas as pl
from jax.experimental.pallas import tpu as pltpu

F32 = jnp.float32
BF16 = jnp.bfloat16

D_MODEL = 1024
CHUNK = 64
HEAD_D = 64
MIX_HEADS = 4
MIX_W = MIX_HEADS * HEAD_D
GLA_GATE_RANK = 16
GLA_TAU = 16.0
MLSTM_CONV = 4
MLA_HEADS = 8
MLA_NOPE = 64
MLA_ROPE = 32
MLA_V = 64
MLA_Q_RANK = 256
MLA_KV_RANK = 128
MLA_QK_PAD = 128
ROPE_BASE = 10000.0
LOG2_E = 1.4426950408889634
MLA_HEAD_PAIR = 2
XA_HEADS = 4
XA_DH = D_MODEL // XA_HEADS
N_GROUPS = 4
EXPERTS_PER_GROUP = 8
N_EXPERTS = N_GROUPS * EXPERTS_PER_GROUP
D_EXPERT = 256
DEPTH = 2
ALPHA = (2 * DEPTH) ** 0.25
LN_EPS = 1e-5
LANES = 128
SUBLANES = 8
MIXER_UNROLL = 4
SEG_W = 1152
MLA_SEG_W = 640
XG_W = D_MODEL + LANES
DMA_UNROLL = 8
GLA_LEVELS = (32, 16, 8, 4, 2, 1)
VMEM_LIMIT = 56 * 1024 * 1024

_IN_SIZES = (256, 256, 256, 256, GLA_GATE_RANK, 512, 256, 256, 4, 4, MLA_Q_RANK, MLA_KV_RANK, MLA_ROPE)
_IN_OFF = np.concatenate([[0], np.cumsum(_IN_SIZES)]).tolist()


def _params(sem):
    return pltpu.CompilerParams(dimension_semantics=sem, vmem_limit_bytes=VMEM_LIMIT)


def _full(shape):
    return pl.BlockSpec(shape, lambda *_: (0,) * len(shape))


def _layer_norm(x, g, b):
    mu = jnp.mean(x, -1, keepdims=True)
    xc = x - mu
    var = jnp.mean(xc * xc, -1, keepdims=True)
    return xc * lax.rsqrt(var + LN_EPS) * g + b


def _log_sigmoid(z):
    return jnp.minimum(z, 0.0) - jnp.log1p(jnp.exp(-jnp.abs(z)))


def _dot_nt(a, b):
    return lax.dot_general(a, b, (((1,), (1,)), ((), ())), preferred_element_type=F32)


def _dot_tn(a, b):
    return lax.dot_general(a, b, (((0,), (0,)), ((), ())), preferred_element_type=F32)


def _dot(a, b):
    return jnp.dot(a, b, preferred_element_type=F32)


def _head_norm(o, g):
    mu = jnp.mean(o, -1, keepdims=True)
    oc = o - mu
    var = jnp.mean(oc * oc, -1, keepdims=True)
    return oc * lax.rsqrt(var + LN_EPS) * g


def _in_proj_kernel(x_ref, wg_ref, wm_ref, wc_ref, wift_ref, cos_ref, sin_ref, gq_ref, gkv_ref,
                    wqa_ref, wqb_ref, wkn_ref, wv_ref, ones_ref,
                    yg_ref, ym_ref, yift_ref, q_ref, k_ref, v_ref, yc_ref):
    xb = x_ref[...].astype(BF16)
    yg_ref[...] = _dot(xb, wg_ref[...])
    ym_ref[...] = _dot(xb, wm_ref[...])
    yift_ref[...] = _dot_nt(wift_ref[...], xb)
    yc_ref[...] = _dot(xb, wc_ref[...])
    _mla_prep_tile(yc_ref, cos_ref, sin_ref, gq_ref, gkv_ref, wqa_ref, wqb_ref, wkn_ref, wv_ref, ones_ref,
                   q_ref, k_ref, v_ref)


def _in_proj(x2d, wg, wm, wc, wift, cos_t, sin_t, gq, gkv, wqa, wqb, wkn, wv, tm):
    n = x2d.shape[0]
    ones_row = np.zeros((1, MLA_HEADS * LANES), np.float32)
    for h in range(MLA_HEADS):
        ones_row[0, h * LANES + (MLA_V if h % 2 == 0 else 0)] = 1.0
    ones_row = jnp.asarray(ones_row)
    row = lambda w: pl.BlockSpec((tm, w), lambda i: (i, 0))
    qk_w = MLA_HEADS * MLA_QK_PAD
    return pl.pallas_call(
        _in_proj_kernel,
        grid=(n // tm,),
        in_specs=[row(D_MODEL), _full(wg.shape), _full(wm.shape), _full(wc.shape), _full(wift.shape),
                  row(LANES), row(LANES), _full(gq.shape), _full(gkv.shape),
                  _full(wqa.shape), _full(wqb.shape), _full(wkn.shape), _full(wv.shape), _full(ones_row.shape)],
        out_specs=[row(SEG_W), row(SEG_W), pl.BlockSpec((8, tm), lambda i: (0, i)),
                   row(qk_w), row(qk_w), row(MLA_HEADS * LANES)],
        out_shape=[jax.ShapeDtypeStruct((n, SEG_W), F32), jax.ShapeDtypeStruct((n, SEG_W), F32),
                   jax.ShapeDtypeStruct((8, n), F32),
                   jax.ShapeDtypeStruct((n, qk_w), BF16), jax.ShapeDtypeStruct((n, qk_w), BF16),
                   jax.ShapeDtypeStruct((n, MLA_HEADS * LANES), BF16)],
        scratch_shapes=[pltpu.VMEM((tm, MLA_SEG_W), F32)],
        compiler_params=_params(("parallel",)),
        name="in_proj_mla_prep",
    )(x2d, wg, wm, wc, wift, cos_t, sin_t, gq, gkv, wqa, wqb, wkn, wv, ones_row)


def _split3(x):
    hi = x.astype(BF16)
    r1 = x - hi.astype(F32)
    mid = r1.astype(BF16)
    lo = (r1 - mid.astype(F32)).astype(BF16)
    return hi, mid, lo


def _cumsum_rows(tri, x):
    hi, mid, lo = _split3(x)
    return _dot(tri, hi) + _dot(tri, mid) + _dot(tri, lo)


def _gla_constants():
    t = np.arange(CHUNK)
    n_lv = len(GLA_LEVELS)
    masks = np.zeros((n_lv + 1, CHUNK, CHUNK), np.float32)
    right = np.zeros((n_lv, CHUNK, 1), np.float32)
    for li, n in enumerate(GLA_LEVELS):
        blk = t // (2 * n)
        is_right = (t % (2 * n)) >= n
        masks[li] = ((blk[:, None] == blk[None, :]) & is_right[:, None] & ~is_right[None, :])
        right[li, :, 0] = is_right
    masks[n_lv] = np.eye(CHUNK)
    return masks, right


def _gla_level_exponents(b, log_a, right_ref):
    row = lax.broadcasted_iota(jnp.int32, (CHUNK, 1), 0)
    exps = []
    for li, n in enumerate(GLA_LEVELS):
        if n >= SUBLANES // 2:
            per_blk = max(2 * n // SUBLANES, 1)
            b4 = b.reshape(CHUNK // (per_blk * SUBLANES), per_blk, SUBLANES, MIX_W)
            r_reg, r_sub = divmod(n - 1, SUBLANES)
            src = b4[:, r_reg:r_reg + 1, r_sub:r_sub + 1, :]
            b_r = jnp.broadcast_to(src, b4.shape).reshape(CHUNK, MIX_W)
            exps.append(jnp.where(right_ref[li] > 0.0, b - b_r, b_r - b))
        elif n == 2:
            pos = row % 4
            nxt = pltpu.roll(log_a, CHUNK - 1, 0)
            prv = pltpu.roll(log_a, 1, 0)
            exps.append(jnp.where(pos == 0, nxt, jnp.where(pos == 1, 0.0,
                                                            jnp.where(pos == 2, log_a, log_a + prv))))
        else:
            exps.append(jnp.where(right_ref[li] > 0.0, log_a, 0.0))
    return exps


def _gla_kernel(y_ref, wa2_ref, ba_ref, g_ref, tri_ref, mask_ref, right_ref, o_ref, st_ref, *, ts, unroll):
    n_lv = len(GLA_LEVELS)

    @pl.when(pl.program_id(1) == 0)
    def _():
        st_ref[...] = jnp.zeros_like(st_ref)

    def chunk(c, carry):
        rows = pl.ds(pl.multiple_of(c * CHUNK, CHUNK), CHUNK)
        q = y_ref[rows, 0:256] * (HEAD_D ** -0.5)
        k = y_ref[rows, 256:512]
        v = y_ref[rows, 512:768].astype(BF16)
        r_gate = y_ref[rows, 768:1024]
        a_lr = y_ref[rows, 1024:1152]
        z = _dot(a_lr.astype(BF16), wa2_ref[...]) + ba_ref[...]
        log_a = _log_sigmoid(z) * (1.0 / GLA_TAU)
        b = _cumsum_rows(tri_ref[...], log_a)
        b_end = b[CHUNK - 1:CHUNK, :]
        scores = [jnp.zeros((CHUNK, CHUNK), F32) for _ in range(MIX_HEADS)]
        for li, e in enumerate(_gla_level_exponents(b, log_a, right_ref)):
            x = (jnp.where(right_ref[li] > 0.0, q, k) * jnp.exp(e)).astype(BF16)
            for h in range(MIX_HEADS):
                xh = x[:, h * HEAD_D:(h + 1) * HEAD_D]
                scores[h] = scores[h] + _dot_nt(xh, xh) * mask_ref[li]
        qb = q.astype(BF16)
        kb = k.astype(BF16)
        q_in = (q * jnp.exp(b)).astype(BF16)
        k_out = (k * jnp.exp(b_end - b)).astype(BF16)
        dec_end = jnp.exp(b_end)
        outs = []
        for h in range(MIX_HEADS):
            sl = slice(h * HEAD_D, (h + 1) * HEAD_D)
            a = scores[h] + _dot_nt(qb[:, sl], kb[:, sl]) * mask_ref[n_lv]
            st = st_ref[h]
            o = _dot(a.astype(BF16), v[:, sl]) + _dot_nt(q_in[:, sl], st.astype(BF16))
            st_ref[h] = st * dec_end[:, sl] + _dot_tn(v[:, sl], k_out[:, sl])
            outs.append(_head_norm(o, g_ref[:, sl]))
        o_all = jnp.concatenate(outs, axis=-1)
        o_ref[rows, :] = (o_all * (r_gate * jax.nn.sigmoid(r_gate))).astype(o_ref.dtype)
        return carry

    lax.fori_loop(0, ts // CHUNK, chunk, 0, unroll=unroll)


def _gla(yg, wa2, ba, g, batch, seq, ts):
    n = yg.shape[0]
    nt = seq // ts
    masks, right = _gla_constants()
    tri = jnp.asarray(np.tril(np.ones((CHUNK, CHUNK), np.float32)), BF16)
    return pl.pallas_call(
        functools.partial(_gla_kernel, ts=ts, unroll=MIXER_UNROLL),
        grid=(batch, nt),
        in_specs=[pl.BlockSpec((ts, SEG_W), lambda b, i: (b * nt + i, 0)),
                  _full(wa2.shape), _full(ba.shape), _full(g.shape),
                  _full(tri.shape), _full(masks.shape), _full(right.shape)],
        out_specs=pl.BlockSpec((ts, MIX_W), lambda b, i: (b * nt + i, 0)),
        out_shape=jax.ShapeDtypeStruct((n, MIX_W), BF16),
        scratch_shapes=[pltpu.VMEM((MIX_HEADS, HEAD_D, HEAD_D), F32)],
        compiler_params=_params(("parallel", "arbitrary")),
        name="gla_mixer",
    )(yg, wa2, ba, g, tri, jnp.asarray(masks), jnp.asarray(right))


def _mlstm_kernel(y_ref, gr_ref, cw_ref, bcol_ref, brow_ref, g_ref, tri_ref, eb_ref, eye_ref, o_ref,
                  xe_ref, qk_ref, vx_ref, fcl_ref, wsl_ref, fcb_ref, wsb_ref, rv_ref, dec_ref,
                  w_ref, em_ref, upd_ref, c_ref, m_ref, *, ts):
    first = pl.program_id(1) == 0

    @pl.when(first)
    def _():
        xe_ref[0:8, :] = jnp.zeros((8, 2 * MIX_W), F32)
        c_ref[...] = jnp.zeros_like(c_ref)
        m_ref[...] = jnp.zeros_like(m_ref)

    @pl.when(jnp.logical_not(first))
    def _():
        xe_ref[0:8, :] = xe_ref[ts:ts + 8, :]

    xe_ref[8:ts + 8, :] = y_ref[:, 0:2 * MIX_W]
    for c in range(ts // CHUNK):
        r0 = 8 + c * CHUNK
        conv = cw_ref[MLSTM_CONV - 1:MLSTM_CONV, :] * xe_ref[r0:r0 + CHUNK, :]
        for j in range(MLSTM_CONV - 1):
            lo = r0 - (MLSTM_CONV - 1) + j
            conv = conv + cw_ref[j:j + 1, :] * xe_ref[lo:lo + CHUNK, :]
        qk_ref[c * CHUNK:(c + 1) * CHUNK, :] = conv * jax.nn.sigmoid(conv)

    ones_col = (lax.broadcasted_iota(jnp.int32, (ts, HEAD_D), 1) == 0).astype(BF16)
    for h in range(MIX_HEADS):
        v_at, one_at = (0, HEAD_D) if h % 2 == 0 else (HEAD_D, 0)
        vx_ref[:, h * LANES + v_at:h * LANES + v_at + HEAD_D] = (
            y_ref[:, 512 + h * HEAD_D:512 + (h + 1) * HEAD_D].astype(BF16))
        vx_ref[:, h * LANES + one_at:h * LANES + one_at + HEAD_D] = ones_col

    tri = tri_ref[...]
    expand = eb_ref[...]
    nck = ts // CHUNK
    lane = lax.broadcasted_iota(jnp.int32, (1, LANES), 1)
    f_lanes = (lane >= MIX_HEADS) & (lane < 2 * MIX_HEADS)
    head_lane = (lax.broadcasted_iota(jnp.int32, (MIX_HEADS, LANES), 1)
                 == lax.broadcasted_iota(jnp.int32, (MIX_HEADS, LANES), 0) + MIX_HEADS)

    def to_lanes(col):
        return jnp.sum(jnp.where(head_lane, col, 0.0), 0, keepdims=True)

    m_col = m_ref[0:MIX_HEADS, 0:1]
    for c in range(nck):
        g_col = y_ref[c * CHUNK:(c + 1) * CHUNK, 1024:1152] + bcol_ref[...]
        fcum_col = jnp.where(f_lanes, _cumsum_rows(tri, _log_sigmoid(g_col)), 0.0)
        fcl_ref[c * CHUNK:(c + 1) * CHUNK, :] = fcum_col
        g_row = gr_ref[c] + brow_ref[...]
        fcum_row = sum(_dot_nt(part, tri) for part in _split3(_log_sigmoid(g_row)))
        f_row = fcum_row[MIX_HEADS:2 * MIX_HEADS, :]
        i_row = g_row[0:MIX_HEADS, :]
        f_end = f_row[:, CHUNK - 1:CHUNK]
        rv_ref[c, 0:MIX_HEADS, :] = jnp.concatenate(
            [i_row - f_row, jnp.broadcast_to(m_col, (MIX_HEADS, HEAD_D))], axis=1)
        m_new = jnp.maximum(f_end + m_col, jnp.max(f_end - f_row + i_row, -1, keepdims=True))
        m_new_l = to_lanes(m_new)
        f_end_l = fcum_col[CHUNK - 1:CHUNK, :]
        i_shift = pltpu.roll(g_col, MIX_HEADS, 1)
        wsl_ref[c * CHUNK:(c + 1) * CHUNK, :] = jnp.where(
            f_lanes, jnp.exp(f_end_l - fcum_col + i_shift - m_new_l), 0.0)
        dec_ref[c] = jnp.broadcast_to(jnp.exp(f_end_l + to_lanes(m_col) - m_new_l), (SUBLANES, LANES))
        m_col = m_new
    m_ref[0:MIX_HEADS, :] = jnp.broadcast_to(m_col, (MIX_HEADS, LANES))
    slab = 2 * CHUNK
    for i in range(ts // slab):
        sr = slice(i * slab, (i + 1) * slab)
        fcb_ref[sr, :] = sum(_dot(part, expand) for part in _split3(fcl_ref[sr, :]))
        ws_hi, ws_mid, _ = _split3(wsl_ref[sr, :])
        wsb_ref[sr, :] = _dot(ws_hi, expand) + _dot(ws_mid, expand)

    t_idx = lax.broadcasted_iota(jnp.int32, (CHUNK, LANES), 0)
    s_idx = lax.broadcasted_iota(jnp.int32, (CHUNK, LANES), 1)
    keep = (s_idx <= t_idx) | (s_idx >= CHUNK)
    low_half = s_idx < HEAD_D

    def weights(c, carry):
        rows = pl.ds(pl.multiple_of(c * CHUNK, CHUNK), CHUNK)
        rv = rv_ref[c, 0:MIX_HEADS, :]
        for p in range(MIX_HEADS // 2):
            q2 = (qk_ref[rows, p * LANES:(p + 1) * LANES] * (HEAD_D ** -0.5)).astype(BF16)
            k2 = qk_ref[rows, MIX_W + p * LANES:MIX_W + (p + 1) * LANES]
            for h in (2 * p, 2 * p + 1):
                mine = low_half if h % 2 == 0 else jnp.logical_not(low_half)
                hb = slice(h * LANES, (h + 1) * LANES)
                logw = jnp.where(keep, fcb_ref[rows, hb] + rv[h:h + 1, :], -jnp.inf)
                m_t = jnp.max(logw, -1, keepdims=True)
                k_h = jnp.where(mine, k2, 0.0)
                qk = _dot_nt(q2, jnp.concatenate([k_h.astype(BF16), eye_ref[h % 2]], axis=0))
                w_ref[c * MIX_HEADS + h] = (jnp.exp(logw - m_t) * qk).astype(BF16)
                em_ref[c * MIX_HEADS + h] = jnp.exp(-m_t)
                upd = _dot_tn((k_h * wsb_ref[rows, hb]).astype(BF16), vx_ref[rows, hb])
                off = (h % 2) * HEAD_D
                upd_ref[c * MIX_HEADS + h] = upd[off:off + HEAD_D, :]
        return carry

    lax.fori_loop(0, nck, weights, 0, unroll=2)

    def chunk(c, carry):
        rows = pl.ds(pl.multiple_of(c * CHUNK, CHUNK), CHUNK)
        dec_l = dec_ref[c]
        for p in range(MIX_HEADS // 2):
            pl_ = slice(p * LANES, (p + 1) * LANES)
            pair = jnp.zeros((CHUNK, LANES), F32)
            for h in (2 * p, 2 * p + 1):
                mine = low_half if h % 2 == 0 else jnp.logical_not(low_half)
                c_st = c_ref[h]
                num = _dot(w_ref[c * MIX_HEADS + h],
                           jnp.concatenate([vx_ref[rows, h * LANES:(h + 1) * LANES], c_st.astype(BF16)], axis=0))
                c_ref[h] = dec_l[0:1, MIX_HEADS + h:MIX_HEADS + h + 1] * c_st + upd_ref[c * MIX_HEADS + h]
                den = num[:, HEAD_D:HEAD_D + 1] if h % 2 == 0 else num[:, 0:1]
                r = 1.0 / jnp.maximum(jnp.abs(den), em_ref[c * MIX_HEADS + h])
                mu = jnp.sum(jnp.where(mine, num, 0.0), -1, keepdims=True) * (1.0 / HEAD_D)
                cen = jnp.where(mine, num - mu, 0.0)
                var = jnp.sum(cen * cen, -1, keepdims=True) * (1.0 / HEAD_D)
                pair = pair + cen * (r * lax.rsqrt(r * r * var + LN_EPS))
            o_gate = y_ref[rows, 768 + p * LANES:768 + (p + 1) * LANES]
            o_ref[rows, pl_] = (pair * g_ref[:, pl_] * jax.nn.sigmoid(o_gate)).astype(o_ref.dtype)
        return carry

    lax.fori_loop(0, nck, chunk, 0, unroll=2)


def _mlstm(ym, gates_row, conv_w, b_col, b_row, g, batch, seq, ts):
    n = ym.shape[0]
    nt = seq // ts
    nck = ts // CHUNK
    tri = jnp.asarray(np.tril(np.ones((CHUNK, CHUNK), np.float32)), BF16)
    expand = np.zeros((LANES, MIX_HEADS * LANES), np.float32)
    for h in range(MIX_HEADS):
        expand[MIX_HEADS + h, h * LANES:(h + 1) * LANES] = 1.0
    eye = np.zeros((2, HEAD_D, LANES), np.float32)
    eye[0, :, 0:HEAD_D] = np.eye(HEAD_D)
    eye[1, :, HEAD_D:] = np.eye(HEAD_D)
    expand = jnp.asarray(expand, BF16)
    eye = jnp.asarray(eye, BF16)
    return pl.pallas_call(
        functools.partial(_mlstm_kernel, ts=ts),
        grid=(batch, nt),
        in_specs=[pl.BlockSpec((ts, SEG_W), lambda b, i: (b * nt + i, 0)),
                  pl.BlockSpec((nck, 8, CHUNK), lambda b, i: (b * nt + i, 0, 0)),
                  _full(conv_w.shape), _full(b_col.shape), _full(b_row.shape), _full(g.shape),
                  _full(tri.shape), _full(expand.shape), _full(eye.shape)],
        out_specs=pl.BlockSpec((ts, MIX_W), lambda b, i: (b * nt + i, 0)),
        out_shape=jax.ShapeDtypeStruct((n, MIX_W), BF16),
        scratch_shapes=[pltpu.VMEM((ts + 8, 2 * MIX_W), F32),
                        pltpu.VMEM((ts, 2 * MIX_W), F32),
                        pltpu.VMEM((ts, MIX_HEADS * LANES), BF16),
                        pltpu.VMEM((ts, LANES), F32),
                        pltpu.VMEM((ts, LANES), F32),
                        pltpu.VMEM((ts, MIX_HEADS * LANES), F32),
                        pltpu.VMEM((ts, MIX_HEADS * LANES), F32),
                        pltpu.VMEM((nck, SUBLANES, LANES), F32),
                        pltpu.VMEM((nck, SUBLANES, LANES), F32),
                        pltpu.VMEM((nck * MIX_HEADS, CHUNK, LANES), BF16),
                        pltpu.VMEM((nck * MIX_HEADS, CHUNK, 1), F32),
                        pltpu.VMEM((nck * MIX_HEADS, HEAD_D, LANES), F32),
                        pltpu.VMEM((MIX_HEADS, HEAD_D, LANES), F32),
                        pltpu.VMEM((SUBLANES, LANES), F32)],
        compiler_params=_params(("parallel", "arbitrary")),
        name="mlstm_mixer",
    )(ym, gates_row, conv_w, b_col, b_row, g, tri, expand, eye)


def _rope_table_kernel(pos_ref, inv_ref, cos_ref, sin_ref):
    ang = pos_ref[...].astype(F32) * inv_ref[...]
    lane = lax.broadcasted_iota(jnp.int32, ang.shape, 1)
    rot = (lane >= MLA_NOPE) & (lane < MLA_NOPE + MLA_ROPE)
    first_half = lane < MLA_NOPE + MLA_ROPE // 2
    cos_ref[...] = jnp.where(lane < MLA_NOPE, 1.0, jnp.where(rot, jnp.cos(ang), 0.0))
    s = jnp.sin(ang)
    sin_ref[...] = jnp.where(rot, jnp.where(first_half, -s, s), 0.0)


def _rope_tables(pos_col, tm):
    n = pos_col.shape[0]
    half = MLA_ROPE // 2
    inv = ROPE_BASE ** (-np.arange(half, dtype=np.float32) / half)
    inv_row = np.zeros((1, LANES), np.float32)
    inv_row[0, MLA_NOPE:MLA_NOPE + half] = inv
    inv_row[0, MLA_NOPE + half:MLA_NOPE + MLA_ROPE] = inv
    return pl.pallas_call(
        _rope_table_kernel,
        grid=(n // tm,),
        in_specs=[pl.BlockSpec((tm, 1), lambda i: (i, 0)), _full((1, LANES))],
        out_specs=[pl.BlockSpec((tm, LANES), lambda i: (i, 0))] * 2,
        out_shape=[jax.ShapeDtypeStruct((n, LANES), F32)] * 2,
        compiler_params=_params(("parallel",)),
        name="rope_tables",
    )(pos_col, jnp.asarray(inv_row))


def _mla_prep_tile(y_ref, cos_ref, sin_ref, gq_ref, gkv_ref, wqa_ref, wqb_ref, wkn_ref, wv_ref, ones_ref,
                   q_ref, k_ref, v_ref):
    def rms(x, g):
        return x * lax.rsqrt(jnp.mean(x * x, -1, keepdims=True) + LN_EPS) * g

    cos = cos_ref[...]
    sin = sin_ref[...]
    cq = rms(y_ref[:, 0:MLA_Q_RANK], gq_ref[...]).astype(BF16)
    ckv = rms(y_ref[:, MLA_Q_RANK:MLA_Q_RANK + MLA_KV_RANK], gkv_ref[...]).astype(BF16)
    k_rope = y_ref[:, 384:512] * cos + y_ref[:, 512:640] * sin
    qa = _dot(cq, wqa_ref[...])
    qb = _dot(cq, wqb_ref[...])
    kn = _dot(ckv, wkn_ref[...])
    scale = (MLA_NOPE + MLA_ROPE) ** -0.5 * LOG2_E
    for h in range(MLA_HEADS):
        sl = slice(h * MLA_QK_PAD, (h + 1) * MLA_QK_PAD)
        q_ref[:, sl] = ((qa[:, sl] * cos + qb[:, sl] * sin) * scale).astype(q_ref.dtype)
        k_ref[:, sl] = (kn[:, sl] + k_rope).astype(k_ref.dtype)
    v_ref[...] = (_dot(ckv, wv_ref[...]) + ones_ref[...]).astype(v_ref.dtype)


def _mla_attn_kernel(q_ref, k_ref, v_ref, o_ref, *, seq, tq):
    t_chunk = lax.broadcasted_iota(jnp.int32, (tq, tq), 0) // CHUNK
    s_chunk = lax.broadcasted_iota(jnp.int32, (tq, tq), 1) // CHUNK
    diag_mask = s_chunk <= t_chunk
    low_half = lax.broadcasted_iota(jnp.int32, (tq, LANES), 1) < MLA_V
    for i in range(seq // tq):
        rows = slice(i * tq, (i + 1) * tq)
        res = []
        for hh in range(MLA_HEAD_PAIR):
            ql = slice(hh * MLA_QK_PAD, (hh + 1) * MLA_QK_PAD)
            vl = slice(hh * LANES, (hh + 1) * LANES)
            q = q_ref[0, rows, ql]
            s_d = jnp.where(diag_mask, _dot_nt(q, k_ref[0, rows, ql]), -jnp.inf)
            m = jnp.max(s_d, -1, keepdims=True)
            if i > 0:
                s_o = _dot_nt(q, k_ref[0, 0:i * tq, ql])
                m = jnp.maximum(m, jnp.max(s_o, -1, keepdims=True))
            o = _dot(jnp.exp2(s_d - m).astype(BF16), v_ref[0, rows, vl])
            if i > 0:
                o = o + _dot(jnp.exp2(s_o - m).astype(BF16), v_ref[0, 0:i * tq, vl])
            l = o[:, MLA_V:MLA_V + 1] if hh == 0 else o[:, 0:1]
            res.append(o / l)
        o_ref[0, rows, :] = jnp.where(low_half, res[0], res[1]).astype(o_ref.dtype)


def _mla_attn(q, k, v, batch, seq, tq):
    n = q.shape[0]
    qk_w = MLA_HEADS * MLA_QK_PAD
    v_w = MLA_HEADS * MLA_V
    pair = lambda w: pl.BlockSpec((1, seq, MLA_HEAD_PAIR * w), lambda b, h: (b, 0, h))
    out = pl.pallas_call(
        functools.partial(_mla_attn_kernel, seq=seq, tq=tq),
        grid=(batch, MLA_HEADS // MLA_HEAD_PAIR),
        in_specs=[pair(MLA_QK_PAD), pair(MLA_QK_PAD), pair(LANES)],
        out_specs=pair(MLA_V),
        out_shape=jax.ShapeDtypeStruct((batch, seq, v_w), BF16),
        compiler_params=_params(("parallel", "parallel")),
        name="mla_attention",
    )(q.reshape(batch, seq, qk_w), k.reshape(batch, seq, qk_w), v.reshape(batch, seq, MLA_HEADS * LANES))
    return out.reshape(n, v_w)


def _out_proj_kernel(og_ref, om_ref, oc_ref, x_ref, wg_ref, wm_ref, wc_ref, g_ref, b_ref, o_ref):
    mix = _dot(og_ref[...], wg_ref[...]) + _dot(om_ref[...], wm_ref[...]) + _dot(oc_ref[...], wc_ref[...])
    o_ref[...] = _layer_norm(ALPHA * x_ref[...] + mix, g_ref[...], b_ref[...])


def _out_proj(og, om, oc, x2d, wg, wm, wc, g, b, tm):
    n = x2d.shape[0]
    row = lambda w: pl.BlockSpec((tm, w), lambda i: (i, 0))
    return pl.pallas_call(
        _out_proj_kernel,
        grid=(n // tm,),
        in_specs=[row(MIX_W), row(MIX_W), row(MLA_HEADS * MLA_V), row(D_MODEL),
                  _full(wg.shape), _full(wm.shape), _full(wc.shape), _full(g.shape), _full(b.shape)],
        out_specs=row(D_MODEL),
        out_shape=jax.ShapeDtypeStruct((n, D_MODEL), F32),
        compiler_params=_params(("parallel",)),
        name="out_proj_ln1",
    )(og, om, oc, x2d, wg, wm, wc, g, b)


def _xa_kv_kernel(mem_ref, w_ref, k_ref, v_ref):
    kv = _dot(mem_ref[...].astype(BF16), w_ref[...])
    k_ref[...] = kv[:, 0:D_MODEL].astype(k_ref.dtype)
    v_ref[...] = kv[:, D_MODEL:2 * D_MODEL].astype(v_ref.dtype)


def _xa_kv(mem2d, w_kv, mem_len):
    n = mem2d.shape[0]
    row = pl.BlockSpec((mem_len, D_MODEL), lambda i: (i, 0))
    return pl.pallas_call(
        _xa_kv_kernel,
        grid=(n // mem_len,),
        in_specs=[row, _full(w_kv.shape)],
        out_specs=[row, row],
        out_shape=[jax.ShapeDtypeStruct((n, D_MODEL), BF16)] * 2,
        compiler_params=_params(("parallel",)),
        name="xattn_kv",
    )(mem2d, w_kv)


def _route_tile(x, wh_ref, wm_ref, b_ref, tri_ref, cnt_ref):
    xh = x.astype(BF16)
    xm = (x - xh.astype(F32)).astype(BF16)
    logits = _dot(xh, wh_ref[...]) + _dot(xh, wm_ref[...]) + _dot(xm, wh_ref[...]) + b_ref[...]
    lane = lax.broadcasted_iota(jnp.int32, logits.shape, 1).astype(F32)
    is_group = (lane >= N_EXPERTS) & (lane < N_EXPERTS + N_GROUPS)
    g_max = jnp.max(jnp.where(is_group, logits, -jnp.inf), -1, keepdims=True)
    g_sum = jnp.sum(jnp.where(is_group, jnp.exp(logits - g_max), 0.0), -1, keepdims=True)
    g_p = 1.0 / g_sum
    g_idx = jnp.min(jnp.where(is_group & (logits == g_max), lane - N_EXPERTS, float(LANES)), -1, keepdims=True)
    in_group = (lane < N_EXPERTS) & (jnp.floor(lane * (1.0 / EXPERTS_PER_GROUP)) == g_idx)
    e_max = jnp.max(jnp.where(in_group, logits, -jnp.inf), -1, keepdims=True)
    e_exp = jnp.where(in_group, jnp.exp(logits - e_max), 0.0)
    prob = e_exp / jnp.sum(e_exp, -1, keepdims=True)
    cand = jnp.where(in_group, prob, -1.0)
    p1 = jnp.max(cand, -1, keepdims=True)
    i1 = jnp.min(jnp.where(cand == p1, lane, float(LANES)), -1, keepdims=True)
    cand2 = jnp.where(lane == i1, -1.0, cand)
    p2 = jnp.max(cand2, -1, keepdims=True)
    i2 = jnp.min(jnp.where(cand2 == p2, lane, float(LANES)), -1, keepdims=True)
    p_sum = p1 + p2
    gates = jnp.where(lane == i1, g_p * (p1 / p_sum), 0.0) + jnp.where(lane == i2, g_p * (p2 / p_sum), 0.0)
    onehot = jnp.where(lane == g_idx, 1.0, 0.0)
    before = _dot(tri_ref[...], onehot.astype(BF16)) + cnt_ref[...]
    rank = jnp.sum(onehot * before, -1, keepdims=True).astype(jnp.int32)
    cnt_ref[...] += jnp.sum(onehot, 0, keepdims=True)
    return gates, rank, g_idx.astype(jnp.int32)


def _xattn_kernel(x_ref, k_ref, v_ref, wq_ref, wo_ref, g_ref, b_ref, wrh_ref, wrm_ref, br_ref, tri_ref,
                  xg_ref, rank_ref, grp_ref, cnt_out_ref, cnt_ref):
    @pl.when((pl.program_id(0) == 0) & (pl.program_id(1) == 0))
    def _():
        cnt_ref[...] = jnp.zeros_like(cnt_ref)

    x = x_ref[...]
    q = (_dot(x.astype(BF16), wq_ref[...]) * (XA_DH ** -0.5)).astype(BF16)
    out = jnp.zeros(x.shape, F32)
    for h in range(XA_HEADS):
        sl = slice(h * XA_DH, (h + 1) * XA_DH)
        s = _dot_nt(q[:, sl], k_ref[:, sl])
        p = jnp.exp(s - jnp.max(s, -1, keepdims=True))
        p = p / jnp.sum(p, -1, keepdims=True)
        o = _dot(p.astype(BF16), v_ref[:, sl])
        out = out + _dot(o.astype(BF16), wo_ref[sl, :])
    x2 = _layer_norm(ALPHA * x + out, g_ref[...], b_ref[...])
    gates, rank, grp = _route_tile(x2, wrh_ref, wrm_ref, br_ref, tri_ref, cnt_ref)
    xg_ref[:, 0:D_MODEL] = x2
    xg_ref[:, D_MODEL:] = gates
    rank_ref[...] = rank
    grp_ref[...] = grp
    cnt_out_ref[...] = cnt_ref[...]


def _xattn_route(x1, xk, xv, wq, wo, g, b, w_route_hi, w_route_mid, b_route, batch, seq, mem_len, tm):
    n = x1.shape[0]
    nt = seq // tm
    tri = jnp.asarray(np.tril(np.ones((tm, tm), np.float32), -1), BF16)
    tile = lambda w: pl.BlockSpec((tm, w), lambda bi, i: (bi * nt + i, 0))
    mem = pl.BlockSpec((mem_len, D_MODEL), lambda bi, i: (bi, 0))
    return pl.pallas_call(
        _xattn_kernel,
        grid=(batch, nt),
        in_specs=[tile(D_MODEL), mem, mem, _full(wq.shape), _full(wo.shape), _full(g.shape), _full(b.shape),
                  _full(w_route_hi.shape), _full(w_route_mid.shape), _full(b_route.shape), _full(tri.shape)],
        out_specs=[tile(XG_W), tile(1), tile(1), _full((1, LANES))],
        out_shape=[jax.ShapeDtypeStruct((n, XG_W), F32), jax.ShapeDtypeStruct((n, 1), jnp.int32),
                   jax.ShapeDtypeStruct((n, 1), jnp.int32), jax.ShapeDtypeStruct((1, LANES), F32)],
        scratch_shapes=[pltpu.VMEM((1, LANES), F32)],
        compiler_params=_params(("arbitrary", "arbitrary")),
        name="xattn_ln2_route",
    )(x1, xk, xv, wq, wo, g, b, w_route_hi, w_route_mid, b_route, tri)


def _dispatch_kernel(pos_ref, x_ref, xs_hbm, sem, *, tm):
    def issue(t, carry):
        pltpu.make_async_copy(x_ref.at[pl.ds(t, 1)], xs_hbm.at[pl.ds(pos_ref[t], 1)], sem).start()
        return carry

    lax.fori_loop(0, tm, issue, 0, unroll=DMA_UNROLL)
    pltpu.make_async_copy(x_ref, xs_hbm.at[pl.ds(0, tm)], sem).wait()


def _dispatch(xg, pos, tm):
    n = xg.shape[0]
    return pl.pallas_call(
        functools.partial(_dispatch_kernel, tm=tm),
        grid=(n // tm,),
        in_specs=[pl.BlockSpec((tm,), lambda i: (i,), memory_space=pltpu.SMEM),
                  pl.BlockSpec((tm, XG_W), lambda i: (i, 0))],
        out_specs=pl.BlockSpec(memory_space=pl.ANY),
        out_shape=jax.ShapeDtypeStruct(xg.shape, xg.dtype),
        scratch_shapes=[pltpu.SemaphoreType.DMA(())],
        compiler_params=_params(("arbitrary",)),
        name="moe_dispatch",
    )(pos, xg)


def _moe_ffn_kernel(blk_ref, grp_ref, first_ref, valid_ref, xs_ref, wg_ref, wu_ref, wd_ref, o_ref, xb_ref):
    w = pl.program_id(0)
    e = pl.program_id(1)

    @pl.when((first_ref[w] == 1) & (e == 0))
    def _():
        xb_ref[...] = xs_ref[:, 0:D_MODEL].astype(BF16)
        o_ref[...] = jnp.zeros_like(o_ref)

    @pl.when(valid_ref[w] == 1)
    def _():
        xb = xb_ref[...]
        gates = xs_ref[:, D_MODEL:]
        lane = lax.broadcasted_iota(jnp.int32, gates.shape, 1)
        w_tok = jnp.sum(jnp.where(lane == grp_ref[w] * EXPERTS_PER_GROUP + e, gates, 0.0), -1, keepdims=True)
        hg = _dot(xb, wg_ref[0].astype(BF16))
        hu = _dot(xb, wu_ref[0].astype(BF16))
        hid = hg * jax.nn.sigmoid(hg) * hu * w_tok
        o_ref[...] += _dot(hid.astype(BF16), wd_ref[0].astype(BF16))


def _moe_ffn(xs, items, w_gate, w_up, w_down, rb, e_base):
    n = xs.shape[0]
    blk, grp, first, valid = items

    def expert(w, e, blk, grp, first, valid):
        return (e_base + grp[w] * EXPERTS_PER_GROUP + jnp.where(valid[w] == 1, e, EXPERTS_PER_GROUP - 1), 0, 0)

    grid_spec = pltpu.PrefetchScalarGridSpec(
        num_scalar_prefetch=4,
        grid=(blk.shape[0], EXPERTS_PER_GROUP),
        in_specs=[pl.BlockSpec((rb, XG_W), lambda w, e, blk, grp, first, valid: (blk[w], 0)),
                  pl.BlockSpec((1, D_MODEL, D_EXPERT), expert),
                  pl.BlockSpec((1, D_MODEL, D_EXPERT), expert),
                  pl.BlockSpec((1, D_EXPERT, D_MODEL), expert)],
        out_specs=pl.BlockSpec((rb, D_MODEL), lambda w, e, blk, grp, first, valid: (blk[w], 0)),
        scratch_shapes=[pltpu.VMEM((rb, D_MODEL), BF16)],
    )
    return pl.pallas_call(
        _moe_ffn_kernel,
        grid_spec=grid_spec,
        out_shape=jax.ShapeDtypeStruct((n, D_MODEL), F32),
        compiler_params=_params(("arbitrary", "arbitrary")),
        name="moe_experts",
    )(blk, grp, first, valid, xs, w_gate, w_up, w_down)


def _combine_kernel(pos_ref, xg_ref, ys_hbm, g_ref, b_ref, o_ref, buf_ref, sem, *, tm):
    def issue(t, carry):
        pltpu.make_async_copy(ys_hbm.at[pl.ds(pos_ref[t], 1)], buf_ref.at[pl.ds(t, 1)], sem).start()
        return carry

    lax.fori_loop(0, tm, issue, 0, unroll=DMA_UNROLL)
    pltpu.make_async_copy(ys_hbm.at[pl.ds(0, tm)], buf_ref, sem).wait()
    o_ref[...] = _layer_norm(ALPHA * xg_ref[:, 0:D_MODEL] + buf_ref[...], g_ref[...], b_ref[...])


def _combine(pos, xg, ys, g, b, tm):
    n = xg.shape[0]
    return pl.pallas_call(
        functools.partial(_combine_kernel, tm=tm),
        grid=(n // tm,),
        in_specs=[pl.BlockSpec((tm,), lambda i: (i,), memory_space=pltpu.SMEM),
                  pl.BlockSpec((tm, XG_W), lambda i: (i, 0)),
                  pl.BlockSpec(memory_space=pl.ANY), _full(g.shape), _full(b.shape)],
        out_specs=pl.BlockSpec((tm, D_MODEL), lambda i: (i, 0)),
        out_shape=jax.ShapeDtypeStruct((n, D_MODEL), F32),
        scratch_shapes=[pltpu.VMEM((tm, D_MODEL), F32), pltpu.SemaphoreType.DMA(())],
        compiler_params=_params(("arbitrary",)),
        name="moe_combine_ln3",
    )(pos, xg, ys, g, b)


def _moe_work_items(counts, n, rb):
    nb = n // rb
    n_items = nb + N_GROUPS - 1
    ends = jnp.cumsum(counts)
    start = jnp.arange(nb, dtype=jnp.int32) * rb
    g_lo = jnp.sum(ends[None, :] <= start[:, None], axis=1).astype(jnp.int32)
    g_hi = jnp.sum(ends[None, :] <= (start + rb - 1)[:, None], axis=1).astype(jnp.int32)
    per_blk = g_hi - g_lo + 1
    item0 = jnp.cumsum(per_blk) - per_blk
    w = jnp.arange(n_items, dtype=jnp.int32)
    valid = w < jnp.sum(per_blk)
    blk = jnp.clip(jnp.sum(item0[None, :] <= w[:, None], axis=1) - 1, 0, nb - 1).astype(jnp.int32)
    grp = jnp.where(valid, g_lo[blk] + (w - item0[blk]), g_hi[nb - 1]).astype(jnp.int32)
    first = (valid & (w == item0[blk])).astype(jnp.int32)
    return blk, grp, first, valid.astype(jnp.int32)


def _moe(xg, rank, grp, counts, w_gate, w_up, w_down, e_base, g, b, t_dma, rb):
    n = xg.shape[0]
    counts = counts[0, :N_GROUPS].astype(jnp.int32)
    offsets = jnp.cumsum(counts) - counts
    pos = (offsets[grp[:, 0]] + rank[:, 0]).astype(jnp.int32)
    xs = _dispatch(xg, pos, t_dma)
    ys = _moe_ffn(xs, _moe_work_items(counts, n, rb), w_gate, w_up, w_down, rb, e_base)
    return _combine(pos, xg, ys, g, b, t_dma)


def _pad_cols(w, width):
    return jnp.pad(w, ((0, 0), (0, width - w.shape[1])))


def _layer_weights(w_in, w_out, gla_w_a2, ml_b_i, ml_b_f, mla_w_uq, mla_w_ukv, moe_w_group, moe_b_group,
                   moe_w_router, moe_b_router):
    o = _IN_OFF
    half = MLA_ROPE // 2
    wg = _pad_cols(w_in[:, o[0]:o[5]], SEG_W).astype(BF16)
    wm = _pad_cols(w_in[:, o[5]:o[10]], SEG_W).astype(BF16)
    kr = w_in[:, o[12]:o[13]]
    zeros = lambda w: jnp.zeros((D_MODEL, w), F32)
    kra = jnp.concatenate([zeros(MLA_NOPE), kr, zeros(LANES - MLA_NOPE - MLA_ROPE)], 1)
    krb = jnp.concatenate([zeros(MLA_NOPE), kr[:, half:], kr[:, :half], zeros(LANES - MLA_NOPE - MLA_ROPE)], 1)
    wc = jnp.concatenate([w_in[:, o[10]:o[12]], kra, krb], 1).astype(BF16)
    wift = w_in[:, o[8]:o[10]].T.astype(BF16)
    wa2 = jnp.pad(gla_w_a2, ((0, LANES - GLA_GATE_RANK), (0, 0))).astype(BF16)
    b_gate = jnp.concatenate([ml_b_i, ml_b_f])
    b_col = jnp.pad(b_gate, (0, LANES - 2 * MIX_HEADS)).reshape(1, LANES)
    b_row = b_gate.reshape(2 * MIX_HEADS, 1)
    uq = mla_w_uq.reshape(MLA_Q_RANK, MLA_HEADS, MLA_NOPE + MLA_ROPE)
    zq = jnp.zeros((MLA_Q_RANK, MLA_HEADS, LANES - MLA_NOPE - MLA_ROPE), F32)
    wqa = jnp.concatenate([uq, zq], -1).reshape(MLA_Q_RANK, -1).astype(BF16)
    wqb = jnp.concatenate([jnp.zeros((MLA_Q_RANK, MLA_HEADS, MLA_NOPE), F32), uq[..., MLA_NOPE + half:],
                           uq[..., MLA_NOPE:MLA_NOPE + half], zq], -1).reshape(MLA_Q_RANK, -1).astype(BF16)
    ukv = mla_w_ukv.reshape(MLA_KV_RANK, MLA_HEADS, MLA_NOPE + MLA_V)
    wkn = jnp.concatenate([ukv[..., :MLA_NOPE], jnp.zeros((MLA_KV_RANK, MLA_HEADS, LANES - MLA_NOPE), F32)],
                          -1).reshape(MLA_KV_RANK, -1).astype(BF16)
    uv = ukv[..., MLA_NOPE:]
    zv = jnp.zeros_like(uv)
    odd = (jnp.arange(MLA_HEADS) % 2 == 1)[None, :, None]
    wv = jnp.concatenate([jnp.where(odd, zv, uv), jnp.where(odd, uv, zv)], -1).reshape(MLA_KV_RANK, -1).astype(BF16)
    wo = w_out.astype(BF16)
    w_route = _pad_cols(jnp.concatenate([moe_w_router, moe_w_group], 1), LANES)
    w_route_hi = w_route.astype(BF16)
    w_route_mid = (w_route - w_route_hi.astype(F32)).astype(BF16)
    b_route = jnp.pad(jnp.concatenate([moe_b_router, moe_b_group]), (0, LANES - N_EXPERTS - N_GROUPS)).reshape(1, LANES)
    return dict(wg=wg, wm=wm, wc=wc, wift=wift, wa2=wa2, b_col=b_col, b_row=b_row, wqa=wqa, wqb=wqb, wkn=wkn,
                wv=wv, wo_g=wo[0:MIX_W], wo_m=wo[MIX_W:2 * MIX_W], wo_c=wo[2 * MIX_W:], w_route_hi=w_route_hi, w_route_mid=w_route_mid,
                b_route=b_route)


def _tile(total, want):
    t = min(total, want)
    assert total % t == 0
    return t


def kernel(x, mem, positions, w_in, w_out, gla_w_a2, gla_b_a, gla_norm_g, ml_conv_w, ml_b_i, ml_b_f, ml_norm_g, mla_q_norm_g, mla_w_uq, mla_kv_norm_g, mla_w_ukv, xa_w_q, xa_w_kv, xa_w_o, moe_w_group, moe_b_group, moe_w_router, moe_b_router, moe_w_gate, moe_w_up, moe_w_down, ln1_g, ln1_b, ln2_g, ln2_b, ln3_g, ln3_b):
    batch, seq, _ = x.shape
    mem_len = mem.shape[1]
    n = batch * seq
    depth = w_in.shape[0]
    assert seq % CHUNK == 0
    t_in = _tile(n, 512)
    tm = _tile(n, 1024)
    ts = _tile(seq, 512)
    tq = _tile(seq, 512)
    t_moe = _tile(n, 1024)
    row = lambda a: a.reshape(1, -1)

    cos_t, sin_t = _rope_tables(positions.reshape(n, 1), _tile(n, 2048))
    mem2d = mem.reshape(batch * mem_len, D_MODEL)
    experts_gate = moe_w_gate.reshape(-1, D_MODEL, D_EXPERT)
    experts_up = moe_w_up.reshape(-1, D_MODEL, D_EXPERT)
    experts_down = moe_w_down.reshape(-1, D_EXPERT, D_MODEL)
    h = x.reshape(n, D_MODEL)
    for l in range(depth):
        w = _layer_weights(w_in[l], w_out[l], gla_w_a2[l], ml_b_i[l], ml_b_f[l], mla_w_uq[l], mla_w_ukv[l],
                           moe_w_group[l], moe_b_group[l], moe_w_router[l], moe_b_router[l])
        yg, ym, yift, q, k, v = _in_proj(h, w["wg"], w["wm"], w["wc"], w["wift"], cos_t, sin_t,
                                         row(mla_q_norm_g[l]), row(mla_kv_norm_g[l]),
                                         w["wqa"], w["wqb"], w["wkn"], w["wv"], t_in)
        gates_row = yift.reshape(2 * MIX_HEADS, n // CHUNK, CHUNK).transpose(1, 0, 2)
        og = _gla(yg, w["wa2"], row(gla_b_a[l]), row(gla_norm_g[l]), batch, seq, ts)
        om = _mlstm(ym, gates_row, ml_conv_w[l], w["b_col"], w["b_row"], row(ml_norm_g[l]), batch, seq, ts)
        oc = _mla_attn(q, k, v, batch, seq, tq)
        x1 = _out_proj(og, om, oc, h, w["wo_g"], w["wo_m"], w["wo_c"], row(ln1_g[l]), row(ln1_b[l]), tm)
        xk, xv = _xa_kv(mem2d, xa_w_kv[l].astype(BF16), mem_len)
        xg, rank, grp, counts = _xattn_route(
            x1, xk, xv, xa_w_q[l].astype(BF16), xa_w_o[l].astype(BF16), row(ln2_g[l]), row(ln2_b[l]),
            w["w_route_hi"], w["w_route_mid"], w["b_route"], batch, seq, mem_len, tm)
        h = _moe(xg, rank, grp, counts, experts_gate, experts_up, experts_down, l * N_EXPERTS,
                 row(ln3_g[l]), row(ln3_b[l]), t_moe, t_moe)
    return h.reshape(batch, seq, D_MODEL)
```

```python
import functools

import numpy as np
import jax
import jax.numpy as jnp
from jax import lax
from jax.experimental import pallas as pl
from jax.experimental.pallas import tpu as pltpu

F32 = jnp.float32
BF16 = jnp.bfloat16

D_MODEL = 1024
CHUNK = 64
HEAD_D = 64
MIX_HEADS = 4
MIX_W = MIX_HEADS * HEAD_D
GLA_GATE_RANK = 16
GLA_TAU = 16.0
MLSTM_CONV = 4
MLA_HEADS = 8
MLA_NOPE = 64
MLA_ROPE = 32
MLA_V = 64
MLA_Q_RANK = 256
MLA_KV_RANK = 128
MLA_QK_PAD = 128
ROPE_BASE = 10000.0
LOG2_E = 1.4426950408889634
MLA_HEAD_PAIR = 2
XA_HEADS = 4
XA_DH = D_MODEL // XA_HEADS
N_GROUPS = 4
EXPERTS_PER_GROUP = 8
N_EXPERTS = N_GROUPS * EXPERTS_PER_GROUP
D_EXPERT = 256
DEPTH = 2
ALPHA = (2 * DEPTH) ** 0.25
LN_EPS = 1e-5
LANES = 128
SUBLANES = 8
MIXER_UNROLL = 4
SEG_W = 1152
MLA_SEG_W = 640
XG_W = D_MODEL + LANES
DMA_UNROLL = 32
GLA_LEVELS = (32, 16, 8, 4, 2, 1)
VMEM_LIMIT = 56 * 1024 * 1024

_IN_SIZES = (256, 256, 256, 256, GLA_GATE_RANK, 512, 256, 256, 4, 4, MLA_Q_RANK, MLA_KV_RANK, MLA_ROPE)
_IN_OFF = np.concatenate([[0], np.cumsum(_IN_SIZES)]).tolist()


def _params(sem):
    return pltpu.CompilerParams(dimension_semantics=sem, vmem_limit_bytes=VMEM_LIMIT)


def _full(shape):
    return pl.BlockSpec(shape, lambda *_: (0,) * len(shape))


def _layer_norm(x, g, b):
    mu = jnp.mean(x, -1, keepdims=True)
    xc = x - mu
    var = jnp.mean(xc * xc, -1, keepdims=True)
    return xc * lax.rsqrt(var + LN_EPS) * g + b


def _log_sigmoid(z):
    return jnp.minimum(z, 0.0) - jnp.log1p(jnp.exp(-jnp.abs(z)))


def _dot_nt(a, b):
    return lax.dot_general(a, b, (((1,), (1,)), ((), ())), preferred_element_type=F32)


def _dot_tn(a, b):
    return lax.dot_general(a, b, (((0,), (0,)), ((), ())), preferred_element_type=F32)


def _dot(a, b):
    return jnp.dot(a, b, preferred_element_type=F32)


def _head_norm(o, g):
    mu = jnp.mean(o, -1, keepdims=True)
    oc = o - mu
    var = jnp.mean(oc * oc, -1, keepdims=True)
    return oc * lax.rsqrt(var + LN_EPS) * g


def _in_proj_kernel(x_ref, wg_ref, wm_ref, wc_ref, wift_ref, cos_ref, sin_ref, gq_ref, gkv_ref,
                    wqa_ref, wqb_ref, wkn_ref, wv_ref, ones_ref,
                    yg_ref, ym_ref, yift_ref, q_ref, k_ref, v_ref, yc_ref):
    xb = x_ref[...].astype(BF16)
    yg_ref[...] = _dot(xb, wg_ref[...])
    ym_ref[...] = _dot(xb, wm_ref[...])
    yift_ref[...] = _dot_nt(wift_ref[...], xb)
    yc_ref[...] = _dot(xb, wc_ref[...])
    _mla_prep_tile(yc_ref, cos_ref, sin_ref, gq_ref, gkv_ref, wqa_ref, wqb_ref, wkn_ref, wv_ref, ones_ref,
                   q_ref, k_ref, v_ref)


def _in_proj(x2d, wg, wm, wc, wift, cos_t, sin_t, gq, gkv, wqa, wqb, wkn, wv, tm):
    n = x2d.shape[0]
    ones_row = np.zeros((1, MLA_HEADS * LANES), np.float32)
    for h in range(MLA_HEADS):
        ones_row[0, h * LANES + (MLA_V if h % 2 == 0 else 0)] = 1.0
    ones_row = jnp.asarray(ones_row)
    row = lambda w: pl.BlockSpec((tm, w), lambda i: (i, 0))
    qk_w = MLA_HEADS * MLA_QK_PAD
    return pl.pallas_call(
        _in_proj_kernel,
        grid=(n // tm,),
        in_specs=[row(D_MODEL), _full(wg.shape), _full(wm.shape), _full(wc.shape), _full(wift.shape),
                  row(LANES), row(LANES), _full(gq.shape), _full(gkv.shape),
                  _full(wqa.shape), _full(wqb.shape), _full(wkn.shape), _full(wv.shape), _full(ones_row.shape)],
        out_specs=[row(SEG_W), row(SEG_W), pl.BlockSpec((8, tm), lambda i: (0, i)),
                   row(qk_w), row(qk_w), row(MLA_HEADS * LANES)],
        out_shape=[jax.ShapeDtypeStruct((n, SEG_W), F32), jax.ShapeDtypeStruct((n, SEG_W), F32),
                   jax.ShapeDtypeStruct((8, n), F32),
                   jax.ShapeDtypeStruct((n, qk_w), BF16), jax.ShapeDtypeStruct((n, qk_w), BF16),
                   jax.ShapeDtypeStruct((n, MLA_HEADS * LANES), BF16)],
        scratch_shapes=[pltpu.VMEM((tm, MLA_SEG_W), F32)],
        compiler_params=_params(("parallel",)),
        name="in_proj_mla_prep",
    )(x2d, wg, wm, wc, wift, cos_t, sin_t, gq, gkv, wqa, wqb, wkn, wv, ones_row)


def _split3(x):
    hi = x.astype(BF16)
    r1 = x - hi.astype(F32)
    mid = r1.astype(BF16)
    lo = (r1 - mid.astype(F32)).astype(BF16)
    return hi, mid, lo


def _cumsum_rows(tri, x):
    hi, mid, lo = _split3(x)
    return _dot(tri, hi) + _dot(tri, mid) + _dot(tri, lo)


def _gla_constants():
    t = np.arange(CHUNK)
    n_lv = len(GLA_LEVELS)
    masks = np.zeros((n_lv + 1, CHUNK, CHUNK), np.float32)
    right = np.zeros((n_lv, CHUNK, 1), np.float32)
    for li, n in enumerate(GLA_LEVELS):
        blk = t // (2 * n)
        is_right = (t % (2 * n)) >= n
        masks[li] = ((blk[:, None] == blk[None, :]) & is_right[:, None] & ~is_right[None, :])
        right[li, :, 0] = is_right
    masks[n_lv] = np.eye(CHUNK)
    return masks, right


def _gla_level_exponents(b, log_a, right_ref):
    row = lax.broadcasted_iota(jnp.int32, (CHUNK, 1), 0)
    exps = []
    for li, n in enumerate(GLA_LEVELS):
        if n >= SUBLANES // 2:
            per_blk = max(2 * n // SUBLANES, 1)
            b4 = b.reshape(CHUNK // (per_blk * SUBLANES), per_blk, SUBLANES, MIX_W)
            r_reg, r_sub = divmod(n - 1, SUBLANES)
            src = b4[:, r_reg:r_reg + 1, r_sub:r_sub + 1, :]
            b_r = jnp.broadcast_to(src, b4.shape).reshape(CHUNK, MIX_W)
            exps.append(jnp.where(right_ref[li] > 0.0, b - b_r, b_r - b))
        elif n == 2:
            pos = row % 4
            nxt = pltpu.roll(log_a, CHUNK - 1, 0)
            prv = pltpu.roll(log_a, 1, 0)
            exps.append(jnp.where(pos == 0, nxt, jnp.where(pos == 1, 0.0,
                                                            jnp.where(pos == 2, log_a, log_a + prv))))
        else:
            exps.append(jnp.where(right_ref[li] > 0.0, log_a, 0.0))
    return exps


def _pair_blocks(z, low_half):
    zero = jnp.zeros_like(z)
    return jnp.concatenate([jnp.where(low_half, z, zero), jnp.where(low_half, zero, z)], axis=0)


def _gla_kernel(y_ref, wa2_ref, ba_ref, g_ref, tri_ref, mask_ref, right_ref, o_ref, st_ref, *, ts, unroll):
    n_lv = len(GLA_LEVELS)
    pairs = MIX_HEADS // 2
    low_half = lax.broadcasted_iota(jnp.int32, (CHUNK, LANES), 1) < HEAD_D
    same_head = (lax.broadcasted_iota(jnp.int32, (LANES, LANES), 0) < HEAD_D) == (
        lax.broadcasted_iota(jnp.int32, (LANES, LANES), 1) < HEAD_D)

    @pl.when(pl.program_id(1) == 0)
    def _():
        st_ref[...] = jnp.zeros_like(st_ref)

    def chunk(c, carry):
        rows = pl.ds(pl.multiple_of(c * CHUNK, CHUNK), CHUNK)
        q = y_ref[rows, 0:256] * (HEAD_D ** -0.5)
        k = y_ref[rows, 256:512]
        a_lr = y_ref[rows, 1024:1152]
        z = _dot(a_lr.astype(BF16), wa2_ref[...]) + ba_ref[...]
        log_a = _log_sigmoid(z) * (1.0 / GLA_TAU)
        b = _cumsum_rows(tri_ref[...], log_a)
        b_end = b[CHUNK - 1:CHUNK, :]
        scores = [jnp.zeros((CHUNK, LANES), F32) for _ in range(pairs)]
        for li, e in enumerate(_gla_level_exponents(b, log_a, right_ref)):
            x = (jnp.where(right_ref[li] > 0.0, q, k) * jnp.exp(e)).astype(BF16)
            for p in range(pairs):
                x2 = x[:, p * LANES:(p + 1) * LANES]
                scores[p] = scores[p] + _dot_nt(x2, _pair_blocks(x2, low_half)) * mask_ref[li]
        qb = q.astype(BF16)
        kb = k.astype(BF16)
        q_in = (q * jnp.exp(b)).astype(BF16)
        k_out = (k * jnp.exp(b_end - b)).astype(BF16)
        dec_end = jnp.exp(b_end)
        for p in range(pairs):
            pl_ = slice(p * LANES, (p + 1) * LANES)
            a = scores[p] + _dot_nt(qb[:, pl_], _pair_blocks(kb[:, pl_], low_half)) * mask_ref[n_lv]
            v2 = y_ref[rows, 512 + p * LANES:512 + (p + 1) * LANES].astype(BF16)
            st = st_ref[p]
            o = _dot(a.astype(BF16), _pair_blocks(v2, low_half)) + _dot_nt(q_in[:, pl_], st.astype(BF16))
            st_ref[p] = st * dec_end[:, pl_] + jnp.where(same_head, _dot_tn(v2, k_out[:, pl_]), 0.0)
            normed = jnp.zeros((CHUNK, LANES), F32)
            for mine in (low_half, jnp.logical_not(low_half)):
                mu = jnp.sum(jnp.where(mine, o, 0.0), -1, keepdims=True) * (1.0 / HEAD_D)
                cen = jnp.where(mine, o - mu, 0.0)
                var = jnp.sum(cen * cen, -1, keepdims=True) * (1.0 / HEAD_D)
                normed = normed + cen * lax.rsqrt(var + LN_EPS)
            r_gate = y_ref[rows, 768 + p * LANES:768 + (p + 1) * LANES]
            o_ref[rows, pl_] = (normed * g_ref[:, pl_] * (r_gate * jax.nn.sigmoid(r_gate))).astype(o_ref.dtype)
        return carry

    lax.fori_loop(0, ts // CHUNK, chunk, 0, unroll=unroll)


def _gla(yg, wa2, ba, g, batch, seq, ts):
    n = yg.shape[0]
    nt = seq // ts
    masks, right = _gla_constants()
    masks = np.concatenate([masks, masks], axis=-1)
    tri = jnp.asarray(np.tril(np.ones((CHUNK, CHUNK), np.float32)), BF16)
    return pl.pallas_call(
        functools.partial(_gla_kernel, ts=ts, unroll=MIXER_UNROLL),
        grid=(batch, nt),
        in_specs=[pl.BlockSpec((ts, SEG_W), lambda b, i: (b * nt + i, 0)),
                  _full(wa2.shape), _full(ba.shape), _full(g.shape),
                  _full(tri.shape), _full(masks.shape), _full(right.shape)],
        out_specs=pl.BlockSpec((ts, MIX_W), lambda b, i: (b * nt + i, 0)),
        out_shape=jax.ShapeDtypeStruct((n, MIX_W), BF16),
        scratch_shapes=[pltpu.VMEM((MIX_HEADS // 2, LANES, LANES), F32)],
        compiler_params=_params(("parallel", "arbitrary")),
        name="gla_mixer",
    )(yg, wa2, ba, g, tri, jnp.asarray(masks), jnp.asarray(right))


def _mlstm_kernel(y_ref, gr_ref, cw_ref, bcol_ref, brow_ref, g_ref, tri_ref, eb_ref, eye_ref, o_ref,
                  xe_ref, qk_ref, vx_ref, fcl_ref, wsl_ref, fcb_ref, wsb_ref, rv_ref, dec_ref,
                  w_ref, em_ref, upd_ref, c_ref, m_ref, *, ts):
    first = pl.program_id(1) == 0

    @pl.when(first)
    def _():
        xe_ref[0:8, :] = jnp.zeros((8, 2 * MIX_W), F32)
        c_ref[...] = jnp.zeros_like(c_ref)
        m_ref[...] = jnp.zeros_like(m_ref)

    @pl.when(jnp.logical_not(first))
    def _():
        xe_ref[0:8, :] = xe_ref[ts:ts + 8, :]

    xe_ref[8:ts + 8, :] = y_ref[:, 0:2 * MIX_W]
    for c in range(ts // CHUNK):
        r0 = 8 + c * CHUNK
        conv = cw_ref[MLSTM_CONV - 1:MLSTM_CONV, :] * xe_ref[r0:r0 + CHUNK, :]
        for j in range(MLSTM_CONV - 1):
            lo = r0 - (MLSTM_CONV - 1) + j
            conv = conv + cw_ref[j:j + 1, :] * xe_ref[lo:lo + CHUNK, :]
        qk_ref[c * CHUNK:(c + 1) * CHUNK, :] = conv * jax.nn.sigmoid(conv)

    ones_col = (lax.broadcasted_iota(jnp.int32, (ts, HEAD_D), 1) == 0).astype(BF16)
    for h in range(MIX_HEADS):
        v_at, one_at = (0, HEAD_D) if h % 2 == 0 else (HEAD_D, 0)
        vx_ref[:, h * LANES + v_at:h * LANES + v_at + HEAD_D] = (
            y_ref[:, 512 + h * HEAD_D:512 + (h + 1) * HEAD_D].astype(BF16))
        vx_ref[:, h * LANES + one_at:h * LANES + one_at + HEAD_D] = ones_col

    tri = tri_ref[...]
    expand = eb_ref[...]
    nck = ts // CHUNK
    lane = lax.broadcasted_iota(jnp.int32, (1, LANES), 1)
    f_lanes = (lane >= MIX_HEADS) & (lane < 2 * MIX_HEADS)
    head_lane = (lax.broadcasted_iota(jnp.int32, (MIX_HEADS, LANES), 1)
                 == lax.broadcasted_iota(jnp.int32, (MIX_HEADS, LANES), 0) + MIX_HEADS)

    def to_lanes(col):
        return jnp.sum(jnp.where(head_lane, col, 0.0), 0, keepdims=True)

    g_rows = gr_ref[...] + brow_ref[...]
    ls_rows = _log_sigmoid(g_rows).reshape(nck * SUBLANES, CHUNK)
    fcum_rows = sum(_dot_nt(part, tri) for part in _split3(ls_rows)).reshape(nck, SUBLANES, CHUNK)
    m_col = m_ref[0:MIX_HEADS, 0:1]
    for c in range(nck):
        g_col = y_ref[c * CHUNK:(c + 1) * CHUNK, 1024:1152] + bcol_ref[...]
        fcum_col = jnp.where(f_lanes, _cumsum_rows(tri, _log_sigmoid(g_col)), 0.0)
        fcl_ref[c * CHUNK:(c + 1) * CHUNK, :] = fcum_col
        g_row = g_rows[c]
        fcum_row = fcum_rows[c]
        f_row = fcum_row[MIX_HEADS:2 * MIX_HEADS, :]
        i_row = g_row[0:MIX_HEADS, :]
        f_end = f_row[:, CHUNK - 1:CHUNK]
        rv_ref[c, 0:MIX_HEADS, :] = jnp.concatenate(
            [i_row - f_row, jnp.broadcast_to(m_col, (MIX_HEADS, HEAD_D))], axis=1)
        m_new = jnp.maximum(f_end + m_col, jnp.max(f_end - f_row + i_row, -1, keepdims=True))
        m_new_l = to_lanes(m_new)
        f_end_l = fcum_col[CHUNK - 1:CHUNK, :]
        i_shift = pltpu.roll(g_col, MIX_HEADS, 1)
        wsl_ref[c * CHUNK:(c + 1) * CHUNK, :] = jnp.where(
            f_lanes, jnp.exp(f_end_l - fcum_col + i_shift - m_new_l), 0.0)
        dec_ref[c] = jnp.broadcast_to(jnp.exp(f_end_l + to_lanes(m_col) - m_new_l), (SUBLANES, LANES))
        m_col = m_new
    m_ref[0:MIX_HEADS, :] = jnp.broadcast_to(m_col, (MIX_HEADS, LANES))
    slab = 2 * CHUNK
    for i in range(ts // slab):
        sr = slice(i * slab, (i + 1) * slab)
        fcb_ref[sr, :] = sum(_dot(part, expand) for part in _split3(fcl_ref[sr, :]))
        ws_hi, ws_mid, _ = _split3(wsl_ref[sr, :])
        wsb_ref[sr, :] = _dot(ws_hi, expand) + _dot(ws_mid, expand)

    t_idx = lax.broadcasted_iota(jnp.int32, (CHUNK, LANES), 0)
    s_idx = lax.broadcasted_iota(jnp.int32, (CHUNK, LANES), 1)
    keep = (s_idx <= t_idx) | (s_idx >= CHUNK)
    low_half = s_idx < HEAD_D

    def weights(c, carry):
        rows = pl.ds(pl.multiple_of(c * CHUNK, CHUNK), CHUNK)
        rv = rv_ref[c, 0:MIX_HEADS, :]
        for p in range(MIX_HEADS // 2):
            q2 = (qk_ref[rows, p * LANES:(p + 1) * LANES] * (HEAD_D ** -0.5)).astype(BF16)
            k2 = qk_ref[rows, MIX_W + p * LANES:MIX_W + (p + 1) * LANES]
            for h in (2 * p, 2 * p + 1):
                mine = low_half if h % 2 == 0 else jnp.logical_not(low_half)
                hb = slice(h * LANES, (h + 1) * LANES)
                logw = jnp.where(keep, fcb_ref[rows, hb] + rv[h:h + 1, :], -jnp.inf)
                m_t = jnp.max(logw, -1, keepdims=True)
                k_h = jnp.where(mine, k2, 0.0)
                qk = _dot_nt(q2, jnp.concatenate([k_h.astype(BF16), eye_ref[h % 2]], axis=0))
                w_ref[c * MIX_HEADS + h] = (jnp.exp(logw - m_t) * qk).astype(BF16)
                em_ref[c * MIX_HEADS + h] = jnp.exp(-m_t)
                upd = _dot_tn((k_h * wsb_ref[rows, hb]).astype(BF16), vx_ref[rows, hb])
                off = (h % 2) * HEAD_D
                upd_ref[c * MIX_HEADS + h] = upd[off:off + HEAD_D, :]
        return carry

    lax.fori_loop(0, nck, weights, 0, unroll=2)

    def chunk(c, carry):
        rows = pl.ds(pl.multiple_of(c * CHUNK, CHUNK), CHUNK)
        dec_l = dec_ref[c]
        for p in range(MIX_HEADS // 2):
            pl_ = slice(p * LANES, (p + 1) * LANES)
            pair = jnp.zeros((CHUNK, LANES), F32)
            for h in (2 * p, 2 * p + 1):
                mine = low_half if h % 2 == 0 else jnp.logical_not(low_half)
                c_st = c_ref[h]
                num = _dot(w_ref[c * MIX_HEADS + h],
                           jnp.concatenate([vx_ref[rows, h * LANES:(h + 1) * LANES], c_st.astype(BF16)], axis=0))
                c_ref[h] = dec_l[0:1, MIX_HEADS + h:MIX_HEADS + h + 1] * c_st + upd_ref[c * MIX_HEADS + h]
                den = num[:, HEAD_D:HEAD_D + 1] if h % 2 == 0 else num[:, 0:1]
                r = 1.0 / jnp.maximum(jnp.abs(den), em_ref[c * MIX_HEADS + h])
                mu = jnp.sum(jnp.where(mine, num, 0.0), -1, keepdims=True) * (1.0 / HEAD_D)
                cen = jnp.where(mine, num - mu, 0.0)
                var = jnp.sum(cen * cen, -1, keepdims=True) * (1.0 / HEAD_D)
                pair = pair + cen * (r * lax.rsqrt(r * r * var + LN_EPS))
            o_gate = y_ref[rows, 768 + p * LANES:768 + (p + 1) * LANES]
            o_ref[rows, pl_] = (pair * g_ref[:, pl_] * jax.nn.sigmoid(o_gate)).astype(o_ref.dtype)
        return carry

    lax.fori_loop(0, nck, chunk, 0, unroll=2)


def _mlstm(ym, gates_row, conv_w, b_col, b_row, g, batch, seq, ts):
    n = ym.shape[0]
    nt = seq // ts
    nck = ts // CHUNK
    tri = jnp.asarray(np.tril(np.ones((CHUNK, CHUNK), np.float32)), BF16)
    expand = np.zeros((LANES, MIX_HEADS * LANES), np.float32)
    for h in range(MIX_HEADS):
        expand[MIX_HEADS + h, h * LANES:(h + 1) * LANES] = 1.0
    eye = np.zeros((2, HEAD_D, LANES), np.float32)
    eye[0, :, 0:HEAD_D] = np.eye(HEAD_D)
    eye[1, :, HEAD_D:] = np.eye(HEAD_D)
    expand = jnp.asarray(expand, BF16)
    eye = jnp.asarray(eye, BF16)
    return pl.pallas_call(
        functools.partial(_mlstm_kernel, ts=ts),
        grid=(batch, nt),
        in_specs=[pl.BlockSpec((ts, SEG_W), lambda b, i: (b * nt + i, 0)),
                  pl.BlockSpec((nck, 8, CHUNK), lambda b, i: (b * nt + i, 0, 0)),
                  _full(conv_w.shape), _full(b_col.shape), _full(b_row.shape), _full(g.shape),
                  _full(tri.shape), _full(expand.shape), _full(eye.shape)],
        out_specs=pl.BlockSpec((ts, MIX_W), lambda b, i: (b * nt + i, 0)),
        out_shape=jax.ShapeDtypeStruct((n, MIX_W), BF16),
        scratch_shapes=[pltpu.VMEM((ts + 8, 2 * MIX_W), F32),
                        pltpu.VMEM((ts, 2 * MIX_W), F32),
                        pltpu.VMEM((ts, MIX_HEADS * LANES), BF16),
                        pltpu.VMEM((ts, LANES), F32),
                        pltpu.VMEM((ts, LANES), F32),
                        pltpu.VMEM((ts, MIX_HEADS * LANES), F32),
                        pltpu.VMEM((ts, MIX_HEADS * LANES), F32),
                        pltpu.VMEM((nck, SUBLANES, LANES), F32),
                        pltpu.VMEM((nck, SUBLANES, LANES), F32),
                        pltpu.VMEM((nck * MIX_HEADS, CHUNK, LANES), BF16),
                        pltpu.VMEM((nck * MIX_HEADS, CHUNK, 1), F32),
                        pltpu.VMEM((nck * MIX_HEADS, HEAD_D, LANES), F32),
                        pltpu.VMEM((MIX_HEADS, HEAD_D, LANES), F32),
                        pltpu.VMEM((SUBLANES, LANES), F32)],
        compiler_params=_params(("parallel", "arbitrary")),
        name="mlstm_mixer",
    )(ym, gates_row, conv_w, b_col, b_row, g, tri, expand, eye)


def _rope_table_kernel(pos_ref, inv_ref, cos_ref, sin_ref):
    ang = pos_ref[...].astype(F32) * inv_ref[...]
    lane = lax.broadcasted_iota(jnp.int32, ang.shape, 1)
    rot = (lane >= MLA_NOPE) & (lane < MLA_NOPE + MLA_ROPE)
    first_half = lane < MLA_NOPE + MLA_ROPE // 2
    cos_ref[...] = jnp.where(lane < MLA_NOPE, 1.0, jnp.where(rot, jnp.cos(ang), 0.0))
    s = jnp.sin(ang)
    sin_ref[...] = jnp.where(rot, jnp.where(first_half, -s, s), 0.0)


def _rope_tables(pos_col, tm):
    n = pos_col.shape[0]
    half = MLA_ROPE // 2
    inv = ROPE_BASE ** (-np.arange(half, dtype=np.float32) / half)
    inv_row = np.zeros((1, LANES), np.float32)
    inv_row[0, MLA_NOPE:MLA_NOPE + half] = inv
    inv_row[0, MLA_NOPE + half:MLA_NOPE + MLA_ROPE] = inv
    return pl.pallas_call(
        _rope_table_kernel,
        grid=(n // tm,),
        in_specs=[pl.BlockSpec((tm, 1), lambda i: (i, 0)), _full((1, LANES))],
        out_specs=[pl.BlockSpec((tm, LANES), lambda i: (i, 0))] * 2,
        out_shape=[jax.ShapeDtypeStruct((n, LANES), F32)] * 2,
        compiler_params=_params(("parallel",)),
        name="rope_tables",
    )(pos_col, jnp.asarray(inv_row))


def _mla_prep_tile(y_ref, cos_ref, sin_ref, gq_ref, gkv_ref, wqa_ref, wqb_ref, wkn_ref, wv_ref, ones_ref,
                   q_ref, k_ref, v_ref):
    def rms(x, g):
        return x * lax.rsqrt(jnp.mean(x * x, -1, keepdims=True) + LN_EPS) * g

    cos = cos_ref[...]
    sin = sin_ref[...]
    cq = rms(y_ref[:, 0:MLA_Q_RANK], gq_ref[...]).astype(BF16)
    ckv = rms(y_ref[:, MLA_Q_RANK:MLA_Q_RANK + MLA_KV_RANK], gkv_ref[...]).astype(BF16)
    k_rope = y_ref[:, 384:512] * cos + y_ref[:, 512:640] * sin
    qa = _dot(cq, wqa_ref[...])
    qb = _dot(cq, wqb_ref[...])
    kn = _dot(ckv, wkn_ref[...])
    scale = (MLA_NOPE + MLA_ROPE) ** -0.5 * LOG2_E
    for h in range(MLA_HEADS):
        sl = slice(h * MLA_QK_PAD, (h + 1) * MLA_QK_PAD)
        q_ref[:, sl] = ((qa[:, sl] * cos + qb[:, sl] * sin) * scale).astype(q_ref.dtype)
        k_ref[:, sl] = (kn[:, sl] + k_rope).astype(k_ref.dtype)
    v_ref[...] = (_dot(ckv, wv_ref[...]) + ones_ref[...]).astype(v_ref.dtype)


def _mla_attn_kernel(q_ref, k_ref, v_ref, o_ref, *, seq, tq):
    t_chunk = lax.broadcasted_iota(jnp.int32, (tq, tq), 0) // CHUNK
    s_chunk = lax.broadcasted_iota(jnp.int32, (tq, tq), 1) // CHUNK
    diag_mask = s_chunk <= t_chunk
    low_half = lax.broadcasted_iota(jnp.int32, (tq, LANES), 1) < MLA_V
    for i in range(seq // tq):
        rows = slice(i * tq, (i + 1) * tq)
        res = []
        for hh in range(MLA_HEAD_PAIR):
            ql = slice(hh * MLA_QK_PAD, (hh + 1) * MLA_QK_PAD)
            vl = slice(hh * LANES, (hh + 1) * LANES)
            q = q_ref[0, rows, ql]
            s_d = jnp.where(diag_mask, _dot_nt(q, k_ref[0, rows, ql]), -jnp.inf)
            m = jnp.max(s_d, -1, keepdims=True)
            if i > 0:
                s_o = _dot_nt(q, k_ref[0, 0:i * tq, ql])
                m = jnp.maximum(m, jnp.max(s_o, -1, keepdims=True))
            o = _dot(jnp.exp2(s_d - m).astype(BF16), v_ref[0, rows, vl])
            if i > 0:
                o = o + _dot(jnp.exp2(s_o - m).astype(BF16), v_ref[0, 0:i * tq, vl])
            l = o[:, MLA_V:MLA_V + 1] if hh == 0 else o[:, 0:1]
            res.append(o / l)
        o_ref[0, rows, :] = jnp.where(low_half, res[0], res[1]).astype(o_ref.dtype)


def _mla_attn(q, k, v, batch, seq, tq):
    n = q.shape[0]
    qk_w = MLA_HEADS * MLA_QK_PAD
    v_w = MLA_HEADS * MLA_V
    pair = lambda w: pl.BlockSpec((1, seq, MLA_HEAD_PAIR * w), lambda b, h: (b, 0, h))
    out = pl.pallas_call(
        functools.partial(_mla_attn_kernel, seq=seq, tq=tq),
        grid=(batch, MLA_HEADS // MLA_HEAD_PAIR),
        in_specs=[pair(MLA_QK_PAD), pair(MLA_QK_PAD), pair(LANES)],
        out_specs=pair(MLA_V),
        out_shape=jax.ShapeDtypeStruct((batch, seq, v_w), BF16),
        compiler_params=_params(("parallel", "parallel")),
        name="mla_attention",
    )(q.reshape(batch, seq, qk_w), k.reshape(batch, seq, qk_w), v.reshape(batch, seq, MLA_HEADS * LANES))
    return out.reshape(n, v_w)


def _out_proj_kernel(og_ref, om_ref, oc_ref, x_ref, wg_ref, wm_ref, wc_ref, g_ref, b_ref, o_ref):
    mix = _dot(og_ref[...], wg_ref[...]) + _dot(om_ref[...], wm_ref[...]) + _dot(oc_ref[...], wc_ref[...])
    o_ref[...] = _layer_norm(ALPHA * x_ref[...] + mix, g_ref[...], b_ref[...])


def _out_proj(og, om, oc, x2d, wg, wm, wc, g, b, tm):
    n = x2d.shape[0]
    row = lambda w: pl.BlockSpec((tm, w), lambda i: (i, 0))
    return pl.pallas_call(
        _out_proj_kernel,
        grid=(n // tm,),
        in_specs=[row(MIX_W), row(MIX_W), row(MLA_HEADS * MLA_V), row(D_MODEL),
                  _full(wg.shape), _full(wm.shape), _full(wc.shape), _full(g.shape), _full(b.shape)],
        out_specs=row(D_MODEL),
        out_shape=jax.ShapeDtypeStruct((n, D_MODEL), F32),
        compiler_params=_params(("parallel",)),
        name="out_proj_ln1",
    )(og, om, oc, x2d, wg, wm, wc, g, b)


def _xa_kv_kernel(mem_ref, w_ref, k_ref, v_ref):
    kv = _dot(mem_ref[...].astype(BF16), w_ref[...])
    k_ref[...] = kv[:, 0:D_MODEL].astype(k_ref.dtype)
    v_ref[...] = kv[:, D_MODEL:2 * D_MODEL].astype(v_ref.dtype)


def _xa_kv(mem2d, w_kv, mem_len):
    n = mem2d.shape[0]
    row = pl.BlockSpec((mem_len, D_MODEL), lambda i: (i, 0))
    return pl.pallas_call(
        _xa_kv_kernel,
        grid=(n // mem_len,),
        in_specs=[row, _full(w_kv.shape)],
        out_specs=[row, row],
        out_shape=[jax.ShapeDtypeStruct((n, D_MODEL), BF16)] * 2,
        compiler_params=_params(("parallel",)),
        name="xattn_kv",
    )(mem2d, w_kv)


def _route_tile(x, wh_ref, wm_ref, b_ref, tri_ref, cnt_ref):
    xh = x.astype(BF16)
    xm = (x - xh.astype(F32)).astype(BF16)
    logits = _dot(xh, wh_ref[...]) + _dot(xh, wm_ref[...]) + _dot(xm, wh_ref[...]) + b_ref[...]
    lane = lax.broadcasted_iota(jnp.int32, logits.shape, 1).astype(F32)
    is_group = (lane >= N_EXPERTS) & (lane < N_EXPERTS + N_GROUPS)
    g_max = jnp.max(jnp.where(is_group, logits, -jnp.inf), -1, keepdims=True)
    g_sum = jnp.sum(jnp.where(is_group, jnp.exp(logits - g_max), 0.0), -1, keepdims=True)
    g_p = 1.0 / g_sum
    g_idx = jnp.min(jnp.where(is_group & (logits == g_max), lane - N_EXPERTS, float(LANES)), -1, keepdims=True)
    in_group = (lane < N_EXPERTS) & (jnp.floor(lane * (1.0 / EXPERTS_PER_GROUP)) == g_idx)
    e_max = jnp.max(jnp.where(in_group, logits, -jnp.inf), -1, keepdims=True)
    e_exp = jnp.where(in_group, jnp.exp(logits - e_max), 0.0)
    prob = e_exp / jnp.sum(e_exp, -1, keepdims=True)
    cand = jnp.where(in_group, prob, -1.0)
    p1 = jnp.max(cand, -1, keepdims=True)
    i1 = jnp.min(jnp.where(cand == p1, lane, float(LANES)), -1, keepdims=True)
    cand2 = jnp.where(lane == i1, -1.0, cand)
    p2 = jnp.max(cand2, -1, keepdims=True)
    i2 = jnp.min(jnp.where(cand2 == p2, lane, float(LANES)), -1, keepdims=True)
    p_sum = p1 + p2
    gates = jnp.where(lane == i1, g_p * (p1 / p_sum), 0.0) + jnp.where(lane == i2, g_p * (p2 / p_sum), 0.0)
    onehot = jnp.where(lane == g_idx, 1.0, 0.0)
    before = _dot(tri_ref[...], onehot.astype(BF16)) + cnt_ref[...]
    rank = jnp.sum(onehot * before, -1, keepdims=True).astype(jnp.int32)
    cnt_ref[...] += jnp.sum(onehot, 0, keepdims=True)
    return gates, rank, g_idx.astype(jnp.int32)


def _xattn_kernel(x_ref, k_ref, v_ref, wq_ref, wo_ref, g_ref, b_ref, wrh_ref, wrm_ref, br_ref, tri_ref,
                  xg_ref, rank_ref, grp_ref, cnt_out_ref, cnt_ref):
    @pl.when((pl.program_id(0) == 0) & (pl.program_id(1) == 0))
    def _():
        cnt_ref[...] = jnp.zeros_like(cnt_ref)

    x = x_ref[...]
    q = (_dot(x.astype(BF16), wq_ref[...]) * (XA_DH ** -0.5)).astype(BF16)
    out = jnp.zeros(x.shape, F32)
    for h in range(XA_HEADS):
        sl = slice(h * XA_DH, (h + 1) * XA_DH)
        s = _dot_nt(q[:, sl], k_ref[:, sl])
        p = jnp.exp(s - jnp.max(s, -1, keepdims=True))
        p = p / jnp.sum(p, -1, keepdims=True)
        o = _dot(p.astype(BF16), v_ref[:, sl])
        out = out + _dot(o.astype(BF16), wo_ref[sl, :])
    x2 = _layer_norm(ALPHA * x + out, g_ref[...], b_ref[...])
    gates, rank, grp = _route_tile(x2, wrh_ref, wrm_ref, br_ref, tri_ref, cnt_ref)
    xg_ref[:, 0:D_MODEL] = x2
    xg_ref[:, D_MODEL:] = gates
    rank_ref[...] = rank
    grp_ref[...] = grp
    cnt_out_ref[...] = cnt_ref[...]


def _xattn_route(x1, xk, xv, wq, wo, g, b, w_route_hi, w_route_mid, b_route, batch, seq, mem_len, tm):
    n = x1.shape[0]
    nt = seq // tm
    tri = jnp.asarray(np.tril(np.ones((tm, tm), np.float32), -1), BF16)
    tile = lambda w: pl.BlockSpec((tm, w), lambda bi, i: (bi * nt + i, 0))
    mem = pl.BlockSpec((mem_len, D_MODEL), lambda bi, i: (bi, 0))
    return pl.pallas_call(
        _xattn_kernel,
        grid=(batch, nt),
        in_specs=[tile(D_MODEL), mem, mem, _full(wq.shape), _full(wo.shape), _full(g.shape), _full(b.shape),
                  _full(w_route_hi.shape), _full(w_route_mid.shape), _full(b_route.shape), _full(tri.shape)],
        out_specs=[tile(XG_W), tile(1), tile(1), _full((1, LANES))],
        out_shape=[jax.ShapeDtypeStruct((n, XG_W), F32), jax.ShapeDtypeStruct((n, 1), jnp.int32),
                   jax.ShapeDtypeStruct((n, 1), jnp.int32), jax.ShapeDtypeStruct((1, LANES), F32)],
        scratch_shapes=[pltpu.VMEM((1, LANES), F32)],
        compiler_params=_params(("arbitrary", "arbitrary")),
        name="xattn_ln2_route",
    )(x1, xk, xv, wq, wo, g, b, w_route_hi, w_route_mid, b_route, tri)


def _dispatch_kernel(pos_ref, x_ref, xs_hbm, sem, *, tm):
    def issue(t, carry):
        pltpu.make_async_copy(x_ref.at[pl.ds(t, 1)], xs_hbm.at[pl.ds(pos_ref[t], 1)], sem).start()
        return carry

    lax.fori_loop(0, tm, issue, 0, unroll=DMA_UNROLL)
    pltpu.make_async_copy(x_ref, xs_hbm.at[pl.ds(0, tm)], sem).wait()


def _dispatch(xg, pos, tm):
    n = xg.shape[0]
    return pl.pallas_call(
        functools.partial(_dispatch_kernel, tm=tm),
        grid=(n // tm,),
        in_specs=[pl.BlockSpec((tm,), lambda i: (i,), memory_space=pltpu.SMEM),
                  pl.BlockSpec((tm, XG_W), lambda i: (i, 0))],
        out_specs=pl.BlockSpec(memory_space=pl.ANY),
        out_shape=jax.ShapeDtypeStruct(xg.shape, xg.dtype),
        scratch_shapes=[pltpu.SemaphoreType.DMA(())],
        compiler_params=_params(("arbitrary",)),
        name="moe_dispatch",
    )(pos, xg)


def _moe_ffn_kernel(blk_ref, grp_ref, first_ref, valid_ref, xs_ref, wg_ref, wu_ref, wd_ref, o_ref, xb_ref):
    w = pl.program_id(0)
    e = pl.program_id(1)

    @pl.when((first_ref[w] == 1) & (e == 0))
    def _():
        xb_ref[...] = xs_ref[:, 0:D_MODEL].astype(BF16)
        o_ref[...] = jnp.zeros_like(o_ref)

    @pl.when(valid_ref[w] == 1)
    def _():
        xb = xb_ref[...]
        gates = xs_ref[:, D_MODEL:]
        lane = lax.broadcasted_iota(jnp.int32, gates.shape, 1)
        w_tok = jnp.sum(jnp.where(lane == grp_ref[w] * EXPERTS_PER_GROUP + e, gates, 0.0), -1, keepdims=True)
        hg = _dot(xb, wg_ref[0].astype(BF16))
        hu = _dot(xb, wu_ref[0].astype(BF16))
        hid = hg * jax.nn.sigmoid(hg) * hu * w_tok
        o_ref[...] += _dot(hid.astype(BF16), wd_ref[0].astype(BF16))


def _moe_ffn(xs, items, w_gate, w_up, w_down, rb, e_base):
    n = xs.shape[0]
    blk, grp, first, valid = items

    def expert(w, e, blk, grp, first, valid):
        return (e_base + grp[w] * EXPERTS_PER_GROUP + jnp.where(valid[w] == 1, e, EXPERTS_PER_GROUP - 1), 0, 0)

    grid_spec = pltpu.PrefetchScalarGridSpec(
        num_scalar_prefetch=4,
        grid=(blk.shape[0], EXPERTS_PER_GROUP),
        in_specs=[pl.BlockSpec((rb, XG_W), lambda w, e, blk, grp, first, valid: (blk[w], 0)),
                  pl.BlockSpec((1, D_MODEL, D_EXPERT), expert),
                  pl.BlockSpec((1, D_MODEL, D_EXPERT), expert),
                  pl.BlockSpec((1, D_EXPERT, D_MODEL), expert)],
        out_specs=pl.BlockSpec((rb, D_MODEL), lambda w, e, blk, grp, first, valid: (blk[w], 0)),
        scratch_shapes=[pltpu.VMEM((rb, D_MODEL), BF16)],
    )
    return pl.pallas_call(
        _moe_ffn_kernel,
        grid_spec=grid_spec,
        out_shape=jax.ShapeDtypeStruct((n, D_MODEL), F32),
        compiler_params=_params(("arbitrary", "arbitrary")),
        name="moe_experts",
    )(blk, grp, first, valid, xs, w_gate, w_up, w_down)


def _combine_kernel(pos_ref, xg_ref, ys_hbm, g_ref, b_ref, o_ref, buf_ref, sem, *, tm):
    def issue(t, carry):
        pltpu.make_async_copy(ys_hbm.at[pl.ds(pos_ref[t], 1)], buf_ref.at[pl.ds(t, 1)], sem).start()
        return carry

    lax.fori_loop(0, tm, issue, 0, unroll=DMA_UNROLL)
    pltpu.make_async_copy(ys_hbm.at[pl.ds(0, tm)], buf_ref, sem).wait()
    o_ref[...] = _layer_norm(ALPHA * xg_ref[:, 0:D_MODEL] + buf_ref[...], g_ref[...], b_ref[...])


def _combine(pos, xg, ys, g, b, tm):
    n = xg.shape[0]
    return pl.pallas_call(
        functools.partial(_combine_kernel, tm=tm),
        grid=(n // tm,),
        in_specs=[pl.BlockSpec((tm,), lambda i: (i,), memory_space=pltpu.SMEM),
                  pl.BlockSpec((tm, XG_W), lambda i: (i, 0)),
                  pl.BlockSpec(memory_space=pl.ANY), _full(g.shape), _full(b.shape)],
        out_specs=pl.BlockSpec((tm, D_MODEL), lambda i: (i, 0)),
        out_shape=jax.ShapeDtypeStruct((n, D_MODEL), F32),
        scratch_shapes=[pltpu.VMEM((tm, D_MODEL), F32), pltpu.SemaphoreType.DMA(())],
        compiler_params=_params(("arbitrary",)),
        name="moe_combine_ln3",
    )(pos, xg, ys, g, b)


def _moe_work_items(counts, n, rb):
    nb = n // rb
    n_items = nb + N_GROUPS - 1
    ends = jnp.cumsum(counts)
    start = jnp.arange(nb, dtype=jnp.int32) * rb
    g_lo = jnp.sum(ends[None, :] <= start[:, None], axis=1).astype(jnp.int32)
    g_hi = jnp.sum(ends[None, :] <= (start + rb - 1)[:, None], axis=1).astype(jnp.int32)
    per_blk = g_hi - g_lo + 1
    item0 = jnp.cumsum(per_blk) - per_blk
    w = jnp.arange(n_items, dtype=jnp.int32)
    valid = w < jnp.sum(per_blk)
    blk = jnp.clip(jnp.sum(item0[None, :] <= w[:, None], axis=1) - 1, 0, nb - 1).astype(jnp.int32)
    grp = jnp.where(valid, g_lo[blk] + (w - item0[blk]), g_hi[nb - 1]).astype(jnp.int32)
    first = (valid & (w == item0[blk])).astype(jnp.int32)
    return blk, grp, first, valid.astype(jnp.int32)


def _moe(xg, rank, grp, counts, w_gate, w_up, w_down, e_base, g, b, t_dma, rb):
    n = xg.shape[0]
    counts = counts[0, :N_GROUPS].astype(jnp.int32)
    offsets = jnp.cumsum(counts) - counts
    pos = (offsets[grp[:, 0]] + rank[:, 0]).astype(jnp.int32)
    xs = _dispatch(xg, pos, t_dma)
    ys = _moe_ffn(xs, _moe_work_items(counts, n, rb), w_gate, w_up, w_down, rb, e_base)
    return _combine(pos, xg, ys, g, b, t_dma)


def _pad_cols(w, width):
    return jnp.pad(w, ((0, 0), (0, width - w.shape[1])))


def _layer_weights(w_in, w_out, gla_w_a2, ml_b_i, ml_b_f, mla_w_uq, mla_w_ukv, moe_w_group, moe_b_group,
                   moe_w_router, moe_b_router):
    o = _IN_OFF
    half = MLA_ROPE // 2
    wg = _pad_cols(w_in[:, o[0]:o[5]], SEG_W).astype(BF16)
    wm = _pad_cols(w_in[:, o[5]:o[10]], SEG_W).astype(BF16)
    kr = w_in[:, o[12]:o[13]]
    zeros = lambda w: jnp.zeros((D_MODEL, w), F32)
    kra = jnp.concatenate([zeros(MLA_NOPE), kr, zeros(LANES - MLA_NOPE - MLA_ROPE)], 1)
    krb = jnp.concatenate([zeros(MLA_NOPE), kr[:, half:], kr[:, :half], zeros(LANES - MLA_NOPE - MLA_ROPE)], 1)
    wc = jnp.concatenate([w_in[:, o[10]:o[12]], kra, krb], 1).astype(BF16)
    wift = w_in[:, o[8]:o[10]].T.astype(BF16)
    wa2 = jnp.pad(gla_w_a2, ((0, LANES - GLA_GATE_RANK), (0, 0))).astype(BF16)
    b_gate = jnp.concatenate([ml_b_i, ml_b_f])
    b_col = jnp.pad(b_gate, (0, LANES - 2 * MIX_HEADS)).reshape(1, LANES)
    b_row = b_gate.reshape(2 * MIX_HEADS, 1)
    uq = mla_w_uq.reshape(MLA_Q_RANK, MLA_HEADS, MLA_NOPE + MLA_ROPE)
    zq = jnp.zeros((MLA_Q_RANK, MLA_HEADS, LANES - MLA_NOPE - MLA_ROPE), F32)
    wqa = jnp.concatenate([uq, zq], -1).reshape(MLA_Q_RANK, -1).astype(BF16)
    wqb = jnp.concatenate([jnp.zeros((MLA_Q_RANK, MLA_HEADS, MLA_NOPE), F32), uq[..., MLA_NOPE + half:],
                           uq[..., MLA_NOPE:MLA_NOPE + half], zq], -1).reshape(MLA_Q_RANK, -1).astype(BF16)
    ukv = mla_w_ukv.reshape(MLA_KV_RANK, MLA_HEADS, MLA_NOPE + MLA_V)
    wkn = jnp.concatenate([ukv[..., :MLA_NOPE], jnp.zeros((MLA_KV_RANK, MLA_HEADS, LANES - MLA_NOPE), F32)],
                          -1).reshape(MLA_KV_RANK, -1).astype(BF16)
    uv = ukv[..., MLA_NOPE:]
    zv = jnp.zeros_like(uv)
    odd = (jnp.arange(MLA_HEADS) % 2 == 1)[None, :, None]
    wv = jnp.concatenate([jnp.where(odd, zv, uv), jnp.where(odd, uv, zv)], -1).reshape(MLA_KV_RANK, -1).astype(BF16)
    wo = w_out.astype(BF16)
    w_route = _pad_cols(jnp.concatenate([moe_w_router, moe_w_group], 1), LANES)
    w_route_hi = w_route.astype(BF16)
    w_route_mid = (w_route - w_route_hi.astype(F32)).astype(BF16)
    b_route = jnp.pad(jnp.concatenate([moe_b_router, moe_b_group]), (0, LANES - N_EXPERTS - N_GROUPS)).reshape(1, LANES)
    return dict(wg=wg, wm=wm, wc=wc, wift=wift, wa2=wa2, b_col=b_col, b_row=b_row, wqa=wqa, wqb=wqb, wkn=wkn,
                wv=wv, wo_g=wo[0:MIX_W], wo_m=wo[MIX_W:2 * MIX_W], wo_c=wo[2 * MIX_W:], w_route_hi=w_route_hi, w_route_mid=w_route_mid,
                b_route=b_route)


def _tile(total, want):
    t = min(total, want)
    assert total % t == 0
    return t


def kernel(x, mem, positions, w_in, w_out, gla_w_a2, gla_b_a, gla_norm_g, ml_conv_w, ml_b_i, ml_b_f, ml_norm_g, mla_q_norm_g, mla_w_uq, mla_kv_norm_g, mla_w_ukv, xa_w_q, xa_w_kv, xa_w_o, moe_w_group, moe_b_group, moe_w_router, moe_b_router, moe_w_gate, moe_w_up, moe_w_down, ln1_g, ln1_b, ln2_g, ln2_b, ln3_g, ln3_b):
    batch, seq, _ = x.shape
    mem_len = mem.shape[1]
    n = batch * seq
    depth = w_in.shape[0]
    assert seq % CHUNK == 0
    t_in = _tile(n, 512)
    tm = _tile(n, 1024)
    ts = _tile(seq, 512)
    tq = _tile(seq, 512)
    t_moe = _tile(n, 1024)
    row = lambda a: a.reshape(1, -1)

    cos_t, sin_t = _rope_tables(positions.reshape(n, 1), _tile(n, 2048))
    mem2d = mem.reshape(batch * mem_len, D_MODEL)
    experts_gate = moe_w_gate.reshape(-1, D_MODEL, D_EXPERT)
    experts_up = moe_w_up.reshape(-1, D_MODEL, D_EXPERT)
    experts_down = moe_w_down.reshape(-1, D_EXPERT, D_MODEL)
    h = x.reshape(n, D_MODEL)
    for l in range(depth):
        w = _layer_weights(w_in[l], w_out[l], gla_w_a2[l], ml_b_i[l], ml_b_f[l], mla_w_uq[l], mla_w_ukv[l],
                           moe_w_group[l], moe_b_group[l], moe_w_router[l], moe_b_router[l])
        yg, ym, yift, q, k, v = _in_proj(h, w["wg"], w["wm"], w["wc"], w["wift"], cos_t, sin_t,
                                         row(mla_q_norm_g[l]), row(mla_kv_norm_g[l]),
                                         w["wqa"], w["wqb"], w["wkn"], w["wv"], t_in)
        gates_row = yift.reshape(2 * MIX_HEADS, n // CHUNK, CHUNK).transpose(1, 0, 2)
        og = _gla(yg, w["wa2"], row(gla_b_a[l]), row(gla_norm_g[l]), batch, seq, ts)
        om = _mlstm(ym, gates_row, ml_conv_w[l], w["b_col"], w["b_row"], row(ml_norm_g[l]), batch, seq, ts)
        oc = _mla_attn(q, k, v, batch, seq, tq)
        x1 = _out_proj(og, om, oc, h, w["wo_g"], w["wo_m"], w["wo_c"], row(ln1_g[l]), row(ln1_b[l]), tm)
        xk, xv = _xa_kv(mem2d, xa_w_kv[l].astype(BF16), mem_len)
        xg, rank, grp, counts = _xattn_route(
            x1, xk, xv, xa_w_q[l].astype(BF16), xa_w_o[l].astype(BF16), row(ln2_g[l]), row(ln2_b[l]),
            w["w_route_hi"], w["w_route_mid"], w["b_route"], batch, seq, mem_len, tm)
        h = _moe(xg, rank, grp, counts, experts_gate, experts_up, experts_down, l * N_EXPERTS,
                 row(ln3_g[l]), row(ln3_b[l]), t_moe, t_moe)
    return h.reshape(batch, seq, D_MODEL)
```

```python
import functools

import numpy as np
import jax
import jax.numpy as jnp
from jax import lax
from jax.experimental import pallas as pl
from jax.experimental.pallas import tpu as pltpu

F32 = jnp.float32
BF16 = jnp.bfloat16

D_MODEL = 1024
CHUNK = 64
HEAD_D = 64
MIX_HEADS = 4
MIX_W = MIX_HEADS * HEAD_D
GLA_GATE_RANK = 16
GLA_TAU = 16.0
MLSTM_CONV = 4
MLA_HEADS = 8
MLA_NOPE = 64
MLA_ROPE = 32
MLA_V = 64
MLA_Q_RANK = 256
MLA_KV_RANK = 128
MLA_QK_PAD = 128
ROPE_BASE = 10000.0
LOG2_E = 1.4426950408889634
MLA_HEAD_PAIR = 2
XA_HEADS = 4
XA_DH = D_MODEL // XA_HEADS
N_GROUPS = 4
EXPERTS_PER_GROUP = 8
N_EXPERTS = N_GROUPS * EXPERTS_PER_GROUP
D_EXPERT = 256
DEPTH = 2
ALPHA = (2 * DEPTH) ** 0.25
LN_EPS = 1e-5
LANES = 128
SUBLANES = 8
MIXER_UNROLL = 4
SEG_W = 1152
MLA_SEG_W = 640
XG_W = D_MODEL + LANES
DMA_UNROLL = 32
GLA_LEVELS = (32, 16, 8, 4, 2, 1)
VMEM_LIMIT = 56 * 1024 * 1024

_IN_SIZES = (256, 256, 256, 256, GLA_GATE_RANK, 512, 256, 256, 4, 4, MLA_Q_RANK, MLA_KV_RANK, MLA_ROPE)
_IN_OFF = np.concatenate([[0], np.cumsum(_IN_SIZES)]).tolist()


def _params(sem):
    return pltpu.CompilerParams(dimension_semantics=sem, vmem_limit_bytes=VMEM_LIMIT)


def _full(shape):
    return pl.BlockSpec(shape, lambda *_: (0,) * len(shape))


def _layer_norm(x, g, b):
    mu = jnp.mean(x, -1, keepdims=True)
    xc = x - mu
    var = jnp.mean(xc * xc, -1, keepdims=True)
    return xc * lax.rsqrt(var + LN_EPS) * g + b


def _log_sigmoid(z):
    return jnp.minimum(z, 0.0) - jnp.log1p(jnp.exp(-jnp.abs(z)))


def _dot_nt(a, b):
    return lax.dot_general(a, b, (((1,), (1,)), ((), ())), preferred_element_type=F32)


def _dot_tn(a, b):
    return lax.dot_general(a, b, (((0,), (0,)), ((), ())), preferred_element_type=F32)


def _dot(a, b):
    return jnp.dot(a, b, preferred_element_type=F32)


def _head_norm(o, g):
    mu = jnp.mean(o, -1, keepdims=True)
    oc = o - mu
    var = jnp.mean(oc * oc, -1, keepdims=True)
    return oc * lax.rsqrt(var + LN_EPS) * g


def _in_proj_kernel(x_ref, wg_ref, wm_ref, wc_ref, wift_ref, cos_ref, sin_ref, gq_ref, gkv_ref,
                    wqa_ref, wqb_ref, wkn_ref, wv_ref, ones_ref,
                    yg_ref, ym_ref, yift_ref, q_ref, k_ref, v_ref, yc_ref):
    xb = x_ref[...].astype(BF16)
    yg_ref[...] = _dot(xb, wg_ref[...])
    ym_ref[...] = _dot(xb, wm_ref[...])
    yift_ref[...] = _dot_nt(wift_ref[...], xb)
    yc_ref[...] = _dot(xb, wc_ref[...])
    _mla_prep_tile(yc_ref, cos_ref, sin_ref, gq_ref, gkv_ref, wqa_ref, wqb_ref, wkn_ref, wv_ref, ones_ref,
                   q_ref, k_ref, v_ref)


def _in_proj(x2d, wg, wm, wc, wift, cos_t, sin_t, gq, gkv, wqa, wqb, wkn, wv, tm):
    n = x2d.shape[0]
    ones_row = np.zeros((1, MLA_HEADS * LANES), np.float32)
    for h in range(MLA_HEADS):
        ones_row[0, h * LANES + (MLA_V if h % 2 == 0 else 0)] = 1.0
    ones_row = jnp.asarray(ones_row)
    row = lambda w: pl.BlockSpec((tm, w), lambda i: (i, 0))
    qk_w = MLA_HEADS * MLA_QK_PAD
    return pl.pallas_call(
        _in_proj_kernel,
        grid=(n // tm,),
        in_specs=[row(D_MODEL), _full(wg.shape), _full(wm.shape), _full(wc.shape), _full(wift.shape),
                  row(LANES), row(LANES), _full(gq.shape), _full(gkv.shape),
                  _full(wqa.shape), _full(wqb.shape), _full(wkn.shape), _full(wv.shape), _full(ones_row.shape)],
        out_specs=[row(SEG_W), row(SEG_W), pl.BlockSpec((8, tm), lambda i: (0, i)),
                   row(qk_w), row(qk_w), row(MLA_HEADS * LANES)],
        out_shape=[jax.ShapeDtypeStruct((n, SEG_W), F32), jax.ShapeDtypeStruct((n, SEG_W), F32),
                   jax.ShapeDtypeStruct((8, n), F32),
                   jax.ShapeDtypeStruct((n, qk_w), BF16), jax.ShapeDtypeStruct((n, qk_w), BF16),
                   jax.ShapeDtypeStruct((n, MLA_HEADS * LANES), BF16)],
        scratch_shapes=[pltpu.VMEM((tm, MLA_SEG_W), F32)],
        compiler_params=_params(("parallel",)),
        name="in_proj_mla_prep",
    )(x2d, wg, wm, wc, wift, cos_t, sin_t, gq, gkv, wqa, wqb, wkn, wv, ones_row)


def _split3(x):
    hi = x.astype(BF16)
    r1 = x - hi.astype(F32)
    mid = r1.astype(BF16)
    lo = (r1 - mid.astype(F32)).astype(BF16)
    return hi, mid, lo


def _cumsum_rows(tri, x):
    hi, mid, lo = _split3(x)
    return _dot(tri, hi) + _dot(tri, mid) + _dot(tri, lo)


def _gla_constants():
    t = np.arange(CHUNK)
    n_lv = len(GLA_LEVELS)
    masks = np.zeros((n_lv + 1, CHUNK, CHUNK), np.float32)
    right = np.zeros((n_lv, CHUNK, 1), np.float32)
    for li, n in enumerate(GLA_LEVELS):
        blk = t // (2 * n)
        is_right = (t % (2 * n)) >= n
        masks[li] = ((blk[:, None] == blk[None, :]) & is_right[:, None] & ~is_right[None, :])
        right[li, :, 0] = is_right
    masks[n_lv] = np.eye(CHUNK)
    return masks, right


def _gla_level_exponents(b, log_a, right_ref):
    row = lax.broadcasted_iota(jnp.int32, (CHUNK, 1), 0)
    exps = []
    for li, n in enumerate(GLA_LEVELS):
        if n >= SUBLANES // 2:
            per_blk = max(2 * n // SUBLANES, 1)
            b4 = b.reshape(CHUNK // (per_blk * SUBLANES), per_blk, SUBLANES, MIX_W)
            r_reg, r_sub = divmod(n - 1, SUBLANES)
            src = b4[:, r_reg:r_reg + 1, r_sub:r_sub + 1, :]
            b_r = jnp.broadcast_to(src, b4.shape).reshape(CHUNK, MIX_W)
            exps.append(jnp.where(right_ref[li] > 0.0, b - b_r, b_r - b))
        elif n == 2:
            pos = row % 4
            nxt = pltpu.roll(log_a, CHUNK - 1, 0)
            prv = pltpu.roll(log_a, 1, 0)
            exps.append(jnp.where(pos == 0, nxt, jnp.where(pos == 1, 0.0,
                                                            jnp.where(pos == 2, log_a, log_a + prv))))
        else:
            exps.append(jnp.where(right_ref[li] > 0.0, log_a, 0.0))
    return exps


def _pair_blocks(z, low_half):
    zero = jnp.zeros_like(z)
    return jnp.concatenate([jnp.where(low_half, z, zero), jnp.where(low_half, zero, z)], axis=0)


def _gla_kernel(y_ref, wa2_ref, ba_ref, g_ref, tri_ref, mask_ref, right_ref, o_ref, st_ref, *, ts, unroll):
    n_lv = len(GLA_LEVELS)
    pairs = MIX_HEADS // 2
    low_half = lax.broadcasted_iota(jnp.int32, (CHUNK, LANES), 1) < HEAD_D
    same_head = (lax.broadcasted_iota(jnp.int32, (LANES, LANES), 0) < HEAD_D) == (
        lax.broadcasted_iota(jnp.int32, (LANES, LANES), 1) < HEAD_D)

    @pl.when(pl.program_id(1) == 0)
    def _():
        st_ref[...] = jnp.zeros_like(st_ref)

    def chunk(c, carry):
        rows = pl.ds(pl.multiple_of(c * CHUNK, CHUNK), CHUNK)
        q = y_ref[rows, 0:256] * (HEAD_D ** -0.5)
        k = y_ref[rows, 256:512]
        a_lr = y_ref[rows, 1024:1152]
        z = _dot(a_lr.astype(BF16), wa2_ref[...]) + ba_ref[...]
        log_a = _log_sigmoid(z) * (1.0 / GLA_TAU)
        b = _cumsum_rows(tri_ref[...], log_a)
        b_end = b[CHUNK - 1:CHUNK, :]
        scores = [jnp.zeros((CHUNK, LANES), F32) for _ in range(pairs)]
        for li, e in enumerate(_gla_level_exponents(b, log_a, right_ref)):
            x = (jnp.where(right_ref[li] > 0.0, q, k) * jnp.exp(e)).astype(BF16)
            for p in range(pairs):
                x2 = x[:, p * LANES:(p + 1) * LANES]
                scores[p] = scores[p] + _dot_nt(x2, _pair_blocks(x2, low_half)) * mask_ref[li]
        qb = q.astype(BF16)
        kb = k.astype(BF16)
        q_in = (q * jnp.exp(b)).astype(BF16)
        k_out = (k * jnp.exp(b_end - b)).astype(BF16)
        dec_end = jnp.exp(b_end)
        for p in range(pairs):
            pl_ = slice(p * LANES, (p + 1) * LANES)
            a = scores[p] + _dot_nt(qb[:, pl_], _pair_blocks(kb[:, pl_], low_half)) * mask_ref[n_lv]
            v2 = y_ref[rows, 512 + p * LANES:512 + (p + 1) * LANES].astype(BF16)
            st = st_ref[p]
            o = _dot(a.astype(BF16), _pair_blocks(v2, low_half)) + _dot_nt(q_in[:, pl_], st.astype(BF16))
            st_ref[p] = st * dec_end[:, pl_] + jnp.where(same_head, _dot_tn(v2, k_out[:, pl_]), 0.0)
            normed = jnp.zeros((CHUNK, LANES), F32)
            for mine in (low_half, jnp.logical_not(low_half)):
                mu = jnp.sum(jnp.where(mine, o, 0.0), -1, keepdims=True) * (1.0 / HEAD_D)
                cen = jnp.where(mine, o - mu, 0.0)
                var = jnp.sum(cen * cen, -1, keepdims=True) * (1.0 / HEAD_D)
                normed = normed + cen * lax.rsqrt(var + LN_EPS)
            r_gate = y_ref[rows, 768 + p * LANES:768 + (p + 1) * LANES]
            o_ref[rows, pl_] = (normed * g_ref[:, pl_] * (r_gate * jax.nn.sigmoid(r_gate))).astype(o_ref.dtype)
        return carry

    lax.fori_loop(0, ts // CHUNK, chunk, 0, unroll=unroll)


def _gla(yg, wa2, ba, g, batch, seq, ts):
    n = yg.shape[0]
    nt = seq // ts
    masks, right = _gla_constants()
    masks = np.concatenate([masks, masks], axis=-1)
    tri = jnp.asarray(np.tril(np.ones((CHUNK, CHUNK), np.float32)), BF16)
    return pl.pallas_call(
        functools.partial(_gla_kernel, ts=ts, unroll=MIXER_UNROLL),
        grid=(batch, nt),
        in_specs=[pl.BlockSpec((ts, SEG_W), lambda b, i: (b * nt + i, 0)),
                  _full(wa2.shape), _full(ba.shape), _full(g.shape),
                  _full(tri.shape), _full(masks.shape), _full(right.shape)],
        out_specs=pl.BlockSpec((ts, MIX_W), lambda b, i: (b * nt + i, 0)),
        out_shape=jax.ShapeDtypeStruct((n, MIX_W), BF16),
        scratch_shapes=[pltpu.VMEM((MIX_HEADS // 2, LANES, LANES), F32)],
        compiler_params=_params(("parallel", "arbitrary")),
        name="gla_mixer",
    )(yg, wa2, ba, g, tri, jnp.asarray(masks), jnp.asarray(right))


def _mlstm_kernel(y_ref, gr_ref, cw_ref, bcol_ref, brow_ref, g_ref, tri_ref, eb_ref, eye_ref, o_ref,
                  xe_ref, qk_ref, vx_ref, fcl_ref, wsl_ref, fcb_ref, wsb_ref, rv_ref, dec_ref,
                  w_ref, em_ref, upd_ref, c_ref, m_ref, *, ts):
    first = pl.program_id(1) == 0

    @pl.when(first)
    def _():
        xe_ref[0:8, :] = jnp.zeros((8, 2 * MIX_W), F32)
        c_ref[...] = jnp.zeros_like(c_ref)
        m_ref[...] = jnp.zeros_like(m_ref)

    @pl.when(jnp.logical_not(first))
    def _():
        xe_ref[0:8, :] = xe_ref[ts:ts + 8, :]

    xe_ref[8:ts + 8, :] = y_ref[:, 0:2 * MIX_W]
    for c in range(ts // CHUNK):
        r0 = 8 + c * CHUNK
        conv = cw_ref[MLSTM_CONV - 1:MLSTM_CONV, :] * xe_ref[r0:r0 + CHUNK, :]
        for j in range(MLSTM_CONV - 1):
            lo = r0 - (MLSTM_CONV - 1) + j
            conv = conv + cw_ref[j:j + 1, :] * xe_ref[lo:lo + CHUNK, :]
        qk_ref[c * CHUNK:(c + 1) * CHUNK, :] = conv * jax.nn.sigmoid(conv)

    ones_col = (lax.broadcasted_iota(jnp.int32, (ts, HEAD_D), 1) == 0).astype(BF16)
    for h in range(MIX_HEADS):
        v_at, one_at = (0, HEAD_D) if h % 2 == 0 else (HEAD_D, 0)
        vx_ref[:, h * LANES + v_at:h * LANES + v_at + HEAD_D] = (
            y_ref[:, 512 + h * HEAD_D:512 + (h + 1) * HEAD_D].astype(BF16))
        vx_ref[:, h * LANES + one_at:h * LANES + one_at + HEAD_D] = ones_col

    tri = tri_ref[...]
    expand = eb_ref[...]
    nck = ts // CHUNK
    lane = lax.broadcasted_iota(jnp.int32, (1, LANES), 1)
    f_lanes = (lane >= MIX_HEADS) & (lane < 2 * MIX_HEADS)
    head_lane = (lax.broadcasted_iota(jnp.int32, (MIX_HEADS, LANES), 1)
                 == lax.broadcasted_iota(jnp.int32, (MIX_HEADS, LANES), 0) + MIX_HEADS)

    def to_lanes(col):
        return jnp.sum(jnp.where(head_lane, col, 0.0), 0, keepdims=True)

    g_rows = gr_ref[...] + brow_ref[...]
    ls_rows = _log_sigmoid(g_rows).reshape(nck * SUBLANES, CHUNK)
    fcum_rows = sum(_dot_nt(part, tri) for part in _split3(ls_rows)).reshape(nck, SUBLANES, CHUNK)
    m_col = m_ref[0:MIX_HEADS, 0:1]
    for c in range(nck):
        g_col = y_ref[c * CHUNK:(c + 1) * CHUNK, 1024:1152] + bcol_ref[...]
        fcum_col = jnp.where(f_lanes, _cumsum_rows(tri, _log_sigmoid(g_col)), 0.0)
        fcl_ref[c * CHUNK:(c + 1) * CHUNK, :] = fcum_col
        g_row = g_rows[c]
        fcum_row = fcum_rows[c]
        f_row = fcum_row[MIX_HEADS:2 * MIX_HEADS, :]
        i_row = g_row[0:MIX_HEADS, :]
        f_end = f_row[:, CHUNK - 1:CHUNK]
        rv_ref[c, 0:MIX_HEADS, :] = jnp.concatenate(
            [i_row - f_row, jnp.broadcast_to(m_col, (MIX_HEADS, HEAD_D))], axis=1)
        m_new = jnp.maximum(f_end + m_col, jnp.max(f_end - f_row + i_row, -1, keepdims=True))
        m_new_l = to_lanes(m_new)
        f_end_l = fcum_col[CHUNK - 1:CHUNK, :]
        i_shift = pltpu.roll(g_col, MIX_HEADS, 1)
        wsl_ref[c * CHUNK:(c + 1) * CHUNK, :] = jnp.where(
            f_lanes, jnp.exp(f_end_l - fcum_col + i_shift - m_new_l), 0.0)
        dec_ref[c] = jnp.broadcast_to(jnp.exp(f_end_l + to_lanes(m_col) - m_new_l), (SUBLANES, LANES))
        m_col = m_new
    m_ref[0:MIX_HEADS, :] = jnp.broadcast_to(m_col, (MIX_HEADS, LANES))
    slab = 2 * CHUNK
    for i in range(ts // slab):
        sr = slice(i * slab, (i + 1) * slab)
        fcb_ref[sr, :] = sum(_dot(part, expand) for part in _split3(fcl_ref[sr, :]))
        ws_hi, ws_mid, _ = _split3(wsl_ref[sr, :])
        wsb_ref[sr, :] = _dot(ws_hi, expand) + _dot(ws_mid, expand)

    t_idx = lax.broadcasted_iota(jnp.int32, (CHUNK, LANES), 0)
    s_idx = lax.broadcasted_iota(jnp.int32, (CHUNK, LANES), 1)
    keep = (s_idx <= t_idx) | (s_idx >= CHUNK)
    low_half = s_idx < HEAD_D

    def weights(c, carry):
        rows = pl.ds(pl.multiple_of(c * CHUNK, CHUNK), CHUNK)
        rv = rv_ref[c, 0:MIX_HEADS, :]
        for p in range(MIX_HEADS // 2):
            q2 = (qk_ref[rows, p * LANES:(p + 1) * LANES] * (HEAD_D ** -0.5)).astype(BF16)
            k2 = qk_ref[rows, MIX_W + p * LANES:MIX_W + (p + 1) * LANES]
            for h in (2 * p, 2 * p + 1):
                mine = low_half if h % 2 == 0 else jnp.logical_not(low_half)
                hb = slice(h * LANES, (h + 1) * LANES)
                logw = jnp.where(keep, fcb_ref[rows, hb] + rv[h:h + 1, :], -jnp.inf)
                m_t = jnp.max(logw, -1, keepdims=True)
                k_h = jnp.where(mine, k2, 0.0)
                qk = _dot_nt(q2, jnp.concatenate([k_h.astype(BF16), eye_ref[h % 2]], axis=0))
                w_ref[c * MIX_HEADS + h] = (jnp.exp(logw - m_t) * qk).astype(BF16)
                em_ref[c * MIX_HEADS + h] = jnp.exp(-m_t)
                upd = _dot_tn((k_h * wsb_ref[rows, hb]).astype(BF16), vx_ref[rows, hb])
                off = (h % 2) * HEAD_D
                upd_ref[c * MIX_HEADS + h] = upd[off:off + HEAD_D, :]
        return carry

    lax.fori_loop(0, nck, weights, 0, unroll=2)

    def chunk(c, carry):
        rows = pl.ds(pl.multiple_of(c * CHUNK, CHUNK), CHUNK)
        dec_l = dec_ref[c]
        for p in range(MIX_HEADS // 2):
            pl_ = slice(p * LANES, (p + 1) * LANES)
            pair = jnp.zeros((CHUNK, LANES), F32)
            for h in (2 * p, 2 * p + 1):
                mine = low_half if h % 2 == 0 else jnp.logical_not(low_half)
                c_st = c_ref[h]
                num = _dot(w_ref[c * MIX_HEADS + h],
                           jnp.concatenate([vx_ref[rows, h * LANES:(h + 1) * LANES], c_st.astype(BF16)], axis=0))
                c_ref[h] = dec_l[0:1, MIX_HEADS + h:MIX_HEADS + h + 1] * c_st + upd_ref[c * MIX_HEADS + h]
                den = num[:, HEAD_D:HEAD_D + 1] if h % 2 == 0 else num[:, 0:1]
                r = 1.0 / jnp.maximum(jnp.abs(den), em_ref[c * MIX_HEADS + h])
                mu = jnp.sum(jnp.where(mine, num, 0.0), -1, keepdims=True) * (1.0 / HEAD_D)
                cen = jnp.where(mine, num - mu, 0.0)
                var = jnp.sum(cen * cen, -1, keepdims=True) * (1.0 / HEAD_D)
                pair = pair + cen * (r * lax.rsqrt(r * r * var + LN_EPS))
            o_gate = y_ref[rows, 768 + p * LANES:768 + (p + 1) * LANES]
            o_ref[rows, pl_] = (pair * g_ref[:, pl_] * jax.nn.sigmoid(o_gate)).astype(o_ref.dtype)
        return carry

    lax.fori_loop(0, nck, chunk, 0, unroll=2)


def _mlstm(ym, gates_row, conv_w, b_col, b_row, g, batch, seq, ts):
    n = ym.shape[0]
    nt = seq // ts
    nck = ts // CHUNK
    tri = jnp.asarray(np.tril(np.ones((CHUNK, CHUNK), np.float32)), BF16)
    expand = np.zeros((LANES, MIX_HEADS * LANES), np.float32)
    for h in range(MIX_HEADS):
        expand[MIX_HEADS + h, h * LANES:(h + 1) * LANES] = 1.0
    eye = np.zeros((2, HEAD_D, LANES), np.float32)
    eye[0, :, 0:HEAD_D] = np.eye(HEAD_D)
    eye[1, :, HEAD_D:] = np.eye(HEAD_D)
    expand = jnp.asarray(expand, BF16)
    eye = jnp.asarray(eye, BF16)
    return pl.pallas_call(
        functools.partial(_mlstm_kernel, ts=ts),
        grid=(batch, nt),
        in_specs=[pl.BlockSpec((ts, SEG_W), lambda b, i: (b * nt + i, 0)),
                  pl.BlockSpec((nck, 8, CHUNK), lambda b, i: (b * nt + i, 0, 0)),
                  _full(conv_w.shape), _full(b_col.shape), _full(b_row.shape), _full(g.shape),
                  _full(tri.shape), _full(expand.shape), _full(eye.shape)],
        out_specs=pl.BlockSpec((ts, MIX_W), lambda b, i: (b * nt + i, 0)),
        out_shape=jax.ShapeDtypeStruct((n, MIX_W), BF16),
        scratch_shapes=[pltpu.VMEM((ts + 8, 2 * MIX_W), F32),
                        pltpu.VMEM((ts, 2 * MIX_W), F32),
                        pltpu.VMEM((ts, MIX_HEADS * LANES), BF16),
                        pltpu.VMEM((ts, LANES), F32),
                        pltpu.VMEM((ts, LANES), F32),
                        pltpu.VMEM((ts, MIX_HEADS * LANES), F32),
                        pltpu.VMEM((ts, MIX_HEADS * LANES), F32),
                        pltpu.VMEM((nck, SUBLANES, LANES), F32),
                        pltpu.VMEM((nck, SUBLANES, LANES), F32),
                        pltpu.VMEM((nck * MIX_HEADS, CHUNK, LANES), BF16),
                        pltpu.VMEM((nck * MIX_HEADS, CHUNK, 1), F32),
                        pltpu.VMEM((nck * MIX_HEADS, HEAD_D, LANES), F32),
                        pltpu.VMEM((MIX_HEADS, HEAD_D, LANES), F32),
                        pltpu.VMEM((SUBLANES, LANES), F32)],
        compiler_params=_params(("parallel", "arbitrary")),
        name="mlstm_mixer",
    )(ym, gates_row, conv_w, b_col, b_row, g, tri, expand, eye)


def _rope_table_kernel(pos_ref, inv_ref, cos_ref, sin_ref):
    ang = pos_ref[...].astype(F32) * inv_ref[...]
    lane = lax.broadcasted_iota(jnp.int32, ang.shape, 1)
    rot = (lane >= MLA_NOPE) & (lane < MLA_NOPE + MLA_ROPE)
    first_half = lane < MLA_NOPE + MLA_ROPE // 2
    cos_ref[...] = jnp.where(lane < MLA_NOPE, 1.0, jnp.where(rot, jnp.cos(ang), 0.0))
    s = jnp.sin(ang)
    sin_ref[...] = jnp.where(rot, jnp.where(first_half, -s, s), 0.0)


def _rope_tables(pos_col, tm):
    n = pos_col.shape[0]
    half = MLA_ROPE // 2
    inv = ROPE_BASE ** (-np.arange(half, dtype=np.float32) / half)
    inv_row = np.zeros((1, LANES), np.float32)
    inv_row[0, MLA_NOPE:MLA_NOPE + half] = inv
    inv_row[0, MLA_NOPE + half:MLA_NOPE + MLA_ROPE] = inv
    return pl.pallas_call(
        _rope_table_kernel,
        grid=(n // tm,),
        in_specs=[pl.BlockSpec((tm, 1), lambda i: (i, 0)), _full((1, LANES))],
        out_specs=[pl.BlockSpec((tm, LANES), lambda i: (i, 0))] * 2,
        out_shape=[jax.ShapeDtypeStruct((n, LANES), F32)] * 2,
        compiler_params=_params(("parallel",)),
        name="rope_tables",
    )(pos_col, jnp.asarray(inv_row))


def _mla_prep_tile(y_ref, cos_ref, sin_ref, gq_ref, gkv_ref, wqa_ref, wqb_ref, wkn_ref, wv_ref, ones_ref,
                   q_ref, k_ref, v_ref):
    def rms(x, g):
        return x * lax.rsqrt(jnp.mean(x * x, -1, keepdims=True) + LN_EPS) * g

    cos = cos_ref[...]
    sin = sin_ref[...]
    cq = rms(y_ref[:, 0:MLA_Q_RANK], gq_ref[...]).astype(BF16)
    ckv = rms(y_ref[:, MLA_Q_RANK:MLA_Q_RANK + MLA_KV_RANK], gkv_ref[...]).astype(BF16)
    k_rope = y_ref[:, 384:512] * cos + y_ref[:, 512:640] * sin
    qa = _dot(cq, wqa_ref[...])
    qb = _dot(cq, wqb_ref[...])
    kn = _dot(ckv, wkn_ref[...])
    scale = (MLA_NOPE + MLA_ROPE) ** -0.5 * LOG2_E
    for h in range(MLA_HEADS):
        sl = slice(h * MLA_QK_PAD, (h + 1) * MLA_QK_PAD)
        q_ref[:, sl] = ((qa[:, sl] * cos + qb[:, sl] * sin) * scale).astype(q_ref.dtype)
        k_ref[:, sl] = (kn[:, sl] + k_rope).astype(k_ref.dtype)
    v_ref[...] = (_dot(ckv, wv_ref[...]) + ones_ref[...]).astype(v_ref.dtype)


def _mla_attn_kernel(q_ref, k_ref, v_ref, o_ref, *, seq, tq):
    t_chunk = lax.broadcasted_iota(jnp.int32, (tq, tq), 0) // CHUNK
    s_chunk = lax.broadcasted_iota(jnp.int32, (tq, tq), 1) // CHUNK
    diag_mask = s_chunk <= t_chunk
    low_half = lax.broadcasted_iota(jnp.int32, (tq, LANES), 1) < MLA_V
    for i in range(seq // tq):
        rows = slice(i * tq, (i + 1) * tq)
        res = []
        for hh in range(MLA_HEAD_PAIR):
            ql = slice(hh * MLA_QK_PAD, (hh + 1) * MLA_QK_PAD)
            vl = slice(hh * LANES, (hh + 1) * LANES)
            q = q_ref[0, rows, ql]
            s_d = jnp.where(diag_mask, _dot_nt(q, k_ref[0, rows, ql]), -jnp.inf)
            m = jnp.max(s_d, -1, keepdims=True)
            if i > 0:
                s_o = _dot_nt(q, k_ref[0, 0:i * tq, ql])
                m = jnp.maximum(m, jnp.max(s_o, -1, keepdims=True))
            o = _dot(jnp.exp2(s_d - m).astype(BF16), v_ref[0, rows, vl])
            if i > 0:
                o = o + _dot(jnp.exp2(s_o - m).astype(BF16), v_ref[0, 0:i * tq, vl])
            l = o[:, MLA_V:MLA_V + 1] if hh == 0 else o[:, 0:1]
            res.append(o / l)
        o_ref[0, rows, :] = jnp.where(low_half, res[0], res[1]).astype(o_ref.dtype)


def _mla_attn(q, k, v, batch, seq, tq):
    n = q.shape[0]
    qk_w = MLA_HEADS * MLA_QK_PAD
    v_w = MLA_HEADS * MLA_V
    pair = lambda w: pl.BlockSpec((1, seq, MLA_HEAD_PAIR * w), lambda b, h: (b, 0, h))
    out = pl.pallas_call(
        functools.partial(_mla_attn_kernel, seq=seq, tq=tq),
        grid=(batch, MLA_HEADS // MLA_HEAD_PAIR),
        in_specs=[pair(MLA_QK_PAD), pair(MLA_QK_PAD), pair(LANES)],
        out_specs=pair(MLA_V),
        out_shape=jax.ShapeDtypeStruct((batch, seq, v_w), BF16),
        compiler_params=_params(("parallel", "parallel")),
        name="mla_attention",
    )(q.reshape(batch, seq, qk_w), k.reshape(batch, seq, qk_w), v.reshape(batch, seq, MLA_HEADS * LANES))
    return out.reshape(n, v_w)


def _out_proj_kernel(og_ref, om_ref, oc_ref, x_ref, wg_ref, wm_ref, wc_ref, g_ref, b_ref, o_ref):
    mix = _dot(og_ref[...], wg_ref[...]) + _dot(om_ref[...], wm_ref[...]) + _dot(oc_ref[...], wc_ref[...])
    o_ref[...] = _layer_norm(ALPHA * x_ref[...] + mix, g_ref[...], b_ref[...])


def _out_proj(og, om, oc, x2d, wg, wm, wc, g, b, tm):
    n = x2d.shape[0]
    row = lambda w: pl.BlockSpec((tm, w), lambda i: (i, 0))
    return pl.pallas_call(
        _out_proj_kernel,
        grid=(n // tm,),
        in_specs=[row(MIX_W), row(MIX_W), row(MLA_HEADS * MLA_V), row(D_MODEL),
                  _full(wg.shape), _full(wm.shape), _full(wc.shape), _full(g.shape), _full(b.shape)],
        out_specs=row(D_MODEL),
        out_shape=jax.ShapeDtypeStruct((n, D_MODEL), F32),
        compiler_params=_params(("parallel",)),
        name="out_proj_ln1",
    )(og, om, oc, x2d, wg, wm, wc, g, b)


def _xa_kv_kernel(mem_ref, w_ref, k_ref, v_ref):
    kv = _dot(mem_ref[...].astype(BF16), w_ref[...])
    k_ref[...] = kv[:, 0:D_MODEL].astype(k_ref.dtype)
    v_ref[...] = kv[:, D_MODEL:2 * D_MODEL].astype(v_ref.dtype)


def _xa_kv(mem2d, w_kv, mem_len):
    n = mem2d.shape[0]
    row = pl.BlockSpec((mem_len, D_MODEL), lambda i: (i, 0))
    return pl.pallas_call(
        _xa_kv_kernel,
        grid=(n // mem_len,),
        in_specs=[row, _full(w_kv.shape)],
        out_specs=[row, row],
        out_shape=[jax.ShapeDtypeStruct((n, D_MODEL), BF16)] * 2,
        compiler_params=_params(("parallel",)),
        name="xattn_kv",
    )(mem2d, w_kv)


def _route_tile(x, wh_ref, wm_ref, b_ref, tri_ref, cnt_ref):
    xh = x.astype(BF16)
    xm = (x - xh.astype(F32)).astype(BF16)
    logits = _dot(xh, wh_ref[...]) + _dot(xh, wm_ref[...]) + _dot(xm, wh_ref[...]) + b_ref[...]
    lane = lax.broadcasted_iota(jnp.int32, logits.shape, 1).astype(F32)
    is_group = (lane >= N_EXPERTS) & (lane < N_EXPERTS + N_GROUPS)
    g_max = jnp.max(jnp.where(is_group, logits, -jnp.inf), -1, keepdims=True)
    g_sum = jnp.sum(jnp.where(is_group, jnp.exp(logits - g_max), 0.0), -1, keepdims=True)
    g_p = 1.0 / g_sum
    g_idx = jnp.min(jnp.where(is_group & (logits == g_max), lane - N_EXPERTS, float(LANES)), -1, keepdims=True)
    in_group = (lane < N_EXPERTS) & (jnp.floor(lane * (1.0 / EXPERTS_PER_GROUP)) == g_idx)
    e_max = jnp.max(jnp.where(in_group, logits, -jnp.inf), -1, keepdims=True)
    e_exp = jnp.where(in_group, jnp.exp(logits - e_max), 0.0)
    prob = e_exp / jnp.sum(e_exp, -1, keepdims=True)
    cand = jnp.where(in_group, prob, -1.0)
    p1 = jnp.max(cand, -1, keepdims=True)
    i1 = jnp.min(jnp.where(cand == p1, lane, float(LANES)), -1, keepdims=True)
    cand2 = jnp.where(lane == i1, -1.0, cand)
    p2 = jnp.max(cand2, -1, keepdims=True)
    i2 = jnp.min(jnp.where(cand2 == p2, lane, float(LANES)), -1, keepdims=True)
    p_sum = p1 + p2
    gates = jnp.where(lane == i1, g_p * (p1 / p_sum), 0.0) + jnp.where(lane == i2, g_p * (p2 / p_sum), 0.0)
    onehot = jnp.where(lane == g_idx, 1.0, 0.0)
    before = _dot(tri_ref[...], onehot.astype(BF16)) + cnt_ref[...]
    rank = jnp.sum(onehot * before, -1, keepdims=True).astype(jnp.int32)
    cnt_ref[...] += jnp.sum(onehot, 0, keepdims=True)
    return gates, rank, g_idx.astype(jnp.int32)


def _xattn_kernel(x_ref, k_ref, v_ref, wq_ref, wo_ref, g_ref, b_ref, wrh_ref, wrm_ref, br_ref, tri_ref,
                  xg_ref, rank_ref, grp_ref, cnt_out_ref, cnt_ref):
    @pl.when((pl.program_id(0) == 0) & (pl.program_id(1) == 0))
    def _():
        cnt_ref[...] = jnp.zeros_like(cnt_ref)

    x = x_ref[...]
    q = (_dot(x.astype(BF16), wq_ref[...]) * (XA_DH ** -0.5)).astype(BF16)
    out = jnp.zeros(x.shape, F32)
    for h in range(XA_HEADS):
        sl = slice(h * XA_DH, (h + 1) * XA_DH)
        s = _dot_nt(q[:, sl], k_ref[:, sl])
        p = jnp.exp(s - jnp.max(s, -1, keepdims=True))
        p = p / jnp.sum(p, -1, keepdims=True)
        o = _dot(p.astype(BF16), v_ref[:, sl])
        out = out + _dot(o.astype(BF16), wo_ref[sl, :])
    x2 = _layer_norm(ALPHA * x + out, g_ref[...], b_ref[...])
    gates, rank, grp = _route_tile(x2, wrh_ref, wrm_ref, br_ref, tri_ref, cnt_ref)
    xg_ref[:, 0:D_MODEL] = x2
    xg_ref[:, D_MODEL:] = gates
    rank_ref[...] = rank
    grp_ref[...] = grp
    cnt_out_ref[...] = cnt_ref[...]


def _xattn_route(x1, xk, xv, wq, wo, g, b, w_route_hi, w_route_mid, b_route, batch, seq, mem_len, tm):
    n = x1.shape[0]
    nt = seq // tm
    tri = jnp.asarray(np.tril(np.ones((tm, tm), np.float32), -1), BF16)
    tile = lambda w: pl.BlockSpec((tm, w), lambda bi, i: (bi * nt + i, 0))
    mem = pl.BlockSpec((mem_len, D_MODEL), lambda bi, i: (bi, 0))
    return pl.pallas_call(
        _xattn_kernel,
        grid=(batch, nt),
        in_specs=[tile(D_MODEL), mem, mem, _full(wq.shape), _full(wo.shape), _full(g.shape), _full(b.shape),
                  _full(w_route_hi.shape), _full(w_route_mid.shape), _full(b_route.shape), _full(tri.shape)],
        out_specs=[tile(XG_W), tile(1), tile(1), _full((1, LANES))],
        out_shape=[jax.ShapeDtypeStruct((n, XG_W), F32), jax.ShapeDtypeStruct((n, 1), jnp.int32),
                   jax.ShapeDtypeStruct((n, 1), jnp.int32), jax.ShapeDtypeStruct((1, LANES), F32)],
        scratch_shapes=[pltpu.VMEM((1, LANES), F32)],
        compiler_params=_params(("arbitrary", "arbitrary")),
        name="xattn_ln2_route",
    )(x1, xk, xv, wq, wo, g, b, w_route_hi, w_route_mid, b_route, tri)


def _dispatch_kernel(pos_ref, x_ref, xs_hbm, sem, *, tm):
    def issue(t, carry):
        pltpu.make_async_copy(x_ref.at[pl.ds(t, 1)], xs_hbm.at[pl.ds(pos_ref[t], 1)], sem).start()
        return carry

    lax.fori_loop(0, tm, issue, 0, unroll=DMA_UNROLL)
    pltpu.make_async_copy(x_ref, xs_hbm.at[pl.ds(0, tm)], sem).wait()


def _dispatch(xg, pos, tm):
    n = xg.shape[0]
    return pl.pallas_call(
        functools.partial(_dispatch_kernel, tm=tm),
        grid=(n // tm,),
        in_specs=[pl.BlockSpec((tm,), lambda i: (i,), memory_space=pltpu.SMEM),
                  pl.BlockSpec((tm, XG_W), lambda i: (i, 0))],
        out_specs=pl.BlockSpec(memory_space=pl.ANY),
        out_shape=jax.ShapeDtypeStruct(xg.shape, xg.dtype),
        scratch_shapes=[pltpu.SemaphoreType.DMA(())],
        compiler_params=_params(("arbitrary",)),
        name="moe_dispatch",
    )(pos, xg)


def _moe_ffn_kernel(blk_ref, grp_ref, first_ref, valid_ref, xs_ref, wg_ref, wu_ref, wd_ref, o_ref, xb_ref):
    w = pl.program_id(0)
    e = pl.program_id(1)

    @pl.when((first_ref[w] == 1) & (e == 0))
    def _():
        xb_ref[...] = xs_ref[:, 0:D_MODEL].astype(BF16)
        o_ref[...] = jnp.zeros_like(o_ref)

    @pl.when(valid_ref[w] == 1)
    def _():
        xb = xb_ref[...]
        gates = xs_ref[:, D_MODEL:]
        lane = lax.broadcasted_iota(jnp.int32, gates.shape, 1)
        w_tok = jnp.sum(jnp.where(lane == grp_ref[w] * EXPERTS_PER_GROUP + e, gates, 0.0), -1, keepdims=True)
        hg = _dot(xb, wg_ref[0].astype(BF16))
        hu = _dot(xb, wu_ref[0].astype(BF16))
        hid = hg * jax.nn.sigmoid(hg) * hu * w_tok
        o_ref[...] += _dot(hid.astype(BF16), wd_ref[0].astype(BF16))


def _moe_ffn(xs, items, w_gate, w_up, w_down, rb, e_base):
    n = xs.shape[0]
    blk, grp, first, valid = items

    def expert(w, e, blk, grp, first, valid):
        return (e_base + grp[w] * EXPERTS_PER_GROUP + jnp.where(valid[w] == 1, e, EXPERTS_PER_GROUP - 1), 0, 0)

    grid_spec = pltpu.PrefetchScalarGridSpec(
        num_scalar_prefetch=4,
        grid=(blk.shape[0], EXPERTS_PER_GROUP),
        in_specs=[pl.BlockSpec((rb, XG_W), lambda w, e, blk, grp, first, valid: (blk[w], 0)),
                  pl.BlockSpec((1, D_MODEL, D_EXPERT), expert),
                  pl.BlockSpec((1, D_MODEL, D_EXPERT), expert),
                  pl.BlockSpec((1, D_EXPERT, D_MODEL), expert)],
        out_specs=pl.BlockSpec((rb, D_MODEL), lambda w, e, blk, grp, first, valid: (blk[w], 0)),
        scratch_shapes=[pltpu.VMEM((rb, D_MODEL), BF16)],
    )
    return pl.pallas_call(
        _moe_ffn_kernel,
        grid_spec=grid_spec,
        out_shape=jax.ShapeDtypeStruct((n, D_MODEL), F32),
        compiler_params=_params(("arbitrary", "arbitrary")),
        name="moe_experts",
    )(blk, grp, first, valid, xs, w_gate, w_up, w_down)


def _combine_kernel(pos_ref, pos_next_ref, xg_ref, ys_hbm, g_ref, b_ref, o_ref, buf_ref, sem, *, tm):
    i = pl.program_id(0)
    slot = i % 2

    def gather(p_ref, s):
        def issue(t, carry):
            pltpu.make_async_copy(ys_hbm.at[pl.ds(p_ref[t], 1)], buf_ref.at[s, pl.ds(t, 1)], sem.at[s]).start()
            return carry
        lax.fori_loop(0, tm, issue, 0, unroll=DMA_UNROLL)

    @pl.when(i == 0)
    def _():
        gather(pos_ref, slot)

    @pl.when(i + 1 < pl.num_programs(0))
    def _():
        gather(pos_next_ref, 1 - slot)

    pltpu.make_async_copy(ys_hbm.at[pl.ds(0, tm)], buf_ref.at[slot], sem.at[slot]).wait()
    o_ref[...] = _layer_norm(ALPHA * xg_ref[:, 0:D_MODEL] + buf_ref[slot], g_ref[...], b_ref[...])


def _combine(pos, xg, ys, g, b, tm):
    n = xg.shape[0]
    last = n // tm - 1
    return pl.pallas_call(
        functools.partial(_combine_kernel, tm=tm),
        grid=(n // tm,),
        in_specs=[pl.BlockSpec((tm,), lambda i: (i,), memory_space=pltpu.SMEM),
                  pl.BlockSpec((tm,), lambda i: (jnp.minimum(i + 1, last),), memory_space=pltpu.SMEM),
                  pl.BlockSpec((tm, XG_W), lambda i: (i, 0)),
                  pl.BlockSpec(memory_space=pl.ANY), _full(g.shape), _full(b.shape)],
        out_specs=pl.BlockSpec((tm, D_MODEL), lambda i: (i, 0)),
        out_shape=jax.ShapeDtypeStruct((n, D_MODEL), F32),
        scratch_shapes=[pltpu.VMEM((2, tm, D_MODEL), F32), pltpu.SemaphoreType.DMA((2,))],
        compiler_params=_params(("arbitrary",)),
        name="moe_combine_ln3",
    )(pos, pos, xg, ys, g, b)


def _moe_work_items(counts, n, rb):
    nb = n // rb
    n_items = nb + N_GROUPS - 1
    ends = jnp.cumsum(counts)
    start = jnp.arange(nb, dtype=jnp.int32) * rb
    g_lo = jnp.sum(ends[None, :] <= start[:, None], axis=1).astype(jnp.int32)
    g_hi = jnp.sum(ends[None, :] <= (start + rb - 1)[:, None], axis=1).astype(jnp.int32)
    per_blk = g_hi - g_lo + 1
    item0 = jnp.cumsum(per_blk) - per_blk
    w = jnp.arange(n_items, dtype=jnp.int32)
    valid = w < jnp.sum(per_blk)
    blk = jnp.clip(jnp.sum(item0[None, :] <= w[:, None], axis=1) - 1, 0, nb - 1).astype(jnp.int32)
    grp = jnp.where(valid, g_lo[blk] + (w - item0[blk]), g_hi[nb - 1]).astype(jnp.int32)
    first = (valid & (w == item0[blk])).astype(jnp.int32)
    return blk, grp, first, valid.astype(jnp.int32)


def _moe(xg, rank, grp, counts, w_gate, w_up, w_down, e_base, g, b, t_dma, rb):
    n = xg.shape[0]
    counts = counts[0, :N_GROUPS].astype(jnp.int32)
    offsets = jnp.cumsum(counts) - counts
    pos = (offsets[grp[:, 0]] + rank[:, 0]).astype(jnp.int32)
    xs = _dispatch(xg, pos, t_dma)
    ys = _moe_ffn(xs, _moe_work_items(counts, n, rb), w_gate, w_up, w_down, rb, e_base)
    return _combine(pos, xg, ys, g, b, t_dma)


def _pad_cols(w, width):
    return jnp.pad(w, ((0, 0), (0, width - w.shape[1])))


def _layer_weights(w_in, w_out, gla_w_a2, ml_b_i, ml_b_f, mla_w_uq, mla_w_ukv, moe_w_group, moe_b_group,
                   moe_w_router, moe_b_router):
    o = _IN_OFF
    half = MLA_ROPE // 2
    wg = _pad_cols(w_in[:, o[0]:o[5]], SEG_W).astype(BF16)
    wm = _pad_cols(w_in[:, o[5]:o[10]], SEG_W).astype(BF16)
    kr = w_in[:, o[12]:o[13]]
    zeros = lambda w: jnp.zeros((D_MODEL, w), F32)
    kra = jnp.concatenate([zeros(MLA_NOPE), kr, zeros(LANES - MLA_NOPE - MLA_ROPE)], 1)
    krb = jnp.concatenate([zeros(MLA_NOPE), kr[:, half:], kr[:, :half], zeros(LANES - MLA_NOPE - MLA_ROPE)], 1)
    wc = jnp.concatenate([w_in[:, o[10]:o[12]], kra, krb], 1).astype(BF16)
    wift = w_in[:, o[8]:o[10]].T.astype(BF16)
    wa2 = jnp.pad(gla_w_a2, ((0, LANES - GLA_GATE_RANK), (0, 0))).astype(BF16)
    b_gate = jnp.concatenate([ml_b_i, ml_b_f])
    b_col = jnp.pad(b_gate, (0, LANES - 2 * MIX_HEADS)).reshape(1, LANES)
    b_row = b_gate.reshape(2 * MIX_HEADS, 1)
    uq = mla_w_uq.reshape(MLA_Q_RANK, MLA_HEADS, MLA_NOPE + MLA_ROPE)
    zq = jnp.zeros((MLA_Q_RANK, MLA_HEADS, LANES - MLA_NOPE - MLA_ROPE), F32)
    wqa = jnp.concatenate([uq, zq], -1).reshape(MLA_Q_RANK, -1).astype(BF16)
    wqb = jnp.concatenate([jnp.zeros((MLA_Q_RANK, MLA_HEADS, MLA_NOPE), F32), uq[..., MLA_NOPE + half:],
                           uq[..., MLA_NOPE:MLA_NOPE + half], zq], -1).reshape(MLA_Q_RANK, -1).astype(BF16)
    ukv = mla_w_ukv.reshape(MLA_KV_RANK, MLA_HEADS, MLA_NOPE + MLA_V)
    wkn = jnp.concatenate([ukv[..., :MLA_NOPE], jnp.zeros((MLA_KV_RANK, MLA_HEADS, LANES - MLA_NOPE), F32)],
                          -1).reshape(MLA_KV_RANK, -1).astype(BF16)
    uv = ukv[..., MLA_NOPE:]
    zv = jnp.zeros_like(uv)
    odd = (jnp.arange(MLA_HEADS) % 2 == 1)[None, :, None]
    wv = jnp.concatenate([jnp.where(odd, zv, uv), jnp.where(odd, uv, zv)], -1).reshape(MLA_KV_RANK, -1).astype(BF16)
    wo = w_out.astype(BF16)
    w_route = _pad_cols(jnp.concatenate([moe_w_router, moe_w_group], 1), LANES)
    w_route_hi = w_route.astype(BF16)
    w_route_mid = (w_route - w_route_hi.astype(F32)).astype(BF16)
    b_route = jnp.pad(jnp.concatenate([moe_b_router, moe_b_group]), (0, LANES - N_EXPERTS - N_GROUPS)).reshape(1, LANES)
    return dict(wg=wg, wm=wm, wc=wc, wift=wift, wa2=wa2, b_col=b_col, b_row=b_row, wqa=wqa, wqb=wqb, wkn=wkn,
                wv=wv, wo_g=wo[0:MIX_W], wo_m=wo[MIX_W:2 * MIX_W], wo_c=wo[2 * MIX_W:], w_route_hi=w_route_hi, w_route_mid=w_route_mid,
                b_route=b_route)


def _tile(total, want):
    t = min(total, want)
    assert total % t == 0
    return t


def kernel(x, mem, positions, w_in, w_out, gla_w_a2, gla_b_a, gla_norm_g, ml_conv_w, ml_b_i, ml_b_f, ml_norm_g, mla_q_norm_g, mla_w_uq, mla_kv_norm_g, mla_w_ukv, xa_w_q, xa_w_kv, xa_w_o, moe_w_group, moe_b_group, moe_w_router, moe_b_router, moe_w_gate, moe_w_up, moe_w_down, ln1_g, ln1_b, ln2_g, ln2_b, ln3_g, ln3_b):
    batch, seq, _ = x.shape
    mem_len = mem.shape[1]
    n = batch * seq
    depth = w_in.shape[0]
    assert seq % CHUNK == 0
    t_in = _tile(n, 512)
    tm = _tile(n, 1024)
    ts = _tile(seq, 512)
    tq = _tile(seq, 512)
    t_moe = _tile(n, 1024)
    row = lambda a: a.reshape(1, -1)

    cos_t, sin_t = _rope_tables(positions.reshape(n, 1), _tile(n, 2048))
    mem2d = mem.reshape(batch * mem_len, D_MODEL)
    experts_gate = moe_w_gate.reshape(-1, D_MODEL, D_EXPERT)
    experts_up = moe_w_up.reshape(-1, D_MODEL, D_EXPERT)
    experts_down = moe_w_down.reshape(-1, D_EXPERT, D_MODEL)
    h = x.reshape(n, D_MODEL)
    for l in range(depth):
        w = _layer_weights(w_in[l], w_out[l], gla_w_a2[l], ml_b_i[l], ml_b_f[l], mla_w_uq[l], mla_w_ukv[l],
                           moe_w_group[l], moe_b_group[l], moe_w_router[l], moe_b_router[l])
        yg, ym, yift, q, k, v = _in_proj(h, w["wg"], w["wm"], w["wc"], w["wift"], cos_t, sin_t,
                                         row(mla_q_norm_g[l]), row(mla_kv_norm_g[l]),
                                         w["wqa"], w["wqb"], w["wkn"], w["wv"], t_in)
        gates_row = yift.reshape(2 * MIX_HEADS, n // CHUNK, CHUNK).transpose(1, 0, 2)
        og = _gla(yg, w["wa2"], row(gla_b_a[l]), row(gla_norm_g[l]), batch, seq, ts)
        om = _mlstm(ym, gates_row, ml_conv_w[l], w["b_col"], w["b_row"], row(ml_norm_g[l]), batch, seq, ts)
        oc = _mla_attn(q, k, v, batch, seq, tq)
        x1 = _out_proj(og, om, oc, h, w["wo_g"], w["wo_m"], w["wo_c"], row(ln1_g[l]), row(ln1_b[l]), tm)
        xk, xv = _xa_kv(mem2d, xa_w_kv[l].astype(BF16), mem_len)
        xg, rank, grp, counts = _xattn_route(
            x1, xk, xv, xa_w_q[l].astype(BF16), xa_w_o[l].astype(BF16), row(ln2_g[l]), row(ln2_b[l]),
            w["w_route_hi"], w["w_route_mid"], w["b_route"], batch, seq, mem_len, tm)
        h = _moe(xg, rank, grp, counts, experts_gate, experts_up, experts_down, l * N_EXPERTS,
                 row(ln3_g[l]), row(ln3_b[l]), t_moe, t_moe)
    return h.reshape(batch, seq, D_MODEL)
```

```python
import functools

import numpy as np
import jax
import jax.numpy as jnp
from jax import lax
from jax.experimental import pallas as pl
from jax.experimental.pallas import tpu as pltpu

F32 = jnp.float32
BF16 = jnp.bfloat16

D_MODEL = 1024
CHUNK = 64
HEAD_D = 64
MIX_HEADS = 4
MIX_W = MIX_HEADS * HEAD_D
GLA_GATE_RANK = 16
GLA_TAU = 16.0
MLSTM_CONV = 4
MLA_HEADS = 8
MLA_NOPE = 64
MLA_ROPE = 32
MLA_V = 64
MLA_Q_RANK = 256
MLA_KV_RANK = 128
MLA_QK_PAD = 128
ROPE_BASE = 10000.0
LOG2_E = 1.4426950408889634
MLA_HEAD_PAIR = 2
XA_HEADS = 4
XA_DH = D_MODEL // XA_HEADS
N_GROUPS = 4
EXPERTS_PER_GROUP = 8
N_EXPERTS = N_GROUPS * EXPERTS_PER_GROUP
D_EXPERT = 256
DEPTH = 2
ALPHA = (2 * DEPTH) ** 0.25
LN_EPS = 1e-5
LANES = 128
SUBLANES = 8
MIXER_UNROLL = 4
SEG_W = 1152
MLA_SEG_W = 640
XG_W = D_MODEL + LANES
EXPERTS_PER_STEP = 4
DMA_UNROLL = 32
GLA_LEVELS = (32, 16, 8, 4, 2, 1)
VMEM_LIMIT = 56 * 1024 * 1024

_IN_SIZES = (256, 256, 256, 256, GLA_GATE_RANK, 512, 256, 256, 4, 4, MLA_Q_RANK, MLA_KV_RANK, MLA_ROPE)
_IN_OFF = np.concatenate([[0], np.cumsum(_IN_SIZES)]).tolist()


def _params(sem):
    return pltpu.CompilerParams(dimension_semantics=sem, vmem_limit_bytes=VMEM_LIMIT)


def _full(shape):
    return pl.BlockSpec(shape, lambda *_: (0,) * len(shape))


def _layer_norm(x, g, b):
    mu = jnp.mean(x, -1, keepdims=True)
    xc = x - mu
    var = jnp.mean(xc * xc, -1, keepdims=True)
    return xc * lax.rsqrt(var + LN_EPS) * g + b


def _log_sigmoid(z):
    return jnp.minimum(z, 0.0) - jnp.log1p(jnp.exp(-jnp.abs(z)))


def _dot_nt(a, b):
    return lax.dot_general(a, b, (((1,), (1,)), ((), ())), preferred_element_type=F32)


def _dot_tn(a, b):
    return lax.dot_general(a, b, (((0,), (0,)), ((), ())), preferred_element_type=F32)


def _dot(a, b):
    return jnp.dot(a, b, preferred_element_type=F32)


def _head_norm(o, g):
    mu = jnp.mean(o, -1, keepdims=True)
    oc = o - mu
    var = jnp.mean(oc * oc, -1, keepdims=True)
    return oc * lax.rsqrt(var + LN_EPS) * g


def _in_proj_kernel(x_ref, wg_ref, wm_ref, wc_ref, wift_ref, cos_ref, sin_ref, gq_ref, gkv_ref,
                    wqa_ref, wqb_ref, wkn_ref, wv_ref, ones_ref,
                    yg_ref, ym_ref, yift_ref, q_ref, k_ref, v_ref, yc_ref):
    xb = x_ref[...].astype(BF16)
    yg_ref[...] = _dot(xb, wg_ref[...])
    ym_ref[...] = _dot(xb, wm_ref[...])
    yift_ref[...] = _dot_nt(wift_ref[...], xb)
    yc_ref[...] = _dot(xb, wc_ref[...])
    _mla_prep_tile(yc_ref, cos_ref, sin_ref, gq_ref, gkv_ref, wqa_ref, wqb_ref, wkn_ref, wv_ref, ones_ref,
                   q_ref, k_ref, v_ref)


def _in_proj(x2d, wg, wm, wc, wift, cos_t, sin_t, gq, gkv, wqa, wqb, wkn, wv, tm):
    n = x2d.shape[0]
    ones_row = np.zeros((1, MLA_HEADS * LANES), np.float32)
    for h in range(MLA_HEADS):
        ones_row[0, h * LANES + (MLA_V if h % 2 == 0 else 0)] = 1.0
    ones_row = jnp.asarray(ones_row)
    row = lambda w: pl.BlockSpec((tm, w), lambda i: (i, 0))
    qk_w = MLA_HEADS * MLA_QK_PAD
    return pl.pallas_call(
        _in_proj_kernel,
        grid=(n // tm,),
        in_specs=[row(D_MODEL), _full(wg.shape), _full(wm.shape), _full(wc.shape), _full(wift.shape),
                  row(LANES), row(LANES), _full(gq.shape), _full(gkv.shape),
                  _full(wqa.shape), _full(wqb.shape), _full(wkn.shape), _full(wv.shape), _full(ones_row.shape)],
        out_specs=[row(SEG_W), row(SEG_W), pl.BlockSpec((8, tm), lambda i: (0, i)),
                   row(qk_w), row(qk_w), row(MLA_HEADS * LANES)],
        out_shape=[jax.ShapeDtypeStruct((n, SEG_W), F32), jax.ShapeDtypeStruct((n, SEG_W), F32),
                   jax.ShapeDtypeStruct((8, n), F32),
                   jax.ShapeDtypeStruct((n, qk_w), BF16), jax.ShapeDtypeStruct((n, qk_w), BF16),
                   jax.ShapeDtypeStruct((n, MLA_HEADS * LANES), BF16)],
        scratch_shapes=[pltpu.VMEM((tm, MLA_SEG_W), F32)],
        compiler_params=_params(("parallel",)),
        name="in_proj_mla_prep",
    )(x2d, wg, wm, wc, wift, cos_t, sin_t, gq, gkv, wqa, wqb, wkn, wv, ones_row)


def _split3(x):
    hi = x.astype(BF16)
    r1 = x - hi.astype(F32)
    mid = r1.astype(BF16)
    lo = (r1 - mid.astype(F32)).astype(BF16)
    return hi, mid, lo


def _cumsum_rows(tri, x):
    hi, mid, lo = _split3(x)
    return _dot(tri, hi) + _dot(tri, mid) + _dot(tri, lo)


def _gla_constants():
    t = np.arange(CHUNK)
    n_lv = len(GLA_LEVELS)
    masks = np.zeros((n_lv + 1, CHUNK, CHUNK), np.float32)
    right = np.zeros((n_lv, CHUNK, 1), np.float32)
    for li, n in enumerate(GLA_LEVELS):
        blk = t // (2 * n)
        is_right = (t % (2 * n)) >= n
        masks[li] = ((blk[:, None] == blk[None, :]) & is_right[:, None] & ~is_right[None, :])
        right[li, :, 0] = is_right
    masks[n_lv] = np.eye(CHUNK)
    return masks, right


def _gla_level_exponents(b, log_a, right_ref):
    row = lax.broadcasted_iota(jnp.int32, (CHUNK, 1), 0)
    exps = []
    for li, n in enumerate(GLA_LEVELS):
        if n >= SUBLANES // 2:
            per_blk = max(2 * n // SUBLANES, 1)
            b4 = b.reshape(CHUNK // (per_blk * SUBLANES), per_blk, SUBLANES, MIX_W)
            r_reg, r_sub = divmod(n - 1, SUBLANES)
            src = b4[:, r_reg:r_reg + 1, r_sub:r_sub + 1, :]
            b_r = jnp.broadcast_to(src, b4.shape).reshape(CHUNK, MIX_W)
            exps.append(jnp.where(right_ref[li] > 0.0, b - b_r, b_r - b))
        elif n == 2:
            pos = row % 4
            nxt = pltpu.roll(log_a, CHUNK - 1, 0)
            prv = pltpu.roll(log_a, 1, 0)
            exps.append(jnp.where(pos == 0, nxt, jnp.where(pos == 1, 0.0,
                                                            jnp.where(pos == 2, log_a, log_a + prv))))
        else:
            exps.append(jnp.where(right_ref[li] > 0.0, log_a, 0.0))
    return exps


def _pair_blocks(z, low_half):
    zero = jnp.zeros_like(z)
    return jnp.concatenate([jnp.where(low_half, z, zero), jnp.where(low_half, zero, z)], axis=0)


def _gla_kernel(y_ref, wa2_ref, ba_ref, g_ref, tri_ref, mask_ref, right_ref, o_ref, st_ref, *, ts, unroll):
    n_lv = len(GLA_LEVELS)
    pairs = MIX_HEADS // 2
    low_half = lax.broadcasted_iota(jnp.int32, (CHUNK, LANES), 1) < HEAD_D
    same_head = (lax.broadcasted_iota(jnp.int32, (LANES, LANES), 0) < HEAD_D) == (
        lax.broadcasted_iota(jnp.int32, (LANES, LANES), 1) < HEAD_D)

    @pl.when(pl.program_id(1) == 0)
    def _():
        st_ref[...] = jnp.zeros_like(st_ref)

    def chunk(c, carry):
        rows = pl.ds(pl.multiple_of(c * CHUNK, CHUNK), CHUNK)
        q = y_ref[rows, 0:256] * (HEAD_D ** -0.5)
        k = y_ref[rows, 256:512]
        a_lr = y_ref[rows, 1024:1152]
        z = _dot(a_lr.astype(BF16), wa2_ref[...]) + ba_ref[...]
        log_a = _log_sigmoid(z) * (1.0 / GLA_TAU)
        b = _cumsum_rows(tri_ref[...], log_a)
        b_end = b[CHUNK - 1:CHUNK, :]
        scores = [jnp.zeros((CHUNK, LANES), F32) for _ in range(pairs)]
        for li, e in enumerate(_gla_level_exponents(b, log_a, right_ref)):
            x = (jnp.where(right_ref[li] > 0.0, q, k) * jnp.exp(e)).astype(BF16)
            for p in range(pairs):
                x2 = x[:, p * LANES:(p + 1) * LANES]
                scores[p] = scores[p] + _dot_nt(x2, _pair_blocks(x2, low_half)) * mask_ref[li]
        qb = q.astype(BF16)
        kb = k.astype(BF16)
        q_in = (q * jnp.exp(b)).astype(BF16)
        k_out = (k * jnp.exp(b_end - b)).astype(BF16)
        dec_end = jnp.exp(b_end)
        for p in range(pairs):
            pl_ = slice(p * LANES, (p + 1) * LANES)
            a = scores[p] + _dot_nt(qb[:, pl_], _pair_blocks(kb[:, pl_], low_half)) * mask_ref[n_lv]
            v2 = y_ref[rows, 512 + p * LANES:512 + (p + 1) * LANES].astype(BF16)
            st = st_ref[p]
            o = _dot(a.astype(BF16), _pair_blocks(v2, low_half)) + _dot_nt(q_in[:, pl_], st.astype(BF16))
            st_ref[p] = st * dec_end[:, pl_] + jnp.where(same_head, _dot_tn(v2, k_out[:, pl_]), 0.0)
            normed = jnp.zeros((CHUNK, LANES), F32)
            for mine in (low_half, jnp.logical_not(low_half)):
                mu = jnp.sum(jnp.where(mine, o, 0.0), -1, keepdims=True) * (1.0 / HEAD_D)
                cen = jnp.where(mine, o - mu, 0.0)
                var = jnp.sum(cen * cen, -1, keepdims=True) * (1.0 / HEAD_D)
                normed = normed + cen * lax.rsqrt(var + LN_EPS)
            r_gate = y_ref[rows, 768 + p * LANES:768 + (p + 1) * LANES]
            o_ref[rows, pl_] = (normed * g_ref[:, pl_] * (r_gate * jax.nn.sigmoid(r_gate))).astype(o_ref.dtype)
        return carry

    lax.fori_loop(0, ts // CHUNK, chunk, 0, unroll=unroll)


def _gla(yg, wa2, ba, g, batch, seq, ts):
    n = yg.shape[0]
    nt = seq // ts
    masks, right = _gla_constants()
    masks = np.concatenate([masks, masks], axis=-1)
    tri = jnp.asarray(np.tril(np.ones((CHUNK, CHUNK), np.float32)), BF16)
    return pl.pallas_call(
        functools.partial(_gla_kernel, ts=ts, unroll=MIXER_UNROLL),
        grid=(batch, nt),
        in_specs=[pl.BlockSpec((ts, SEG_W), lambda b, i: (b * nt + i, 0)),
                  _full(wa2.shape), _full(ba.shape), _full(g.shape),
                  _full(tri.shape), _full(masks.shape), _full(right.shape)],
        out_specs=pl.BlockSpec((ts, MIX_W), lambda b, i: (b * nt + i, 0)),
        out_shape=jax.ShapeDtypeStruct((n, MIX_W), BF16),
        scratch_shapes=[pltpu.VMEM((MIX_HEADS // 2, LANES, LANES), F32)],
        compiler_params=_params(("parallel", "arbitrary")),
        name="gla_mixer",
    )(yg, wa2, ba, g, tri, jnp.asarray(masks), jnp.asarray(right))


def _mlstm_kernel(y_ref, gr_ref, cw_ref, bcol_ref, brow_ref, g_ref, tri_ref, eb_ref, eye_ref, o_ref,
                  xe_ref, qk_ref, vx_ref, fcl_ref, wsl_ref, fcb_ref, wsb_ref, rv_ref, dec_ref,
                  w_ref, em_ref, upd_ref, c_ref, m_ref, *, ts):
    first = pl.program_id(1) == 0

    @pl.when(first)
    def _():
        xe_ref[0:8, :] = jnp.zeros((8, 2 * MIX_W), F32)
        c_ref[...] = jnp.zeros_like(c_ref)
        m_ref[...] = jnp.zeros_like(m_ref)

    @pl.when(jnp.logical_not(first))
    def _():
        xe_ref[0:8, :] = xe_ref[ts:ts + 8, :]

    xe_ref[8:ts + 8, :] = y_ref[:, 0:2 * MIX_W]
    for c in range(ts // CHUNK):
        r0 = 8 + c * CHUNK
        conv = cw_ref[MLSTM_CONV - 1:MLSTM_CONV, :] * xe_ref[r0:r0 + CHUNK, :]
        for j in range(MLSTM_CONV - 1):
            lo = r0 - (MLSTM_CONV - 1) + j
            conv = conv + cw_ref[j:j + 1, :] * xe_ref[lo:lo + CHUNK, :]
        qk_ref[c * CHUNK:(c + 1) * CHUNK, :] = conv * jax.nn.sigmoid(conv)

    ones_col = (lax.broadcasted_iota(jnp.int32, (ts, HEAD_D), 1) == 0).astype(BF16)
    for h in range(MIX_HEADS):
        v_at, one_at = (0, HEAD_D) if h % 2 == 0 else (HEAD_D, 0)
        vx_ref[:, h * LANES + v_at:h * LANES + v_at + HEAD_D] = (
            y_ref[:, 512 + h * HEAD_D:512 + (h + 1) * HEAD_D].astype(BF16))
        vx_ref[:, h * LANES + one_at:h * LANES + one_at + HEAD_D] = ones_col

    tri = tri_ref[...]
    expand = eb_ref[...]
    nck = ts // CHUNK
    lane = lax.broadcasted_iota(jnp.int32, (1, LANES), 1)
    f_lanes = (lane >= MIX_HEADS) & (lane < 2 * MIX_HEADS)
    head_lane = (lax.broadcasted_iota(jnp.int32, (MIX_HEADS, LANES), 1)
                 == lax.broadcasted_iota(jnp.int32, (MIX_HEADS, LANES), 0) + MIX_HEADS)

    def to_lanes(col):
        return jnp.sum(jnp.where(head_lane, col, 0.0), 0, keepdims=True)

    g_rows = gr_ref[...] + brow_ref[...]
    ls_rows = _log_sigmoid(g_rows).reshape(nck * SUBLANES, CHUNK)
    fcum_rows = sum(_dot_nt(part, tri) for part in _split3(ls_rows)).reshape(nck, SUBLANES, CHUNK)
    m_col = m_ref[0:MIX_HEADS, 0:1]
    for c in range(nck):
        g_col = y_ref[c * CHUNK:(c + 1) * CHUNK, 1024:1152] + bcol_ref[...]
        fcum_col = jnp.where(f_lanes, _cumsum_rows(tri, _log_sigmoid(g_col)), 0.0)
        fcl_ref[c * CHUNK:(c + 1) * CHUNK, :] = fcum_col
        g_row = g_rows[c]
        fcum_row = fcum_rows[c]
        f_row = fcum_row[MIX_HEADS:2 * MIX_HEADS, :]
        i_row = g_row[0:MIX_HEADS, :]
        f_end = f_row[:, CHUNK - 1:CHUNK]
        rv_ref[c, 0:MIX_HEADS, :] = jnp.concatenate(
            [i_row - f_row, jnp.broadcast_to(m_col, (MIX_HEADS, HEAD_D))], axis=1)
        m_new = jnp.maximum(f_end + m_col, jnp.max(f_end - f_row + i_row, -1, keepdims=True))
        m_new_l = to_lanes(m_new)
        f_end_l = fcum_col[CHUNK - 1:CHUNK, :]
        i_shift = pltpu.roll(g_col, MIX_HEADS, 1)
        wsl_ref[c * CHUNK:(c + 1) * CHUNK, :] = jnp.where(
            f_lanes, jnp.exp(f_end_l - fcum_col + i_shift - m_new_l), 0.0)
        dec_ref[c] = jnp.broadcast_to(jnp.exp(f_end_l + to_lanes(m_col) - m_new_l), (SUBLANES, LANES))
        m_col = m_new
    m_ref[0:MIX_HEADS, :] = jnp.broadcast_to(m_col, (MIX_HEADS, LANES))
    slab = 2 * CHUNK
    for i in range(ts // slab):
        sr = slice(i * slab, (i + 1) * slab)
        fcb_ref[sr, :] = sum(_dot(part, expand) for part in _split3(fcl_ref[sr, :]))
        ws_hi, ws_mid, _ = _split3(wsl_ref[sr, :])
        wsb_ref[sr, :] = _dot(ws_hi, expand) + _dot(ws_mid, expand)

    t_idx = lax.broadcasted_iota(jnp.int32, (CHUNK, LANES), 0)
    s_idx = lax.broadcasted_iota(jnp.int32, (CHUNK, LANES), 1)
    keep = (s_idx <= t_idx) | (s_idx >= CHUNK)
    low_half = s_idx < HEAD_D

    def weights(c, carry):
        rows = pl.ds(pl.multiple_of(c * CHUNK, CHUNK), CHUNK)
        rv = rv_ref[c, 0:MIX_HEADS, :]
        for p in range(MIX_HEADS // 2):
            q2 = (qk_ref[rows, p * LANES:(p + 1) * LANES] * (HEAD_D ** -0.5)).astype(BF16)
            k2 = qk_ref[rows, MIX_W + p * LANES:MIX_W + (p + 1) * LANES]
            for h in (2 * p, 2 * p + 1):
                mine = low_half if h % 2 == 0 else jnp.logical_not(low_half)
                hb = slice(h * LANES, (h + 1) * LANES)
                logw = jnp.where(keep, fcb_ref[rows, hb] + rv[h:h + 1, :], -jnp.inf)
                m_t = jnp.max(logw, -1, keepdims=True)
                k_h = jnp.where(mine, k2, 0.0)
                qk = _dot_nt(q2, jnp.concatenate([k_h.astype(BF16), eye_ref[h % 2]], axis=0))
                w_ref[c * MIX_HEADS + h] = (jnp.exp(logw - m_t) * qk).astype(BF16)
                em_ref[c * MIX_HEADS + h] = jnp.exp(-m_t)
                upd = _dot_tn((k_h * wsb_ref[rows, hb]).astype(BF16), vx_ref[rows, hb])
                off = (h % 2) * HEAD_D
                upd_ref[c * MIX_HEADS + h] = upd[off:off + HEAD_D, :]
        return carry

    lax.fori_loop(0, nck, weights, 0, unroll=2)

    def chunk(c, carry):
        rows = pl.ds(pl.multiple_of(c * CHUNK, CHUNK), CHUNK)
        dec_l = dec_ref[c]
        for p in range(MIX_HEADS // 2):
            pl_ = slice(p * LANES, (p + 1) * LANES)
            pair = jnp.zeros((CHUNK, LANES), F32)
            for h in (2 * p, 2 * p + 1):
                mine = low_half if h % 2 == 0 else jnp.logical_not(low_half)
                c_st = c_ref[h]
                num = _dot(w_ref[c * MIX_HEADS + h],
                           jnp.concatenate([vx_ref[rows, h * LANES:(h + 1) * LANES], c_st.astype(BF16)], axis=0))
                c_ref[h] = dec_l[0:1, MIX_HEADS + h:MIX_HEADS + h + 1] * c_st + upd_ref[c * MIX_HEADS + h]
                den = num[:, HEAD_D:HEAD_D + 1] if h % 2 == 0 else num[:, 0:1]
                r = 1.0 / jnp.maximum(jnp.abs(den), em_ref[c * MIX_HEADS + h])
                mu = jnp.sum(jnp.where(mine, num, 0.0), -1, keepdims=True) * (1.0 / HEAD_D)
                cen = jnp.where(mine, num - mu, 0.0)
                var = jnp.sum(cen * cen, -1, keepdims=True) * (1.0 / HEAD_D)
                pair = pair + cen * (r * lax.rsqrt(r * r * var + LN_EPS))
            o_gate = y_ref[rows, 768 + p * LANES:768 + (p + 1) * LANES]
            o_ref[rows, pl_] = (pair * g_ref[:, pl_] * jax.nn.sigmoid(o_gate)).astype(o_ref.dtype)
        return carry

    lax.fori_loop(0, nck, chunk, 0, unroll=2)


def _mlstm(ym, gates_row, conv_w, b_col, b_row, g, batch, seq, ts):
    n = ym.shape[0]
    nt = seq // ts
    nck = ts // CHUNK
    tri = jnp.asarray(np.tril(np.ones((CHUNK, CHUNK), np.float32)), BF16)
    expand = np.zeros((LANES, MIX_HEADS * LANES), np.float32)
    for h in range(MIX_HEADS):
        expand[MIX_HEADS + h, h * LANES:(h + 1) * LANES] = 1.0
    eye = np.zeros((2, HEAD_D, LANES), np.float32)
    eye[0, :, 0:HEAD_D] = np.eye(HEAD_D)
    eye[1, :, HEAD_D:] = np.eye(HEAD_D)
    expand = jnp.asarray(expand, BF16)
    eye = jnp.asarray(eye, BF16)
    return pl.pallas_call(
        functools.partial(_mlstm_kernel, ts=ts),
        grid=(batch, nt),
        in_specs=[pl.BlockSpec((ts, SEG_W), lambda b, i: (b * nt + i, 0)),
                  pl.BlockSpec((nck, 8, CHUNK), lambda b, i: (b * nt + i, 0, 0)),
                  _full(conv_w.shape), _full(b_col.shape), _full(b_row.shape), _full(g.shape),
                  _full(tri.shape), _full(expand.shape), _full(eye.shape)],
        out_specs=pl.BlockSpec((ts, MIX_W), lambda b, i: (b * nt + i, 0)),
        out_shape=jax.ShapeDtypeStruct((n, MIX_W), BF16),
        scratch_shapes=[pltpu.VMEM((ts + 8, 2 * MIX_W), F32),
                        pltpu.VMEM((ts, 2 * MIX_W), F32),
                        pltpu.VMEM((ts, MIX_HEADS * LANES), BF16),
                        pltpu.VMEM((ts, LANES), F32),
                        pltpu.VMEM((ts, LANES), F32),
                        pltpu.VMEM((ts, MIX_HEADS * LANES), F32),
                        pltpu.VMEM((ts, MIX_HEADS * LANES), F32),
                        pltpu.VMEM((nck, SUBLANES, LANES), F32),
                        pltpu.VMEM((nck, SUBLANES, LANES), F32),
                        pltpu.VMEM((nck * MIX_HEADS, CHUNK, LANES), BF16),
                        pltpu.VMEM((nck * MIX_HEADS, CHUNK, 1), F32),
                        pltpu.VMEM((nck * MIX_HEADS, HEAD_D, LANES), F32),
                        pltpu.VMEM((MIX_HEADS, HEAD_D, LANES), F32),
                        pltpu.VMEM((SUBLANES, LANES), F32)],
        compiler_params=_params(("parallel", "arbitrary")),
        name="mlstm_mixer",
    )(ym, gates_row, conv_w, b_col, b_row, g, tri, expand, eye)


def _rope_table_kernel(pos_ref, inv_ref, cos_ref, sin_ref):
    ang = pos_ref[...].astype(F32) * inv_ref[...]
    lane = lax.broadcasted_iota(jnp.int32, ang.shape, 1)
    rot = (lane >= MLA_NOPE) & (lane < MLA_NOPE + MLA_ROPE)
    first_half = lane < MLA_NOPE + MLA_ROPE // 2
    cos_ref[...] = jnp.where(lane < MLA_NOPE, 1.0, jnp.where(rot, jnp.cos(ang), 0.0))
    s = jnp.sin(ang)
    sin_ref[...] = jnp.where(rot, jnp.where(first_half, -s, s), 0.0)


def _rope_tables(pos_col, tm):
    n = pos_col.shape[0]
    half = MLA_ROPE // 2
    inv = ROPE_BASE ** (-np.arange(half, dtype=np.float32) / half)
    inv_row = np.zeros((1, LANES), np.float32)
    inv_row[0, MLA_NOPE:MLA_NOPE + half] = inv
    inv_row[0, MLA_NOPE + half:MLA_NOPE + MLA_ROPE] = inv
    return pl.pallas_call(
        _rope_table_kernel,
        grid=(n // tm,),
        in_specs=[pl.BlockSpec((tm, 1), lambda i: (i, 0)), _full((1, LANES))],
        out_specs=[pl.BlockSpec((tm, LANES), lambda i: (i, 0))] * 2,
        out_shape=[jax.ShapeDtypeStruct((n, LANES), F32)] * 2,
        compiler_params=_params(("parallel",)),
        name="rope_tables",
    )(pos_col, jnp.asarray(inv_row))


def _mla_prep_tile(y_ref, cos_ref, sin_ref, gq_ref, gkv_ref, wqa_ref, wqb_ref, wkn_ref, wv_ref, ones_ref,
                   q_ref, k_ref, v_ref):
    def rms(x, g):
        return x * lax.rsqrt(jnp.mean(x * x, -1, keepdims=True) + LN_EPS) * g

    cos = cos_ref[...]
    sin = sin_ref[...]
    cq = rms(y_ref[:, 0:MLA_Q_RANK], gq_ref[...]).astype(BF16)
    ckv = rms(y_ref[:, MLA_Q_RANK:MLA_Q_RANK + MLA_KV_RANK], gkv_ref[...]).astype(BF16)
    k_rope = y_ref[:, 384:512] * cos + y_ref[:, 512:640] * sin
    qa = _dot(cq, wqa_ref[...])
    qb = _dot(cq, wqb_ref[...])
    kn = _dot(ckv, wkn_ref[...])
    scale = (MLA_NOPE + MLA_ROPE) ** -0.5 * LOG2_E
    for h in range(MLA_HEADS):
        sl = slice(h * MLA_QK_PAD, (h + 1) * MLA_QK_PAD)
        q_ref[:, sl] = ((qa[:, sl] * cos + qb[:, sl] * sin) * scale).astype(q_ref.dtype)
        k_ref[:, sl] = (kn[:, sl] + k_rope).astype(k_ref.dtype)
    v_ref[...] = (_dot(ckv, wv_ref[...]) + ones_ref[...]).astype(v_ref.dtype)


def _mla_attn_kernel(q_ref, k_ref, v_ref, o_ref, *, seq, tq):
    t_chunk = lax.broadcasted_iota(jnp.int32, (tq, tq), 0) // CHUNK
    s_chunk = lax.broadcasted_iota(jnp.int32, (tq, tq), 1) // CHUNK
    diag_mask = s_chunk <= t_chunk
    low_half = lax.broadcasted_iota(jnp.int32, (tq, LANES), 1) < MLA_V
    for i in range(seq // tq):
        rows = slice(i * tq, (i + 1) * tq)
        res = []
        for hh in range(MLA_HEAD_PAIR):
            ql = slice(hh * MLA_QK_PAD, (hh + 1) * MLA_QK_PAD)
            vl = slice(hh * LANES, (hh + 1) * LANES)
            q = q_ref[0, rows, ql]
            s_d = jnp.where(diag_mask, _dot_nt(q, k_ref[0, rows, ql]), -jnp.inf)
            m = jnp.max(s_d, -1, keepdims=True)
            if i > 0:
                s_o = _dot_nt(q, k_ref[0, 0:i * tq, ql])
                m = jnp.maximum(m, jnp.max(s_o, -1, keepdims=True))
            o = _dot(jnp.exp2(s_d - m).astype(BF16), v_ref[0, rows, vl])
            if i > 0:
                o = o + _dot(jnp.exp2(s_o - m).astype(BF16), v_ref[0, 0:i * tq, vl])
            l = o[:, MLA_V:MLA_V + 1] if hh == 0 else o[:, 0:1]
            res.append(o / l)
        o_ref[0, rows, :] = jnp.where(low_half, res[0], res[1]).astype(o_ref.dtype)


def _mla_attn(q, k, v, batch, seq, tq):
    n = q.shape[0]
    qk_w = MLA_HEADS * MLA_QK_PAD
    v_w = MLA_HEADS * MLA_V
    pair = lambda w: pl.BlockSpec((1, seq, MLA_HEAD_PAIR * w), lambda b, h: (b, 0, h))
    out = pl.pallas_call(
        functools.partial(_mla_attn_kernel, seq=seq, tq=tq),
        grid=(batch, MLA_HEADS // MLA_HEAD_PAIR),
        in_specs=[pair(MLA_QK_PAD), pair(MLA_QK_PAD), pair(LANES)],
        out_specs=pair(MLA_V),
        out_shape=jax.ShapeDtypeStruct((batch, seq, v_w), BF16),
        compiler_params=_params(("parallel", "parallel")),
        name="mla_attention",
    )(q.reshape(batch, seq, qk_w), k.reshape(batch, seq, qk_w), v.reshape(batch, seq, MLA_HEADS * LANES))
    return out.reshape(n, v_w)


def _out_proj_kernel(og_ref, om_ref, oc_ref, x_ref, wg_ref, wm_ref, wc_ref, g_ref, b_ref, o_ref):
    mix = _dot(og_ref[...], wg_ref[...]) + _dot(om_ref[...], wm_ref[...]) + _dot(oc_ref[...], wc_ref[...])
    o_ref[...] = _layer_norm(ALPHA * x_ref[...] + mix, g_ref[...], b_ref[...])


def _out_proj(og, om, oc, x2d, wg, wm, wc, g, b, tm):
    n = x2d.shape[0]
    row = lambda w: pl.BlockSpec((tm, w), lambda i: (i, 0))
    return pl.pallas_call(
        _out_proj_kernel,
        grid=(n // tm,),
        in_specs=[row(MIX_W), row(MIX_W), row(MLA_HEADS * MLA_V), row(D_MODEL),
                  _full(wg.shape), _full(wm.shape), _full(wc.shape), _full(g.shape), _full(b.shape)],
        out_specs=row(D_MODEL),
        out_shape=jax.ShapeDtypeStruct((n, D_MODEL), F32),
        compiler_params=_params(("parallel",)),
        name="out_proj_ln1",
    )(og, om, oc, x2d, wg, wm, wc, g, b)


def _xa_kv_kernel(mem_ref, w_ref, k_ref, v_ref):
    kv = _dot(mem_ref[...].astype(BF16), w_ref[...])
    k_ref[...] = kv[:, 0:D_MODEL].astype(k_ref.dtype)
    v_ref[...] = kv[:, D_MODEL:2 * D_MODEL].astype(v_ref.dtype)


def _xa_kv(mem2d, w_kv, mem_len):
    n = mem2d.shape[0]
    row = pl.BlockSpec((mem_len, D_MODEL), lambda i: (i, 0))
    return pl.pallas_call(
        _xa_kv_kernel,
        grid=(n // mem_len,),
        in_specs=[row, _full(w_kv.shape)],
        out_specs=[row, row],
        out_shape=[jax.ShapeDtypeStruct((n, D_MODEL), BF16)] * 2,
        compiler_params=_params(("parallel",)),
        name="xattn_kv",
    )(mem2d, w_kv)


def _route_tile(x, wh_ref, wm_ref, b_ref, tri_ref, cnt_ref):
    xh = x.astype(BF16)
    xm = (x - xh.astype(F32)).astype(BF16)
    logits = _dot(xh, wh_ref[...]) + _dot(xh, wm_ref[...]) + _dot(xm, wh_ref[...]) + b_ref[...]
    lane = lax.broadcasted_iota(jnp.int32, logits.shape, 1).astype(F32)
    is_group = (lane >= N_EXPERTS) & (lane < N_EXPERTS + N_GROUPS)
    g_max = jnp.max(jnp.where(is_group, logits, -jnp.inf), -1, keepdims=True)
    g_sum = jnp.sum(jnp.where(is_group, jnp.exp(logits - g_max), 0.0), -1, keepdims=True)
    g_p = 1.0 / g_sum
    g_idx = jnp.min(jnp.where(is_group & (logits == g_max), lane - N_EXPERTS, float(LANES)), -1, keepdims=True)
    in_group = (lane < N_EXPERTS) & (jnp.floor(lane * (1.0 / EXPERTS_PER_GROUP)) == g_idx)
    e_max = jnp.max(jnp.where(in_group, logits, -jnp.inf), -1, keepdims=True)
    e_exp = jnp.where(in_group, jnp.exp(logits - e_max), 0.0)
    prob = e_exp / jnp.sum(e_exp, -1, keepdims=True)
    cand = jnp.where(in_group, prob, -1.0)
    p1 = jnp.max(cand, -1, keepdims=True)
    i1 = jnp.min(jnp.where(cand == p1, lane, float(LANES)), -1, keepdims=True)
    cand2 = jnp.where(lane == i1, -1.0, cand)
    p2 = jnp.max(cand2, -1, keepdims=True)
    i2 = jnp.min(jnp.where(cand2 == p2, lane, float(LANES)), -1, keepdims=True)
    p_sum = p1 + p2
    gates = jnp.where(lane == i1, g_p * (p1 / p_sum), 0.0) + jnp.where(lane == i2, g_p * (p2 / p_sum), 0.0)
    onehot = jnp.where(lane == g_idx, 1.0, 0.0)
    before = _dot(tri_ref[...], onehot.astype(BF16)) + cnt_ref[...]
    rank = jnp.sum(onehot * before, -1, keepdims=True).astype(jnp.int32)
    cnt_ref[...] += jnp.sum(onehot, 0, keepdims=True)
    return gates, rank, g_idx.astype(jnp.int32)


def _xattn_kernel(x_ref, k_ref, v_ref, wq_ref, wo_ref, g_ref, b_ref, wrh_ref, wrm_ref, br_ref, tri_ref,
                  xg_ref, rank_ref, grp_ref, cnt_out_ref, cnt_ref):
    @pl.when((pl.program_id(0) == 0) & (pl.program_id(1) == 0))
    def _():
        cnt_ref[...] = jnp.zeros_like(cnt_ref)

    x = x_ref[...]
    q = (_dot(x.astype(BF16), wq_ref[...]) * (XA_DH ** -0.5)).astype(BF16)
    out = jnp.zeros(x.shape, F32)
    for h in range(XA_HEADS):
        sl = slice(h * XA_DH, (h + 1) * XA_DH)
        s = _dot_nt(q[:, sl], k_ref[:, sl])
        p = jnp.exp(s - jnp.max(s, -1, keepdims=True))
        p = p / jnp.sum(p, -1, keepdims=True)
        o = _dot(p.astype(BF16), v_ref[:, sl])
        out = out + _dot(o.astype(BF16), wo_ref[sl, :])
    x2 = _layer_norm(ALPHA * x + out, g_ref[...], b_ref[...])
    gates, rank, grp = _route_tile(x2, wrh_ref, wrm_ref, br_ref, tri_ref, cnt_ref)
    xg_ref[:, 0:D_MODEL] = x2
    xg_ref[:, D_MODEL:] = gates
    rank_ref[...] = rank
    grp_ref[...] = grp
    cnt_out_ref[...] = cnt_ref[...]


def _xattn_route(x1, xk, xv, wq, wo, g, b, w_route_hi, w_route_mid, b_route, batch, seq, mem_len, tm):
    n = x1.shape[0]
    nt = seq // tm
    tri = jnp.asarray(np.tril(np.ones((tm, tm), np.float32), -1), BF16)
    tile = lambda w: pl.BlockSpec((tm, w), lambda bi, i: (bi * nt + i, 0))
    mem = pl.BlockSpec((mem_len, D_MODEL), lambda bi, i: (bi, 0))
    return pl.pallas_call(
        _xattn_kernel,
        grid=(batch, nt),
        in_specs=[tile(D_MODEL), mem, mem, _full(wq.shape), _full(wo.shape), _full(g.shape), _full(b.shape),
                  _full(w_route_hi.shape), _full(w_route_mid.shape), _full(b_route.shape), _full(tri.shape)],
        out_specs=[tile(XG_W), tile(1), tile(1), _full((1, LANES))],
        out_shape=[jax.ShapeDtypeStruct((n, XG_W), F32), jax.ShapeDtypeStruct((n, 1), jnp.int32),
                   jax.ShapeDtypeStruct((n, 1), jnp.int32), jax.ShapeDtypeStruct((1, LANES), F32)],
        scratch_shapes=[pltpu.VMEM((1, LANES), F32)],
        compiler_params=_params(("arbitrary", "arbitrary")),
        name="xattn_ln2_route",
    )(x1, xk, xv, wq, wo, g, b, w_route_hi, w_route_mid, b_route, tri)


def _dispatch_kernel(pos_ref, x_ref, xs_hbm, sem, *, tm):
    def issue(t, carry):
        pltpu.make_async_copy(x_ref.at[pl.ds(t, 1)], xs_hbm.at[pl.ds(pos_ref[t], 1)], sem).start()
        return carry

    lax.fori_loop(0, tm, issue, 0, unroll=DMA_UNROLL)
    pltpu.make_async_copy(x_ref, xs_hbm.at[pl.ds(0, tm)], sem).wait()


def _dispatch(xg, pos, tm):
    n = xg.shape[0]
    return pl.pallas_call(
        functools.partial(_dispatch_kernel, tm=tm),
        grid=(n // tm,),
        in_specs=[pl.BlockSpec((tm,), lambda i: (i,), memory_space=pltpu.SMEM),
                  pl.BlockSpec((tm, XG_W), lambda i: (i, 0))],
        out_specs=pl.BlockSpec(memory_space=pl.ANY),
        out_shape=jax.ShapeDtypeStruct(xg.shape, xg.dtype),
        scratch_shapes=[pltpu.SemaphoreType.DMA(())],
        compiler_params=_params(("arbitrary",)),
        name="moe_dispatch",
    )(pos, xg)


def _moe_ffn_kernel(blk_ref, grp_ref, first_ref, valid_ref, xs_ref, wg_ref, wu_ref, wd_ref, o_ref, xb_ref):
    w = pl.program_id(0)
    e = pl.program_id(1)

    @pl.when((first_ref[w] == 1) & (e == 0))
    def _():
        xb_ref[...] = xs_ref[:, 0:D_MODEL].astype(BF16)
        o_ref[...] = jnp.zeros_like(o_ref)

    @pl.when(valid_ref[w] == 1)
    def _():
        xb = xb_ref[...]
        gates = xs_ref[:, D_MODEL:]
        lane = lax.broadcasted_iota(jnp.int32, gates.shape, 1)
        hidden = []
        for j in range(EXPERTS_PER_STEP):
            expert = grp_ref[w] * EXPERTS_PER_GROUP + e * EXPERTS_PER_STEP + j
            w_tok = jnp.sum(jnp.where(lane == expert, gates, 0.0), -1, keepdims=True)
            hg = _dot(xb, wg_ref[j].astype(BF16))
            hu = _dot(xb, wu_ref[j].astype(BF16))
            hidden.append((hg * jax.nn.sigmoid(hg) * hu * w_tok).astype(BF16))
        wd = wd_ref[...].astype(BF16).reshape(EXPERTS_PER_STEP * D_EXPERT, D_MODEL)
        o_ref[...] += _dot(jnp.concatenate(hidden, axis=1), wd)


def _moe_ffn(xs, items, w_gate, w_up, w_down, rb, e_base):
    n = xs.shape[0]
    blk, grp, first, valid = items

    steps = EXPERTS_PER_GROUP // EXPERTS_PER_STEP

    def expert(w, e, blk, grp, first, valid):
        return ((e_base + grp[w] * EXPERTS_PER_GROUP) // EXPERTS_PER_STEP
                + jnp.where(valid[w] == 1, e, steps - 1), 0, 0)

    grid_spec = pltpu.PrefetchScalarGridSpec(
        num_scalar_prefetch=4,
        grid=(blk.shape[0], steps),
        in_specs=[pl.BlockSpec((rb, XG_W), lambda w, e, blk, grp, first, valid: (blk[w], 0)),
                  pl.BlockSpec((EXPERTS_PER_STEP, D_MODEL, D_EXPERT), expert),
                  pl.BlockSpec((EXPERTS_PER_STEP, D_MODEL, D_EXPERT), expert),
                  pl.BlockSpec((EXPERTS_PER_STEP, D_EXPERT, D_MODEL), expert)],
        out_specs=pl.BlockSpec((rb, D_MODEL), lambda w, e, blk, grp, first, valid: (blk[w], 0)),
        scratch_shapes=[pltpu.VMEM((rb, D_MODEL), BF16)],
    )
    return pl.pallas_call(
        _moe_ffn_kernel,
        grid_spec=grid_spec,
        out_shape=jax.ShapeDtypeStruct((n, D_MODEL), F32),
        compiler_params=_params(("arbitrary", "arbitrary")),
        name="moe_experts",
    )(blk, grp, first, valid, xs, w_gate, w_up, w_down)


def _combine_kernel(pos_ref, pos_next_ref, xg_ref, ys_hbm, g_ref, b_ref, o_ref, buf_ref, sem, *, tm):
    i = pl.program_id(0)
    slot = i % 2

    def gather(p_ref, s):
        def issue(t, carry):
            pltpu.make_async_copy(ys_hbm.at[pl.ds(p_ref[t], 1)], buf_ref.at[s, pl.ds(t, 1)], sem.at[s]).start()
            return carry
        lax.fori_loop(0, tm, issue, 0, unroll=DMA_UNROLL)

    @pl.when(i == 0)
    def _():
        gather(pos_ref, slot)

    @pl.when(i + 1 < pl.num_programs(0))
    def _():
        gather(pos_next_ref, 1 - slot)

    pltpu.make_async_copy(ys_hbm.at[pl.ds(0, tm)], buf_ref.at[slot], sem.at[slot]).wait()
    o_ref[...] = _layer_norm(ALPHA * xg_ref[:, 0:D_MODEL] + buf_ref[slot], g_ref[...], b_ref[...])


def _combine(pos, xg, ys, g, b, tm):
    n = xg.shape[0]
    last = n // tm - 1
    return pl.pallas_call(
        functools.partial(_combine_kernel, tm=tm),
        grid=(n // tm,),
        in_specs=[pl.BlockSpec((tm,), lambda i: (i,), memory_space=pltpu.SMEM),
                  pl.BlockSpec((tm,), lambda i: (jnp.minimum(i + 1, last),), memory_space=pltpu.SMEM),
                  pl.BlockSpec((tm, XG_W), lambda i: (i, 0)),
                  pl.BlockSpec(memory_space=pl.ANY), _full(g.shape), _full(b.shape)],
        out_specs=pl.BlockSpec((tm, D_MODEL), lambda i: (i, 0)),
        out_shape=jax.ShapeDtypeStruct((n, D_MODEL), F32),
        scratch_shapes=[pltpu.VMEM((2, tm, D_MODEL), F32), pltpu.SemaphoreType.DMA((2,))],
        compiler_params=_params(("arbitrary",)),
        name="moe_combine_ln3",
    )(pos, pos, xg, ys, g, b)


def _moe_work_items(counts, n, rb):
    nb = n // rb
    n_items = nb + N_GROUPS - 1
    ends = jnp.cumsum(counts)
    start = jnp.arange(nb, dtype=jnp.int32) * rb
    g_lo = jnp.sum(ends[None, :] <= start[:, None], axis=1).astype(jnp.int32)
    g_hi = jnp.sum(ends[None, :] <= (start + rb - 1)[:, None], axis=1).astype(jnp.int32)
    per_blk = g_hi - g_lo + 1
    item0 = jnp.cumsum(per_blk) - per_blk
    w = jnp.arange(n_items, dtype=jnp.int32)
    valid = w < jnp.sum(per_blk)
    blk = jnp.clip(jnp.sum(item0[None, :] <= w[:, None], axis=1) - 1, 0, nb - 1).astype(jnp.int32)
    grp = jnp.where(valid, g_lo[blk] + (w - item0[blk]), g_hi[nb - 1]).astype(jnp.int32)
    first = (valid & (w == item0[blk])).astype(jnp.int32)
    return blk, grp, first, valid.astype(jnp.int32)


def _moe(xg, rank, grp, counts, w_gate, w_up, w_down, e_base, g, b, t_dma, rb):
    n = xg.shape[0]
    counts = counts[0, :N_GROUPS].astype(jnp.int32)
    offsets = jnp.cumsum(counts) - counts
    pos = (offsets[grp[:, 0]] + rank[:, 0]).astype(jnp.int32)
    xs = _dispatch(xg, pos, t_dma)
    ys = _moe_ffn(xs, _moe_work_items(counts, n, rb), w_gate, w_up, w_down, rb, e_base)
    return _combine(pos, xg, ys, g, b, t_dma)


def _pad_cols(w, width):
    return jnp.pad(w, ((0, 0), (0, width - w.shape[1])))


def _layer_weights(w_in, w_out, gla_w_a2, ml_b_i, ml_b_f, mla_w_uq, mla_w_ukv, moe_w_group, moe_b_group,
                   moe_w_router, moe_b_router):
    o = _IN_OFF
    half = MLA_ROPE // 2
    wg = _pad_cols(w_in[:, o[0]:o[5]], SEG_W).astype(BF16)
    wm = _pad_cols(w_in[:, o[5]:o[10]], SEG_W).astype(BF16)
    kr = w_in[:, o[12]:o[13]]
    zeros = lambda w: jnp.zeros((D_MODEL, w), F32)
    kra = jnp.concatenate([zeros(MLA_NOPE), kr, zeros(LANES - MLA_NOPE - MLA_ROPE)], 1)
    krb = jnp.concatenate([zeros(MLA_NOPE), kr[:, half:], kr[:, :half], zeros(LANES - MLA_NOPE - MLA_ROPE)], 1)
    wc = jnp.concatenate([w_in[:, o[10]:o[12]], kra, krb], 1).astype(BF16)
    wift = w_in[:, o[8]:o[10]].T.astype(BF16)
    wa2 = jnp.pad(gla_w_a2, ((0, LANES - GLA_GATE_RANK), (0, 0))).astype(BF16)
    b_gate = jnp.concatenate([ml_b_i, ml_b_f])
    b_col = jnp.pad(b_gate, (0, LANES - 2 * MIX_HEADS)).reshape(1, LANES)
    b_row = b_gate.reshape(2 * MIX_HEADS, 1)
    uq = mla_w_uq.reshape(MLA_Q_RANK, MLA_HEADS, MLA_NOPE + MLA_ROPE)
    zq = jnp.zeros((MLA_Q_RANK, MLA_HEADS, LANES - MLA_NOPE - MLA_ROPE), F32)
    wqa = jnp.concatenate([uq, zq], -1).reshape(MLA_Q_RANK, -1).astype(BF16)
    wqb = jnp.concatenate([jnp.zeros((MLA_Q_RANK, MLA_HEADS, MLA_NOPE), F32), uq[..., MLA_NOPE + half:],
                           uq[..., MLA_NOPE:MLA_NOPE + half], zq], -1).reshape(MLA_Q_RANK, -1).astype(BF16)
    ukv = mla_w_ukv.reshape(MLA_KV_RANK, MLA_HEADS, MLA_NOPE + MLA_V)
    wkn = jnp.concatenate([ukv[..., :MLA_NOPE], jnp.zeros((MLA_KV_RANK, MLA_HEADS, LANES - MLA_NOPE), F32)],
                          -1).reshape(MLA_KV_RANK, -1).astype(BF16)
    uv = ukv[..., MLA_NOPE:]
    zv = jnp.zeros_like(uv)
    odd = (jnp.arange(MLA_HEADS) % 2 == 1)[None, :, None]
    wv = jnp.concatenate([jnp.where(odd, zv, uv), jnp.where(odd, uv, zv)], -1).reshape(MLA_KV_RANK, -1).astype(BF16)
    wo = w_out.astype(BF16)
    w_route = _pad_cols(jnp.concatenate([moe_w_router, moe_w_group], 1), LANES)
    w_route_hi = w_route.astype(BF16)
    w_route_mid = (w_route - w_route_hi.astype(F32)).astype(BF16)
    b_route = jnp.pad(jnp.concatenate([moe_b_router, moe_b_group]), (0, LANES - N_EXPERTS - N_GROUPS)).reshape(1, LANES)
    return dict(wg=wg, wm=wm, wc=wc, wift=wift, wa2=wa2, b_col=b_col, b_row=b_row, wqa=wqa, wqb=wqb, wkn=wkn,
                wv=wv, wo_g=wo[0:MIX_W], wo_m=wo[MIX_W:2 * MIX_W], wo_c=wo[2 * MIX_W:], w_route_hi=w_route_hi, w_route_mid=w_route_mid,
                b_route=b_route)


def _tile(total, want):
    t = min(total, want)
    assert total % t == 0
    return t


def kernel(x, mem, positions, w_in, w_out, gla_w_a2, gla_b_a, gla_norm_g, ml_conv_w, ml_b_i, ml_b_f, ml_norm_g, mla_q_norm_g, mla_w_uq, mla_kv_norm_g, mla_w_ukv, xa_w_q, xa_w_kv, xa_w_o, moe_w_group, moe_b_group, moe_w_router, moe_b_router, moe_w_gate, moe_w_up, moe_w_down, ln1_g, ln1_b, ln2_g, ln2_b, ln3_g, ln3_b):
    batch, seq, _ = x.shape
    mem_len = mem.shape[1]
    n = batch * seq
    depth = w_in.shape[0]
    assert seq % CHUNK == 0
    t_in = _tile(n, 512)
    tm = _tile(n, 1024)
    ts = _tile(seq, 512)
    tq = _tile(seq, 512)
    t_moe = _tile(n, 1024)
    row = lambda a: a.reshape(1, -1)

    cos_t, sin_t = _rope_tables(positions.reshape(n, 1), _tile(n, 2048))
    mem2d = mem.reshape(batch * mem_len, D_MODEL)
    experts_gate = moe_w_gate.reshape(-1, D_MODEL, D_EXPERT)
    experts_up = moe_w_up.reshape(-1, D_MODEL, D_EXPERT)
    experts_down = moe_w_down.reshape(-1, D_EXPERT, D_MODEL)
    h = x.reshape(n, D_MODEL)
    for l in range(depth):
        w = _layer_weights(w_in[l], w_out[l], gla_w_a2[l], ml_b_i[l], ml_b_f[l], mla_w_uq[l], mla_w_ukv[l],
                           moe_w_group[l], moe_b_group[l], moe_w_router[l], moe_b_router[l])
        yg, ym, yift, q, k, v = _in_proj(h, w["wg"], w["wm"], w["wc"], w["wift"], cos_t, sin_t,
                                         row(mla_q_norm_g[l]), row(mla_kv_norm_g[l]),
                                         w["wqa"], w["wqb"], w["wkn"], w["wv"], t_in)
        gates_row = yift.reshape(2 * MIX_HEADS, n // CHUNK, CHUNK).transpose(1, 0, 2)
        og = _gla(yg, w["wa2"], row(gla_b_a[l]), row(gla_norm_g[l]), batch, seq, ts)
        om = _mlstm(ym, gates_row, ml_conv_w[l], w["b_col"], w["b_row"], row(ml_norm_g[l]), batch, seq, ts)
        oc = _mla_attn(q, k, v, batch, seq, tq)
        x1 = _out_proj(og, om, oc, h, w["wo_g"], w["wo_m"], w["wo_c"], row(ln1_g[l]), row(ln1_b[l]), tm)
        xk, xv = _xa_kv(mem2d, xa_w_kv[l].astype(BF16), mem_len)
        xg, rank, grp, counts = _xattn_route(
            x1, xk, xv, xa_w_q[l].astype(BF16), xa_w_o[l].astype(BF16), row(ln2_g[l]), row(ln2_b[l]),
            w["w_route_hi"], w["w_route_mid"], w["b_route"], batch, seq, mem_len, tm)
        h = _moe(xg, rank, grp, counts, experts_gate, experts_up, experts_down, l * N_EXPERTS,
                 row(ln3_g[l]), row(ln3_b[l]), t_moe, t_moe)
    return h.reshape(batch, seq, D_MODEL)
```

```python
import functools

import numpy as np
import jax
import jax.numpy as jnp
from jax import lax
from jax.experimental import pallas as pl
from jax.experimental.pallas import tpu as pltpu

F32 = jnp.float32
BF16 = jnp.bfloat16

D_MODEL = 1024
CHUNK = 64
HEAD_D = 64
MIX_HEADS = 4
MIX_W = MIX_HEADS * HEAD_D
GLA_GATE_RANK = 16
GLA_TAU = 16.0
MLSTM_CONV = 4
MLA_HEADS = 8
MLA_NOPE = 64
MLA_ROPE = 32
MLA_V = 64
MLA_Q_RANK = 256
MLA_KV_RANK = 128
MLA_QK_PAD = 128
ROPE_BASE = 10000.0
LOG2_E = 1.4426950408889634
MLA_HEAD_PAIR = 2
XA_HEADS = 4
XA_DH = D_MODEL // XA_HEADS
N_GROUPS = 4
EXPERTS_PER_GROUP = 8
N_EXPERTS = N_GROUPS * EXPERTS_PER_GROUP
D_EXPERT = 256
DEPTH = 2
ALPHA = (2 * DEPTH) ** 0.25
LN_EPS = 1e-5
LANES = 128
SUBLANES = 8
MIXER_UNROLL = 4
SEG_W = 1152
MLA_SEG_W = 640
XG_W = D_MODEL + LANES
EXPERTS_PER_STEP = 4
DMA_UNROLL = 32
GLA_LEVELS = (32, 16, 8, 4, 2, 1)
VMEM_LIMIT = 56 * 1024 * 1024

_IN_SIZES = (256, 256, 256, 256, GLA_GATE_RANK, 512, 256, 256, 4, 4, MLA_Q_RANK, MLA_KV_RANK, MLA_ROPE)
_IN_OFF = np.concatenate([[0], np.cumsum(_IN_SIZES)]).tolist()


def _params(sem):
    return pltpu.CompilerParams(dimension_semantics=sem, vmem_limit_bytes=VMEM_LIMIT)


def _full(shape):
    return pl.BlockSpec(shape, lambda *_: (0,) * len(shape))


def _layer_norm(x, g, b):
    mu = jnp.mean(x, -1, keepdims=True)
    xc = x - mu
    var = jnp.mean(xc * xc, -1, keepdims=True)
    return xc * lax.rsqrt(var + LN_EPS) * g + b


def _log_sigmoid(z):
    return jnp.minimum(z, 0.0) - jnp.log1p(jnp.exp(-jnp.abs(z)))


def _dot_nt(a, b):
    return lax.dot_general(a, b, (((1,), (1,)), ((), ())), preferred_element_type=F32)


def _dot_tn(a, b):
    return lax.dot_general(a, b, (((0,), (0,)), ((), ())), preferred_element_type=F32)


def _dot(a, b):
    return jnp.dot(a, b, preferred_element_type=F32)


def _head_norm(o, g):
    mu = jnp.mean(o, -1, keepdims=True)
    oc = o - mu
    var = jnp.mean(oc * oc, -1, keepdims=True)
    return oc * lax.rsqrt(var + LN_EPS) * g


def _in_proj_kernel(x_ref, wg_ref, wm_ref, wc_ref, wift_ref, cos_ref, sin_ref, gq_ref, gkv_ref,
                    wqa_ref, wqb_ref, wkn_ref, wv_ref, ones_ref,
                    yg_ref, ym_ref, yift_ref, q_ref, k_ref, v_ref, yc_ref):
    xb = x_ref[...].astype(BF16)
    yg_ref[...] = _dot(xb, wg_ref[...])
    ym_ref[...] = _dot(xb, wm_ref[...])
    yift_ref[...] = _dot_nt(wift_ref[...], xb)
    yc_ref[...] = _dot(xb, wc_ref[...])
    _mla_prep_tile(yc_ref, cos_ref, sin_ref, gq_ref, gkv_ref, wqa_ref, wqb_ref, wkn_ref, wv_ref, ones_ref,
                   q_ref, k_ref, v_ref)


def _in_proj(x2d, wg, wm, wc, wift, cos_t, sin_t, gq, gkv, wqa, wqb, wkn, wv, tm):
    n = x2d.shape[0]
    ones_row = np.zeros((1, MLA_HEADS * LANES), np.float32)
    for h in range(MLA_HEADS):
        ones_row[0, h * LANES + (MLA_V if h % 2 == 0 else 0)] = 1.0
    ones_row = jnp.asarray(ones_row)
    row = lambda w: pl.BlockSpec((tm, w), lambda i: (i, 0))
    qk_w = MLA_HEADS * MLA_QK_PAD
    return pl.pallas_call(
        _in_proj_kernel,
        grid=(n // tm,),
        in_specs=[row(D_MODEL), _full(wg.shape), _full(wm.shape), _full(wc.shape), _full(wift.shape),
                  row(LANES), row(LANES), _full(gq.shape), _full(gkv.shape),
                  _full(wqa.shape), _full(wqb.shape), _full(wkn.shape), _full(wv.shape), _full(ones_row.shape)],
        out_specs=[row(SEG_W), row(SEG_W), pl.BlockSpec((8, tm), lambda i: (0, i)),
                   row(qk_w), row(qk_w), row(MLA_HEADS * LANES)],
        out_shape=[jax.ShapeDtypeStruct((n, SEG_W), F32), jax.ShapeDtypeStruct((n, SEG_W), F32),
                   jax.ShapeDtypeStruct((8, n), F32),
                   jax.ShapeDtypeStruct((n, qk_w), BF16), jax.ShapeDtypeStruct((n, qk_w), BF16),
                   jax.ShapeDtypeStruct((n, MLA_HEADS * LANES), BF16)],
        scratch_shapes=[pltpu.VMEM((tm, MLA_SEG_W), F32)],
        compiler_params=_params(("parallel",)),
        name="in_proj_mla_prep",
    )(x2d, wg, wm, wc, wift, cos_t, sin_t, gq, gkv, wqa, wqb, wkn, wv, ones_row)


def _split3(x):
    hi = x.astype(BF16)
    r1 = x - hi.astype(F32)
    mid = r1.astype(BF16)
    lo = (r1 - mid.astype(F32)).astype(BF16)
    return hi, mid, lo


def _cumsum_rows(tri, x):
    hi, mid, lo = _split3(x)
    return _dot(tri, hi) + _dot(tri, mid) + _dot(tri, lo)


def _gla_constants():
    t = np.arange(CHUNK)
    n_lv = len(GLA_LEVELS)
    masks = np.zeros((n_lv + 1, CHUNK, CHUNK), np.float32)
    right = np.zeros((n_lv, CHUNK, 1), np.float32)
    for li, n in enumerate(GLA_LEVELS):
        blk = t // (2 * n)
        is_right = (t % (2 * n)) >= n
        masks[li] = ((blk[:, None] == blk[None, :]) & is_right[:, None] & ~is_right[None, :])
        right[li, :, 0] = is_right
    masks[n_lv] = np.eye(CHUNK)
    return masks, right


def _gla_level_exponents(b, log_a, right_ref):
    row = lax.broadcasted_iota(jnp.int32, (CHUNK, 1), 0)
    exps = []
    for li, n in enumerate(GLA_LEVELS):
        if n >= SUBLANES // 2:
            per_blk = max(2 * n // SUBLANES, 1)
            b4 = b.reshape(CHUNK // (per_blk * SUBLANES), per_blk, SUBLANES, MIX_W)
            r_reg, r_sub = divmod(n - 1, SUBLANES)
            src = b4[:, r_reg:r_reg + 1, r_sub:r_sub + 1, :]
            b_r = jnp.broadcast_to(src, b4.shape).reshape(CHUNK, MIX_W)
            exps.append(jnp.where(right_ref[li] > 0.0, b - b_r, b_r - b))
        elif n == 2:
            pos = row % 4
            nxt = pltpu.roll(log_a, CHUNK - 1, 0)
            prv = pltpu.roll(log_a, 1, 0)
            exps.append(jnp.where(pos == 0, nxt, jnp.where(pos == 1, 0.0,
                                                            jnp.where(pos == 2, log_a, log_a + prv))))
        else:
            exps.append(jnp.where(right_ref[li] > 0.0, log_a, 0.0))
    return exps


def _pair_blocks(z, low_half):
    zero = jnp.zeros_like(z)
    return jnp.concatenate([jnp.where(low_half, z, zero), jnp.where(low_half, zero, z)], axis=0)


def _gla_kernel(y_ref, wa2_ref, ba_ref, g_ref, tri_ref, mask_ref, right_ref, o_ref, st_ref, *, ts, unroll):
    n_lv = len(GLA_LEVELS)
    pairs = MIX_HEADS // 2
    low_half = lax.broadcasted_iota(jnp.int32, (CHUNK, LANES), 1) < HEAD_D
    same_head = (lax.broadcasted_iota(jnp.int32, (LANES, LANES), 0) < HEAD_D) == (
        lax.broadcasted_iota(jnp.int32, (LANES, LANES), 1) < HEAD_D)

    @pl.when(pl.program_id(1) == 0)
    def _():
        st_ref[...] = jnp.zeros_like(st_ref)

    def chunk(c, carry):
        rows = pl.ds(pl.multiple_of(c * CHUNK, CHUNK), CHUNK)
        q = y_ref[rows, 0:256] * (HEAD_D ** -0.5)
        k = y_ref[rows, 256:512]
        a_lr = y_ref[rows, 1024:1152]
        z = _dot(a_lr.astype(BF16), wa2_ref[...]) + ba_ref[...]
        log_a = _log_sigmoid(z) * (1.0 / GLA_TAU)
        b = _cumsum_rows(tri_ref[...], log_a)
        b_end = b[CHUNK - 1:CHUNK, :]
        scores = [jnp.zeros((CHUNK, LANES), F32) for _ in range(pairs)]
        for li, e in enumerate(_gla_level_exponents(b, log_a, right_ref)):
            x = (jnp.where(right_ref[li] > 0.0, q, k) * jnp.exp(e)).astype(BF16)
            for p in range(pairs):
                x2 = x[:, p * LANES:(p + 1) * LANES]
                scores[p] = scores[p] + _dot_nt(x2, _pair_blocks(x2, low_half)) * mask_ref[li]
        qb = q.astype(BF16)
        kb = k.astype(BF16)
        q_in = (q * jnp.exp(b)).astype(BF16)
        k_out = (k * jnp.exp(b_end - b)).astype(BF16)
        dec_end = jnp.exp(b_end)
        for p in range(pairs):
            pl_ = slice(p * LANES, (p + 1) * LANES)
            a = scores[p] + _dot_nt(qb[:, pl_], _pair_blocks(kb[:, pl_], low_half)) * mask_ref[n_lv]
            v2 = y_ref[rows, 512 + p * LANES:512 + (p + 1) * LANES].astype(BF16)
            st = st_ref[p]
            o = _dot(a.astype(BF16), _pair_blocks(v2, low_half)) + _dot_nt(q_in[:, pl_], st.astype(BF16))
            st_ref[p] = st * dec_end[:, pl_] + jnp.where(same_head, _dot_tn(v2, k_out[:, pl_]), 0.0)
            normed = jnp.zeros((CHUNK, LANES), F32)
            for mine in (low_half, jnp.logical_not(low_half)):
                mu = jnp.sum(jnp.where(mine, o, 0.0), -1, keepdims=True) * (1.0 / HEAD_D)
                cen = jnp.where(mine, o - mu, 0.0)
                var = jnp.sum(cen * cen, -1, keepdims=True) * (1.0 / HEAD_D)
                normed = normed + cen * lax.rsqrt(var + LN_EPS)
            r_gate = y_ref[rows, 768 + p * LANES:768 + (p + 1) * LANES]
            o_ref[rows, pl_] = (normed * g_ref[:, pl_] * (r_gate * jax.nn.sigmoid(r_gate))).astype(o_ref.dtype)
        return carry

    lax.fori_loop(0, ts // CHUNK, chunk, 0, unroll=unroll)


def _gla(yg, wa2, ba, g, batch, seq, ts):
    n = yg.shape[0]
    nt = seq // ts
    masks, right = _gla_constants()
    masks = np.concatenate([masks, masks], axis=-1)
    tri = jnp.asarray(np.tril(np.ones((CHUNK, CHUNK), np.float32)), BF16)
    return pl.pallas_call(
        functools.partial(_gla_kernel, ts=ts, unroll=MIXER_UNROLL),
        grid=(batch, nt),
        in_specs=[pl.BlockSpec((ts, SEG_W), lambda b, i: (b * nt + i, 0)),
                  _full(wa2.shape), _full(ba.shape), _full(g.shape),
                  _full(tri.shape), _full(masks.shape), _full(right.shape)],
        out_specs=pl.BlockSpec((ts, MIX_W), lambda b, i: (b * nt + i, 0)),
        out_shape=jax.ShapeDtypeStruct((n, MIX_W), BF16),
        scratch_shapes=[pltpu.VMEM((MIX_HEADS // 2, LANES, LANES), F32)],
        compiler_params=_params(("parallel", "arbitrary")),
        name="gla_mixer",
    )(yg, wa2, ba, g, tri, jnp.asarray(masks), jnp.asarray(right))


def _mlstm_kernel(y_ref, gr_ref, cw_ref, bcol_ref, brow_ref, g_ref, tri_ref, eb_ref, eye_ref, o_ref,
                  xe_ref, qk_ref, vx_ref, fcl_ref, wsl_ref, fcb_ref, wsb_ref, rv_ref, dec_ref,
                  w_ref, em_ref, upd_ref, c_ref, m_ref, *, ts):
    first = pl.program_id(1) == 0

    @pl.when(first)
    def _():
        xe_ref[0:8, :] = jnp.zeros((8, 2 * MIX_W), F32)
        c_ref[...] = jnp.zeros_like(c_ref)
        m_ref[...] = jnp.zeros_like(m_ref)

    @pl.when(jnp.logical_not(first))
    def _():
        xe_ref[0:8, :] = xe_ref[ts:ts + 8, :]

    xe_ref[8:ts + 8, :] = y_ref[:, 0:2 * MIX_W]
    for c in range(ts // CHUNK):
        r0 = 8 + c * CHUNK
        conv = cw_ref[MLSTM_CONV - 1:MLSTM_CONV, :] * xe_ref[r0:r0 + CHUNK, :]
        for j in range(MLSTM_CONV - 1):
            lo = r0 - (MLSTM_CONV - 1) + j
            conv = conv + cw_ref[j:j + 1, :] * xe_ref[lo:lo + CHUNK, :]
        qk_ref[c * CHUNK:(c + 1) * CHUNK, :] = conv * jax.nn.sigmoid(conv)

    ones_col = (lax.broadcasted_iota(jnp.int32, (ts, HEAD_D), 1) == 0).astype(BF16)
    for h in range(MIX_HEADS):
        v_at, one_at = (0, HEAD_D) if h % 2 == 0 else (HEAD_D, 0)
        vx_ref[:, h * LANES + v_at:h * LANES + v_at + HEAD_D] = (
            y_ref[:, 512 + h * HEAD_D:512 + (h + 1) * HEAD_D].astype(BF16))
        vx_ref[:, h * LANES + one_at:h * LANES + one_at + HEAD_D] = ones_col

    tri = tri_ref[...]
    expand = eb_ref[...]
    nck = ts // CHUNK
    lane = lax.broadcasted_iota(jnp.int32, (1, LANES), 1)
    f_lanes = (lane >= MIX_HEADS) & (lane < 2 * MIX_HEADS)
    head_lane = (lax.broadcasted_iota(jnp.int32, (MIX_HEADS, LANES), 1)
                 == lax.broadcasted_iota(jnp.int32, (MIX_HEADS, LANES), 0) + MIX_HEADS)

    def to_lanes(col):
        return jnp.sum(jnp.where(head_lane, col, 0.0), 0, keepdims=True)

    g_rows = gr_ref[...] + brow_ref[...]
    ls_rows = _log_sigmoid(g_rows).reshape(nck * SUBLANES, CHUNK)
    fcum_rows = sum(_dot_nt(part, tri) for part in _split3(ls_rows)).reshape(nck, SUBLANES, CHUNK)
    m_col = m_ref[0:MIX_HEADS, 0:1]
    for c in range(nck):
        g_col = y_ref[c * CHUNK:(c + 1) * CHUNK, 1024:1152] + bcol_ref[...]
        fcum_col = jnp.where(f_lanes, _cumsum_rows(tri, _log_sigmoid(g_col)), 0.0)
        fcl_ref[c * CHUNK:(c + 1) * CHUNK, :] = fcum_col
        g_row = g_rows[c]
        fcum_row = fcum_rows[c]
        f_row = fcum_row[MIX_HEADS:2 * MIX_HEADS, :]
        i_row = g_row[0:MIX_HEADS, :]
        f_end = f_row[:, CHUNK - 1:CHUNK]
        rv_ref[c, 0:MIX_HEADS, :] = jnp.concatenate(
            [i_row - f_row, jnp.broadcast_to(m_col, (MIX_HEADS, HEAD_D))], axis=1)
        m_new = jnp.maximum(f_end + m_col, jnp.max(f_end - f_row + i_row, -1, keepdims=True))
        m_new_l = to_lanes(m_new)
        f_end_l = fcum_col[CHUNK - 1:CHUNK, :]
        i_shift = pltpu.roll(g_col, MIX_HEADS, 1)
        wsl_ref[c * CHUNK:(c + 1) * CHUNK, :] = jnp.where(
            f_lanes, jnp.exp(f_end_l - fcum_col + i_shift - m_new_l), 0.0)
        dec_ref[c] = jnp.broadcast_to(jnp.exp(f_end_l + to_lanes(m_col) - m_new_l), (SUBLANES, LANES))
        m_col = m_new
    m_ref[0:MIX_HEADS, :] = jnp.broadcast_to(m_col, (MIX_HEADS, LANES))
    slab = 2 * CHUNK
    for i in range(ts // slab):
        sr = slice(i * slab, (i + 1) * slab)
        fcb_ref[sr, :] = sum(_dot(part, expand) for part in _split3(fcl_ref[sr, :]))
        ws_hi, ws_mid, _ = _split3(wsl_ref[sr, :])
        wsb_ref[sr, :] = _dot(ws_hi, expand) + _dot(ws_mid, expand)

    t_idx = lax.broadcasted_iota(jnp.int32, (CHUNK, LANES), 0)
    s_idx = lax.broadcasted_iota(jnp.int32, (CHUNK, LANES), 1)
    keep = (s_idx <= t_idx) | (s_idx >= CHUNK)
    low_half = s_idx < HEAD_D

    def weights(c, carry):
        rows = pl.ds(pl.multiple_of(c * CHUNK, CHUNK), CHUNK)
        rv = rv_ref[c, 0:MIX_HEADS, :]
        for p in range(MIX_HEADS // 2):
            q2 = (qk_ref[rows, p * LANES:(p + 1) * LANES] * (HEAD_D ** -0.5)).astype(BF16)
            k2 = qk_ref[rows, MIX_W + p * LANES:MIX_W + (p + 1) * LANES]
            for h in (2 * p, 2 * p + 1):
                mine = low_half if h % 2 == 0 else jnp.logical_not(low_half)
                hb = slice(h * LANES, (h + 1) * LANES)
                logw = jnp.where(keep, fcb_ref[rows, hb] + rv[h:h + 1, :], -jnp.inf)
                m_t = jnp.max(logw, -1, keepdims=True)
                k_h = jnp.where(mine, k2, 0.0)
                qk = _dot_nt(q2, jnp.concatenate([k_h.astype(BF16), eye_ref[h % 2]], axis=0))
                w_ref[c * MIX_HEADS + h] = (jnp.exp(logw - m_t) * qk).astype(BF16)
                em_ref[c * MIX_HEADS + h] = jnp.exp(-m_t)
                upd = _dot_tn((k_h * wsb_ref[rows, hb]).astype(BF16), vx_ref[rows, hb])
                off = (h % 2) * HEAD_D
                upd_ref[c * MIX_HEADS + h] = upd[off:off + HEAD_D, :]
        return carry

    lax.fori_loop(0, nck, weights, 0, unroll=2)

    def chunk(c, carry):
        rows = pl.ds(pl.multiple_of(c * CHUNK, CHUNK), CHUNK)
        dec_l = dec_ref[c]
        for p in range(MIX_HEADS // 2):
            pl_ = slice(p * LANES, (p + 1) * LANES)
            pair = jnp.zeros((CHUNK, LANES), F32)
            for h in (2 * p, 2 * p + 1):
                mine = low_half if h % 2 == 0 else jnp.logical_not(low_half)
                c_st = c_ref[h]
                num = _dot(w_ref[c * MIX_HEADS + h],
                           jnp.concatenate([vx_ref[rows, h * LANES:(h + 1) * LANES], c_st.astype(BF16)], axis=0))
                c_ref[h] = dec_l[0:1, MIX_HEADS + h:MIX_HEADS + h + 1] * c_st + upd_ref[c * MIX_HEADS + h]
                den = num[:, HEAD_D:HEAD_D + 1] if h % 2 == 0 else num[:, 0:1]
                r = 1.0 / jnp.maximum(jnp.abs(den), em_ref[c * MIX_HEADS + h])
                mu = jnp.sum(jnp.where(mine, num, 0.0), -1, keepdims=True) * (1.0 / HEAD_D)
                cen = jnp.where(mine, num - mu, 0.0)
                var = jnp.sum(cen * cen, -1, keepdims=True) * (1.0 / HEAD_D)
                pair = pair + cen * (r * lax.rsqrt(r * r * var + LN_EPS))
            o_gate = y_ref[rows, 768 + p * LANES:768 + (p + 1) * LANES]
            o_ref[rows, pl_] = (pair * g_ref[:, pl_] * jax.nn.sigmoid(o_gate)).astype(o_ref.dtype)
        return carry

    lax.fori_loop(0, nck, chunk, 0, unroll=2)


def _mlstm(ym, gates_row, conv_w, b_col, b_row, g, batch, seq, ts):
    n = ym.shape[0]
    nt = seq // ts
    nck = ts // CHUNK
    tri = jnp.asarray(np.tril(np.ones((CHUNK, CHUNK), np.float32)), BF16)
    expand = np.zeros((LANES, MIX_HEADS * LANES), np.float32)
    for h in range(MIX_HEADS):
        expand[MIX_HEADS + h, h * LANES:(h + 1) * LANES] = 1.0
    eye = np.zeros((2, HEAD_D, LANES), np.float32)
    eye[0, :, 0:HEAD_D] = np.eye(HEAD_D)
    eye[1, :, HEAD_D:] = np.eye(HEAD_D)
    expand = jnp.asarray(expand, BF16)
    eye = jnp.asarray(eye, BF16)
    return pl.pallas_call(
        functools.partial(_mlstm_kernel, ts=ts),
        grid=(batch, nt),
        in_specs=[pl.BlockSpec((ts, SEG_W), lambda b, i: (b * nt + i, 0)),
                  pl.BlockSpec((nck, 8, CHUNK), lambda b, i: (b * nt + i, 0, 0)),
                  _full(conv_w.shape), _full(b_col.shape), _full(b_row.shape), _full(g.shape),
                  _full(tri.shape), _full(expand.shape), _full(eye.shape)],
        out_specs=pl.BlockSpec((ts, MIX_W), lambda b, i: (b * nt + i, 0)),
        out_shape=jax.ShapeDtypeStruct((n, MIX_W), BF16),
        scratch_shapes=[pltpu.VMEM((ts + 8, 2 * MIX_W), F32),
                        pltpu.VMEM((ts, 2 * MIX_W), F32),
                        pltpu.VMEM((ts, MIX_HEADS * LANES), BF16),
                        pltpu.VMEM((ts, LANES), F32),
                        pltpu.VMEM((ts, LANES), F32),
                        pltpu.VMEM((ts, MIX_HEADS * LANES), F32),
                        pltpu.VMEM((ts, MIX_HEADS * LANES), F32),
                        pltpu.VMEM((nck, SUBLANES, LANES), F32),
                        pltpu.VMEM((nck, SUBLANES, LANES), F32),
                        pltpu.VMEM((nck * MIX_HEADS, CHUNK, LANES), BF16),
                        pltpu.VMEM((nck * MIX_HEADS, CHUNK, 1), F32),
                        pltpu.VMEM((nck * MIX_HEADS, HEAD_D, LANES), F32),
                        pltpu.VMEM((MIX_HEADS, HEAD_D, LANES), F32),
                        pltpu.VMEM((SUBLANES, LANES), F32)],
        compiler_params=_params(("parallel", "arbitrary")),
        name="mlstm_mixer",
    )(ym, gates_row, conv_w, b_col, b_row, g, tri, expand, eye)


def _rope_table_kernel(pos_ref, inv_ref, cos_ref, sin_ref):
    ang = pos_ref[...].astype(F32) * inv_ref[...]
    lane = lax.broadcasted_iota(jnp.int32, ang.shape, 1)
    rot = (lane >= MLA_NOPE) & (lane < MLA_NOPE + MLA_ROPE)
    first_half = lane < MLA_NOPE + MLA_ROPE // 2
    cos_ref[...] = jnp.where(lane < MLA_NOPE, 1.0, jnp.where(rot, jnp.cos(ang), 0.0))
    s = jnp.sin(ang)
    sin_ref[...] = jnp.where(rot, jnp.where(first_half, -s, s), 0.0)


def _rope_tables(pos_col, tm):
    n = pos_col.shape[0]
    half = MLA_ROPE // 2
    inv = ROPE_BASE ** (-np.arange(half, dtype=np.float32) / half)
    inv_row = np.zeros((1, LANES), np.float32)
    inv_row[0, MLA_NOPE:MLA_NOPE + half] = inv
    inv_row[0, MLA_NOPE + half:MLA_NOPE + MLA_ROPE] = inv
    return pl.pallas_call(
        _rope_table_kernel,
        grid=(n // tm,),
        in_specs=[pl.BlockSpec((tm, 1), lambda i: (i, 0)), _full((1, LANES))],
        out_specs=[pl.BlockSpec((tm, LANES), lambda i: (i, 0))] * 2,
        out_shape=[jax.ShapeDtypeStruct((n, LANES), F32)] * 2,
        compiler_params=_params(("parallel",)),
        name="rope_tables",
    )(pos_col, jnp.asarray(inv_row))


def _mla_prep_tile(y_ref, cos_ref, sin_ref, gq_ref, gkv_ref, wqa_ref, wqb_ref, wkn_ref, wv_ref, ones_ref,
                   q_ref, k_ref, v_ref):
    def rms(x, g):
        return x * lax.rsqrt(jnp.mean(x * x, -1, keepdims=True) + LN_EPS) * g

    cos = cos_ref[...]
    sin = sin_ref[...]
    cq = rms(y_ref[:, 0:MLA_Q_RANK], gq_ref[...]).astype(BF16)
    ckv = rms(y_ref[:, MLA_Q_RANK:MLA_Q_RANK + MLA_KV_RANK], gkv_ref[...]).astype(BF16)
    k_rope = y_ref[:, 384:512] * cos + y_ref[:, 512:640] * sin
    qa = _dot(cq, wqa_ref[...])
    qb = _dot(cq, wqb_ref[...])
    kn = _dot(ckv, wkn_ref[...])
    scale = (MLA_NOPE + MLA_ROPE) ** -0.5 * LOG2_E
    for h in range(MLA_HEADS):
        sl = slice(h * MLA_QK_PAD, (h + 1) * MLA_QK_PAD)
        q_ref[:, sl] = ((qa[:, sl] * cos + qb[:, sl] * sin) * scale).astype(q_ref.dtype)
        k_ref[:, sl] = (kn[:, sl] + k_rope).astype(k_ref.dtype)
    v_ref[...] = (_dot(ckv, wv_ref[...]) + ones_ref[...]).astype(v_ref.dtype)


def _mla_attn_kernel(q_ref, k_ref, v_ref, o_ref, *, seq, tq):
    t_chunk = lax.broadcasted_iota(jnp.int32, (tq, tq), 0) // CHUNK
    s_chunk = lax.broadcasted_iota(jnp.int32, (tq, tq), 1) // CHUNK
    diag_mask = s_chunk <= t_chunk
    low_half = lax.broadcasted_iota(jnp.int32, (tq, LANES), 1) < MLA_V
    for i in range(seq // tq):
        rows = slice(i * tq, (i + 1) * tq)
        res = []
        for hh in range(MLA_HEAD_PAIR):
            ql = slice(hh * MLA_QK_PAD, (hh + 1) * MLA_QK_PAD)
            vl = slice(hh * LANES, (hh + 1) * LANES)
            q = q_ref[0, rows, ql]
            s_d = jnp.where(diag_mask, _dot_nt(q, k_ref[0, rows, ql]), -jnp.inf)
            m = jnp.max(s_d, -1, keepdims=True)
            if i > 0:
                s_o = _dot_nt(q, k_ref[0, 0:i * tq, ql])
                m = jnp.maximum(m, jnp.max(s_o, -1, keepdims=True))
            o = _dot(jnp.exp2(s_d - m).astype(BF16), v_ref[0, rows, vl])
            if i > 0:
                o = o + _dot(jnp.exp2(s_o - m).astype(BF16), v_ref[0, 0:i * tq, vl])
            l = o[:, MLA_V:MLA_V + 1] if hh == 0 else o[:, 0:1]
            res.append(o / l)
        o_ref[0, rows, :] = jnp.where(low_half, res[0], res[1]).astype(o_ref.dtype)


def _mla_attn(q, k, v, batch, seq, tq):
    n = q.shape[0]
    qk_w = MLA_HEADS * MLA_QK_PAD
    v_w = MLA_HEADS * MLA_V
    pair = lambda w: pl.BlockSpec((1, seq, MLA_HEAD_PAIR * w), lambda b, h: (b, 0, h))
    out = pl.pallas_call(
        functools.partial(_mla_attn_kernel, seq=seq, tq=tq),
        grid=(batch, MLA_HEADS // MLA_HEAD_PAIR),
        in_specs=[pair(MLA_QK_PAD), pair(MLA_QK_PAD), pair(LANES)],
        out_specs=pair(MLA_V),
        out_shape=jax.ShapeDtypeStruct((batch, seq, v_w), BF16),
        compiler_params=_params(("parallel", "parallel")),
        name="mla_attention",
    )(q.reshape(batch, seq, qk_w), k.reshape(batch, seq, qk_w), v.reshape(batch, seq, MLA_HEADS * LANES))
    return out.reshape(n, v_w)


def _xa_kv_kernel(mem_ref, w_ref, k_ref, v_ref):
    kv = _dot(mem_ref[...].astype(BF16), w_ref[...])
    k_ref[...] = kv[:, 0:D_MODEL].astype(k_ref.dtype)
    v_ref[...] = kv[:, D_MODEL:2 * D_MODEL].astype(v_ref.dtype)


def _xa_kv(mem2d, w_kv, mem_len):
    n = mem2d.shape[0]
    row = pl.BlockSpec((mem_len, D_MODEL), lambda i: (i, 0))
    return pl.pallas_call(
        _xa_kv_kernel,
        grid=(n // mem_len,),
        in_specs=[row, _full(w_kv.shape)],
        out_specs=[row, row],
        out_shape=[jax.ShapeDtypeStruct((n, D_MODEL), BF16)] * 2,
        compiler_params=_params(("parallel",)),
        name="xattn_kv",
    )(mem2d, w_kv)


def _route_tile(x, wh_ref, wm_ref, b_ref, tri_ref, cnt_ref):
    xh = x.astype(BF16)
    xm = (x - xh.astype(F32)).astype(BF16)
    logits = _dot(xh, wh_ref[...]) + _dot(xh, wm_ref[...]) + _dot(xm, wh_ref[...]) + b_ref[...]
    lane = lax.broadcasted_iota(jnp.int32, logits.shape, 1).astype(F32)
    is_group = (lane >= N_EXPERTS) & (lane < N_EXPERTS + N_GROUPS)
    g_max = jnp.max(jnp.where(is_group, logits, -jnp.inf), -1, keepdims=True)
    g_sum = jnp.sum(jnp.where(is_group, jnp.exp(logits - g_max), 0.0), -1, keepdims=True)
    g_p = 1.0 / g_sum
    g_idx = jnp.min(jnp.where(is_group & (logits == g_max), lane - N_EXPERTS, float(LANES)), -1, keepdims=True)
    in_group = (lane < N_EXPERTS) & (jnp.floor(lane * (1.0 / EXPERTS_PER_GROUP)) == g_idx)
    e_max = jnp.max(jnp.where(in_group, logits, -jnp.inf), -1, keepdims=True)
    e_exp = jnp.where(in_group, jnp.exp(logits - e_max), 0.0)
    prob = e_exp / jnp.sum(e_exp, -1, keepdims=True)
    cand = jnp.where(in_group, prob, -1.0)
    p1 = jnp.max(cand, -1, keepdims=True)
    i1 = jnp.min(jnp.where(cand == p1, lane, float(LANES)), -1, keepdims=True)
    cand2 = jnp.where(lane == i1, -1.0, cand)
    p2 = jnp.max(cand2, -1, keepdims=True)
    i2 = jnp.min(jnp.where(cand2 == p2, lane, float(LANES)), -1, keepdims=True)
    p_sum = p1 + p2
    gates = jnp.where(lane == i1, g_p * (p1 / p_sum), 0.0) + jnp.where(lane == i2, g_p * (p2 / p_sum), 0.0)
    onehot = jnp.where(lane == g_idx, 1.0, 0.0)
    before = _dot(tri_ref[...], onehot.astype(BF16)) + cnt_ref[...]
    rank = jnp.sum(onehot * before, -1, keepdims=True).astype(jnp.int32)
    cnt_ref[...] += jnp.sum(onehot, 0, keepdims=True)
    return gates, rank, g_idx.astype(jnp.int32)


def _xattn_kernel(og_ref, om_ref, oc_ref, x0_ref, wog_ref, wom_ref, woc_ref, g1_ref, b1_ref,
                  k_ref, v_ref, wq_ref, wo_ref, g_ref, b_ref, wrh_ref, wrm_ref, br_ref, tri_ref,
                  xg_ref, rank_ref, grp_ref, cnt_out_ref, cnt_ref):
    @pl.when((pl.program_id(0) == 0) & (pl.program_id(1) == 0))
    def _():
        cnt_ref[...] = jnp.zeros_like(cnt_ref)

    mix = _dot(og_ref[...], wog_ref[...]) + _dot(om_ref[...], wom_ref[...]) + _dot(oc_ref[...], woc_ref[...])
    x = _layer_norm(ALPHA * x0_ref[...] + mix, g1_ref[...], b1_ref[...])
    q =(_dot(x.astype(BF16), wq_ref[...]) * (XA_DH ** -0.5)).astype(BF16)
    heads = []
    for h in range(XA_HEADS):
        sl = slice(h * XA_DH, (h + 1) * XA_DH)
        s = _dot_nt(q[:, sl], k_ref[:, sl])
        p = jnp.exp(s - jnp.max(s, -1, keepdims=True))
        p = p / jnp.sum(p, -1, keepdims=True)
        heads.append(_dot(p.astype(BF16), v_ref[:, sl]).astype(BF16))
    out = _dot(jnp.concatenate(heads, axis=1), wo_ref[...])
    x2 = _layer_norm(ALPHA * x + out, g_ref[...], b_ref[...])
    gates, rank, grp = _route_tile(x2, wrh_ref, wrm_ref, br_ref, tri_ref, cnt_ref)
    xg_ref[:, 0:D_MODEL] = x2
    xg_ref[:, D_MODEL:] = gates
    rank_ref[...] = rank
    grp_ref[...] = grp
    cnt_out_ref[...] = cnt_ref[...]


def _out_xattn_route(og, om, oc, x0, wog, wom, woc, g1, b1, xk, xv, wq, wo, g, b, w_route_hi, w_route_mid, b_route,
                     batch, seq, mem_len, tm):
    n = x0.shape[0]
    nt = seq // tm
    tri = jnp.asarray(np.tril(np.ones((tm, tm), np.float32), -1), BF16)
    tile = lambda w: pl.BlockSpec((tm, w), lambda bi, i: (bi * nt + i, 0))
    mem = pl.BlockSpec((mem_len, D_MODEL), lambda bi, i: (bi, 0))
    return pl.pallas_call(
        _xattn_kernel,
        grid=(batch, nt),
        in_specs=[tile(MIX_W), tile(MIX_W), tile(MLA_HEADS * MLA_V), tile(D_MODEL),
                  _full(wog.shape), _full(wom.shape), _full(woc.shape), _full(g1.shape), _full(b1.shape),
                  mem, mem, _full(wq.shape), _full(wo.shape), _full(g.shape), _full(b.shape),
                  _full(w_route_hi.shape), _full(w_route_mid.shape), _full(b_route.shape), _full(tri.shape)],
        out_specs=[tile(XG_W), tile(1), tile(1), _full((1, LANES))],
        out_shape=[jax.ShapeDtypeStruct((n, XG_W), F32), jax.ShapeDtypeStruct((n, 1), jnp.int32),
                   jax.ShapeDtypeStruct((n, 1), jnp.int32), jax.ShapeDtypeStruct((1, LANES), F32)],
        scratch_shapes=[pltpu.VMEM((1, LANES), F32)],
        compiler_params=_params(("arbitrary", "arbitrary")),
        name="out_proj_xattn_route",
    )(og, om, oc, x0, wog, wom, woc, g1, b1, xk, xv, wq, wo, g, b, w_route_hi, w_route_mid, b_route, tri)


def _dispatch_kernel(pos_ref, x_ref, xs_hbm, sem, *, tm):
    def issue(t, carry):
        pltpu.make_async_copy(x_ref.at[pl.ds(t, 1)], xs_hbm.at[pl.ds(pos_ref[t], 1)], sem).start()
        return carry

    lax.fori_loop(0, tm, issue, 0, unroll=DMA_UNROLL)
    pltpu.make_async_copy(x_ref, xs_hbm.at[pl.ds(0, tm)], sem).wait()


def _dispatch(xg, pos, tm):
    n = xg.shape[0]
    return pl.pallas_call(
        functools.partial(_dispatch_kernel, tm=tm),
        grid=(n // tm,),
        in_specs=[pl.BlockSpec((tm,), lambda i: (i,), memory_space=pltpu.SMEM),
                  pl.BlockSpec((tm, XG_W), lambda i: (i, 0))],
        out_specs=pl.BlockSpec(memory_space=pl.ANY),
        out_shape=jax.ShapeDtypeStruct(xg.shape, xg.dtype),
        scratch_shapes=[pltpu.SemaphoreType.DMA(())],
        compiler_params=_params(("arbitrary",)),
        name="moe_dispatch",
    )(pos, xg)


def _moe_ffn_kernel(blk_ref, grp_ref, first_ref, valid_ref, xs_ref, wg_ref, wu_ref, wd_ref, o_ref, xb_ref):
    w = pl.program_id(0)
    e = pl.program_id(1)

    @pl.when((first_ref[w] == 1) & (e == 0))
    def _():
        xb_ref[...] = xs_ref[:, 0:D_MODEL].astype(BF16)
        o_ref[...] = jnp.zeros_like(o_ref)

    @pl.when(valid_ref[w] == 1)
    def _():
        xb = xb_ref[...]
        gates = xs_ref[:, D_MODEL:]
        lane = lax.broadcasted_iota(jnp.int32, gates.shape, 1)
        hidden = []
        for j in range(EXPERTS_PER_STEP):
            expert = grp_ref[w] * EXPERTS_PER_GROUP + e * EXPERTS_PER_STEP + j
            w_tok = jnp.sum(jnp.where(lane == expert, gates, 0.0), -1, keepdims=True)
            hg = _dot(xb, wg_ref[j].astype(BF16))
            hu = _dot(xb, wu_ref[j].astype(BF16))
            hidden.append((hg * jax.nn.sigmoid(hg) * hu * w_tok).astype(BF16))
        wd = wd_ref[...].astype(BF16).reshape(EXPERTS_PER_STEP * D_EXPERT, D_MODEL)
        o_ref[...] += _dot(jnp.concatenate(hidden, axis=1), wd)


def _moe_ffn(xs, items, w_gate, w_up, w_down, rb, e_base):
    n = xs.shape[0]
    blk, grp, first, valid = items

    steps = EXPERTS_PER_GROUP // EXPERTS_PER_STEP

    def expert(w, e, blk, grp, first, valid):
        return ((e_base + grp[w] * EXPERTS_PER_GROUP) // EXPERTS_PER_STEP
                + jnp.where(valid[w] == 1, e, steps - 1), 0, 0)

    grid_spec = pltpu.PrefetchScalarGridSpec(
        num_scalar_prefetch=4,
        grid=(blk.shape[0], steps),
        in_specs=[pl.BlockSpec((rb, XG_W), lambda w, e, blk, grp, first, valid: (blk[w], 0)),
                  pl.BlockSpec((EXPERTS_PER_STEP, D_MODEL, D_EXPERT), expert),
                  pl.BlockSpec((EXPERTS_PER_STEP, D_MODEL, D_EXPERT), expert),
                  pl.BlockSpec((EXPERTS_PER_STEP, D_EXPERT, D_MODEL), expert)],
        out_specs=pl.BlockSpec((rb, D_MODEL), lambda w, e, blk, grp, first, valid: (blk[w], 0)),
        scratch_shapes=[pltpu.VMEM((rb, D_MODEL), BF16)],
    )
    return pl.pallas_call(
        _moe_ffn_kernel,
        grid_spec=grid_spec,
        out_shape=jax.ShapeDtypeStruct((n, D_MODEL), F32),
        compiler_params=_params(("arbitrary", "arbitrary")),
        name="moe_experts",
    )(blk, grp, first, valid, xs, w_gate, w_up, w_down)


def _combine_kernel(pos_ref, pos_next_ref, xg_ref, ys_hbm, g_ref, b_ref, o_ref, buf_ref, sem, *, tm):
    i = pl.program_id(0)
    slot = i % 2

    def gather(p_ref, s):
        def issue(t, carry):
            pltpu.make_async_copy(ys_hbm.at[pl.ds(p_ref[t], 1)], buf_ref.at[s, pl.ds(t, 1)], sem.at[s]).start()
            return carry
        lax.fori_loop(0, tm, issue, 0, unroll=DMA_UNROLL)

    @pl.when(i == 0)
    def _():
        gather(pos_ref, slot)

    @pl.when(i + 1 < pl.num_programs(0))
    def _():
        gather(pos_next_ref, 1 - slot)

    pltpu.make_async_copy(ys_hbm.at[pl.ds(0, tm)], buf_ref.at[slot], sem.at[slot]).wait()
    o_ref[...] = _layer_norm(ALPHA * xg_ref[:, 0:D_MODEL] + buf_ref[slot], g_ref[...], b_ref[...])


def _combine(pos, xg, ys, g, b, tm):
    n = xg.shape[0]
    last = n // tm - 1
    return pl.pallas_call(
        functools.partial(_combine_kernel, tm=tm),
        grid=(n // tm,),
        in_specs=[pl.BlockSpec((tm,), lambda i: (i,), memory_space=pltpu.SMEM),
                  pl.BlockSpec((tm,), lambda i: (jnp.minimum(i + 1, last),), memory_space=pltpu.SMEM),
                  pl.BlockSpec((tm, XG_W), lambda i: (i, 0)),
                  pl.BlockSpec(memory_space=pl.ANY), _full(g.shape), _full(b.shape)],
        out_specs=pl.BlockSpec((tm, D_MODEL), lambda i: (i, 0)),
        out_shape=jax.ShapeDtypeStruct((n, D_MODEL), F32),
        scratch_shapes=[pltpu.VMEM((2, tm, D_MODEL), F32), pltpu.SemaphoreType.DMA((2,))],
        compiler_params=_params(("arbitrary",)),
        name="moe_combine_ln3",
    )(pos, pos, xg, ys, g, b)


def _moe_work_items(counts, n, rb):
    nb = n // rb
    n_items = nb + N_GROUPS - 1
    ends = jnp.cumsum(counts)
    start = jnp.arange(nb, dtype=jnp.int32) * rb
    g_lo = jnp.sum(ends[None, :] <= start[:, None], axis=1).astype(jnp.int32)
    g_hi = jnp.sum(ends[None, :] <= (start + rb - 1)[:, None], axis=1).astype(jnp.int32)
    per_blk = g_hi - g_lo + 1
    item0 = jnp.cumsum(per_blk) - per_blk
    w = jnp.arange(n_items, dtype=jnp.int32)
    valid = w < jnp.sum(per_blk)
    blk = jnp.clip(jnp.sum(item0[None, :] <= w[:, None], axis=1) - 1, 0, nb - 1).astype(jnp.int32)
    grp = jnp.where(valid, g_lo[blk] + (w - item0[blk]), g_hi[nb - 1]).astype(jnp.int32)
    first = (valid & (w == item0[blk])).astype(jnp.int32)
    return blk, grp, first, valid.astype(jnp.int32)


def _moe(xg, rank, grp, counts, w_gate, w_up, w_down, e_base, g, b, t_dma, rb):
    n = xg.shape[0]
    counts = counts[0, :N_GROUPS].astype(jnp.int32)
    offsets = jnp.cumsum(counts) - counts
    pos = (offsets[grp[:, 0]] + rank[:, 0]).astype(jnp.int32)
    xs = _dispatch(xg, pos, t_dma)
    ys = _moe_ffn(xs, _moe_work_items(counts, n, rb), w_gate, w_up, w_down, rb, e_base)
    return _combine(pos, xg, ys, g, b, t_dma)


def _pad_cols(w, width):
    return jnp.pad(w, ((0, 0), (0, width - w.shape[1])))


def _layer_weights(w_in, w_out, gla_w_a2, ml_b_i, ml_b_f, mla_w_uq, mla_w_ukv, moe_w_group, moe_b_group,
                   moe_w_router, moe_b_router):
    o = _IN_OFF
    half = MLA_ROPE // 2
    wg = _pad_cols(w_in[:, o[0]:o[5]], SEG_W).astype(BF16)
    wm = _pad_cols(w_in[:, o[5]:o[10]], SEG_W).astype(BF16)
    kr = w_in[:, o[12]:o[13]]
    zeros = lambda w: jnp.zeros((D_MODEL, w), F32)
    kra = jnp.concatenate([zeros(MLA_NOPE), kr, zeros(LANES - MLA_NOPE - MLA_ROPE)], 1)
    krb = jnp.concatenate([zeros(MLA_NOPE), kr[:, half:], kr[:, :half], zeros(LANES - MLA_NOPE - MLA_ROPE)], 1)
    wc = jnp.concatenate([w_in[:, o[10]:o[12]], kra, krb], 1).astype(BF16)
    wift = w_in[:, o[8]:o[10]].T.astype(BF16)
    wa2 = jnp.pad(gla_w_a2, ((0, LANES - GLA_GATE_RANK), (0, 0))).astype(BF16)
    b_gate = jnp.concatenate([ml_b_i, ml_b_f])
    b_col = jnp.pad(b_gate, (0, LANES - 2 * MIX_HEADS)).reshape(1, LANES)
    b_row = b_gate.reshape(2 * MIX_HEADS, 1)
    uq = mla_w_uq.reshape(MLA_Q_RANK, MLA_HEADS, MLA_NOPE + MLA_ROPE)
    zq = jnp.zeros((MLA_Q_RANK, MLA_HEADS, LANES - MLA_NOPE - MLA_ROPE), F32)
    wqa = jnp.concatenate([uq, zq], -1).reshape(MLA_Q_RANK, -1).astype(BF16)
    wqb = jnp.concatenate([jnp.zeros((MLA_Q_RANK, MLA_HEADS, MLA_NOPE), F32), uq[..., MLA_NOPE + half:],
                           uq[..., MLA_NOPE:MLA_NOPE + half], zq], -1).reshape(MLA_Q_RANK, -1).astype(BF16)
    ukv = mla_w_ukv.reshape(MLA_KV_RANK, MLA_HEADS, MLA_NOPE + MLA_V)
    wkn = jnp.concatenate([ukv[..., :MLA_NOPE], jnp.zeros((MLA_KV_RANK, MLA_HEADS, LANES - MLA_NOPE), F32)],
                          -1).reshape(MLA_KV_RANK, -1).astype(BF16)
    uv = ukv[..., MLA_NOPE:]
    zv = jnp.zeros_like(uv)
    odd = (jnp.arange(MLA_HEADS) % 2 == 1)[None, :, None]
    wv = jnp.concatenate([jnp.where(odd, zv, uv), jnp.where(odd, uv, zv)], -1).reshape(MLA_KV_RANK, -1).astype(BF16)
    wo = w_out.astype(BF16)
    w_route = _pad_cols(jnp.concatenate([moe_w_router, moe_w_group], 1), LANES)
    w_route_hi = w_route.astype(BF16)
    w_route_mid = (w_route - w_route_hi.astype(F32)).astype(BF16)
    b_route = jnp.pad(jnp.concatenate([moe_b_router, moe_b_group]), (0, LANES - N_EXPERTS - N_GROUPS)).reshape(1, LANES)
    return dict(wg=wg, wm=wm, wc=wc, wift=wift, wa2=wa2, b_col=b_col, b_row=b_row, wqa=wqa, wqb=wqb, wkn=wkn,
                wv=wv, wo_g=wo[0:MIX_W], wo_m=wo[MIX_W:2 * MIX_W], wo_c=wo[2 * MIX_W:], w_route_hi=w_route_hi, w_route_mid=w_route_mid,
                b_route=b_route)


def _tile(total, want):
    t = min(total, want)
    assert total % t == 0
    return t


def kernel(x, mem, positions, w_in, w_out, gla_w_a2, gla_b_a, gla_norm_g, ml_conv_w, ml_b_i, ml_b_f, ml_norm_g, mla_q_norm_g, mla_w_uq, mla_kv_norm_g, mla_w_ukv, xa_w_q, xa_w_kv, xa_w_o, moe_w_group, moe_b_group, moe_w_router, moe_b_router, moe_w_gate, moe_w_up, moe_w_down, ln1_g, ln1_b, ln2_g, ln2_b, ln3_g, ln3_b):
    batch, seq, _ = x.shape
    mem_len = mem.shape[1]
    n = batch * seq
    depth = w_in.shape[0]
    assert seq % CHUNK == 0
    t_in = _tile(n, 512)
    tm = _tile(n, 1024)
    ts = _tile(seq, 512)
    tq = _tile(seq, 512)
    t_moe = _tile(n, 1024)
    row = lambda a: a.reshape(1, -1)

    cos_t, sin_t = _rope_tables(positions.reshape(n, 1), _tile(n, 2048))
    mem2d = mem.reshape(batch * mem_len, D_MODEL)
    experts_gate = moe_w_gate.reshape(-1, D_MODEL, D_EXPERT)
    experts_up = moe_w_up.reshape(-1, D_MODEL, D_EXPERT)
    experts_down = moe_w_down.reshape(-1, D_EXPERT, D_MODEL)
    h = x.reshape(n, D_MODEL)
    for l in range(depth):
        w = _layer_weights(w_in[l], w_out[l], gla_w_a2[l], ml_b_i[l], ml_b_f[l], mla_w_uq[l], mla_w_ukv[l],
                           moe_w_group[l], moe_b_group[l], moe_w_router[l], moe_b_router[l])
        yg, ym, yift, q, k, v = _in_proj(h, w["wg"], w["wm"], w["wc"], w["wift"], cos_t, sin_t,
                                         row(mla_q_norm_g[l]), row(mla_kv_norm_g[l]),
                                         w["wqa"], w["wqb"], w["wkn"], w["wv"], t_in)
        gates_row = yift.reshape(2 * MIX_HEADS, n // CHUNK, CHUNK).transpose(1, 0, 2)
        og = _gla(yg, w["wa2"], row(gla_b_a[l]), row(gla_norm_g[l]), batch, seq, ts)
        om = _mlstm(ym, gates_row, ml_conv_w[l], w["b_col"], w["b_row"], row(ml_norm_g[l]), batch, seq, ts)
        oc = _mla_attn(q, k, v, batch, seq, tq)
        xk, xv = _xa_kv(mem2d, xa_w_kv[l].astype(BF16), mem_len)
        xg, rank, grp, counts = _out_xattn_route(
            og, om, oc, h, w["wo_g"], w["wo_m"], w["wo_c"], row(ln1_g[l]), row(ln1_b[l]),
            xk, xv, xa_w_q[l].astype(BF16), xa_w_o[l].astype(BF16), row(ln2_g[l]), row(ln2_b[l]),
            w["w_route_hi"], w["w_route_mid"], w["b_route"], batch, seq, mem_len, tm)
        h = _moe(xg, rank, grp, counts, experts_gate, experts_up, experts_down, l * N_EXPERTS,
                 row(ln3_g[l]), row(ln3_b[l]), t_moe, t_moe)
    return h.reshape(batch, seq, D_MODEL)
```

```python
import functools

import numpy as np
import jax
import jax.numpy as jnp
from jax import lax
from jax.experimental import pallas as pl
from jax.experimental.pallas import tpu as pltpu

F32 = jnp.float32
BF16 = jnp.bfloat16

D_MODEL = 1024
CHUNK = 64
HEAD_D = 64
MIX_HEADS = 4
MIX_W = MIX_HEADS * HEAD_D
GLA_GATE_RANK = 16
GLA_TAU = 16.0
MLSTM_CONV = 4
MLA_HEADS = 8
MLA_NOPE = 64
MLA_ROPE = 32
MLA_V = 64
MLA_Q_RANK = 256
MLA_KV_RANK = 128
MLA_QK_PAD = 128
ROPE_BASE = 10000.0
LOG2_E = 1.4426950408889634
MLA_HEAD_PAIR = 2
XA_HEADS = 4
XA_DH = D_MODEL // XA_HEADS
N_GROUPS = 4
EXPERTS_PER_GROUP = 8
N_EXPERTS = N_GROUPS * EXPERTS_PER_GROUP
D_EXPERT = 256
DEPTH = 2
ALPHA = (2 * DEPTH) ** 0.25
LN_EPS = 1e-5
LANES = 128
SUBLANES = 8
MIXER_UNROLL = 8
MLSTM_UNROLL = 4
SEG_V = 2 * MIX_W
SEG_GATE = 3 * MIX_W
SEG_TAIL = 4 * MIX_W
SEG_W = SEG_TAIL + LANES
MLA_SEG_W = MLA_Q_RANK + MLA_KV_RANK + 2 * LANES
XG_W = D_MODEL + LANES
EXPERTS_PER_STEP = 4
DMA_UNROLL = 32
GLA_LEVELS = (32, 16, 8, 4, 2, 1)
VMEM_LIMIT = 56 * 1024 * 1024

_IN_SIZES = (256, 256, 256, 256, GLA_GATE_RANK, 512, 256, 256, 4, 4, MLA_Q_RANK, MLA_KV_RANK, MLA_ROPE)
_IN_OFF = np.concatenate([[0], np.cumsum(_IN_SIZES)]).tolist()


def _params(sem):
    return pltpu.CompilerParams(dimension_semantics=sem, vmem_limit_bytes=VMEM_LIMIT)


def _full(shape):
    return pl.BlockSpec(shape, lambda *_: (0,) * len(shape))


def _layer_norm(x, g, b):
    mu = jnp.mean(x, -1, keepdims=True)
    xc = x - mu
    var = jnp.mean(xc * xc, -1, keepdims=True)
    return xc * lax.rsqrt(var + LN_EPS) * g + b


def _log_sigmoid(z):
    return jnp.minimum(z, 0.0) - jnp.log1p(jnp.exp(-jnp.abs(z)))


def _dot_nt(a, b):
    return lax.dot_general(a, b, (((1,), (1,)), ((), ())), preferred_element_type=F32)


def _dot_tn(a, b):
    return lax.dot_general(a, b, (((0,), (0,)), ((), ())), preferred_element_type=F32)


def _dot(a, b):
    return jnp.dot(a, b, preferred_element_type=F32)


def _head_norm(o, g):
    mu = jnp.mean(o, -1, keepdims=True)
    oc = o - mu
    var = jnp.mean(oc * oc, -1, keepdims=True)
    return oc * lax.rsqrt(var + LN_EPS) * g


def _in_proj_kernel(x_ref, wg_ref, wm_ref, wc_ref, wift_ref, cos_ref, sin_ref, gq_ref, gkv_ref,
                    wqa_ref, wqb_ref, wkn_ref, wv_ref, ones_ref,
                    yg_ref, ym_ref, yift_ref, q_ref, k_ref, v_ref, yc_ref):
    xb = x_ref[...].astype(BF16)
    yg_ref[...] = _dot(xb, wg_ref[...])
    ym_ref[...] = _dot(xb, wm_ref[...])
    yift_ref[...] = _dot_nt(wift_ref[...], xb)
    yc_ref[...] = _dot(xb, wc_ref[...])
    _mla_prep_tile(yc_ref, cos_ref, sin_ref, gq_ref, gkv_ref, wqa_ref, wqb_ref, wkn_ref, wv_ref, ones_ref,
                   q_ref, k_ref, v_ref)


def _in_proj(x2d, wg, wm, wc, wift, cos_t, sin_t, gq, gkv, wqa, wqb, wkn, wv, tm):
    n = x2d.shape[0]
    ones_row = np.zeros((1, MLA_HEADS * LANES), np.float32)
    for h in range(MLA_HEADS):
        ones_row[0, h * LANES + (MLA_V if h % 2 == 0 else 0)] = 1.0
    ones_row = jnp.asarray(ones_row)
    row = lambda w: pl.BlockSpec((tm, w), lambda i: (i, 0))
    qk_w = MLA_HEADS * MLA_QK_PAD
    return pl.pallas_call(
        _in_proj_kernel,
        grid=(n // tm,),
        in_specs=[row(D_MODEL), _full(wg.shape), _full(wm.shape), _full(wc.shape), _full(wift.shape),
                  row(LANES), row(LANES), _full(gq.shape), _full(gkv.shape),
                  _full(wqa.shape), _full(wqb.shape), _full(wkn.shape), _full(wv.shape), _full(ones_row.shape)],
        out_specs=[row(SEG_W), row(SEG_W), pl.BlockSpec((8, tm), lambda i: (0, i)),
                   row(qk_w), row(qk_w), row(MLA_HEADS * LANES)],
        out_shape=[jax.ShapeDtypeStruct((n, SEG_W), F32), jax.ShapeDtypeStruct((n, SEG_W), F32),
                   jax.ShapeDtypeStruct((8, n), F32),
                   jax.ShapeDtypeStruct((n, qk_w), BF16), jax.ShapeDtypeStruct((n, qk_w), BF16),
                   jax.ShapeDtypeStruct((n, MLA_HEADS * LANES), BF16)],
        scratch_shapes=[pltpu.VMEM((tm, MLA_SEG_W), F32)],
        compiler_params=_params(("parallel",)),
        name="in_proj_mla_prep",
    )(x2d, wg, wm, wc, wift, cos_t, sin_t, gq, gkv, wqa, wqb, wkn, wv, ones_row)


def _split3(x):
    hi = x.astype(BF16)
    r1 = x - hi.astype(F32)
    mid = r1.astype(BF16)
    lo = (r1 - mid.astype(F32)).astype(BF16)
    return hi, mid, lo


def _cumsum_rows(tri, x):
    hi, mid, lo = _split3(x)
    return _dot(tri, hi) + _dot(tri, mid) + _dot(tri, lo)


def _gla_constants():
    t = np.arange(CHUNK)
    n_lv = len(GLA_LEVELS)
    masks = np.zeros((n_lv + 1, CHUNK, CHUNK), np.float32)
    right = np.zeros((n_lv, CHUNK, 1), np.float32)
    for li, n in enumerate(GLA_LEVELS):
        blk = t // (2 * n)
        is_right = (t % (2 * n)) >= n
        masks[li] = ((blk[:, None] == blk[None, :]) & is_right[:, None] & ~is_right[None, :])
        right[li, :, 0] = is_right
    masks[n_lv] = np.eye(CHUNK)
    return masks, right


def _gla_level_exponents(b, log_a, right_ref):
    row = lax.broadcasted_iota(jnp.int32, (CHUNK, 1), 0)
    exps = []
    for li, n in enumerate(GLA_LEVELS):
        if n >= SUBLANES // 2:
            per_blk = max(2 * n // SUBLANES, 1)
            b4 = b.reshape(CHUNK // (per_blk * SUBLANES), per_blk, SUBLANES, MIX_W)
            r_reg, r_sub = divmod(n - 1, SUBLANES)
            src = b4[:, r_reg:r_reg + 1, r_sub:r_sub + 1, :]
            b_r = jnp.broadcast_to(src, b4.shape).reshape(CHUNK, MIX_W)
            exps.append(jnp.where(right_ref[li] > 0.0, b - b_r, b_r - b))
        elif n == 2:
            pos = row % 4
            nxt = pltpu.roll(log_a, CHUNK - 1, 0)
            prv = pltpu.roll(log_a, 1, 0)
            exps.append(jnp.where(pos == 0, nxt, jnp.where(pos == 1, 0.0,
                                                            jnp.where(pos == 2, log_a, log_a + prv))))
        else:
            exps.append(jnp.where(right_ref[li] > 0.0, log_a, 0.0))
    return exps


def _pair_blocks(z, low_half):
    zero = jnp.zeros_like(z)
    return jnp.concatenate([jnp.where(low_half, z, zero), jnp.where(low_half, zero, z)], axis=0)


def _gla_kernel(y_ref, wa2_ref, ba_ref, g_ref, tri_ref, mask_ref, right_ref, o_ref, st_ref, *, ts, unroll):
    n_lv = len(GLA_LEVELS)
    pairs = MIX_HEADS // 2
    low_half = lax.broadcasted_iota(jnp.int32, (CHUNK, LANES), 1) < HEAD_D
    same_head = (lax.broadcasted_iota(jnp.int32, (LANES, LANES), 0) < HEAD_D) == (
        lax.broadcasted_iota(jnp.int32, (LANES, LANES), 1) < HEAD_D)

    @pl.when(pl.program_id(1) == 0)
    def _():
        st_ref[...] = jnp.zeros_like(st_ref)

    def chunk(c, carry):
        rows = pl.ds(pl.multiple_of(c * CHUNK, CHUNK), CHUNK)
        q = y_ref[rows, 0:MIX_W] * (HEAD_D ** -0.5)
        k = y_ref[rows, MIX_W:SEG_V]
        a_lr = y_ref[rows, SEG_TAIL:SEG_W]
        z = _dot(a_lr.astype(BF16), wa2_ref[...]) + ba_ref[...]
        log_a = _log_sigmoid(z) * (1.0 / GLA_TAU)
        b = _cumsum_rows(tri_ref[...], log_a)
        b_end = b[CHUNK - 1:CHUNK, :]
        scores = [jnp.zeros((CHUNK, LANES), F32) for _ in range(pairs)]
        for li, e in enumerate(_gla_level_exponents(b, log_a, right_ref)):
            x = (jnp.where(right_ref[li] > 0.0, q, k) * jnp.exp(e)).astype(BF16)
            for p in range(pairs):
                x2 = x[:, p * LANES:(p + 1) * LANES]
                scores[p] = scores[p] + _dot_nt(x2, _pair_blocks(x2, low_half)) * mask_ref[li]
        qb = q.astype(BF16)
        kb = k.astype(BF16)
        q_in = (q * jnp.exp(b)).astype(BF16)
        k_out = (k * jnp.exp(b_end - b)).astype(BF16)
        dec_end = jnp.exp(b_end)
        for p in range(pairs):
            pl_ = slice(p * LANES, (p + 1) * LANES)
            a = scores[p] + _dot_nt(qb[:, pl_], _pair_blocks(kb[:, pl_], low_half)) * mask_ref[n_lv]
            v2 = y_ref[rows, SEG_V + p * LANES:SEG_V + (p + 1) * LANES].astype(BF16)
            st = st_ref[p]
            o = _dot(a.astype(BF16), _pair_blocks(v2, low_half)) + _dot_nt(q_in[:, pl_], st.astype(BF16))
            st_ref[p] = st * dec_end[:, pl_] + jnp.where(same_head, _dot_tn(v2, k_out[:, pl_]), 0.0)
            normed = jnp.zeros((CHUNK, LANES), F32)
            for mine in (low_half, jnp.logical_not(low_half)):
                mu = jnp.sum(jnp.where(mine, o, 0.0), -1, keepdims=True) * (1.0 / HEAD_D)
                cen = jnp.where(mine, o - mu, 0.0)
                var = jnp.sum(cen * cen, -1, keepdims=True) * (1.0 / HEAD_D)
                normed = normed + cen * lax.rsqrt(var + LN_EPS)
            r_gate = y_ref[rows, SEG_GATE + p * LANES:SEG_GATE + (p + 1) * LANES]
            o_ref[rows, pl_] = (normed * g_ref[:, pl_] * (r_gate * jax.nn.sigmoid(r_gate))).astype(o_ref.dtype)
        return carry

    lax.fori_loop(0, ts // CHUNK, chunk, 0, unroll=unroll)


def _gla(yg, wa2, ba, g, batch, seq, ts):
    n = yg.shape[0]
    nt = seq // ts
    masks, right = _gla_constants()
    masks = np.concatenate([masks, masks], axis=-1)
    tri = jnp.asarray(np.tril(np.ones((CHUNK, CHUNK), np.float32)), BF16)
    return pl.pallas_call(
        functools.partial(_gla_kernel, ts=ts, unroll=MIXER_UNROLL),
        grid=(batch, nt),
        in_specs=[pl.BlockSpec((ts, SEG_W), lambda b, i: (b * nt + i, 0)),
                  _full(wa2.shape), _full(ba.shape), _full(g.shape),
                  _full(tri.shape), _full(masks.shape), _full(right.shape)],
        out_specs=pl.BlockSpec((ts, MIX_W), lambda b, i: (b * nt + i, 0)),
        out_shape=jax.ShapeDtypeStruct((n, MIX_W), BF16),
        scratch_shapes=[pltpu.VMEM((MIX_HEADS // 2, LANES, LANES), F32)],
        compiler_params=_params(("parallel", "arbitrary")),
        name="gla_mixer",
    )(yg, wa2, ba, g, tri, jnp.asarray(masks), jnp.asarray(right))


def _mlstm_kernel(y_ref, gr_ref, cw_ref, bcol_ref, brow_ref, g_ref, tri_ref, eb_ref, eye_ref, o_ref,
                  xe_ref, qk_ref, vx_ref, fcl_ref, wsl_ref, fcb_ref, wsb_ref, rv_ref, dec_ref,
                  w_ref, em_ref, upd_ref, c_ref, m_ref, *, ts):
    first = pl.program_id(1) == 0

    @pl.when(first)
    def _():
        xe_ref[0:8, :] = jnp.zeros((8, 2 * MIX_W), F32)
        c_ref[...] = jnp.zeros_like(c_ref)
        m_ref[...] = jnp.zeros_like(m_ref)

    @pl.when(jnp.logical_not(first))
    def _():
        xe_ref[0:8, :] = xe_ref[ts:ts + 8, :]

    xe_ref[8:ts + 8, :] = y_ref[:, 0:2 * MIX_W]
    for c in range(ts // CHUNK):
        r0 = 8 + c * CHUNK
        conv = cw_ref[MLSTM_CONV - 1:MLSTM_CONV, :] * xe_ref[r0:r0 + CHUNK, :]
        for j in range(MLSTM_CONV - 1):
            lo = r0 - (MLSTM_CONV - 1) + j
            conv = conv + cw_ref[j:j + 1, :] * xe_ref[lo:lo + CHUNK, :]
        qk_ref[c * CHUNK:(c + 1) * CHUNK, :] = conv * jax.nn.sigmoid(conv)

    ones_col = (lax.broadcasted_iota(jnp.int32, (ts, HEAD_D), 1) == 0).astype(BF16)
    for h in range(MIX_HEADS):
        v_at, one_at = (0, HEAD_D) if h % 2 == 0 else (HEAD_D, 0)
        vx_ref[:, h * LANES + v_at:h * LANES + v_at + HEAD_D] = (
            y_ref[:, SEG_V + h * HEAD_D:SEG_V + (h + 1) * HEAD_D].astype(BF16))
        vx_ref[:, h * LANES + one_at:h * LANES + one_at + HEAD_D] = ones_col

    tri = tri_ref[...]
    expand = eb_ref[...]
    nck = ts // CHUNK
    lane = lax.broadcasted_iota(jnp.int32, (1, LANES), 1)
    f_lanes = (lane >= MIX_HEADS) & (lane < 2 * MIX_HEADS)
    head_lane = (lax.broadcasted_iota(jnp.int32, (MIX_HEADS, LANES), 1)
                 == lax.broadcasted_iota(jnp.int32, (MIX_HEADS, LANES), 0) + MIX_HEADS)

    def to_lanes(col):
        return jnp.sum(jnp.where(head_lane, col, 0.0), 0, keepdims=True)

    g_rows = gr_ref[...] + brow_ref[...]
    ls_rows = _log_sigmoid(g_rows).reshape(nck * SUBLANES, CHUNK)
    fcum_rows = sum(_dot_nt(part, tri) for part in _split3(ls_rows)).reshape(nck, SUBLANES, CHUNK)
    m_col = m_ref[0:MIX_HEADS, 0:1]
    for c in range(nck):
        g_col = y_ref[c * CHUNK:(c + 1) * CHUNK, SEG_TAIL:SEG_W] + bcol_ref[...]
        fcum_col = jnp.where(f_lanes, _cumsum_rows(tri, _log_sigmoid(g_col)), 0.0)
        fcl_ref[c * CHUNK:(c + 1) * CHUNK, :] = fcum_col
        g_row = g_rows[c]
        fcum_row = fcum_rows[c]
        f_row = fcum_row[MIX_HEADS:2 * MIX_HEADS, :]
        i_row = g_row[0:MIX_HEADS, :]
        f_end = f_row[:, CHUNK - 1:CHUNK]
        rv_ref[c, 0:MIX_HEADS, :] = jnp.concatenate(
            [i_row - f_row, jnp.broadcast_to(m_col, (MIX_HEADS, HEAD_D))], axis=1)
        m_new = jnp.maximum(f_end + m_col, jnp.max(f_end - f_row + i_row, -1, keepdims=True))
        m_new_l = to_lanes(m_new)
        f_end_l = fcum_col[CHUNK - 1:CHUNK, :]
        i_shift = pltpu.roll(g_col, MIX_HEADS, 1)
        wsl_ref[c * CHUNK:(c + 1) * CHUNK, :] = jnp.where(
            f_lanes, jnp.exp(f_end_l - fcum_col + i_shift - m_new_l), 0.0)
        dec_ref[c] = jnp.broadcast_to(jnp.exp(f_end_l + to_lanes(m_col) - m_new_l), (SUBLANES, LANES))
        m_col = m_new
    m_ref[0:MIX_HEADS, :] = jnp.broadcast_to(m_col, (MIX_HEADS, LANES))
    slab = 2 * CHUNK
    for i in range(ts // slab):
        sr = slice(i * slab, (i + 1) * slab)
        fcb_ref[sr, :] = sum(_dot(part, expand) for part in _split3(fcl_ref[sr, :]))
        ws_hi, ws_mid, _ = _split3(wsl_ref[sr, :])
        wsb_ref[sr, :] = _dot(ws_hi, expand) + _dot(ws_mid, expand)

    t_idx = lax.broadcasted_iota(jnp.int32, (CHUNK, LANES), 0)
    s_idx = lax.broadcasted_iota(jnp.int32, (CHUNK, LANES), 1)
    keep = (s_idx <= t_idx) | (s_idx >= CHUNK)
    low_half = s_idx < HEAD_D

    def weights(c, carry):
        rows = pl.ds(pl.multiple_of(c * CHUNK, CHUNK), CHUNK)
        rv = rv_ref[c, 0:MIX_HEADS, :]
        for p in range(MIX_HEADS // 2):
            q2 = (qk_ref[rows, p * LANES:(p + 1) * LANES] * (HEAD_D ** -0.5)).astype(BF16)
            k2 = qk_ref[rows, MIX_W + p * LANES:MIX_W + (p + 1) * LANES]
            for h in (2 * p, 2 * p + 1):
                mine = low_half if h % 2 == 0 else jnp.logical_not(low_half)
                hb = slice(h * LANES, (h + 1) * LANES)
                logw = jnp.where(keep, fcb_ref[rows, hb] + rv[h:h + 1, :], -jnp.inf)
                m_t = jnp.max(logw, -1, keepdims=True)
                k_h = jnp.where(mine, k2, 0.0)
                qk = _dot_nt(q2, jnp.concatenate([k_h.astype(BF16), eye_ref[h % 2]], axis=0))
                w_ref[c * MIX_HEADS + h] = (jnp.exp(logw - m_t) * qk).astype(BF16)
                em_ref[c * MIX_HEADS + h] = jnp.exp(-m_t)
                upd = _dot_tn((k_h * wsb_ref[rows, hb]).astype(BF16), vx_ref[rows, hb])
                off = (h % 2) * HEAD_D
                upd_ref[c * MIX_HEADS + h] = upd[off:off + HEAD_D, :]
        return carry

    lax.fori_loop(0, nck, weights, 0, unroll=MLSTM_UNROLL)

    def chunk(c, carry):
        rows = pl.ds(pl.multiple_of(c * CHUNK, CHUNK), CHUNK)
        dec_l = dec_ref[c]
        for p in range(MIX_HEADS // 2):
            pl_ = slice(p * LANES, (p + 1) * LANES)
            pair = jnp.zeros((CHUNK, LANES), F32)
            for h in (2 * p, 2 * p + 1):
                mine = low_half if h % 2 == 0 else jnp.logical_not(low_half)
                c_st = c_ref[h]
                num = _dot(w_ref[c * MIX_HEADS + h],
                           jnp.concatenate([vx_ref[rows, h * LANES:(h + 1) * LANES], c_st.astype(BF16)], axis=0))
                c_ref[h] = dec_l[0:1, MIX_HEADS + h:MIX_HEADS + h + 1] * c_st + upd_ref[c * MIX_HEADS + h]
                den = num[:, HEAD_D:HEAD_D + 1] if h % 2 == 0 else num[:, 0:1]
                r = 1.0 / jnp.maximum(jnp.abs(den), em_ref[c * MIX_HEADS + h])
                mu = jnp.sum(jnp.where(mine, num, 0.0), -1, keepdims=True) * (1.0 / HEAD_D)
                cen = jnp.where(mine, num - mu, 0.0)
                var = jnp.sum(cen * cen, -1, keepdims=True) * (1.0 / HEAD_D)
                pair = pair + cen * (r * lax.rsqrt(r * r * var + LN_EPS))
            o_gate = y_ref[rows, SEG_GATE + p * LANES:SEG_GATE + (p + 1) * LANES]
            o_ref[rows, pl_] = (pair * g_ref[:, pl_] * jax.nn.sigmoid(o_gate)).astype(o_ref.dtype)
        return carry

    lax.fori_loop(0, nck, chunk, 0, unroll=MLSTM_UNROLL // 2)


def _mlstm(ym, gates_row, conv_w, b_col, b_row, g, batch, seq, ts):
    n = ym.shape[0]
    nt = seq // ts
    nck = ts // CHUNK
    tri = jnp.asarray(np.tril(np.ones((CHUNK, CHUNK), np.float32)), BF16)
    expand = np.zeros((LANES, MIX_HEADS * LANES), np.float32)
    for h in range(MIX_HEADS):
        expand[MIX_HEADS + h, h * LANES:(h + 1) * LANES] = 1.0
    eye = np.zeros((2, HEAD_D, LANES), np.float32)
    eye[0, :, 0:HEAD_D] = np.eye(HEAD_D)
    eye[1, :, HEAD_D:] = np.eye(HEAD_D)
    expand = jnp.asarray(expand, BF16)
    eye = jnp.asarray(eye, BF16)
    return pl.pallas_call(
        functools.partial(_mlstm_kernel, ts=ts),
        grid=(batch, nt),
        in_specs=[pl.BlockSpec((ts, SEG_W), lambda b, i: (b * nt + i, 0)),
                  pl.BlockSpec((nck, 8, CHUNK), lambda b, i: (b * nt + i, 0, 0)),
                  _full(conv_w.shape), _full(b_col.shape), _full(b_row.shape), _full(g.shape),
                  _full(tri.shape), _full(expand.shape), _full(eye.shape)],
        out_specs=pl.BlockSpec((ts, MIX_W), lambda b, i: (b * nt + i, 0)),
        out_shape=jax.ShapeDtypeStruct((n, MIX_W), BF16),
        scratch_shapes=[pltpu.VMEM((ts + 8, 2 * MIX_W), F32),
                        pltpu.VMEM((ts, 2 * MIX_W), F32),
                        pltpu.VMEM((ts, MIX_HEADS * LANES), BF16),
                        pltpu.VMEM((ts, LANES), F32),
                        pltpu.VMEM((ts, LANES), F32),
                        pltpu.VMEM((ts, MIX_HEADS * LANES), F32),
                        pltpu.VMEM((ts, MIX_HEADS * LANES), F32),
                        pltpu.VMEM((nck, SUBLANES, LANES), F32),
                        pltpu.VMEM((nck, SUBLANES, LANES), F32),
                        pltpu.VMEM((nck * MIX_HEADS, CHUNK, LANES), BF16),
                        pltpu.VMEM((nck * MIX_HEADS, CHUNK, 1), F32),
                        pltpu.VMEM((nck * MIX_HEADS, HEAD_D, LANES), F32),
                        pltpu.VMEM((MIX_HEADS, HEAD_D, LANES), F32),
                        pltpu.VMEM((SUBLANES, LANES), F32)],
        compiler_params=_params(("parallel", "arbitrary")),
        name="mlstm_mixer",
    )(ym, gates_row, conv_w, b_col, b_row, g, tri, expand, eye)


def _rope_table_kernel(pos_ref, inv_ref, cos_ref, sin_ref):
    ang = pos_ref[...].astype(F32) * inv_ref[...]
    lane = lax.broadcasted_iota(jnp.int32, ang.shape, 1)
    rot = (lane >= MLA_NOPE) & (lane < MLA_NOPE + MLA_ROPE)
    first_half = lane < MLA_NOPE + MLA_ROPE // 2
    cos_ref[...] = jnp.where(lane < MLA_NOPE, 1.0, jnp.where(rot, jnp.cos(ang), 0.0))
    s = jnp.sin(ang)
    sin_ref[...] = jnp.where(rot, jnp.where(first_half, -s, s), 0.0)


def _rope_tables(pos_col, tm):
    n = pos_col.shape[0]
    half = MLA_ROPE // 2
    inv = ROPE_BASE ** (-np.arange(half, dtype=np.float32) / half)
    inv_row = np.zeros((1, LANES), np.float32)
    inv_row[0, MLA_NOPE:MLA_NOPE + half] = inv
    inv_row[0, MLA_NOPE + half:MLA_NOPE + MLA_ROPE] = inv
    return pl.pallas_call(
        _rope_table_kernel,
        grid=(n // tm,),
        in_specs=[pl.BlockSpec((tm, 1), lambda i: (i, 0)), _full((1, LANES))],
        out_specs=[pl.BlockSpec((tm, LANES), lambda i: (i, 0))] * 2,
        out_shape=[jax.ShapeDtypeStruct((n, LANES), F32)] * 2,
        compiler_params=_params(("parallel",)),
        name="rope_tables",
    )(pos_col, jnp.asarray(inv_row))


def _mla_prep_tile(y_ref, cos_ref, sin_ref, gq_ref, gkv_ref, wqa_ref, wqb_ref, wkn_ref, wv_ref, ones_ref,
                   q_ref, k_ref, v_ref):
    def rms(x, g):
        return x * lax.rsqrt(jnp.mean(x * x, -1, keepdims=True) + LN_EPS) * g

    cos = cos_ref[...]
    sin = sin_ref[...]
    cq = rms(y_ref[:, 0:MLA_Q_RANK], gq_ref[...]).astype(BF16)
    ckv = rms(y_ref[:, MLA_Q_RANK:MLA_Q_RANK + MLA_KV_RANK], gkv_ref[...]).astype(BF16)
    kr = MLA_Q_RANK + MLA_KV_RANK
    k_rope = y_ref[:, kr:kr + LANES] * cos + y_ref[:, kr + LANES:MLA_SEG_W] * sin
    qa = _dot(cq, wqa_ref[...])
    qb = _dot(cq, wqb_ref[...])
    kn = _dot(ckv, wkn_ref[...])
    scale = (MLA_NOPE + MLA_ROPE) ** -0.5 * LOG2_E
    for h in range(MLA_HEADS):
        sl = slice(h * MLA_QK_PAD, (h + 1) * MLA_QK_PAD)
        q_ref[:, sl] = ((qa[:, sl] * cos + qb[:, sl] * sin) * scale).astype(q_ref.dtype)
        k_ref[:, sl] = (kn[:, sl] + k_rope).astype(k_ref.dtype)
    v_ref[...] = (_dot(ckv, wv_ref[...]) + ones_ref[...]).astype(v_ref.dtype)


def _mla_attn_kernel(q_ref, k_ref, v_ref, o_ref, *, seq, tq):
    t_chunk = lax.broadcasted_iota(jnp.int32, (tq, tq), 0) // CHUNK
    s_chunk = lax.broadcasted_iota(jnp.int32, (tq, tq), 1) // CHUNK
    diag_mask = s_chunk <= t_chunk
    low_half = lax.broadcasted_iota(jnp.int32, (tq, LANES), 1) < MLA_V
    for i in range(seq // tq):
        rows = slice(i * tq, (i + 1) * tq)
        res = []
        for hh in range(MLA_HEAD_PAIR):
            ql = slice(hh * MLA_QK_PAD, (hh + 1) * MLA_QK_PAD)
            vl = slice(hh * LANES, (hh + 1) * LANES)
            q = q_ref[0, rows, ql]
            s_d = jnp.where(diag_mask, _dot_nt(q, k_ref[0, rows, ql]), -jnp.inf)
            m = jnp.max(s_d, -1, keepdims=True)
            if i > 0:
                s_o = _dot_nt(q, k_ref[0, 0:i * tq, ql])
                m = jnp.maximum(m, jnp.max(s_o, -1, keepdims=True))
            o = _dot(jnp.exp2(s_d - m).astype(BF16), v_ref[0, rows, vl])
            if i > 0:
                o = o + _dot(jnp.exp2(s_o - m).astype(BF16), v_ref[0, 0:i * tq, vl])
            l = o[:, MLA_V:MLA_V + 1] if hh == 0 else o[:, 0:1]
            res.append(o / l)
        o_ref[0, rows, :] = jnp.where(low_half, res[0], res[1]).astype(o_ref.dtype)


def _mla_attn(q, k, v, batch, seq, tq):
    n = q.shape[0]
    qk_w = MLA_HEADS * MLA_QK_PAD
    v_w = MLA_HEADS * MLA_V
    pair = lambda w: pl.BlockSpec((1, seq, MLA_HEAD_PAIR * w), lambda b, h: (b, 0, h))
    out = pl.pallas_call(
        functools.partial(_mla_attn_kernel, seq=seq, tq=tq),
        grid=(batch, MLA_HEADS // MLA_HEAD_PAIR),
        in_specs=[pair(MLA_QK_PAD), pair(MLA_QK_PAD), pair(LANES)],
        out_specs=pair(MLA_V),
        out_shape=jax.ShapeDtypeStruct((batch, seq, v_w), BF16),
        compiler_params=_params(("parallel", "parallel")),
        name="mla_attention",
    )(q.reshape(batch, seq, qk_w), k.reshape(batch, seq, qk_w), v.reshape(batch, seq, MLA_HEADS * LANES))
    return out.reshape(n, v_w)


def _xa_kv_kernel(mem_ref, w_ref, k_ref, v_ref):
    kv = _dot(mem_ref[...].astype(BF16), w_ref[...])
    k_ref[...] = kv[:, 0:D_MODEL].astype(k_ref.dtype)
    v_ref[...] = kv[:, D_MODEL:2 * D_MODEL].astype(v_ref.dtype)


def _xa_kv(mem2d, w_kv, mem_len):
    n = mem2d.shape[0]
    row = pl.BlockSpec((mem_len, D_MODEL), lambda i: (i, 0))
    return pl.pallas_call(
        _xa_kv_kernel,
        grid=(n // mem_len,),
        in_specs=[row, _full(w_kv.shape)],
        out_specs=[row, row],
        out_shape=[jax.ShapeDtypeStruct((n, D_MODEL), BF16)] * 2,
        compiler_params=_params(("parallel",)),
        name="xattn_kv",
    )(mem2d, w_kv)


def _route_tile(x, wh_ref, wm_ref, b_ref, tri_ref, cnt_ref):
    xh = x.astype(BF16)
    xm = (x - xh.astype(F32)).astype(BF16)
    logits = _dot(xh, wh_ref[...]) + _dot(xh, wm_ref[...]) + _dot(xm, wh_ref[...]) + b_ref[...]
    lane = lax.broadcasted_iota(jnp.int32, logits.shape, 1).astype(F32)
    is_group = (lane >= N_EXPERTS) & (lane < N_EXPERTS + N_GROUPS)
    g_max = jnp.max(jnp.where(is_group, logits, -jnp.inf), -1, keepdims=True)
    g_sum = jnp.sum(jnp.where(is_group, jnp.exp(logits - g_max), 0.0), -1, keepdims=True)
    g_p = 1.0 / g_sum
    g_idx = jnp.min(jnp.where(is_group & (logits == g_max), lane - N_EXPERTS, float(LANES)), -1, keepdims=True)
    in_group = (lane < N_EXPERTS) & (jnp.floor(lane * (1.0 / EXPERTS_PER_GROUP)) == g_idx)
    e_max = jnp.max(jnp.where(in_group, logits, -jnp.inf), -1, keepdims=True)
    e_exp = jnp.where(in_group, jnp.exp(logits - e_max), 0.0)
    prob = e_exp / jnp.sum(e_exp, -1, keepdims=True)
    cand = jnp.where(in_group, prob, -1.0)
    p1 = jnp.max(cand, -1, keepdims=True)
    i1 = jnp.min(jnp.where(cand == p1, lane, float(LANES)), -1, keepdims=True)
    cand2 = jnp.where(lane == i1, -1.0, cand)
    p2 = jnp.max(cand2, -1, keepdims=True)
    i2 = jnp.min(jnp.where(cand2 == p2, lane, float(LANES)), -1, keepdims=True)
    p_sum = p1 + p2
    gates = jnp.where(lane == i1, g_p * (p1 / p_sum), 0.0) + jnp.where(lane == i2, g_p * (p2 / p_sum), 0.0)
    onehot = jnp.where(lane == g_idx, 1.0, 0.0)
    before = _dot(tri_ref[...], onehot.astype(BF16)) + cnt_ref[...]
    rank = jnp.sum(onehot * before, -1, keepdims=True).astype(jnp.int32)
    cnt_ref[...] += jnp.sum(onehot, 0, keepdims=True)
    return gates, rank, g_idx.astype(jnp.int32)


def _xattn_kernel(og_ref, om_ref, oc_ref, x0_ref, wog_ref, wom_ref, woc_ref, g1_ref, b1_ref,
                  k_ref, v_ref, wq_ref, wo_ref, g_ref, b_ref, wrh_ref, wrm_ref, br_ref, tri_ref,
                  xg_ref, rank_ref, grp_ref, cnt_out_ref, cnt_ref):
    @pl.when((pl.program_id(0) == 0) & (pl.program_id(1) == 0))
    def _():
        cnt_ref[...] = jnp.zeros_like(cnt_ref)

    mix = _dot(og_ref[...], wog_ref[...]) + _dot(om_ref[...], wom_ref[...]) + _dot(oc_ref[...], woc_ref[...])
    x = _layer_norm(ALPHA * x0_ref[...] + mix, g1_ref[...], b1_ref[...])
    q =(_dot(x.astype(BF16), wq_ref[...]) * (XA_DH ** -0.5)).astype(BF16)
    heads = []
    for h in range(XA_HEADS):
        sl = slice(h * XA_DH, (h + 1) * XA_DH)
        s = _dot_nt(q[:, sl], k_ref[:, sl])
        p = jnp.exp(s - jnp.max(s, -1, keepdims=True))
        p = p / jnp.sum(p, -1, keepdims=True)
        heads.append(_dot(p.astype(BF16), v_ref[:, sl]).astype(BF16))
    out = _dot(jnp.concatenate(heads, axis=1), wo_ref[...])
    x2 = _layer_norm(ALPHA * x + out, g_ref[...], b_ref[...])
    gates, rank, grp = _route_tile(x2, wrh_ref, wrm_ref, br_ref, tri_ref, cnt_ref)
    xg_ref[:, 0:D_MODEL] = x2
    xg_ref[:, D_MODEL:] = gates
    rank_ref[...] = rank
    grp_ref[...] = grp
    cnt_out_ref[...] = cnt_ref[...]


def _out_xattn_route(og, om, oc, x0, wog, wom, woc, g1, b1, xk, xv, wq, wo, g, b, w_route_hi, w_route_mid, b_route,
                     batch, seq, mem_len, tm):
    n = x0.shape[0]
    nt = seq // tm
    tri = jnp.asarray(np.tril(np.ones((tm, tm), np.float32), -1), BF16)
    tile = lambda w: pl.BlockSpec((tm, w), lambda bi, i: (bi * nt + i, 0))
    mem = pl.BlockSpec((mem_len, D_MODEL), lambda bi, i: (bi, 0))
    return pl.pallas_call(
        _xattn_kernel,
        grid=(batch, nt),
        in_specs=[tile(MIX_W), tile(MIX_W), tile(MLA_HEADS * MLA_V), tile(D_MODEL),
                  _full(wog.shape), _full(wom.shape), _full(woc.shape), _full(g1.shape), _full(b1.shape),
                  mem, mem, _full(wq.shape), _full(wo.shape), _full(g.shape), _full(b.shape),
                  _full(w_route_hi.shape), _full(w_route_mid.shape), _full(b_route.shape), _full(tri.shape)],
        out_specs=[tile(XG_W), tile(1), tile(1), _full((1, LANES))],
        out_shape=[jax.ShapeDtypeStruct((n, XG_W), F32), jax.ShapeDtypeStruct((n, 1), jnp.int32),
                   jax.ShapeDtypeStruct((n, 1), jnp.int32), jax.ShapeDtypeStruct((1, LANES), F32)],
        scratch_shapes=[pltpu.VMEM((1, LANES), F32)],
        compiler_params=_params(("arbitrary", "arbitrary")),
        name="out_proj_xattn_route",
    )(og, om, oc, x0, wog, wom, woc, g1, b1, xk, xv, wq, wo, g, b, w_route_hi, w_route_mid, b_route, tri)


def _dispatch_kernel(pos_ref, x_ref, xs_hbm, sem, *, tm):
    def issue(t, carry):
        pltpu.make_async_copy(x_ref.at[pl.ds(t, 1)], xs_hbm.at[pl.ds(pos_ref[t], 1)], sem).start()
        return carry

    lax.fori_loop(0, tm, issue, 0, unroll=DMA_UNROLL)
    pltpu.make_async_copy(x_ref, xs_hbm.at[pl.ds(0, tm)], sem).wait()


def _dispatch(xg, pos, tm):
    n = xg.shape[0]
    return pl.pallas_call(
        functools.partial(_dispatch_kernel, tm=tm),
        grid=(n // tm,),
        in_specs=[pl.BlockSpec((tm,), lambda i: (i,), memory_space=pltpu.SMEM),
                  pl.BlockSpec((tm, XG_W), lambda i: (i, 0))],
        out_specs=pl.BlockSpec(memory_space=pl.ANY),
        out_shape=jax.ShapeDtypeStruct(xg.shape, xg.dtype),
        scratch_shapes=[pltpu.SemaphoreType.DMA(())],
        compiler_params=_params(("arbitrary",)),
        name="moe_dispatch",
    )(pos, xg)


def _moe_ffn_kernel(blk_ref, grp_ref, first_ref, valid_ref, xs_ref, wg_ref, wu_ref, wd_ref, o_ref, xb_ref):
    w = pl.program_id(0)
    e = pl.program_id(1)

    @pl.when((first_ref[w] == 1) & (e == 0))
    def _():
        xb_ref[...] = xs_ref[:, 0:D_MODEL].astype(BF16)
        o_ref[...] = jnp.zeros_like(o_ref)

    @pl.when(valid_ref[w] == 1)
    def _():
        xb = xb_ref[...]
        gates = xs_ref[:, D_MODEL:]
        lane = lax.broadcasted_iota(jnp.int32, gates.shape, 1)
        hidden = []
        for j in range(EXPERTS_PER_STEP):
            expert = grp_ref[w] * EXPERTS_PER_GROUP + e * EXPERTS_PER_STEP + j
            w_tok = jnp.sum(jnp.where(lane == expert, gates, 0.0), -1, keepdims=True)
            hg = _dot(xb, wg_ref[j].astype(BF16))
            hu = _dot(xb, wu_ref[j].astype(BF16))
            hidden.append((hg * jax.nn.sigmoid(hg) * hu * w_tok).astype(BF16))
        wd = wd_ref[...].astype(BF16).reshape(EXPERTS_PER_STEP * D_EXPERT, D_MODEL)
        o_ref[...] += _dot(jnp.concatenate(hidden, axis=1), wd)


def _moe_ffn(xs, items, w_gate, w_up, w_down, rb, e_base):
    n = xs.shape[0]
    blk, grp, first, valid = items

    steps = EXPERTS_PER_GROUP // EXPERTS_PER_STEP

    def expert(w, e, blk, grp, first, valid):
        return ((e_base + grp[w] * EXPERTS_PER_GROUP) // EXPERTS_PER_STEP
                + jnp.where(valid[w] == 1, e, steps - 1), 0, 0)

    grid_spec = pltpu.PrefetchScalarGridSpec(
        num_scalar_prefetch=4,
        grid=(blk.shape[0], steps),
        in_specs=[pl.BlockSpec((rb, XG_W), lambda w, e, blk, grp, first, valid: (blk[w], 0)),
                  pl.BlockSpec((EXPERTS_PER_STEP, D_MODEL, D_EXPERT), expert),
                  pl.BlockSpec((EXPERTS_PER_STEP, D_MODEL, D_EXPERT), expert),
                  pl.BlockSpec((EXPERTS_PER_STEP, D_EXPERT, D_MODEL), expert)],
        out_specs=pl.BlockSpec((rb, D_MODEL), lambda w, e, blk, grp, first, valid: (blk[w], 0)),
        scratch_shapes=[pltpu.VMEM((rb, D_MODEL), BF16)],
    )
    return pl.pallas_call(
        _moe_ffn_kernel,
        grid_spec=grid_spec,
        out_shape=jax.ShapeDtypeStruct((n, D_MODEL), F32),
        compiler_params=_params(("arbitrary", "arbitrary")),
        name="moe_experts",
    )(blk, grp, first, valid, xs, w_gate, w_up, w_down)


def _combine_kernel(pos_ref, pos_next_ref, xg_ref, ys_hbm, g_ref, b_ref, o_ref, buf_ref, sem, *, tm):
    i = pl.program_id(0)
    slot = i % 2

    def gather(p_ref, s):
        def issue(t, carry):
            pltpu.make_async_copy(ys_hbm.at[pl.ds(p_ref[t], 1)], buf_ref.at[s, pl.ds(t, 1)], sem.at[s]).start()
            return carry
        lax.fori_loop(0, tm, issue, 0, unroll=DMA_UNROLL)

    @pl.when(i == 0)
    def _():
        gather(pos_ref, slot)

    @pl.when(i + 1 < pl.num_programs(0))
    def _():
        gather(pos_next_ref, 1 - slot)

    pltpu.make_async_copy(ys_hbm.at[pl.ds(0, tm)], buf_ref.at[slot], sem.at[slot]).wait()
    o_ref[...] = _layer_norm(ALPHA * xg_ref[:, 0:D_MODEL] + buf_ref[slot], g_ref[...], b_ref[...])


def _combine(pos, xg, ys, g, b, tm):
    n = xg.shape[0]
    last = n // tm - 1
    return pl.pallas_call(
        functools.partial(_combine_kernel, tm=tm),
        grid=(n // tm,),
        in_specs=[pl.BlockSpec((tm,), lambda i: (i,), memory_space=pltpu.SMEM),
                  pl.BlockSpec((tm,), lambda i: (jnp.minimum(i + 1, last),), memory_space=pltpu.SMEM),
                  pl.BlockSpec((tm, XG_W), lambda i: (i, 0)),
                  pl.BlockSpec(memory_space=pl.ANY), _full(g.shape), _full(b.shape)],
        out_specs=pl.BlockSpec((tm, D_MODEL), lambda i: (i, 0)),
        out_shape=jax.ShapeDtypeStruct((n, D_MODEL), F32),
        scratch_shapes=[pltpu.VMEM((2, tm, D_MODEL), F32), pltpu.SemaphoreType.DMA((2,))],
        compiler_params=_params(("arbitrary",)),
        name="moe_combine_ln3",
    )(pos, pos, xg, ys, g, b)


def _moe_work_items(counts, n, rb):
    nb = n // rb
    n_items = nb + N_GROUPS - 1
    ends = jnp.cumsum(counts)
    start = jnp.arange(nb, dtype=jnp.int32) * rb
    g_lo = jnp.sum(ends[None, :] <= start[:, None], axis=1).astype(jnp.int32)
    g_hi = jnp.sum(ends[None, :] <= (start + rb - 1)[:, None], axis=1).astype(jnp.int32)
    per_blk = g_hi - g_lo + 1
    item0 = jnp.cumsum(per_blk) - per_blk
    w = jnp.arange(n_items, dtype=jnp.int32)
    valid = w < jnp.sum(per_blk)
    blk = jnp.clip(jnp.sum(item0[None, :] <= w[:, None], axis=1) - 1, 0, nb - 1).astype(jnp.int32)
    grp = jnp.where(valid, g_lo[blk] + (w - item0[blk]), g_hi[nb - 1]).astype(jnp.int32)
    first = (valid & (w == item0[blk])).astype(jnp.int32)
    return blk, grp, first, valid.astype(jnp.int32)


def _moe(xg, rank, grp, counts, w_gate, w_up, w_down, e_base, g, b, t_dma, rb):
    n = xg.shape[0]
    counts = counts[0, :N_GROUPS].astype(jnp.int32)
    offsets = jnp.cumsum(counts) - counts
    pos = (offsets[grp[:, 0]] + rank[:, 0]).astype(jnp.int32)
    xs = _dispatch(xg, pos, t_dma)
    ys = _moe_ffn(xs, _moe_work_items(counts, n, rb), w_gate, w_up, w_down, rb, e_base)
    return _combine(pos, xg, ys, g, b, t_dma)


def _pad_cols(w, width):
    return jnp.pad(w, ((0, 0), (0, width - w.shape[1])))


def _layer_weights(w_in, w_out, gla_w_a2, ml_b_i, ml_b_f, mla_w_uq, mla_w_ukv, moe_w_group, moe_b_group,
                   moe_w_router, moe_b_router):
    o = _IN_OFF
    half = MLA_ROPE // 2
    wg = _pad_cols(w_in[:, o[0]:o[5]], SEG_W).astype(BF16)
    wm = _pad_cols(w_in[:, o[5]:o[10]], SEG_W).astype(BF16)
    kr = w_in[:, o[12]:o[13]]
    zeros = lambda w: jnp.zeros((D_MODEL, w), F32)
    kra = jnp.concatenate([zeros(MLA_NOPE), kr, zeros(LANES - MLA_NOPE - MLA_ROPE)], 1)
    krb = jnp.concatenate([zeros(MLA_NOPE), kr[:, half:], kr[:, :half], zeros(LANES - MLA_NOPE - MLA_ROPE)], 1)
    wc = jnp.concatenate([w_in[:, o[10]:o[12]], kra, krb], 1).astype(BF16)
    wift = w_in[:, o[8]:o[10]].T.astype(BF16)
    wa2 = jnp.pad(gla_w_a2, ((0, LANES - GLA_GATE_RANK), (0, 0))).astype(BF16)
    b_gate = jnp.concatenate([ml_b_i, ml_b_f])
    b_col = jnp.pad(b_gate, (0, LANES - 2 * MIX_HEADS)).reshape(1, LANES)
    b_row = b_gate.reshape(2 * MIX_HEADS, 1)
    uq = mla_w_uq.reshape(MLA_Q_RANK, MLA_HEADS, MLA_NOPE + MLA_ROPE)
    zq = jnp.zeros((MLA_Q_RANK, MLA_HEADS, LANES - MLA_NOPE - MLA_ROPE), F32)
    wqa = jnp.concatenate([uq, zq], -1).reshape(MLA_Q_RANK, -1).astype(BF16)
    wqb = jnp.concatenate([jnp.zeros((MLA_Q_RANK, MLA_HEADS, MLA_NOPE), F32), uq[..., MLA_NOPE + half:],
                           uq[..., MLA_NOPE:MLA_NOPE + half], zq], -1).reshape(MLA_Q_RANK, -1).astype(BF16)
    ukv = mla_w_ukv.reshape(MLA_KV_RANK, MLA_HEADS, MLA_NOPE + MLA_V)
    wkn = jnp.concatenate([ukv[..., :MLA_NOPE], jnp.zeros((MLA_KV_RANK, MLA_HEADS, LANES - MLA_NOPE), F32)],
                          -1).reshape(MLA_KV_RANK, -1).astype(BF16)
    uv = ukv[..., MLA_NOPE:]
    zv = jnp.zeros_like(uv)
    odd = (jnp.arange(MLA_HEADS) % 2 == 1)[None, :, None]
    wv = jnp.concatenate([jnp.where(odd, zv, uv), jnp.where(odd, uv, zv)], -1).reshape(MLA_KV_RANK, -1).astype(BF16)
    wo = w_out.astype(BF16)
    w_route = _pad_cols(jnp.concatenate([moe_w_router, moe_w_group], 1), LANES)
    w_route_hi = w_route.astype(BF16)
    w_route_mid = (w_route - w_route_hi.astype(F32)).astype(BF16)
    b_route = jnp.pad(jnp.concatenate([moe_b_router, moe_b_group]), (0, LANES - N_EXPERTS - N_GROUPS)).reshape(1, LANES)
    return dict(wg=wg, wm=wm, wc=wc, wift=wift, wa2=wa2, b_col=b_col, b_row=b_row, wqa=wqa, wqb=wqb, wkn=wkn,
                wv=wv, wo_g=wo[0:MIX_W], wo_m=wo[MIX_W:2 * MIX_W], wo_c=wo[2 * MIX_W:], w_route_hi=w_route_hi, w_route_mid=w_route_mid,
                b_route=b_route)


def _tile(total, want):
    t = min(total, want)
    assert total % t == 0
    return t


def kernel(x, mem, positions, w_in, w_out, gla_w_a2, gla_b_a, gla_norm_g, ml_conv_w, ml_b_i, ml_b_f, ml_norm_g, mla_q_norm_g, mla_w_uq, mla_kv_norm_g, mla_w_ukv, xa_w_q, xa_w_kv, xa_w_o, moe_w_group, moe_b_group, moe_w_router, moe_b_router, moe_w_gate, moe_w_up, moe_w_down, ln1_g, ln1_b, ln2_g, ln2_b, ln3_g, ln3_b):
    batch, seq, _ = x.shape
    mem_len = mem.shape[1]
    n = batch * seq
    depth = w_in.shape[0]
    assert seq % CHUNK == 0
    t_in = _tile(n, 512)
    tm = _tile(n, 1024)
    ts = _tile(seq, 512)
    tq = _tile(seq, 512)
    t_moe = _tile(n, 1024)
    row = lambda a: a.reshape(1, -1)

    cos_t, sin_t = _rope_tables(positions.reshape(n, 1), _tile(n, 2048))
    mem2d = mem.reshape(batch * mem_len, D_MODEL)
    experts_gate = moe_w_gate.reshape(-1, D_MODEL, D_EXPERT)
    experts_up = moe_w_up.reshape(-1, D_MODEL, D_EXPERT)
    experts_down = moe_w_down.reshape(-1, D_EXPERT, D_MODEL)
    h = x.reshape(n, D_MODEL)
    for l in range(depth):
        w = _layer_weights(w_in[l], w_out[l], gla_w_a2[l], ml_b_i[l], ml_b_f[l], mla_w_uq[l], mla_w_ukv[l],
                           moe_w_group[l], moe_b_group[l], moe_w_router[l], moe_b_router[l])
        yg, ym, yift, q, k, v = _in_proj(h, w["wg"], w["wm"], w["wc"], w["wift"], cos_t, sin_t,
                                         row(mla_q_norm_g[l]), row(mla_kv_norm_g[l]),
                                         w["wqa"], w["wqb"], w["wkn"], w["wv"], t_in)
        gates_row = yift.reshape(2 * MIX_HEADS, n // CHUNK, CHUNK).transpose(1, 0, 2)
        og = _gla(yg, w["wa2"], row(gla_b_a[l]), row(gla_norm_g[l]), batch, seq, ts)
        om = _mlstm(ym, gates_row, ml_conv_w[l], w["b_col"], w["b_row"], row(ml_norm_g[l]), batch, seq, ts)
        oc = _mla_attn(q, k, v, batch, seq, tq)
        xk, xv = _xa_kv(mem2d, xa_w_kv[l].astype(BF16), mem_len)
        xg, rank, grp, counts = _out_xattn_route(
            og, om, oc, h, w["wo_g"], w["wo_m"], w["wo_c"], row(ln1_g[l]), row(ln1_b[l]),
            xk, xv, xa_w_q[l].astype(BF16), xa_w_o[l].astype(BF16), row(ln2_g[l]), row(ln2_b[l]),
            w["w_route_hi"], w["w_route_mid"], w["b_route"], batch, seq, mem_len, tm)
        h = _moe(xg, rank, grp, counts, experts_gate, experts_up, experts_down, l * N_EXPERTS,
                 row(ln3_g[l]), row(ln3_b[l]), t_moe, t_moe)
    return h.reshape(batch, seq, D_MODEL)
```

```python
import functools

import numpy as np
import jax
import jax.numpy as jnp
from jax import lax
from jax.experimental import pallas as pl
from jax.experimental.pallas import tpu as pltpu

F32 = jnp.float32
BF16 = jnp.bfloat16

D_MODEL = 1024
CHUNK = 64
HEAD_D = 64
MIX_HEADS = 4
MIX_W = MIX_HEADS * HEAD_D
GLA_GATE_RANK = 16
GLA_TAU = 16.0
MLSTM_CONV = 4
MLA_HEADS = 8
MLA_NOPE = 64
MLA_ROPE = 32
MLA_V = 64
MLA_Q_RANK = 256
MLA_KV_RANK = 128
MLA_QK_PAD = 128
ROPE_BASE = 10000.0
LOG2_E = 1.4426950408889634
MLA_HEAD_PAIR = 2
XA_HEADS = 4
XA_DH = D_MODEL // XA_HEADS
N_GROUPS = 4
EXPERTS_PER_GROUP = 8
N_EXPERTS = N_GROUPS * EXPERTS_PER_GROUP
D_EXPERT = 256
DEPTH = 2
ALPHA = (2 * DEPTH) ** 0.25
LN_EPS = 1e-5
LANES = 128
SUBLANES = 8
MIXER_UNROLL = 8
MLSTM_UNROLL = 4
SEG_V = 2 * MIX_W
SEG_GATE = 3 * MIX_W
SEG_TAIL = 4 * MIX_W
SEG_W = SEG_TAIL + LANES
MLA_SEG_W = MLA_Q_RANK + MLA_KV_RANK + 2 * LANES
XG_W = D_MODEL + LANES
EXPERTS_PER_STEP = 4
DMA_UNROLL = 32
GLA_LEVELS = (32, 16, 8, 4, 2, 1)
VMEM_LIMIT = 56 * 1024 * 1024

_IN_SIZES = (256, 256, 256, 256, GLA_GATE_RANK, 512, 256, 256, 4, 4, MLA_Q_RANK, MLA_KV_RANK, MLA_ROPE)
_IN_OFF = np.concatenate([[0], np.cumsum(_IN_SIZES)]).tolist()


def _params(sem):
    return pltpu.CompilerParams(dimension_semantics=sem, vmem_limit_bytes=VMEM_LIMIT)


def _full(shape):
    return pl.BlockSpec(shape, lambda *_: (0,) * len(shape))


def _layer_norm(x, g, b):
    mu = jnp.mean(x, -1, keepdims=True)
    xc = x - mu
    var = jnp.mean(xc * xc, -1, keepdims=True)
    return xc * lax.rsqrt(var + LN_EPS) * g + b


def _log_sigmoid(z):
    return jnp.minimum(z, 0.0) - jnp.log1p(jnp.exp(-jnp.abs(z)))


def _dot_nt(a, b):
    return lax.dot_general(a, b, (((1,), (1,)), ((), ())), preferred_element_type=F32)


def _dot_tn(a, b):
    return lax.dot_general(a, b, (((0,), (0,)), ((), ())), preferred_element_type=F32)


def _dot(a, b):
    return jnp.dot(a, b, preferred_element_type=F32)


def _head_norm(o, g):
    mu = jnp.mean(o, -1, keepdims=True)
    oc = o - mu
    var = jnp.mean(oc * oc, -1, keepdims=True)
    return oc * lax.rsqrt(var + LN_EPS) * g


def _in_proj_kernel(x_ref, wg_ref, wm_ref, wc_ref, wift_ref, cos_ref, sin_ref, gq_ref, gkv_ref,
                    wqa_ref, wqb_ref, wkn_ref, wv_ref, ones_ref,
                    yg_ref, ym_ref, yift_ref, q_ref, k_ref, v_ref, yc_ref):
    xb = x_ref[...].astype(BF16)
    yg_ref[...] = _dot(xb, wg_ref[...])
    ym_ref[...] = _dot(xb, wm_ref[...])
    yift_ref[...] = _dot_nt(wift_ref[...], xb)
    yc_ref[...] = _dot(xb, wc_ref[...])
    _mla_prep_tile(yc_ref, cos_ref, sin_ref, gq_ref, gkv_ref, wqa_ref, wqb_ref, wkn_ref, wv_ref, ones_ref,
                   q_ref, k_ref, v_ref)


def _in_proj(x2d, wg, wm, wc, wift, cos_t, sin_t, gq, gkv, wqa, wqb, wkn, wv, tm):
    n = x2d.shape[0]
    ones_row = np.zeros((1, MLA_HEADS * LANES), np.float32)
    for h in range(MLA_HEADS):
        ones_row[0, h * LANES + (MLA_V if h % 2 == 0 else 0)] = 1.0
    ones_row = jnp.asarray(ones_row)
    row = lambda w: pl.BlockSpec((tm, w), lambda i: (i, 0))
    qk_w = MLA_HEADS * MLA_QK_PAD
    return pl.pallas_call(
        _in_proj_kernel,
        grid=(n // tm,),
        in_specs=[row(D_MODEL), _full(wg.shape), _full(wm.shape), _full(wc.shape), _full(wift.shape),
                  row(LANES), row(LANES), _full(gq.shape), _full(gkv.shape),
                  _full(wqa.shape), _full(wqb.shape), _full(wkn.shape), _full(wv.shape), _full(ones_row.shape)],
        out_specs=[row(SEG_W), row(SEG_W), pl.BlockSpec((8, tm), lambda i: (0, i)),
                   row(qk_w), row(qk_w), row(MLA_HEADS * LANES)],
        out_shape=[jax.ShapeDtypeStruct((n, SEG_W), F32), jax.ShapeDtypeStruct((n, SEG_W), F32),
                   jax.ShapeDtypeStruct((8, n), F32),
                   jax.ShapeDtypeStruct((n, qk_w), BF16), jax.ShapeDtypeStruct((n, qk_w), BF16),
                   jax.ShapeDtypeStruct((n, MLA_HEADS * LANES), BF16)],
        scratch_shapes=[pltpu.VMEM((tm, MLA_SEG_W), F32)],
        compiler_params=_params(("parallel",)),
        name="in_proj_mla_prep",
    )(x2d, wg, wm, wc, wift, cos_t, sin_t, gq, gkv, wqa, wqb, wkn, wv, ones_row)


def _split3(x):
    hi = x.astype(BF16)
    r1 = x - hi.astype(F32)
    mid = r1.astype(BF16)
    lo = (r1 - mid.astype(F32)).astype(BF16)
    return hi, mid, lo


def _cumsum_rows(tri, x):
    w = x.shape[1]
    y = _dot(tri, jnp.concatenate(_split3(x), axis=1))
    return y[:, 0:w] + y[:, w:2 * w] + y[:, 2 * w:3 * w]


def _gla_constants():
    t = np.arange(CHUNK)
    n_lv = len(GLA_LEVELS)
    masks = np.zeros((n_lv + 1, CHUNK, CHUNK), np.float32)
    right = np.zeros((n_lv, CHUNK, 1), np.float32)
    for li, n in enumerate(GLA_LEVELS):
        blk = t // (2 * n)
        is_right = (t % (2 * n)) >= n
        masks[li] = ((blk[:, None] == blk[None, :]) & is_right[:, None] & ~is_right[None, :])
        right[li, :, 0] = is_right
    masks[n_lv] = np.eye(CHUNK)
    return masks, right


def _gla_level_exponents(b, log_a, right_ref):
    row = lax.broadcasted_iota(jnp.int32, (CHUNK, 1), 0)
    exps = []
    for li, n in enumerate(GLA_LEVELS):
        if n >= SUBLANES // 2:
            per_blk = max(2 * n // SUBLANES, 1)
            b4 = b.reshape(CHUNK // (per_blk * SUBLANES), per_blk, SUBLANES, MIX_W)
            r_reg, r_sub = divmod(n - 1, SUBLANES)
            src = b4[:, r_reg:r_reg + 1, r_sub:r_sub + 1, :]
            b_r = jnp.broadcast_to(src, b4.shape).reshape(CHUNK, MIX_W)
            exps.append(jnp.where(right_ref[li] > 0.0, b - b_r, b_r - b))
        elif n == 2:
            pos = row % 4
            nxt = pltpu.roll(log_a, CHUNK - 1, 0)
            prv = pltpu.roll(log_a, 1, 0)
            exps.append(jnp.where(pos == 0, nxt, jnp.where(pos == 1, 0.0,
                                                            jnp.where(pos == 2, log_a, log_a + prv))))
        else:
            exps.append(jnp.where(right_ref[li] > 0.0, log_a, 0.0))
    return exps


def _pair_blocks(z, low_half):
    zero = jnp.zeros_like(z)
    return jnp.concatenate([jnp.where(low_half, z, zero), jnp.where(low_half, zero, z)], axis=0)


def _gla_kernel(y_ref, wa2_ref, ba_ref, g_ref, tri_ref, mask_ref, right_ref, o_ref, st_ref, la_ref, *, ts,
                unroll):
    n_lv = len(GLA_LEVELS)
    pairs = MIX_HEADS // 2
    low_half = lax.broadcasted_iota(jnp.int32, (CHUNK, LANES), 1) < HEAD_D
    same_head = (lax.broadcasted_iota(jnp.int32, (LANES, LANES), 0) < HEAD_D) == (
        lax.broadcasted_iota(jnp.int32, (LANES, LANES), 1) < HEAD_D)

    @pl.when(pl.program_id(1) == 0)
    def _():
        st_ref[...] = jnp.zeros_like(st_ref)

    z = _dot(y_ref[:, SEG_TAIL:SEG_W].astype(BF16), wa2_ref[...]) + ba_ref[...]
    la_ref[...] = _log_sigmoid(z) * (1.0 / GLA_TAU)

    def chunk(c, carry):
        rows = pl.ds(pl.multiple_of(c * CHUNK, CHUNK), CHUNK)
        q = y_ref[rows, 0:MIX_W] * (HEAD_D ** -0.5)
        k = y_ref[rows, MIX_W:SEG_V]
        log_a = la_ref[rows, :]
        b = _cumsum_rows(tri_ref[...], log_a)
        b_end = b[CHUNK - 1:CHUNK, :]
        scores = [jnp.zeros((CHUNK, LANES), F32) for _ in range(pairs)]
        for li, e in enumerate(_gla_level_exponents(b, log_a, right_ref)):
            x = (jnp.where(right_ref[li] > 0.0, q, k) * jnp.exp(e)).astype(BF16)
            for p in range(pairs):
                x2 = x[:, p * LANES:(p + 1) * LANES]
                scores[p] = scores[p] + _dot_nt(x2, _pair_blocks(x2, low_half)) * mask_ref[li]
        qb = q.astype(BF16)
        kb = k.astype(BF16)
        q_in = (q * jnp.exp(b)).astype(BF16)
        k_out = (k * jnp.exp(b_end - b)).astype(BF16)
        dec_end = jnp.exp(b_end)
        for p in range(pairs):
            pl_ = slice(p * LANES, (p + 1) * LANES)
            a = scores[p] + _dot_nt(qb[:, pl_], _pair_blocks(kb[:, pl_], low_half)) * mask_ref[n_lv]
            v2 = y_ref[rows, SEG_V + p * LANES:SEG_V + (p + 1) * LANES].astype(BF16)
            st = st_ref[p]
            o = _dot(a.astype(BF16), _pair_blocks(v2, low_half)) + _dot_nt(q_in[:, pl_], st.astype(BF16))
            st_ref[p] = st * dec_end[:, pl_] + jnp.where(same_head, _dot_tn(v2, k_out[:, pl_]), 0.0)
            normed = jnp.zeros((CHUNK, LANES), F32)
            for mine in (low_half, jnp.logical_not(low_half)):
                mu = jnp.sum(jnp.where(mine, o, 0.0), -1, keepdims=True) * (1.0 / HEAD_D)
                cen = jnp.where(mine, o - mu, 0.0)
                var = jnp.sum(cen * cen, -1, keepdims=True) * (1.0 / HEAD_D)
                normed = normed + cen * lax.rsqrt(var + LN_EPS)
            r_gate = y_ref[rows, SEG_GATE + p * LANES:SEG_GATE + (p + 1) * LANES]
            o_ref[rows, pl_] = (normed * g_ref[:, pl_] * (r_gate * jax.nn.sigmoid(r_gate))).astype(o_ref.dtype)
        return carry

    lax.fori_loop(0, ts // CHUNK, chunk, 0, unroll=unroll)


def _gla(yg, wa2, ba, g, batch, seq, ts):
    n = yg.shape[0]
    nt = seq // ts
    masks, right = _gla_constants()
    masks = np.concatenate([masks, masks], axis=-1)
    tri = jnp.asarray(np.tril(np.ones((CHUNK, CHUNK), np.float32)), BF16)
    return pl.pallas_call(
        functools.partial(_gla_kernel, ts=ts, unroll=MIXER_UNROLL),
        grid=(batch, nt),
        in_specs=[pl.BlockSpec((ts, SEG_W), lambda b, i: (b * nt + i, 0)),
                  _full(wa2.shape), _full(ba.shape), _full(g.shape),
                  _full(tri.shape), _full(masks.shape), _full(right.shape)],
        out_specs=pl.BlockSpec((ts, MIX_W), lambda b, i: (b * nt + i, 0)),
        out_shape=jax.ShapeDtypeStruct((n, MIX_W), BF16),
        scratch_shapes=[pltpu.VMEM((MIX_HEADS // 2, LANES, LANES), F32),
                        pltpu.VMEM((ts, MIX_W), F32)],
        compiler_params=_params(("parallel", "arbitrary")),
        name="gla_mixer",
    )(yg, wa2, ba, g, tri, jnp.asarray(masks), jnp.asarray(right))


def _mlstm_kernel(y_ref, gr_ref, cw_ref, bcol_ref, brow_ref, g_ref, tri_ref, eb_ref, eye_ref, o_ref,
                  xe_ref, qk_ref, vx_ref, fcl_ref, wsl_ref, fcb_ref, wsb_ref, rv_ref, dec_ref,
                  w_ref, em_ref, upd_ref, c_ref, m_ref, *, ts):
    first = pl.program_id(1) == 0

    @pl.when(first)
    def _():
        xe_ref[0:8, :] = jnp.zeros((8, 2 * MIX_W), F32)
        c_ref[...] = jnp.zeros_like(c_ref)
        m_ref[...] = jnp.zeros_like(m_ref)

    @pl.when(jnp.logical_not(first))
    def _():
        xe_ref[0:8, :] = xe_ref[ts:ts + 8, :]

    xe_ref[8:ts + 8, :] = y_ref[:, 0:2 * MIX_W]
    for c in range(ts // CHUNK):
        r0 = 8 + c * CHUNK
        conv = cw_ref[MLSTM_CONV - 1:MLSTM_CONV, :] * xe_ref[r0:r0 + CHUNK, :]
        for j in range(MLSTM_CONV - 1):
            lo = r0 - (MLSTM_CONV - 1) + j
            conv = conv + cw_ref[j:j + 1, :] * xe_ref[lo:lo + CHUNK, :]
        qk_ref[c * CHUNK:(c + 1) * CHUNK, :] = conv * jax.nn.sigmoid(conv)

    ones_col = (lax.broadcasted_iota(jnp.int32, (ts, HEAD_D), 1) == 0).astype(BF16)
    for h in range(MIX_HEADS):
        v_at, one_at = (0, HEAD_D) if h % 2 == 0 else (HEAD_D, 0)
        vx_ref[:, h * LANES + v_at:h * LANES + v_at + HEAD_D] = (
            y_ref[:, SEG_V + h * HEAD_D:SEG_V + (h + 1) * HEAD_D].astype(BF16))
        vx_ref[:, h * LANES + one_at:h * LANES + one_at + HEAD_D] = ones_col

    tri = tri_ref[...]
    expand = eb_ref[...]
    nck = ts // CHUNK
    lane = lax.broadcasted_iota(jnp.int32, (1, LANES), 1)
    f_lanes = (lane >= MIX_HEADS) & (lane < 2 * MIX_HEADS)
    head_lane = (lax.broadcasted_iota(jnp.int32, (MIX_HEADS, LANES), 1)
                 == lax.broadcasted_iota(jnp.int32, (MIX_HEADS, LANES), 0) + MIX_HEADS)

    def to_lanes(col):
        return jnp.sum(jnp.where(head_lane, col, 0.0), 0, keepdims=True)

    g_rows = gr_ref[...] + brow_ref[...]
    ls_rows = _log_sigmoid(g_rows).reshape(nck * SUBLANES, CHUNK)
    fcum_rows = sum(_dot_nt(part, tri) for part in _split3(ls_rows)).reshape(nck, SUBLANES, CHUNK)
    m_col = m_ref[0:MIX_HEADS, 0:1]
    for c in range(nck):
        g_col = y_ref[c * CHUNK:(c + 1) * CHUNK, SEG_TAIL:SEG_W] + bcol_ref[...]
        fcum_col = jnp.where(f_lanes, _cumsum_rows(tri, _log_sigmoid(g_col)), 0.0)
        fcl_ref[c * CHUNK:(c + 1) * CHUNK, :] = fcum_col
        g_row = g_rows[c]
        fcum_row = fcum_rows[c]
        f_row = fcum_row[MIX_HEADS:2 * MIX_HEADS, :]
        i_row = g_row[0:MIX_HEADS, :]
        f_end = f_row[:, CHUNK - 1:CHUNK]
        rv_ref[c, 0:MIX_HEADS, :] = jnp.concatenate(
            [i_row - f_row, jnp.broadcast_to(m_col, (MIX_HEADS, HEAD_D))], axis=1)
        m_new = jnp.maximum(f_end + m_col, jnp.max(f_end - f_row + i_row, -1, keepdims=True))
        m_new_l = to_lanes(m_new)
        f_end_l = fcum_col[CHUNK - 1:CHUNK, :]
        i_shift = pltpu.roll(g_col, MIX_HEADS, 1)
        wsl_ref[c * CHUNK:(c + 1) * CHUNK, :] = jnp.where(
            f_lanes, jnp.exp(f_end_l - fcum_col + i_shift - m_new_l), 0.0)
        dec_ref[c] = jnp.broadcast_to(jnp.exp(f_end_l + to_lanes(m_col) - m_new_l), (SUBLANES, LANES))
        m_col = m_new
    m_ref[0:MIX_HEADS, :] = jnp.broadcast_to(m_col, (MIX_HEADS, LANES))
    slab = 2 * CHUNK
    for i in range(ts // slab):
        sr = slice(i * slab, (i + 1) * slab)
        fcb_ref[sr, :] = sum(_dot(part, expand) for part in _split3(fcl_ref[sr, :]))
        ws_hi, ws_mid, _ = _split3(wsl_ref[sr, :])
        wsb_ref[sr, :] = _dot(ws_hi, expand) + _dot(ws_mid, expand)

    t_idx = lax.broadcasted_iota(jnp.int32, (CHUNK, LANES), 0)
    s_idx = lax.broadcasted_iota(jnp.int32, (CHUNK, LANES), 1)
    keep = (s_idx <= t_idx) | (s_idx >= CHUNK)
    low_half = s_idx < HEAD_D

    def weights(c, carry):
        rows = pl.ds(pl.multiple_of(c * CHUNK, CHUNK), CHUNK)
        rv = rv_ref[c, 0:MIX_HEADS, :]
        for p in range(MIX_HEADS // 2):
            q2 = (qk_ref[rows, p * LANES:(p + 1) * LANES] * (HEAD_D ** -0.5)).astype(BF16)
            k2 = qk_ref[rows, MIX_W + p * LANES:MIX_W + (p + 1) * LANES]
            for h in (2 * p, 2 * p + 1):
                mine = low_half if h % 2 == 0 else jnp.logical_not(low_half)
                hb = slice(h * LANES, (h + 1) * LANES)
                logw = jnp.where(keep, fcb_ref[rows, hb] + rv[h:h + 1, :], -jnp.inf)
                m_t = jnp.max(logw, -1, keepdims=True)
                k_h = jnp.where(mine, k2, 0.0)
                qk = _dot_nt(q2, jnp.concatenate([k_h.astype(BF16), eye_ref[h % 2]], axis=0))
                w_ref[c * MIX_HEADS + h] = (jnp.exp(logw - m_t) * qk).astype(BF16)
                em_ref[c * MIX_HEADS + h] = jnp.exp(-m_t)
                upd = _dot_tn((k_h * wsb_ref[rows, hb]).astype(BF16), vx_ref[rows, hb])
                off = (h % 2) * HEAD_D
                upd_ref[c * MIX_HEADS + h] = upd[off:off + HEAD_D, :]
        return carry

    lax.fori_loop(0, nck, weights, 0, unroll=MLSTM_UNROLL)

    def chunk(c, carry):
        rows = pl.ds(pl.multiple_of(c * CHUNK, CHUNK), CHUNK)
        dec_l = dec_ref[c]
        for p in range(MIX_HEADS // 2):
            pl_ = slice(p * LANES, (p + 1) * LANES)
            pair = jnp.zeros((CHUNK, LANES), F32)
            for h in (2 * p, 2 * p + 1):
                mine = low_half if h % 2 == 0 else jnp.logical_not(low_half)
                c_st = c_ref[h]
                num = _dot(w_ref[c * MIX_HEADS + h],
                           jnp.concatenate([vx_ref[rows, h * LANES:(h + 1) * LANES], c_st.astype(BF16)], axis=0))
                c_ref[h] = dec_l[0:1, MIX_HEADS + h:MIX_HEADS + h + 1] * c_st + upd_ref[c * MIX_HEADS + h]
                den = num[:, HEAD_D:HEAD_D + 1] if h % 2 == 0 else num[:, 0:1]
                r = 1.0 / jnp.maximum(jnp.abs(den), em_ref[c * MIX_HEADS + h])
                mu = jnp.sum(jnp.where(mine, num, 0.0), -1, keepdims=True) * (1.0 / HEAD_D)
                cen = jnp.where(mine, num - mu, 0.0)
                var = jnp.sum(cen * cen, -1, keepdims=True) * (1.0 / HEAD_D)
                pair = pair + cen * (r * lax.rsqrt(r * r * var + LN_EPS))
            o_gate = y_ref[rows, SEG_GATE + p * LANES:SEG_GATE + (p + 1) * LANES]
            o_ref[rows, pl_] = (pair * g_ref[:, pl_] * jax.nn.sigmoid(o_gate)).astype(o_ref.dtype)
        return carry

    lax.fori_loop(0, nck, chunk, 0, unroll=MLSTM_UNROLL // 2)


def _mlstm(ym, gates_row, conv_w, b_col, b_row, g, batch, seq, ts):
    n = ym.shape[0]
    nt = seq // ts
    nck = ts // CHUNK
    tri = jnp.asarray(np.tril(np.ones((CHUNK, CHUNK), np.float32)), BF16)
    expand = np.zeros((LANES, MIX_HEADS * LANES), np.float32)
    for h in range(MIX_HEADS):
        expand[MIX_HEADS + h, h * LANES:(h + 1) * LANES] = 1.0
    eye = np.zeros((2, HEAD_D, LANES), np.float32)
    eye[0, :, 0:HEAD_D] = np.eye(HEAD_D)
    eye[1, :, HEAD_D:] = np.eye(HEAD_D)
    expand = jnp.asarray(expand, BF16)
    eye = jnp.asarray(eye, BF16)
    return pl.pallas_call(
        functools.partial(_mlstm_kernel, ts=ts),
        grid=(batch, nt),
        in_specs=[pl.BlockSpec((ts, SEG_W), lambda b, i: (b * nt + i, 0)),
                  pl.BlockSpec((nck, 8, CHUNK), lambda b, i: (b * nt + i, 0, 0)),
                  _full(conv_w.shape), _full(b_col.shape), _full(b_row.shape), _full(g.shape),
                  _full(tri.shape), _full(expand.shape), _full(eye.shape)],
        out_specs=pl.BlockSpec((ts, MIX_W), lambda b, i: (b * nt + i, 0)),
        out_shape=jax.ShapeDtypeStruct((n, MIX_W), BF16),
        scratch_shapes=[pltpu.VMEM((ts + 8, 2 * MIX_W), F32),
                        pltpu.VMEM((ts, 2 * MIX_W), F32),
                        pltpu.VMEM((ts, MIX_HEADS * LANES), BF16),
                        pltpu.VMEM((ts, LANES), F32),
                        pltpu.VMEM((ts, LANES), F32),
                        pltpu.VMEM((ts, MIX_HEADS * LANES), F32),
                        pltpu.VMEM((ts, MIX_HEADS * LANES), F32),
                        pltpu.VMEM((nck, SUBLANES, LANES), F32),
                        pltpu.VMEM((nck, SUBLANES, LANES), F32),
                        pltpu.VMEM((nck * MIX_HEADS, CHUNK, LANES), BF16),
                        pltpu.VMEM((nck * MIX_HEADS, CHUNK, 1), F32),
                        pltpu.VMEM((nck * MIX_HEADS, HEAD_D, LANES), F32),
                        pltpu.VMEM((MIX_HEADS, HEAD_D, LANES), F32),
                        pltpu.VMEM((SUBLANES, LANES), F32)],
        compiler_params=_params(("parallel", "arbitrary")),
        name="mlstm_mixer",
    )(ym, gates_row, conv_w, b_col, b_row, g, tri, expand, eye)


def _rope_table_kernel(pos_ref, inv_ref, cos_ref, sin_ref):
    ang = pos_ref[...].astype(F32) * inv_ref[...]
    lane = lax.broadcasted_iota(jnp.int32, ang.shape, 1)
    rot = (lane >= MLA_NOPE) & (lane < MLA_NOPE + MLA_ROPE)
    first_half = lane < MLA_NOPE + MLA_ROPE // 2
    cos_ref[...] = jnp.where(lane < MLA_NOPE, 1.0, jnp.where(rot, jnp.cos(ang), 0.0))
    s = jnp.sin(ang)
    sin_ref[...] = jnp.where(rot, jnp.where(first_half, -s, s), 0.0)


def _rope_tables(pos_col, tm):
    n = pos_col.shape[0]
    half = MLA_ROPE // 2
    inv = ROPE_BASE ** (-np.arange(half, dtype=np.float32) / half)
    inv_row = np.zeros((1, LANES), np.float32)
    inv_row[0, MLA_NOPE:MLA_NOPE + half] = inv
    inv_row[0, MLA_NOPE + half:MLA_NOPE + MLA_ROPE] = inv
    return pl.pallas_call(
        _rope_table_kernel,
        grid=(n // tm,),
        in_specs=[pl.BlockSpec((tm, 1), lambda i: (i, 0)), _full((1, LANES))],
        out_specs=[pl.BlockSpec((tm, LANES), lambda i: (i, 0))] * 2,
        out_shape=[jax.ShapeDtypeStruct((n, LANES), F32)] * 2,
        compiler_params=_params(("parallel",)),
        name="rope_tables",
    )(pos_col, jnp.asarray(inv_row))


def _mla_prep_tile(y_ref, cos_ref, sin_ref, gq_ref, gkv_ref, wqa_ref, wqb_ref, wkn_ref, wv_ref, ones_ref,
                   q_ref, k_ref, v_ref):
    def rms(x, g):
        return x * lax.rsqrt(jnp.mean(x * x, -1, keepdims=True) + LN_EPS) * g

    cos = cos_ref[...]
    sin = sin_ref[...]
    cq = rms(y_ref[:, 0:MLA_Q_RANK], gq_ref[...]).astype(BF16)
    ckv = rms(y_ref[:, MLA_Q_RANK:MLA_Q_RANK + MLA_KV_RANK], gkv_ref[...]).astype(BF16)
    kr = MLA_Q_RANK + MLA_KV_RANK
    k_rope = y_ref[:, kr:kr + LANES] * cos + y_ref[:, kr + LANES:MLA_SEG_W] * sin
    qa = _dot(cq, wqa_ref[...])
    qb = _dot(cq, wqb_ref[...])
    kn = _dot(ckv, wkn_ref[...])
    scale = (MLA_NOPE + MLA_ROPE) ** -0.5 * LOG2_E
    for h in range(MLA_HEADS):
        sl = slice(h * MLA_QK_PAD, (h + 1) * MLA_QK_PAD)
        q_ref[:, sl] = ((qa[:, sl] * cos + qb[:, sl] * sin) * scale).astype(q_ref.dtype)
        k_ref[:, sl] = (kn[:, sl] + k_rope).astype(k_ref.dtype)
    v_ref[...] = (_dot(ckv, wv_ref[...]) + ones_ref[...]).astype(v_ref.dtype)


def _mla_attn_kernel(q_ref, k_ref, v_ref, o_ref, *, seq, tq):
    t_chunk = lax.broadcasted_iota(jnp.int32, (tq, tq), 0) // CHUNK
    s_chunk = lax.broadcasted_iota(jnp.int32, (tq, tq), 1) // CHUNK
    diag_mask = s_chunk <= t_chunk
    low_half = lax.broadcasted_iota(jnp.int32, (tq, LANES), 1) < MLA_V
    for i in range(seq // tq):
        rows = slice(i * tq, (i + 1) * tq)
        res = []
        for hh in range(MLA_HEAD_PAIR):
            ql = slice(hh * MLA_QK_PAD, (hh + 1) * MLA_QK_PAD)
            vl = slice(hh * LANES, (hh + 1) * LANES)
            q = q_ref[0, rows, ql]
            s_d = jnp.where(diag_mask, _dot_nt(q, k_ref[0, rows, ql]), -jnp.inf)
            m = jnp.max(s_d, -1, keepdims=True)
            if i > 0:
                s_o = _dot_nt(q, k_ref[0, 0:i * tq, ql])
                m = jnp.maximum(m, jnp.max(s_o, -1, keepdims=True))
            o = _dot(jnp.exp2(s_d - m).astype(BF16), v_ref[0, rows, vl])
            if i > 0:
                o = o + _dot(jnp.exp2(s_o - m).astype(BF16), v_ref[0, 0:i * tq, vl])
            l = o[:, MLA_V:MLA_V + 1] if hh == 0 else o[:, 0:1]
            res.append(o / l)
        o_ref[0, rows, :] = jnp.where(low_half, res[0], res[1]).astype(o_ref.dtype)


def _mla_attn(q, k, v, batch, seq, tq):
    n = q.shape[0]
    qk_w = MLA_HEADS * MLA_QK_PAD
    v_w = MLA_HEADS * MLA_V
    pair = lambda w: pl.BlockSpec((1, seq, MLA_HEAD_PAIR * w), lambda b, h: (b, 0, h))
    out = pl.pallas_call(
        functools.partial(_mla_attn_kernel, seq=seq, tq=tq),
        grid=(batch, MLA_HEADS // MLA_HEAD_PAIR),
        in_specs=[pair(MLA_QK_PAD), pair(MLA_QK_PAD), pair(LANES)],
        out_specs=pair(MLA_V),
        out_shape=jax.ShapeDtypeStruct((batch, seq, v_w), BF16),
        compiler_params=_params(("parallel", "parallel")),
        name="mla_attention",
    )(q.reshape(batch, seq, qk_w), k.reshape(batch, seq, qk_w), v.reshape(batch, seq, MLA_HEADS * LANES))
    return out.reshape(n, v_w)


def _xa_kv_kernel(mem_ref, w_ref, k_ref, v_ref):
    kv = _dot(mem_ref[...].astype(BF16), w_ref[...])
    k_ref[...] = kv[:, 0:D_MODEL].astype(k_ref.dtype)
    v_ref[...] = kv[:, D_MODEL:2 * D_MODEL].astype(v_ref.dtype)


def _xa_kv(mem2d, w_kv, mem_len):
    n = mem2d.shape[0]
    row = pl.BlockSpec((mem_len, D_MODEL), lambda i: (i, 0))
    return pl.pallas_call(
        _xa_kv_kernel,
        grid=(n // mem_len,),
        in_specs=[row, _full(w_kv.shape)],
        out_specs=[row, row],
        out_shape=[jax.ShapeDtypeStruct((n, D_MODEL), BF16)] * 2,
        compiler_params=_params(("parallel",)),
        name="xattn_kv",
    )(mem2d, w_kv)


def _route_tile(x, wh_ref, wm_ref, b_ref, tri_ref, cnt_ref):
    xh = x.astype(BF16)
    xm = (x - xh.astype(F32)).astype(BF16)
    logits = _dot(xh, wh_ref[...]) + _dot(xh, wm_ref[...]) + _dot(xm, wh_ref[...]) + b_ref[...]
    lane = lax.broadcasted_iota(jnp.int32, logits.shape, 1).astype(F32)
    is_group = (lane >= N_EXPERTS) & (lane < N_EXPERTS + N_GROUPS)
    g_max = jnp.max(jnp.where(is_group, logits, -jnp.inf), -1, keepdims=True)
    g_sum = jnp.sum(jnp.where(is_group, jnp.exp(logits - g_max), 0.0), -1, keepdims=True)
    g_p = 1.0 / g_sum
    g_idx = jnp.min(jnp.where(is_group & (logits == g_max), lane - N_EXPERTS, float(LANES)), -1, keepdims=True)
    in_group = (lane < N_EXPERTS) & (jnp.floor(lane * (1.0 / EXPERTS_PER_GROUP)) == g_idx)
    e_max = jnp.max(jnp.where(in_group, logits, -jnp.inf), -1, keepdims=True)
    e_exp = jnp.where(in_group, jnp.exp(logits - e_max), 0.0)
    prob = e_exp / jnp.sum(e_exp, -1, keepdims=True)
    cand = jnp.where(in_group, prob, -1.0)
    p1 = jnp.max(cand, -1, keepdims=True)
    i1 = jnp.min(jnp.where(cand == p1, lane, float(LANES)), -1, keepdims=True)
    cand2 = jnp.where(lane == i1, -1.0, cand)
    p2 = jnp.max(cand2, -1, keepdims=True)
    i2 = jnp.min(jnp.where(cand2 == p2, lane, float(LANES)), -1, keepdims=True)
    p_sum = p1 + p2
    gates = jnp.where(lane == i1, g_p * (p1 / p_sum), 0.0) + jnp.where(lane == i2, g_p * (p2 / p_sum), 0.0)
    onehot = jnp.where(lane == g_idx, 1.0, 0.0)
    before = _dot(tri_ref[...], onehot.astype(BF16)) + cnt_ref[...]
    rank = jnp.sum(onehot * before, -1, keepdims=True).astype(jnp.int32)
    cnt_ref[...] += jnp.sum(onehot, 0, keepdims=True)
    return gates, rank, g_idx.astype(jnp.int32)


def _xattn_kernel(og_ref, om_ref, oc_ref, x0_ref, wog_ref, wom_ref, woc_ref, g1_ref, b1_ref,
                  k_ref, v_ref, wq_ref, wo_ref, g_ref, b_ref, wrh_ref, wrm_ref, br_ref, tri_ref,
                  xg_ref, rank_ref, grp_ref, cnt_out_ref, cnt_ref):
    @pl.when((pl.program_id(0) == 0) & (pl.program_id(1) == 0))
    def _():
        cnt_ref[...] = jnp.zeros_like(cnt_ref)

    mix = _dot(og_ref[...], wog_ref[...]) + _dot(om_ref[...], wom_ref[...]) + _dot(oc_ref[...], woc_ref[...])
    x = _layer_norm(ALPHA * x0_ref[...] + mix, g1_ref[...], b1_ref[...])
    q =(_dot(x.astype(BF16), wq_ref[...]) * (XA_DH ** -0.5)).astype(BF16)
    heads = []
    for h in range(XA_HEADS):
        sl = slice(h * XA_DH, (h + 1) * XA_DH)
        s = _dot_nt(q[:, sl], k_ref[:, sl])
        p = jnp.exp(s - jnp.max(s, -1, keepdims=True))
        p = p / jnp.sum(p, -1, keepdims=True)
        heads.append(_dot(p.astype(BF16), v_ref[:, sl]).astype(BF16))
    out = _dot(jnp.concatenate(heads, axis=1), wo_ref[...])
    x2 = _layer_norm(ALPHA * x + out, g_ref[...], b_ref[...])
    gates, rank, grp = _route_tile(x2, wrh_ref, wrm_ref, br_ref, tri_ref, cnt_ref)
    xg_ref[:, 0:D_MODEL] = x2
    xg_ref[:, D_MODEL:] = gates
    rank_ref[...] = rank
    grp_ref[...] = grp
    cnt_out_ref[...] = cnt_ref[...]


def _out_xattn_route(og, om, oc, x0, wog, wom, woc, g1, b1, xk, xv, wq, wo, g, b, w_route_hi, w_route_mid, b_route,
                     batch, seq, mem_len, tm):
    n = x0.shape[0]
    nt = seq // tm
    tri = jnp.asarray(np.tril(np.ones((tm, tm), np.float32), -1), BF16)
    tile = lambda w: pl.BlockSpec((tm, w), lambda bi, i: (bi * nt + i, 0))
    mem = pl.BlockSpec((mem_len, D_MODEL), lambda bi, i: (bi, 0))
    return pl.pallas_call(
        _xattn_kernel,
        grid=(batch, nt),
        in_specs=[tile(MIX_W), tile(MIX_W), tile(MLA_HEADS * MLA_V), tile(D_MODEL),
                  _full(wog.shape), _full(wom.shape), _full(woc.shape), _full(g1.shape), _full(b1.shape),
                  mem, mem, _full(wq.shape), _full(wo.shape), _full(g.shape), _full(b.shape),
                  _full(w_route_hi.shape), _full(w_route_mid.shape), _full(b_route.shape), _full(tri.shape)],
        out_specs=[tile(XG_W), tile(1), tile(1), _full((1, LANES))],
        out_shape=[jax.ShapeDtypeStruct((n, XG_W), F32), jax.ShapeDtypeStruct((n, 1), jnp.int32),
                   jax.ShapeDtypeStruct((n, 1), jnp.int32), jax.ShapeDtypeStruct((1, LANES), F32)],
        scratch_shapes=[pltpu.VMEM((1, LANES), F32)],
        compiler_params=_params(("arbitrary", "arbitrary")),
        name="out_proj_xattn_route",
    )(og, om, oc, x0, wog, wom, woc, g1, b1, xk, xv, wq, wo, g, b, w_route_hi, w_route_mid, b_route, tri)


def _dispatch_kernel(pos_ref, x_ref, xs_hbm, sem, *, tm):
    def issue(t, carry):
        pltpu.make_async_copy(x_ref.at[pl.ds(t, 1)], xs_hbm.at[pl.ds(pos_ref[t], 1)], sem).start()
        return carry

    lax.fori_loop(0, tm, issue, 0, unroll=DMA_UNROLL)
    pltpu.make_async_copy(x_ref, xs_hbm.at[pl.ds(0, tm)], sem).wait()


def _dispatch(xg, pos, tm):
    n = xg.shape[0]
    return pl.pallas_call(
        functools.partial(_dispatch_kernel, tm=tm),
        grid=(n // tm,),
        in_specs=[pl.BlockSpec((tm,), lambda i: (i,), memory_space=pltpu.SMEM),
                  pl.BlockSpec((tm, XG_W), lambda i: (i, 0))],
        out_specs=pl.BlockSpec(memory_space=pl.ANY),
        out_shape=jax.ShapeDtypeStruct(xg.shape, xg.dtype),
        scratch_shapes=[pltpu.SemaphoreType.DMA(())],
        compiler_params=_params(("arbitrary",)),
        name="moe_dispatch",
    )(pos, xg)


def _moe_ffn_kernel(blk_ref, grp_ref, first_ref, valid_ref, xs_ref, wg_ref, wu_ref, wd_ref, o_ref, xb_ref):
    w = pl.program_id(0)
    e = pl.program_id(1)

    @pl.when((first_ref[w] == 1) & (e == 0))
    def _():
        xb_ref[...] = xs_ref[:, 0:D_MODEL].astype(BF16)
        o_ref[...] = jnp.zeros_like(o_ref)

    @pl.when(valid_ref[w] == 1)
    def _():
        xb = xb_ref[...]
        gates = xs_ref[:, D_MODEL:]
        lane = lax.broadcasted_iota(jnp.int32, gates.shape, 1)
        hidden = []
        for j in range(EXPERTS_PER_STEP):
            expert = grp_ref[w] * EXPERTS_PER_GROUP + e * EXPERTS_PER_STEP + j
            w_tok = jnp.sum(jnp.where(lane == expert, gates, 0.0), -1, keepdims=True)
            hg = _dot(xb, wg_ref[j].astype(BF16))
            hu = _dot(xb, wu_ref[j].astype(BF16))
            hidden.append((hg * jax.nn.sigmoid(hg) * hu * w_tok).astype(BF16))
        wd = wd_ref[...].astype(BF16).reshape(EXPERTS_PER_STEP * D_EXPERT, D_MODEL)
        o_ref[...] += _dot(jnp.concatenate(hidden, axis=1), wd)


def _moe_ffn(xs, items, w_gate, w_up, w_down, rb, e_base):
    n = xs.shape[0]
    blk, grp, first, valid = items

    steps = EXPERTS_PER_GROUP // EXPERTS_PER_STEP

    def expert(w, e, blk, grp, first, valid):
        return ((e_base + grp[w] * EXPERTS_PER_GROUP) // EXPERTS_PER_STEP
                + jnp.where(valid[w] == 1, e, steps - 1), 0, 0)

    grid_spec = pltpu.PrefetchScalarGridSpec(
        num_scalar_prefetch=4,
        grid=(blk.shape[0], steps),
        in_specs=[pl.BlockSpec((rb, XG_W), lambda w, e, blk, grp, first, valid: (blk[w], 0)),
                  pl.BlockSpec((EXPERTS_PER_STEP, D_MODEL, D_EXPERT), expert),
                  pl.BlockSpec((EXPERTS_PER_STEP, D_MODEL, D_EXPERT), expert),
                  pl.BlockSpec((EXPERTS_PER_STEP, D_EXPERT, D_MODEL), expert)],
        out_specs=pl.BlockSpec((rb, D_MODEL), lambda w, e, blk, grp, first, valid: (blk[w], 0)),
        scratch_shapes=[pltpu.VMEM((rb, D_MODEL), BF16)],
    )
    return pl.pallas_call(
        _moe_ffn_kernel,
        grid_spec=grid_spec,
        out_shape=jax.ShapeDtypeStruct((n, D_MODEL), F32),
        compiler_params=_params(("arbitrary", "arbitrary")),
        name="moe_experts",
    )(blk, grp, first, valid, xs, w_gate, w_up, w_down)


def _combine_kernel(pos_ref, pos_next_ref, xg_ref, ys_hbm, g_ref, b_ref, o_ref, buf_ref, sem, *, tm):
    i = pl.program_id(0)
    slot = i % 2

    def gather(p_ref, s):
        def issue(t, carry):
            pltpu.make_async_copy(ys_hbm.at[pl.ds(p_ref[t], 1)], buf_ref.at[s, pl.ds(t, 1)], sem.at[s]).start()
            return carry
        lax.fori_loop(0, tm, issue, 0, unroll=DMA_UNROLL)

    @pl.when(i == 0)
    def _():
        gather(pos_ref, slot)

    @pl.when(i + 1 < pl.num_programs(0))
    def _():
        gather(pos_next_ref, 1 - slot)

    pltpu.make_async_copy(ys_hbm.at[pl.ds(0, tm)], buf_ref.at[slot], sem.at[slot]).wait()
    o_ref[...] = _layer_norm(ALPHA * xg_ref[:, 0:D_MODEL] + buf_ref[slot], g_ref[...], b_ref[...])


def _combine(pos, xg, ys, g, b, tm):
    n = xg.shape[0]
    last = n // tm - 1
    return pl.pallas_call(
        functools.partial(_combine_kernel, tm=tm),
        grid=(n // tm,),
        in_specs=[pl.BlockSpec((tm,), lambda i: (i,), memory_space=pltpu.SMEM),
                  pl.BlockSpec((tm,), lambda i: (jnp.minimum(i + 1, last),), memory_space=pltpu.SMEM),
                  pl.BlockSpec((tm, XG_W), lambda i: (i, 0)),
                  pl.BlockSpec(memory_space=pl.ANY), _full(g.shape), _full(b.shape)],
        out_specs=pl.BlockSpec((tm, D_MODEL), lambda i: (i, 0)),
        out_shape=jax.ShapeDtypeStruct((n, D_MODEL), F32),
        scratch_shapes=[pltpu.VMEM((2, tm, D_MODEL), F32), pltpu.SemaphoreType.DMA((2,))],
        compiler_params=_params(("arbitrary",)),
        name="moe_combine_ln3",
    )(pos, pos, xg, ys, g, b)


def _moe_work_items(counts, n, rb):
    nb = n // rb
    n_items = nb + N_GROUPS - 1
    ends = jnp.cumsum(counts)
    start = jnp.arange(nb, dtype=jnp.int32) * rb
    g_lo = jnp.sum(ends[None, :] <= start[:, None], axis=1).astype(jnp.int32)
    g_hi = jnp.sum(ends[None, :] <= (start + rb - 1)[:, None], axis=1).astype(jnp.int32)
    per_blk = g_hi - g_lo + 1
    item0 = jnp.cumsum(per_blk) - per_blk
    w = jnp.arange(n_items, dtype=jnp.int32)
    valid = w < jnp.sum(per_blk)
    blk = jnp.clip(jnp.sum(item0[None, :] <= w[:, None], axis=1) - 1, 0, nb - 1).astype(jnp.int32)
    grp = jnp.where(valid, g_lo[blk] + (w - item0[blk]), g_hi[nb - 1]).astype(jnp.int32)
    first = (valid & (w == item0[blk])).astype(jnp.int32)
    return blk, grp, first, valid.astype(jnp.int32)


def _moe(xg, rank, grp, counts, w_gate, w_up, w_down, e_base, g, b, t_dma, rb):
    n = xg.shape[0]
    counts = counts[0, :N_GROUPS].astype(jnp.int32)
    offsets = jnp.cumsum(counts) - counts
    pos = (offsets[grp[:, 0]] + rank[:, 0]).astype(jnp.int32)
    xs = _dispatch(xg, pos, t_dma)
    ys = _moe_ffn(xs, _moe_work_items(counts, n, rb), w_gate, w_up, w_down, rb, e_base)
    return _combine(pos, xg, ys, g, b, t_dma)


def _pad_cols(w, width):
    return jnp.pad(w, ((0, 0), (0, width - w.shape[1])))


def _layer_weights(w_in, w_out, gla_w_a2, ml_b_i, ml_b_f, mla_w_uq, mla_w_ukv, moe_w_group, moe_b_group,
                   moe_w_router, moe_b_router):
    o = _IN_OFF
    half = MLA_ROPE // 2
    wg = _pad_cols(w_in[:, o[0]:o[5]], SEG_W).astype(BF16)
    wm = _pad_cols(w_in[:, o[5]:o[10]], SEG_W).astype(BF16)
    kr = w_in[:, o[12]:o[13]]
    zeros = lambda w: jnp.zeros((D_MODEL, w), F32)
    kra = jnp.concatenate([zeros(MLA_NOPE), kr, zeros(LANES - MLA_NOPE - MLA_ROPE)], 1)
    krb = jnp.concatenate([zeros(MLA_NOPE), kr[:, half:], kr[:, :half], zeros(LANES - MLA_NOPE - MLA_ROPE)], 1)
    wc = jnp.concatenate([w_in[:, o[10]:o[12]], kra, krb], 1).astype(BF16)
    wift = w_in[:, o[8]:o[10]].T.astype(BF16)
    wa2 = jnp.pad(gla_w_a2, ((0, LANES - GLA_GATE_RANK), (0, 0))).astype(BF16)
    b_gate = jnp.concatenate([ml_b_i, ml_b_f])
    b_col = jnp.pad(b_gate, (0, LANES - 2 * MIX_HEADS)).reshape(1, LANES)
    b_row = b_gate.reshape(2 * MIX_HEADS, 1)
    uq = mla_w_uq.reshape(MLA_Q_RANK, MLA_HEADS, MLA_NOPE + MLA_ROPE)
    zq = jnp.zeros((MLA_Q_RANK, MLA_HEADS, LANES - MLA_NOPE - MLA_ROPE), F32)
    wqa = jnp.concatenate([uq, zq], -1).reshape(MLA_Q_RANK, -1).astype(BF16)
    wqb = jnp.concatenate([jnp.zeros((MLA_Q_RANK, MLA_HEADS, MLA_NOPE), F32), uq[..., MLA_NOPE + half:],
                           uq[..., MLA_NOPE:MLA_NOPE + half], zq], -1).reshape(MLA_Q_RANK, -1).astype(BF16)
    ukv = mla_w_ukv.reshape(MLA_KV_RANK, MLA_HEADS, MLA_NOPE + MLA_V)
    wkn = jnp.concatenate([ukv[..., :MLA_NOPE], jnp.zeros((MLA_KV_RANK, MLA_HEADS, LANES - MLA_NOPE), F32)],
                          -1).reshape(MLA_KV_RANK, -1).astype(BF16)
    uv = ukv[..., MLA_NOPE:]
    zv = jnp.zeros_like(uv)
    odd = (jnp.arange(MLA_HEADS) % 2 == 1)[None, :, None]
    wv = jnp.concatenate([jnp.where(odd, zv, uv), jnp.where(odd, uv, zv)], -1).reshape(MLA_KV_RANK, -1).astype(BF16)
    wo = w_out.astype(BF16)
    w_route = _pad_cols(jnp.concatenate([moe_w_router, moe_w_group], 1), LANES)
    w_route_hi = w_route.astype(BF16)
    w_route_mid = (w_route - w_route_hi.astype(F32)).astype(BF16)
    b_route = jnp.pad(jnp.concatenate([moe_b_router, moe_b_group]), (0, LANES - N_EXPERTS - N_GROUPS)).reshape(1, LANES)
    return dict(wg=wg, wm=wm, wc=wc, wift=wift, wa2=wa2, b_col=b_col, b_row=b_row, wqa=wqa, wqb=wqb, wkn=wkn,
                wv=wv, wo_g=wo[0:MIX_W], wo_m=wo[MIX_W:2 * MIX_W], wo_c=wo[2 * MIX_W:], w_route_hi=w_route_hi, w_route_mid=w_route_mid,
                b_route=b_route)


def _tile(total, want):
    t = min(total, want)
    assert total % t == 0
    return t


def kernel(x, mem, positions, w_in, w_out, gla_w_a2, gla_b_a, gla_norm_g, ml_conv_w, ml_b_i, ml_b_f, ml_norm_g, mla_q_norm_g, mla_w_uq, mla_kv_norm_g, mla_w_ukv, xa_w_q, xa_w_kv, xa_w_o, moe_w_group, moe_b_group, moe_w_router, moe_b_router, moe_w_gate, moe_w_up, moe_w_down, ln1_g, ln1_b, ln2_g, ln2_b, ln3_g, ln3_b):
    batch, seq, _ = x.shape
    mem_len = mem.shape[1]
    n = batch * seq
    depth = w_in.shape[0]
    assert seq % CHUNK == 0
    t_in = _tile(n, 512)
    tm = _tile(n, 1024)
    ts = _tile(seq, 512)
    tq = _tile(seq, 512)
    t_moe = _tile(n, 1024)
    row = lambda a: a.reshape(1, -1)

    cos_t, sin_t = _rope_tables(positions.reshape(n, 1), _tile(n, 2048))
    mem2d = mem.reshape(batch * mem_len, D_MODEL)
    experts_gate = moe_w_gate.reshape(-1, D_MODEL, D_EXPERT)
    experts_up = moe_w_up.reshape(-1, D_MODEL, D_EXPERT)
    experts_down = moe_w_down.reshape(-1, D_EXPERT, D_MODEL)
    h = x.reshape(n, D_MODEL)
    for l in range(depth):
        w = _layer_weights(w_in[l], w_out[l], gla_w_a2[l], ml_b_i[l], ml_b_f[l], mla_w_uq[l], mla_w_ukv[l],
                           moe_w_group[l], moe_b_group[l], moe_w_router[l], moe_b_router[l])
        yg, ym, yift, q, k, v = _in_proj(h, w["wg"], w["wm"], w["wc"], w["wift"], cos_t, sin_t,
                                         row(mla_q_norm_g[l]), row(mla_kv_norm_g[l]),
                                         w["wqa"], w["wqb"], w["wkn"], w["wv"], t_in)
        gates_row = yift.reshape(2 * MIX_HEADS, n // CHUNK, CHUNK).transpose(1, 0, 2)
        og = _gla(yg, w["wa2"], row(gla_b_a[l]), row(gla_norm_g[l]), batch, seq, ts)
        om = _mlstm(ym, gates_row, ml_conv_w[l], w["b_col"], w["b_row"], row(ml_norm_g[l]), batch, seq, ts)
        oc = _mla_attn(q, k, v, batch, seq, tq)
        xk, xv = _xa_kv(mem2d, xa_w_kv[l].astype(BF16), mem_len)
        xg, rank, grp, counts = _out_xattn_route(
            og, om, oc, h, w["wo_g"], w["wo_m"], w["wo_c"], row(ln1_g[l]), row(ln1_b[l]),
            xk, xv, xa_w_q[l].astype(BF16), xa_w_o[l].astype(BF16), row(ln2_g[l]), row(ln2_b[l]),
            w["w_route_hi"], w["w_route_mid"], w["b_route"], batch, seq, mem_len, tm)
        h = _moe(xg, rank, grp, counts, experts_gate, experts_up, experts_down, l * N_EXPERTS,
                 row(ln3_g[l]), row(ln3_b[l]), t_moe, t_moe)
    return h.reshape(batch, seq, D_MODEL)
```

```python
import functools

import numpy as np
import jax
import jax.numpy as jnp
from jax import lax
from jax.experimental import pallas as pl
from jax.experimental.pallas import tpu as pltpu

F32 = jnp.float32
BF16 = jnp.bfloat16

D_MODEL = 1024
CHUNK = 64
HEAD_D = 64
MIX_HEADS = 4
MIX_W = MIX_HEADS * HEAD_D
GLA_GATE_RANK = 16
GLA_TAU = 16.0
MLSTM_CONV = 4
MLA_HEADS = 8
MLA_NOPE = 64
MLA_ROPE = 32
MLA_V = 64
MLA_Q_RANK = 256
MLA_KV_RANK = 128
MLA_QK_PAD = 128
ROPE_BASE = 10000.0
LOG2_E = 1.4426950408889634
MLA_HEAD_PAIR = 2
XA_HEADS = 4
XA_DH = D_MODEL // XA_HEADS
N_GROUPS = 4
EXPERTS_PER_GROUP = 8
N_EXPERTS = N_GROUPS * EXPERTS_PER_GROUP
D_EXPERT = 256
DEPTH = 2
ALPHA = (2 * DEPTH) ** 0.25
LN_EPS = 1e-5
LANES = 128
SUBLANES = 8
MIXER_UNROLL = 8
SEG_V = 2 * MIX_W
SEG_GATE = 3 * MIX_W
SEG_TAIL = 4 * MIX_W
SEG_W = SEG_TAIL + LANES
MLA_SEG_W = MLA_Q_RANK + MLA_KV_RANK + 2 * LANES
XG_W = D_MODEL + LANES
EXPERTS_PER_STEP = 4
DMA_UNROLL = 32
GLA_LEVELS = (32, 16, 8, 4, 2, 1)
VMEM_LIMIT = 56 * 1024 * 1024

_IN_SIZES = (256, 256, 256, 256, GLA_GATE_RANK, 512, 256, 256, 4, 4, MLA_Q_RANK, MLA_KV_RANK, MLA_ROPE)
_IN_OFF = np.concatenate([[0], np.cumsum(_IN_SIZES)]).tolist()


def _params(sem):
    return pltpu.CompilerParams(dimension_semantics=sem, vmem_limit_bytes=VMEM_LIMIT)


def _full(shape):
    return pl.BlockSpec(shape, lambda *_: (0,) * len(shape))


def _layer_norm(x, g, b):
    mu = jnp.mean(x, -1, keepdims=True)
    xc = x - mu
    var = jnp.mean(xc * xc, -1, keepdims=True)
    return xc * lax.rsqrt(var + LN_EPS) * g + b


def _log_sigmoid(z):
    return jnp.minimum(z, 0.0) - jnp.log1p(jnp.exp(-jnp.abs(z)))


def _dot_nt(a, b):
    return lax.dot_general(a, b, (((1,), (1,)), ((), ())), preferred_element_type=F32)


def _dot_tn(a, b):
    return lax.dot_general(a, b, (((0,), (0,)), ((), ())), preferred_element_type=F32)


def _dot(a, b):
    return jnp.dot(a, b, preferred_element_type=F32)


def _head_norm(o, g):
    mu = jnp.mean(o, -1, keepdims=True)
    oc = o - mu
    var = jnp.mean(oc * oc, -1, keepdims=True)
    return oc * lax.rsqrt(var + LN_EPS) * g


def _in_proj_kernel(x_ref, wg_ref, wm_ref, wc_ref, wift_ref, cos_ref, sin_ref, gq_ref, gkv_ref,
                    wqa_ref, wqb_ref, wkn_ref, wv_ref, ones_ref,
                    yg_ref, ym_ref, yift_ref, q_ref, k_ref, v_ref, yc_ref):
    xb = x_ref[...].astype(BF16)
    yg_ref[...] = _dot(xb, wg_ref[...])
    ym_ref[...] = _dot(xb, wm_ref[...])
    yift_ref[...] = _dot_nt(wift_ref[...], xb)
    yc_ref[...] = _dot(xb, wc_ref[...])
    _mla_prep_tile(yc_ref, cos_ref, sin_ref, gq_ref, gkv_ref, wqa_ref, wqb_ref, wkn_ref, wv_ref, ones_ref,
                   q_ref, k_ref, v_ref)


def _in_proj(x2d, wg, wm, wc, wift, cos_t, sin_t, gq, gkv, wqa, wqb, wkn, wv, tm):
    n = x2d.shape[0]
    ones_row = np.zeros((1, MLA_HEADS * LANES), np.float32)
    for h in range(MLA_HEADS):
        ones_row[0, h * LANES + (MLA_V if h % 2 == 0 else 0)] = 1.0
    ones_row = jnp.asarray(ones_row)
    row = lambda w: pl.BlockSpec((tm, w), lambda i: (i, 0))
    qk_w = MLA_HEADS * MLA_QK_PAD
    return pl.pallas_call(
        _in_proj_kernel,
        grid=(n // tm,),
        in_specs=[row(D_MODEL), _full(wg.shape), _full(wm.shape), _full(wc.shape), _full(wift.shape),
                  row(LANES), row(LANES), _full(gq.shape), _full(gkv.shape),
                  _full(wqa.shape), _full(wqb.shape), _full(wkn.shape), _full(wv.shape), _full(ones_row.shape)],
        out_specs=[row(SEG_W), row(SEG_W), pl.BlockSpec((8, tm), lambda i: (0, i)),
                   row(qk_w), row(qk_w), row(MLA_HEADS * LANES)],
        out_shape=[jax.ShapeDtypeStruct((n, SEG_W), F32), jax.ShapeDtypeStruct((n, SEG_W), F32),
                   jax.ShapeDtypeStruct((8, n), F32),
                   jax.ShapeDtypeStruct((n, qk_w), BF16), jax.ShapeDtypeStruct((n, qk_w), BF16),
                   jax.ShapeDtypeStruct((n, MLA_HEADS * LANES), BF16)],
        scratch_shapes=[pltpu.VMEM((tm, MLA_SEG_W), F32)],
        compiler_params=_params(("parallel",)),
        name="in_proj_mla_prep",
    )(x2d, wg, wm, wc, wift, cos_t, sin_t, gq, gkv, wqa, wqb, wkn, wv, ones_row)


def _split3(x):
    hi = x.astype(BF16)
    r1 = x - hi.astype(F32)
    mid = r1.astype(BF16)
    lo = (r1 - mid.astype(F32)).astype(BF16)
    return hi, mid, lo


def _cumsum_rows(tri, x):
    w = x.shape[1]
    y = _dot(tri, jnp.concatenate(_split3(x), axis=1))
    return y[:, 0:w] + y[:, w:2 * w] + y[:, 2 * w:3 * w]


def _gla_constants():
    t = np.arange(CHUNK)
    n_lv = len(GLA_LEVELS)
    masks = np.zeros((n_lv + 1, CHUNK, CHUNK), np.float32)
    right = np.zeros((n_lv, CHUNK, 1), np.float32)
    for li, n in enumerate(GLA_LEVELS):
        blk = t // (2 * n)
        is_right = (t % (2 * n)) >= n
        masks[li] = ((blk[:, None] == blk[None, :]) & is_right[:, None] & ~is_right[None, :])
        right[li, :, 0] = is_right
    masks[n_lv] = np.eye(CHUNK)
    return masks, right


def _gla_level_exponents(b, log_a, right_ref):
    row = lax.broadcasted_iota(jnp.int32, (CHUNK, 1), 0)
    exps = []
    for li, n in enumerate(GLA_LEVELS):
        if n >= SUBLANES // 2:
            per_blk = max(2 * n // SUBLANES, 1)
            b4 = b.reshape(CHUNK // (per_blk * SUBLANES), per_blk, SUBLANES, MIX_W)
            r_reg, r_sub = divmod(n - 1, SUBLANES)
            src = b4[:, r_reg:r_reg + 1, r_sub:r_sub + 1, :]
            b_r = jnp.broadcast_to(src, b4.shape).reshape(CHUNK, MIX_W)
            exps.append(jnp.where(right_ref[li] > 0.0, b - b_r, b_r - b))
        elif n == 2:
            pos = row % 4
            nxt = pltpu.roll(log_a, CHUNK - 1, 0)
            prv = pltpu.roll(log_a, 1, 0)
            exps.append(jnp.where(pos == 0, nxt, jnp.where(pos == 1, 0.0,
                                                            jnp.where(pos == 2, log_a, log_a + prv))))
        else:
            exps.append(jnp.where(right_ref[li] > 0.0, log_a, 0.0))
    return exps


def _pair_blocks(z, low_half):
    zero = jnp.zeros_like(z)
    return jnp.concatenate([jnp.where(low_half, z, zero), jnp.where(low_half, zero, z)], axis=0)


def _gla_kernel(y_ref, wa2_ref, ba_ref, g_ref, tri_ref, mask_ref, right_ref, o_ref, st_ref, la_ref, *, ts,
                unroll):
    n_lv = len(GLA_LEVELS)
    pairs = MIX_HEADS // 2
    low_half = lax.broadcasted_iota(jnp.int32, (CHUNK, LANES), 1) < HEAD_D
    same_head = (lax.broadcasted_iota(jnp.int32, (LANES, LANES), 0) < HEAD_D) == (
        lax.broadcasted_iota(jnp.int32, (LANES, LANES), 1) < HEAD_D)

    @pl.when(pl.program_id(1) == 0)
    def _():
        st_ref[...] = jnp.zeros_like(st_ref)

    z = _dot(y_ref[:, SEG_TAIL:SEG_W].astype(BF16), wa2_ref[...]) + ba_ref[...]
    la_ref[...] = _log_sigmoid(z) * (1.0 / GLA_TAU)

    def chunk(c, carry):
        rows = pl.ds(pl.multiple_of(c * CHUNK, CHUNK), CHUNK)
        q = y_ref[rows, 0:MIX_W] * (HEAD_D ** -0.5)
        k = y_ref[rows, MIX_W:SEG_V]
        log_a = la_ref[rows, :]
        b = _cumsum_rows(tri_ref[...], log_a)
        b_end = b[CHUNK - 1:CHUNK, :]
        scores = [jnp.zeros((CHUNK, LANES), F32) for _ in range(pairs)]
        for li, e in enumerate(_gla_level_exponents(b, log_a, right_ref)):
            x = (jnp.where(right_ref[li] > 0.0, q, k) * jnp.exp(e)).astype(BF16)
            for p in range(pairs):
                x2 = x[:, p * LANES:(p + 1) * LANES]
                scores[p] = scores[p] + _dot_nt(x2, _pair_blocks(x2, low_half)) * mask_ref[li]
        qb = q.astype(BF16)
        kb = k.astype(BF16)
        q_in = (q * jnp.exp(b)).astype(BF16)
        k_out = (k * jnp.exp(b_end - b)).astype(BF16)
        dec_end = jnp.exp(b_end)
        for p in range(pairs):
            pl_ = slice(p * LANES, (p + 1) * LANES)
            a = scores[p] + _dot_nt(qb[:, pl_], _pair_blocks(kb[:, pl_], low_half)) * mask_ref[n_lv]
            v2 = y_ref[rows, SEG_V + p * LANES:SEG_V + (p + 1) * LANES].astype(BF16)
            st = st_ref[p]
            o = _dot(a.astype(BF16), _pair_blocks(v2, low_half)) + _dot_nt(q_in[:, pl_], st.astype(BF16))
            st_ref[p] = st * dec_end[:, pl_] + jnp.where(same_head, _dot_tn(v2, k_out[:, pl_]), 0.0)
            normed = jnp.zeros((CHUNK, LANES), F32)
            for mine in (low_half, jnp.logical_not(low_half)):
                mu = jnp.sum(jnp.where(mine, o, 0.0), -1, keepdims=True) * (1.0 / HEAD_D)
                cen = jnp.where(mine, o - mu, 0.0)
                var = jnp.sum(cen * cen, -1, keepdims=True) * (1.0 / HEAD_D)
                normed = normed + cen * lax.rsqrt(var + LN_EPS)
            r_gate = y_ref[rows, SEG_GATE + p * LANES:SEG_GATE + (p + 1) * LANES]
            o_ref[rows, pl_] = (normed * g_ref[:, pl_] * (r_gate * jax.nn.sigmoid(r_gate))).astype(o_ref.dtype)
        return carry

    lax.fori_loop(0, ts // CHUNK, chunk, 0, unroll=unroll)


def _gla(yg, wa2, ba, g, batch, seq, ts):
    n = yg.shape[0]
    nt = seq // ts
    masks, right = _gla_constants()
    masks = np.concatenate([masks, masks], axis=-1)
    tri = jnp.asarray(np.tril(np.ones((CHUNK, CHUNK), np.float32)), BF16)
    return pl.pallas_call(
        functools.partial(_gla_kernel, ts=ts, unroll=MIXER_UNROLL),
        grid=(batch, nt),
        in_specs=[pl.BlockSpec((ts, SEG_W), lambda b, i: (b * nt + i, 0)),
                  _full(wa2.shape), _full(ba.shape), _full(g.shape),
                  _full(tri.shape), _full(masks.shape), _full(right.shape)],
        out_specs=pl.BlockSpec((ts, MIX_W), lambda b, i: (b * nt + i, 0)),
        out_shape=jax.ShapeDtypeStruct((n, MIX_W), BF16),
        scratch_shapes=[pltpu.VMEM((MIX_HEADS // 2, LANES, LANES), F32),
                        pltpu.VMEM((ts, MIX_W), F32)],
        compiler_params=_params(("parallel", "arbitrary")),
        name="gla_mixer",
    )(yg, wa2, ba, g, tri, jnp.asarray(masks), jnp.asarray(right))


def _mlstm_kernel(y_ref, gr_ref, cw_ref, bcol_ref, brow_ref, g_ref, tri_ref, eb_ref, eye_ref, o_ref,
                  xe_ref, qk_ref, vx_ref, fcl_ref, wsl_ref, fcb_ref, wsb_ref, rv_ref, dec_ref,
                  w_ref, em_ref, upd_ref, c_ref, m_ref, *, ts):
    first = pl.program_id(1) == 0

    @pl.when(first)
    def _():
        xe_ref[0:8, :] = jnp.zeros((8, 2 * MIX_W), F32)
        c_ref[...] = jnp.zeros_like(c_ref)
        m_ref[...] = jnp.zeros_like(m_ref)

    @pl.when(jnp.logical_not(first))
    def _():
        xe_ref[0:8, :] = xe_ref[ts:ts + 8, :]

    xe_ref[8:ts + 8, :] = y_ref[:, 0:2 * MIX_W]
    for c in range(ts // CHUNK):
        r0 = 8 + c * CHUNK
        conv = cw_ref[MLSTM_CONV - 1:MLSTM_CONV, :] * xe_ref[r0:r0 + CHUNK, :]
        for j in range(MLSTM_CONV - 1):
            lo = r0 - (MLSTM_CONV - 1) + j
            conv = conv + cw_ref[j:j + 1, :] * xe_ref[lo:lo + CHUNK, :]
        qk_ref[c * CHUNK:(c + 1) * CHUNK, :] = conv * jax.nn.sigmoid(conv)

    ones_col = (lax.broadcasted_iota(jnp.int32, (ts, HEAD_D), 1) == 0).astype(BF16)
    for h in range(MIX_HEADS):
        v_at, one_at = (0, HEAD_D) if h % 2 == 0 else (HEAD_D, 0)
        vx_ref[:, h * LANES + v_at:h * LANES + v_at + HEAD_D] = (
            y_ref[:, SEG_V + h * HEAD_D:SEG_V + (h + 1) * HEAD_D].astype(BF16))
        vx_ref[:, h * LANES + one_at:h * LANES + one_at + HEAD_D] = ones_col

    tri = tri_ref[...]
    expand = eb_ref[...]
    nck = ts // CHUNK
    lane = lax.broadcasted_iota(jnp.int32, (1, LANES), 1)
    f_lanes = (lane >= MIX_HEADS) & (lane < 2 * MIX_HEADS)
    head_lane = (lax.broadcasted_iota(jnp.int32, (MIX_HEADS, LANES), 1)
                 == lax.broadcasted_iota(jnp.int32, (MIX_HEADS, LANES), 0) + MIX_HEADS)

    def to_lanes(col):
        return jnp.sum(jnp.where(head_lane, col, 0.0), 0, keepdims=True)

    g_rows = gr_ref[...] + brow_ref[...]
    ls_rows = _log_sigmoid(g_rows).reshape(nck * SUBLANES, CHUNK)
    fcum_rows = sum(_dot_nt(part, tri) for part in _split3(ls_rows)).reshape(nck, SUBLANES, CHUNK)
    m_col = m_ref[0:MIX_HEADS, 0:1]
    for c in range(nck):
        g_col = y_ref[c * CHUNK:(c + 1) * CHUNK, SEG_TAIL:SEG_W] + bcol_ref[...]
        fcum_col = jnp.where(f_lanes, _cumsum_rows(tri, _log_sigmoid(g_col)), 0.0)
        fcl_ref[c * CHUNK:(c + 1) * CHUNK, :] = fcum_col
        g_row = g_rows[c]
        fcum_row = fcum_rows[c]
        f_row = fcum_row[MIX_HEADS:2 * MIX_HEADS, :]
        i_row = g_row[0:MIX_HEADS, :]
        f_end = f_row[:, CHUNK - 1:CHUNK]
        rv_ref[c, 0:MIX_HEADS, :] = jnp.concatenate(
            [i_row - f_row, jnp.broadcast_to(m_col, (MIX_HEADS, HEAD_D))], axis=1)
        m_new = jnp.maximum(f_end + m_col, jnp.max(f_end - f_row + i_row, -1, keepdims=True))
        m_new_l = to_lanes(m_new)
        f_end_l = fcum_col[CHUNK - 1:CHUNK, :]
        i_shift = pltpu.roll(g_col, MIX_HEADS, 1)
        wsl_ref[c * CHUNK:(c + 1) * CHUNK, :] = jnp.where(
            f_lanes, jnp.exp(f_end_l - fcum_col + i_shift - m_new_l), 0.0)
        dec_ref[c] = jnp.broadcast_to(jnp.exp(f_end_l + to_lanes(m_col) - m_new_l), (SUBLANES, LANES))
        m_col = m_new
    m_ref[0:MIX_HEADS, :] = jnp.broadcast_to(m_col, (MIX_HEADS, LANES))
    slab = 2 * CHUNK
    for i in range(ts // slab):
        sr = slice(i * slab, (i + 1) * slab)
        fcb_ref[sr, :] = sum(_dot(part, expand) for part in _split3(fcl_ref[sr, :]))
        ws_hi, ws_mid, _ = _split3(wsl_ref[sr, :])
        wsb_ref[sr, :] = _dot(ws_hi, expand) + _dot(ws_mid, expand)

    t_idx = lax.broadcasted_iota(jnp.int32, (CHUNK, LANES), 0)
    s_idx = lax.broadcasted_iota(jnp.int32, (CHUNK, LANES), 1)
    keep = (s_idx <= t_idx) | (s_idx >= CHUNK)
    low_half = s_idx < HEAD_D

    def weights(c, carry):
        rows = pl.ds(pl.multiple_of(c * CHUNK, CHUNK), CHUNK)
        rv = rv_ref[c, 0:MIX_HEADS, :]
        for p in range(MIX_HEADS // 2):
            q2 = (qk_ref[rows, p * LANES:(p + 1) * LANES] * (HEAD_D ** -0.5)).astype(BF16)
            k2 = qk_ref[rows, MIX_W + p * LANES:MIX_W + (p + 1) * LANES]
            for h in (2 * p, 2 * p + 1):
                mine = low_half if h % 2 == 0 else jnp.logical_not(low_half)
                hb = slice(h * LANES, (h + 1) * LANES)
                logw = jnp.where(keep, fcb_ref[rows, hb] + rv[h:h + 1, :], -jnp.inf)
                m_t = jnp.max(logw, -1, keepdims=True)
                k_h = jnp.where(mine, k2, 0.0)
                qk = _dot_nt(q2, jnp.concatenate([k_h.astype(BF16), eye_ref[h % 2]], axis=0))
                w_ref[c * MIX_HEADS + h] = (jnp.exp(logw - m_t) * qk).astype(BF16)
                em_ref[c * MIX_HEADS + h] = jnp.exp(-m_t)
                upd = _dot_tn((k_h * wsb_ref[rows, hb]).astype(BF16), vx_ref[rows, hb])
                off = (h % 2) * HEAD_D
                upd_ref[c * MIX_HEADS + h] = upd[off:off + HEAD_D, :]
        return carry

    lax.fori_loop(0, nck, weights, 0, unroll=MIXER_UNROLL)

    def chunk(c, carry):
        rows = pl.ds(pl.multiple_of(c * CHUNK, CHUNK), CHUNK)
        dec_l = dec_ref[c]
        for p in range(MIX_HEADS // 2):
            pl_ = slice(p * LANES, (p + 1) * LANES)
            pair = jnp.zeros((CHUNK, LANES), F32)
            for h in (2 * p, 2 * p + 1):
                mine = low_half if h % 2 == 0 else jnp.logical_not(low_half)
                c_st = c_ref[h]
                num = _dot(w_ref[c * MIX_HEADS + h],
                           jnp.concatenate([vx_ref[rows, h * LANES:(h + 1) * LANES], c_st.astype(BF16)], axis=0))
                c_ref[h] = dec_l[0:1, MIX_HEADS + h:MIX_HEADS + h + 1] * c_st + upd_ref[c * MIX_HEADS + h]
                den = num[:, HEAD_D:HEAD_D + 1] if h % 2 == 0 else num[:, 0:1]
                r = 1.0 / jnp.maximum(jnp.abs(den), em_ref[c * MIX_HEADS + h])
                mu = jnp.sum(jnp.where(mine, num, 0.0), -1, keepdims=True) * (1.0 / HEAD_D)
                cen = jnp.where(mine, num - mu, 0.0)
                var = jnp.sum(cen * cen, -1, keepdims=True) * (1.0 / HEAD_D)
                pair = pair + cen * (r * lax.rsqrt(r * r * var + LN_EPS))
            o_gate = y_ref[rows, SEG_GATE + p * LANES:SEG_GATE + (p + 1) * LANES]
            o_ref[rows, pl_] = (pair * g_ref[:, pl_] * jax.nn.sigmoid(o_gate)).astype(o_ref.dtype)
        return carry

    lax.fori_loop(0, nck, chunk, 0, unroll=MIXER_UNROLL)


def _mlstm(ym, gates_row, conv_w, b_col, b_row, g, batch, seq, ts):
    n = ym.shape[0]
    nt = seq // ts
    nck = ts // CHUNK
    tri = jnp.asarray(np.tril(np.ones((CHUNK, CHUNK), np.float32)), BF16)
    expand = np.zeros((LANES, MIX_HEADS * LANES), np.float32)
    for h in range(MIX_HEADS):
        expand[MIX_HEADS + h, h * LANES:(h + 1) * LANES] = 1.0
    eye = np.zeros((2, HEAD_D, LANES), np.float32)
    eye[0, :, 0:HEAD_D] = np.eye(HEAD_D)
    eye[1, :, HEAD_D:] = np.eye(HEAD_D)
    expand = jnp.asarray(expand, BF16)
    eye = jnp.asarray(eye, BF16)
    return pl.pallas_call(
        functools.partial(_mlstm_kernel, ts=ts),
        grid=(batch, nt),
        in_specs=[pl.BlockSpec((ts, SEG_W), lambda b, i: (b * nt + i, 0)),
                  pl.BlockSpec((nck, 8, CHUNK), lambda b, i: (b * nt + i, 0, 0)),
                  _full(conv_w.shape), _full(b_col.shape), _full(b_row.shape), _full(g.shape),
                  _full(tri.shape), _full(expand.shape), _full(eye.shape)],
        out_specs=pl.BlockSpec((ts, MIX_W), lambda b, i: (b * nt + i, 0)),
        out_shape=jax.ShapeDtypeStruct((n, MIX_W), BF16),
        scratch_shapes=[pltpu.VMEM((ts + 8, 2 * MIX_W), F32),
                        pltpu.VMEM((ts, 2 * MIX_W), F32),
                        pltpu.VMEM((ts, MIX_HEADS * LANES), BF16),
                        pltpu.VMEM((ts, LANES), F32),
                        pltpu.VMEM((ts, LANES), F32),
                        pltpu.VMEM((ts, MIX_HEADS * LANES), F32),
                        pltpu.VMEM((ts, MIX_HEADS * LANES), F32),
                        pltpu.VMEM((nck, SUBLANES, LANES), F32),
                        pltpu.VMEM((nck, SUBLANES, LANES), F32),
                        pltpu.VMEM((nck * MIX_HEADS, CHUNK, LANES), BF16),
                        pltpu.VMEM((nck * MIX_HEADS, CHUNK, 1), F32),
                        pltpu.VMEM((nck * MIX_HEADS, HEAD_D, LANES), F32),
                        pltpu.VMEM((MIX_HEADS, HEAD_D, LANES), F32),
                        pltpu.VMEM((SUBLANES, LANES), F32)],
        compiler_params=_params(("parallel", "arbitrary")),
        name="mlstm_mixer",
    )(ym, gates_row, conv_w, b_col, b_row, g, tri, expand, eye)


def _rope_table_kernel(pos_ref, inv_ref, cos_ref, sin_ref):
    ang = pos_ref[...].astype(F32) * inv_ref[...]
    lane = lax.broadcasted_iota(jnp.int32, ang.shape, 1)
    rot = (lane >= MLA_NOPE) & (lane < MLA_NOPE + MLA_ROPE)
    first_half = lane < MLA_NOPE + MLA_ROPE // 2
    cos_ref[...] = jnp.where(lane < MLA_NOPE, 1.0, jnp.where(rot, jnp.cos(ang), 0.0))
    s = jnp.sin(ang)
    sin_ref[...] = jnp.where(rot, jnp.where(first_half, -s, s), 0.0)


def _rope_tables(pos_col, tm):
    n = pos_col.shape[0]
    half = MLA_ROPE // 2
    inv = ROPE_BASE ** (-np.arange(half, dtype=np.float32) / half)
    inv_row = np.zeros((1, LANES), np.float32)
    inv_row[0, MLA_NOPE:MLA_NOPE + half] = inv
    inv_row[0, MLA_NOPE + half:MLA_NOPE + MLA_ROPE] = inv
    return pl.pallas_call(
        _rope_table_kernel,
        grid=(n // tm,),
        in_specs=[pl.BlockSpec((tm, 1), lambda i: (i, 0)), _full((1, LANES))],
        out_specs=[pl.BlockSpec((tm, LANES), lambda i: (i, 0))] * 2,
        out_shape=[jax.ShapeDtypeStruct((n, LANES), F32)] * 2,
        compiler_params=_params(("parallel",)),
        name="rope_tables",
    )(pos_col, jnp.asarray(inv_row))


def _mla_prep_tile(y_ref, cos_ref, sin_ref, gq_ref, gkv_ref, wqa_ref, wqb_ref, wkn_ref, wv_ref, ones_ref,
                   q_ref, k_ref, v_ref):
    def rms(x, g):
        return x * lax.rsqrt(jnp.mean(x * x, -1, keepdims=True) + LN_EPS) * g

    cos = cos_ref[...]
    sin = sin_ref[...]
    cq = rms(y_ref[:, 0:MLA_Q_RANK], gq_ref[...]).astype(BF16)
    ckv = rms(y_ref[:, MLA_Q_RANK:MLA_Q_RANK + MLA_KV_RANK], gkv_ref[...]).astype(BF16)
    kr = MLA_Q_RANK + MLA_KV_RANK
    k_rope = y_ref[:, kr:kr + LANES] * cos + y_ref[:, kr + LANES:MLA_SEG_W] * sin
    qa = _dot(cq, wqa_ref[...])
    qb = _dot(cq, wqb_ref[...])
    kn = _dot(ckv, wkn_ref[...])
    scale = (MLA_NOPE + MLA_ROPE) ** -0.5 * LOG2_E
    for h in range(MLA_HEADS):
        sl = slice(h * MLA_QK_PAD, (h + 1) * MLA_QK_PAD)
        q_ref[:, sl] = ((qa[:, sl] * cos + qb[:, sl] * sin) * scale).astype(q_ref.dtype)
        k_ref[:, sl] = (kn[:, sl] + k_rope).astype(k_ref.dtype)
    v_ref[...] = (_dot(ckv, wv_ref[...]) + ones_ref[...]).astype(v_ref.dtype)


def _mla_attn_kernel(q_ref, k_ref, v_ref, o_ref, *, seq, tq):
    t_chunk = lax.broadcasted_iota(jnp.int32, (tq, tq), 0) // CHUNK
    s_chunk = lax.broadcasted_iota(jnp.int32, (tq, tq), 1) // CHUNK
    diag_mask = s_chunk <= t_chunk
    low_half = lax.broadcasted_iota(jnp.int32, (tq, LANES), 1) < MLA_V
    for i in range(seq // tq):
        rows = slice(i * tq, (i + 1) * tq)
        res = []
        for hh in range(MLA_HEAD_PAIR):
            ql = slice(hh * MLA_QK_PAD, (hh + 1) * MLA_QK_PAD)
            vl = slice(hh * LANES, (hh + 1) * LANES)
            q = q_ref[0, rows, ql]
            s_d = jnp.where(diag_mask, _dot_nt(q, k_ref[0, rows, ql]), -jnp.inf)
            m = jnp.max(s_d, -1, keepdims=True)
            if i > 0:
                s_o = _dot_nt(q, k_ref[0, 0:i * tq, ql])
                m = jnp.maximum(m, jnp.max(s_o, -1, keepdims=True))
            o = _dot(jnp.exp2(s_d - m).astype(BF16), v_ref[0, rows, vl])
            if i > 0:
                o = o + _dot(jnp.exp2(s_o - m).astype(BF16), v_ref[0, 0:i * tq, vl])
            l = o[:, MLA_V:MLA_V + 1] if hh == 0 else o[:, 0:1]
            res.append(o / l)
        o_ref[0, rows, :] = jnp.where(low_half, res[0], res[1]).astype(o_ref.dtype)


def _mla_attn(q, k, v, batch, seq, tq):
    n = q.shape[0]
    qk_w = MLA_HEADS * MLA_QK_PAD
    v_w = MLA_HEADS * MLA_V
    pair = lambda w: pl.BlockSpec((1, seq, MLA_HEAD_PAIR * w), lambda b, h: (b, 0, h))
    out = pl.pallas_call(
        functools.partial(_mla_attn_kernel, seq=seq, tq=tq),
        grid=(batch, MLA_HEADS // MLA_HEAD_PAIR),
        in_specs=[pair(MLA_QK_PAD), pair(MLA_QK_PAD), pair(LANES)],
        out_specs=pair(MLA_V),
        out_shape=jax.ShapeDtypeStruct((batch, seq, v_w), BF16),
        compiler_params=_params(("parallel", "parallel")),
        name="mla_attention",
    )(q.reshape(batch, seq, qk_w), k.reshape(batch, seq, qk_w), v.reshape(batch, seq, MLA_HEADS * LANES))
    return out.reshape(n, v_w)


def _xa_kv_kernel(mem_ref, w_ref, k_ref, v_ref):
    kv = _dot(mem_ref[...].astype(BF16), w_ref[...])
    k_ref[...] = kv[:, 0:D_MODEL].astype(k_ref.dtype)
    v_ref[...] = kv[:, D_MODEL:2 * D_MODEL].astype(v_ref.dtype)


def _xa_kv(mem2d, w_kv, mem_len):
    n = mem2d.shape[0]
    row = pl.BlockSpec((mem_len, D_MODEL), lambda i: (i, 0))
    return pl.pallas_call(
        _xa_kv_kernel,
        grid=(n // mem_len,),
        in_specs=[row, _full(w_kv.shape)],
        out_specs=[row, row],
        out_shape=[jax.ShapeDtypeStruct((n, D_MODEL), BF16)] * 2,
        compiler_params=_params(("parallel",)),
        name="xattn_kv",
    )(mem2d, w_kv)


def _route_tile(x, wh_ref, wm_ref, b_ref, tri_ref, cnt_ref):
    xh = x.astype(BF16)
    xm = (x - xh.astype(F32)).astype(BF16)
    logits = _dot(xh, wh_ref[...]) + _dot(xh, wm_ref[...]) + _dot(xm, wh_ref[...]) + b_ref[...]
    lane = lax.broadcasted_iota(jnp.int32, logits.shape, 1).astype(F32)
    is_group = (lane >= N_EXPERTS) & (lane < N_EXPERTS + N_GROUPS)
    g_max = jnp.max(jnp.where(is_group, logits, -jnp.inf), -1, keepdims=True)
    g_sum = jnp.sum(jnp.where(is_group, jnp.exp(logits - g_max), 0.0), -1, keepdims=True)
    g_p = 1.0 / g_sum
    g_idx = jnp.min(jnp.where(is_group & (logits == g_max), lane - N_EXPERTS, float(LANES)), -1, keepdims=True)
    in_group = (lane < N_EXPERTS) & (jnp.floor(lane * (1.0 / EXPERTS_PER_GROUP)) == g_idx)
    e_max = jnp.max(jnp.where(in_group, logits, -jnp.inf), -1, keepdims=True)
    e_exp = jnp.where(in_group, jnp.exp(logits - e_max), 0.0)
    prob = e_exp / jnp.sum(e_exp, -1, keepdims=True)
    cand = jnp.where(in_group, prob, -1.0)
    p1 = jnp.max(cand, -1, keepdims=True)
    i1 = jnp.min(jnp.where(cand == p1, lane, float(LANES)), -1, keepdims=True)
    cand2 = jnp.where(lane == i1, -1.0, cand)
    p2 = jnp.max(cand2, -1, keepdims=True)
    i2 = jnp.min(jnp.where(cand2 == p2, lane, float(LANES)), -1, keepdims=True)
    p_sum = p1 + p2
    gates = jnp.where(lane == i1, g_p * (p1 / p_sum), 0.0) + jnp.where(lane == i2, g_p * (p2 / p_sum), 0.0)
    onehot = jnp.where(lane == g_idx, 1.0, 0.0)
    before = _dot(tri_ref[...], onehot.astype(BF16)) + cnt_ref[...]
    rank = jnp.sum(onehot * before, -1, keepdims=True).astype(jnp.int32)
    cnt_ref[...] += jnp.sum(onehot, 0, keepdims=True)
    return gates, rank, g_idx.astype(jnp.int32)


def _xattn_kernel(og_ref, om_ref, oc_ref, x0_ref, wog_ref, wom_ref, woc_ref, g1_ref, b1_ref,
                  k_ref, v_ref, wq_ref, wo_ref, g_ref, b_ref, wrh_ref, wrm_ref, br_ref, tri_ref,
                  xg_ref, rank_ref, grp_ref, cnt_out_ref, cnt_ref):
    @pl.when((pl.program_id(0) == 0) & (pl.program_id(1) == 0))
    def _():
        cnt_ref[...] = jnp.zeros_like(cnt_ref)

    mix = _dot(og_ref[...], wog_ref[...]) + _dot(om_ref[...], wom_ref[...]) + _dot(oc_ref[...], woc_ref[...])
    x = _layer_norm(ALPHA * x0_ref[...] + mix, g1_ref[...], b1_ref[...])
    q =(_dot(x.astype(BF16), wq_ref[...]) * (XA_DH ** -0.5)).astype(BF16)
    heads = []
    for h in range(XA_HEADS):
        sl = slice(h * XA_DH, (h + 1) * XA_DH)
        s = _dot_nt(q[:, sl], k_ref[:, sl])
        p = jnp.exp(s - jnp.max(s, -1, keepdims=True))
        p = p / jnp.sum(p, -1, keepdims=True)
        heads.append(_dot(p.astype(BF16), v_ref[:, sl]).astype(BF16))
    out = _dot(jnp.concatenate(heads, axis=1), wo_ref[...])
    x2 = _layer_norm(ALPHA * x + out, g_ref[...], b_ref[...])
    gates, rank, grp = _route_tile(x2, wrh_ref, wrm_ref, br_ref, tri_ref, cnt_ref)
    xg_ref[:, 0:D_MODEL] = x2
    xg_ref[:, D_MODEL:] = gates
    rank_ref[...] = rank
    grp_ref[...] = grp
    cnt_out_ref[...] = cnt_ref[...]


def _out_xattn_route(og, om, oc, x0, wog, wom, woc, g1, b1, xk, xv, wq, wo, g, b, w_route_hi, w_route_mid, b_route,
                     batch, seq, mem_len, tm):
    n = x0.shape[0]
    nt = seq // tm
    tri = jnp.asarray(np.tril(np.ones((tm, tm), np.float32), -1), BF16)
    tile = lambda w: pl.BlockSpec((tm, w), lambda bi, i: (bi * nt + i, 0))
    mem = pl.BlockSpec((mem_len, D_MODEL), lambda bi, i: (bi, 0))
    return pl.pallas_call(
        _xattn_kernel,
        grid=(batch, nt),
        in_specs=[tile(MIX_W), tile(MIX_W), tile(MLA_HEADS * MLA_V), tile(D_MODEL),
                  _full(wog.shape), _full(wom.shape), _full(woc.shape), _full(g1.shape), _full(b1.shape),
                  mem, mem, _full(wq.shape), _full(wo.shape), _full(g.shape), _full(b.shape),
                  _full(w_route_hi.shape), _full(w_route_mid.shape), _full(b_route.shape), _full(tri.shape)],
        out_specs=[tile(XG_W), tile(1), tile(1), _full((1, LANES))],
        out_shape=[jax.ShapeDtypeStruct((n, XG_W), F32), jax.ShapeDtypeStruct((n, 1), jnp.int32),
                   jax.ShapeDtypeStruct((n, 1), jnp.int32), jax.ShapeDtypeStruct((1, LANES), F32)],
        scratch_shapes=[pltpu.VMEM((1, LANES), F32)],
        compiler_params=_params(("arbitrary", "arbitrary")),
        name="out_proj_xattn_route",
    )(og, om, oc, x0, wog, wom, woc, g1, b1, xk, xv, wq, wo, g, b, w_route_hi, w_route_mid, b_route, tri)


def _dispatch_kernel(pos_ref, x_ref, xs_hbm, sem, *, tm):
    def issue(t, carry):
        pltpu.make_async_copy(x_ref.at[pl.ds(t, 1)], xs_hbm.at[pl.ds(pos_ref[t], 1)], sem).start()
        return carry

    lax.fori_loop(0, tm, issue, 0, unroll=DMA_UNROLL)
    pltpu.make_async_copy(x_ref, xs_hbm.at[pl.ds(0, tm)], sem).wait()


def _dispatch(xg, pos, tm):
    n = xg.shape[0]
    return pl.pallas_call(
        functools.partial(_dispatch_kernel, tm=tm),
        grid=(n // tm,),
        in_specs=[pl.BlockSpec((tm,), lambda i: (i,), memory_space=pltpu.SMEM),
                  pl.BlockSpec((tm, XG_W), lambda i: (i, 0))],
        out_specs=pl.BlockSpec(memory_space=pl.ANY),
        out_shape=jax.ShapeDtypeStruct(xg.shape, xg.dtype),
        scratch_shapes=[pltpu.SemaphoreType.DMA(())],
        compiler_params=_params(("arbitrary",)),
        name="moe_dispatch",
    )(pos, xg)


def _moe_ffn_kernel(blk_ref, grp_ref, first_ref, valid_ref, xs_ref, wg_ref, wu_ref, wd_ref, o_ref, xb_ref):
    w = pl.program_id(0)
    e = pl.program_id(1)

    @pl.when((first_ref[w] == 1) & (e == 0))
    def _():
        xb_ref[...] = xs_ref[:, 0:D_MODEL].astype(BF16)
        o_ref[...] = jnp.zeros_like(o_ref)

    @pl.when(valid_ref[w] == 1)
    def _():
        xb = xb_ref[...]
        gates = xs_ref[:, D_MODEL:]
        lane = lax.broadcasted_iota(jnp.int32, gates.shape, 1)
        hidden = []
        for j in range(EXPERTS_PER_STEP):
            expert = grp_ref[w] * EXPERTS_PER_GROUP + e * EXPERTS_PER_STEP + j
            w_tok = jnp.sum(jnp.where(lane == expert, gates, 0.0), -1, keepdims=True)
            hg = _dot(xb, wg_ref[j].astype(BF16))
            hu = _dot(xb, wu_ref[j].astype(BF16))
            hidden.append((hg * jax.nn.sigmoid(hg) * hu * w_tok).astype(BF16))
        wd = wd_ref[...].astype(BF16).reshape(EXPERTS_PER_STEP * D_EXPERT, D_MODEL)
        o_ref[...] += _dot(jnp.concatenate(hidden, axis=1), wd)


def _moe_ffn(xs, items, w_gate, w_up, w_down, rb, e_base):
    n = xs.shape[0]
    blk, grp, first, valid = items

    steps = EXPERTS_PER_GROUP // EXPERTS_PER_STEP

    def expert(w, e, blk, grp, first, valid):
        return ((e_base + grp[w] * EXPERTS_PER_GROUP) // EXPERTS_PER_STEP
                + jnp.where(valid[w] == 1, e, steps - 1), 0, 0)

    grid_spec = pltpu.PrefetchScalarGridSpec(
        num_scalar_prefetch=4,
        grid=(blk.shape[0], steps),
        in_specs=[pl.BlockSpec((rb, XG_W), lambda w, e, blk, grp, first, valid: (blk[w], 0)),
                  pl.BlockSpec((EXPERTS_PER_STEP, D_MODEL, D_EXPERT), expert),
                  pl.BlockSpec((EXPERTS_PER_STEP, D_MODEL, D_EXPERT), expert),
                  pl.BlockSpec((EXPERTS_PER_STEP, D_EXPERT, D_MODEL), expert)],
        out_specs=pl.BlockSpec((rb, D_MODEL), lambda w, e, blk, grp, first, valid: (blk[w], 0)),
        scratch_shapes=[pltpu.VMEM((rb, D_MODEL), BF16)],
    )
    return pl.pallas_call(
        _moe_ffn_kernel,
        grid_spec=grid_spec,
        out_shape=jax.ShapeDtypeStruct((n, D_MODEL), F32),
        compiler_params=_params(("arbitrary", "arbitrary")),
        name="moe_experts",
    )(blk, grp, first, valid, xs, w_gate, w_up, w_down)


def _combine_kernel(pos_ref, pos_next_ref, xg_ref, ys_hbm, g_ref, b_ref, o_ref, buf_ref, sem, *, tm):
    i = pl.program_id(0)
    slot = i % 2

    def gather(p_ref, s):
        def issue(t, carry):
            pltpu.make_async_copy(ys_hbm.at[pl.ds(p_ref[t], 1)], buf_ref.at[s, pl.ds(t, 1)], sem.at[s]).start()
            return carry
        lax.fori_loop(0, tm, issue, 0, unroll=DMA_UNROLL)

    @pl.when(i == 0)
    def _():
        gather(pos_ref, slot)

    @pl.when(i + 1 < pl.num_programs(0))
    def _():
        gather(pos_next_ref, 1 - slot)

    pltpu.make_async_copy(ys_hbm.at[pl.ds(0, tm)], buf_ref.at[slot], sem.at[slot]).wait()
    o_ref[...] = _layer_norm(ALPHA * xg_ref[:, 0:D_MODEL] + buf_ref[slot], g_ref[...], b_ref[...])


def _combine(pos, xg, ys, g, b, tm):
    n = xg.shape[0]
    last = n // tm - 1
    return pl.pallas_call(
        functools.partial(_combine_kernel, tm=tm),
        grid=(n // tm,),
        in_specs=[pl.BlockSpec((tm,), lambda i: (i,), memory_space=pltpu.SMEM),
                  pl.BlockSpec((tm,), lambda i: (jnp.minimum(i + 1, last),), memory_space=pltpu.SMEM),
                  pl.BlockSpec((tm, XG_W), lambda i: (i, 0)),
                  pl.BlockSpec(memory_space=pl.ANY), _full(g.shape), _full(b.shape)],
        out_specs=pl.BlockSpec((tm, D_MODEL), lambda i: (i, 0)),
        out_shape=jax.ShapeDtypeStruct((n, D_MODEL), F32),
        scratch_shapes=[pltpu.VMEM((2, tm, D_MODEL), F32), pltpu.SemaphoreType.DMA((2,))],
        compiler_params=_params(("arbitrary",)),
        name="moe_combine_ln3",
    )(pos, pos, xg, ys, g, b)


def _moe_work_items(counts, n, rb):
    nb = n // rb
    n_items = nb + N_GROUPS - 1
    ends = jnp.cumsum(counts)
    start = jnp.arange(nb, dtype=jnp.int32) * rb
    g_lo = jnp.sum(ends[None, :] <= start[:, None], axis=1).astype(jnp.int32)
    g_hi = jnp.sum(ends[None, :] <= (start + rb - 1)[:, None], axis=1).astype(jnp.int32)
    per_blk = g_hi - g_lo + 1
    item0 = jnp.cumsum(per_blk) - per_blk
    w = jnp.arange(n_items, dtype=jnp.int32)
    valid = w < jnp.sum(per_blk)
    blk = jnp.clip(jnp.sum(item0[None, :] <= w[:, None], axis=1) - 1, 0, nb - 1).astype(jnp.int32)
    grp = jnp.where(valid, g_lo[blk] + (w - item0[blk]), g_hi[nb - 1]).astype(jnp.int32)
    first = (valid & (w == item0[blk])).astype(jnp.int32)
    return blk, grp, first, valid.astype(jnp.int32)


def _moe(xg, rank, grp, counts, w_gate, w_up, w_down, e_base, g, b, t_dma, rb):
    n = xg.shape[0]
    counts = counts[0, :N_GROUPS].astype(jnp.int32)
    offsets = jnp.cumsum(counts) - counts
    pos = (offsets[grp[:, 0]] + rank[:, 0]).astype(jnp.int32)
    xs = _dispatch(xg, pos, t_dma)
    ys = _moe_ffn(xs, _moe_work_items(counts, n, rb), w_gate, w_up, w_down, rb, e_base)
    return _combine(pos, xg, ys, g, b, t_dma)


def _pad_cols(w, width):
    return jnp.pad(w, ((0, 0), (0, width - w.shape[1])))


def _layer_weights(w_in, w_out, gla_w_a2, ml_b_i, ml_b_f, mla_w_uq, mla_w_ukv, moe_w_group, moe_b_group,
                   moe_w_router, moe_b_router):
    o = _IN_OFF
    half = MLA_ROPE // 2
    wg = _pad_cols(w_in[:, o[0]:o[5]], SEG_W).astype(BF16)
    wm = _pad_cols(w_in[:, o[5]:o[10]], SEG_W).astype(BF16)
    kr = w_in[:, o[12]:o[13]]
    zeros = lambda w: jnp.zeros((D_MODEL, w), F32)
    kra = jnp.concatenate([zeros(MLA_NOPE), kr, zeros(LANES - MLA_NOPE - MLA_ROPE)], 1)
    krb = jnp.concatenate([zeros(MLA_NOPE), kr[:, half:], kr[:, :half], zeros(LANES - MLA_NOPE - MLA_ROPE)], 1)
    wc = jnp.concatenate([w_in[:, o[10]:o[12]], kra, krb], 1).astype(BF16)
    wift = w_in[:, o[8]:o[10]].T.astype(BF16)
    wa2 = jnp.pad(gla_w_a2, ((0, LANES - GLA_GATE_RANK), (0, 0))).astype(BF16)
    b_gate = jnp.concatenate([ml_b_i, ml_b_f])
    b_col = jnp.pad(b_gate, (0, LANES - 2 * MIX_HEADS)).reshape(1, LANES)
    b_row = b_gate.reshape(2 * MIX_HEADS, 1)
    uq = mla_w_uq.reshape(MLA_Q_RANK, MLA_HEADS, MLA_NOPE + MLA_ROPE)
    zq = jnp.zeros((MLA_Q_RANK, MLA_HEADS, LANES - MLA_NOPE - MLA_ROPE), F32)
    wqa = jnp.concatenate([uq, zq], -1).reshape(MLA_Q_RANK, -1).astype(BF16)
    wqb = jnp.concatenate([jnp.zeros((MLA_Q_RANK, MLA_HEADS, MLA_NOPE), F32), uq[..., MLA_NOPE + half:],
                           uq[..., MLA_NOPE:MLA_NOPE + half], zq], -1).reshape(MLA_Q_RANK, -1).astype(BF16)
    ukv = mla_w_ukv.reshape(MLA_KV_RANK, MLA_HEADS, MLA_NOPE + MLA_V)
    wkn = jnp.concatenate([ukv[..., :MLA_NOPE], jnp.zeros((MLA_KV_RANK, MLA_HEADS, LANES - MLA_NOPE), F32)],
                          -1).reshape(MLA_KV_RANK, -1).astype(BF16)
    uv = ukv[..., MLA_NOPE:]
    zv = jnp.zeros_like(uv)
    odd = (jnp.arange(MLA_HEADS) % 2 == 1)[None, :, None]
    wv = jnp.concatenate([jnp.where(odd, zv, uv), jnp.where(odd, uv, zv)], -1).reshape(MLA_KV_RANK, -1).astype(BF16)
    wo = w_out.astype(BF16)
    w_route = _pad_cols(jnp.concatenate([moe_w_router, moe_w_group], 1), LANES)
    w_route_hi = w_route.astype(BF16)
    w_route_mid = (w_route - w_route_hi.astype(F32)).astype(BF16)
    b_route = jnp.pad(jnp.concatenate([moe_b_router, moe_b_group]), (0, LANES - N_EXPERTS - N_GROUPS)).reshape(1, LANES)
    return dict(wg=wg, wm=wm, wc=wc, wift=wift, wa2=wa2, b_col=b_col, b_row=b_row, wqa=wqa, wqb=wqb, wkn=wkn,
                wv=wv, wo_g=wo[0:MIX_W], wo_m=wo[MIX_W:2 * MIX_W], wo_c=wo[2 * MIX_W:], w_route_hi=w_route_hi, w_route_mid=w_route_mid,
                b_route=b_route)


def _tile(total, want):
    t = min(total, want)
    assert total % t == 0
    return t


def kernel(x, mem, positions, w_in, w_out, gla_w_a2, gla_b_a, gla_norm_g, ml_conv_w, ml_b_i, ml_b_f, ml_norm_g, mla_q_norm_g, mla_w_uq, mla_kv_norm_g, mla_w_ukv, xa_w_q, xa_w_kv, xa_w_o, moe_w_group, moe_b_group, moe_w_router, moe_b_router, moe_w_gate, moe_w_up, moe_w_down, ln1_g, ln1_b, ln2_g, ln2_b, ln3_g, ln3_b):
    batch, seq, _ = x.shape
    mem_len = mem.shape[1]
    n = batch * seq
    depth = w_in.shape[0]
    assert seq % CHUNK == 0
    t_in = _tile(n, 512)
    tm = _tile(n, 1024)
    ts = _tile(seq, 512)
    tq = _tile(seq, 512)
    t_moe = _tile(n, 1024)
    row = lambda a: a.reshape(1, -1)

    cos_t, sin_t = _rope_tables(positions.reshape(n, 1), _tile(n, 2048))
    mem2d = mem.reshape(batch * mem_len, D_MODEL)
    experts_gate = moe_w_gate.reshape(-1, D_MODEL, D_EXPERT)
    experts_up = moe_w_up.reshape(-1, D_MODEL, D_EXPERT)
    experts_down = moe_w_down.reshape(-1, D_EXPERT, D_MODEL)
    h = x.reshape(n, D_MODEL)
    for l in range(depth):
        w = _layer_weights(w_in[l], w_out[l], gla_w_a2[l], ml_b_i[l], ml_b_f[l], mla_w_uq[l], mla_w_ukv[l],
                           moe_w_group[l], moe_b_group[l], moe_w_router[l], moe_b_router[l])
        yg, ym, yift, q, k, v = _in_proj(h, w["wg"], w["wm"], w["wc"], w["wift"], cos_t, sin_t,
                                         row(mla_q_norm_g[l]), row(mla_kv_norm_g[l]),
                                         w["wqa"], w["wqb"], w["wkn"], w["wv"], t_in)
        gates_row = yift.reshape(2 * MIX_HEADS, n // CHUNK, CHUNK).transpose(1, 0, 2)
        og = _gla(yg, w["wa2"], row(gla_b_a[l]), row(gla_norm_g[l]), batch, seq, ts)
        om = _mlstm(ym, gates_row, ml_conv_w[l], w["b_col"], w["b_row"], row(ml_norm_g[l]), batch, seq, ts)
        oc = _mla_attn(q, k, v, batch, seq, tq)
        xk, xv = _xa_kv(mem2d, xa_w_kv[l].astype(BF16), mem_len)
        xg, rank, grp, counts = _out_xattn_route(
            og, om, oc, h, w["wo_g"], w["wo_m"], w["wo_c"], row(ln1_g[l]), row(ln1_b[l]),
            xk, xv, xa_w_q[l].astype(BF16), xa_w_o[l].astype(BF16), row(ln2_g[l]), row(ln2_b[l]),
            w["w_route_hi"], w["w_route_mid"], w["b_route"], batch, seq, mem_len, tm)
        h = _moe(xg, rank, grp, counts, experts_gate, experts_up, experts_down, l * N_EXPERTS,
                 row(ln3_g[l]), row(ln3_b[l]), t_moe, t_moe)
    return h.reshape(batch, seq, D_MODEL)
```

```python
import functools

import numpy as np
import jax
import jax.numpy as jnp
from jax import lax
from jax.experimental import pallas as pl
from jax.experimental.pallas import tpu as pltpu

F32 = jnp.float32
BF16 = jnp.bfloat16

D_MODEL = 1024
CHUNK = 64
HEAD_D = 64
MIX_HEADS = 4
MIX_W = MIX_HEADS * HEAD_D
GLA_GATE_RANK = 16
GLA_TAU = 16.0
MLSTM_CONV = 4
MLA_HEADS = 8
MLA_NOPE = 64
MLA_ROPE = 32
MLA_V = 64
MLA_Q_RANK = 256
MLA_KV_RANK = 128
MLA_QK_PAD = 128
ROPE_BASE = 10000.0
LOG2_E = 1.4426950408889634
MLA_HEAD_PAIR = 2
XA_HEADS = 4
XA_DH = D_MODEL // XA_HEADS
N_GROUPS = 4
EXPERTS_PER_GROUP = 8
N_EXPERTS = N_GROUPS * EXPERTS_PER_GROUP
D_EXPERT = 256
DEPTH = 2
ALPHA = (2 * DEPTH) ** 0.25
LN_EPS = 1e-5
LANES = 128
SUBLANES = 8
MIXER_UNROLL = 8
SEG_V = 2 * MIX_W
SEG_GATE = 3 * MIX_W
SEG_TAIL = 4 * MIX_W
SEG_W = SEG_TAIL + LANES
MLA_SEG_W = MLA_Q_RANK + MLA_KV_RANK + 2 * LANES
XG_W = D_MODEL + LANES
EXPERTS_PER_STEP = 4
DMA_UNROLL = 32
GLA_LEVELS = (32, 16, 8, 4, 2, 1)
VMEM_LIMIT = 56 * 1024 * 1024

_IN_SIZES = (256, 256, 256, 256, GLA_GATE_RANK, 512, 256, 256, 4, 4, MLA_Q_RANK, MLA_KV_RANK, MLA_ROPE)
_IN_OFF = np.concatenate([[0], np.cumsum(_IN_SIZES)]).tolist()


def _params(sem):
    return pltpu.CompilerParams(dimension_semantics=sem, vmem_limit_bytes=VMEM_LIMIT)


def _full(shape):
    return pl.BlockSpec(shape, lambda *_: (0,) * len(shape))


def _layer_norm(x, g, b):
    mu = jnp.mean(x, -1, keepdims=True)
    xc = x - mu
    var = jnp.mean(xc * xc, -1, keepdims=True)
    return xc * lax.rsqrt(var + LN_EPS) * g + b


def _log_sigmoid(z):
    return jnp.minimum(z, 0.0) - jnp.log1p(jnp.exp(-jnp.abs(z)))


def _dot_nt(a, b):
    return lax.dot_general(a, b, (((1,), (1,)), ((), ())), preferred_element_type=F32)


def _dot_tn(a, b):
    return lax.dot_general(a, b, (((0,), (0,)), ((), ())), preferred_element_type=F32)


def _dot(a, b):
    return jnp.dot(a, b, preferred_element_type=F32)


def _in_proj_kernel(x_ref, wg_ref, wm_ref, wc_ref, wift_ref, cos_ref, sin_ref, gq_ref, gkv_ref,
                    wqa_ref, wqb_ref, wkn_ref, wv_ref, ones_ref,
                    yg_ref, ym_ref, yift_ref, q_ref, k_ref, v_ref, yc_ref):
    xb = x_ref[...].astype(BF16)
    yg_ref[...] = _dot(xb, wg_ref[...])
    ym_ref[...] = _dot(xb, wm_ref[...])
    yift_ref[...] = _dot_nt(wift_ref[...], xb)
    yc_ref[...] = _dot(xb, wc_ref[...])
    _mla_prep_tile(yc_ref, cos_ref, sin_ref, gq_ref, gkv_ref, wqa_ref, wqb_ref, wkn_ref, wv_ref, ones_ref,
                   q_ref, k_ref, v_ref)


def _in_proj(x2d, wg, wm, wc, wift, cos_t, sin_t, gq, gkv, wqa, wqb, wkn, wv, tm):
    n = x2d.shape[0]
    ones_row = np.zeros((1, MLA_HEADS * LANES), np.float32)
    for h in range(MLA_HEADS):
        ones_row[0, h * LANES + (MLA_V if h % 2 == 0 else 0)] = 1.0
    ones_row = jnp.asarray(ones_row)
    row = lambda w: pl.BlockSpec((tm, w), lambda i: (i, 0))
    qk_w = MLA_HEADS * MLA_QK_PAD
    return pl.pallas_call(
        _in_proj_kernel,
        grid=(n // tm,),
        in_specs=[row(D_MODEL), _full(wg.shape), _full(wm.shape), _full(wc.shape), _full(wift.shape),
                  row(LANES), row(LANES), _full(gq.shape), _full(gkv.shape),
                  _full(wqa.shape), _full(wqb.shape), _full(wkn.shape), _full(wv.shape), _full(ones_row.shape)],
        out_specs=[row(SEG_W), row(SEG_W), pl.BlockSpec((8, tm), lambda i: (0, i)),
                   row(qk_w), row(qk_w), row(MLA_HEADS * LANES)],
        out_shape=[jax.ShapeDtypeStruct((n, SEG_W), F32), jax.ShapeDtypeStruct((n, SEG_W), F32),
                   jax.ShapeDtypeStruct((8, n), F32),
                   jax.ShapeDtypeStruct((n, qk_w), BF16), jax.ShapeDtypeStruct((n, qk_w), BF16),
                   jax.ShapeDtypeStruct((n, MLA_HEADS * LANES), BF16)],
        scratch_shapes=[pltpu.VMEM((tm, MLA_SEG_W), F32)],
        compiler_params=_params(("parallel",)),
        name="in_proj_mla_prep",
    )(x2d, wg, wm, wc, wift, cos_t, sin_t, gq, gkv, wqa, wqb, wkn, wv, ones_row)


def _split3(x):
    hi = x.astype(BF16)
    r1 = x - hi.astype(F32)
    mid = r1.astype(BF16)
    lo = (r1 - mid.astype(F32)).astype(BF16)
    return hi, mid, lo


def _cumsum_rows(tri, x):
    w = x.shape[1]
    y = _dot(tri, jnp.concatenate(_split3(x), axis=1))
    return y[:, 0:w] + y[:, w:2 * w] + y[:, 2 * w:3 * w]


def _gla_constants():
    t = np.arange(CHUNK)
    n_lv = len(GLA_LEVELS)
    masks = np.zeros((n_lv + 1, CHUNK, CHUNK), np.float32)
    right = np.zeros((n_lv, CHUNK, 1), np.float32)
    for li, n in enumerate(GLA_LEVELS):
        blk = t // (2 * n)
        is_right = (t % (2 * n)) >= n
        masks[li] = ((blk[:, None] == blk[None, :]) & is_right[:, None] & ~is_right[None, :])
        right[li, :, 0] = is_right
    masks[n_lv] = np.eye(CHUNK)
    return masks, right


def _gla_level_exponents(b, log_a, right_ref):
    row = lax.broadcasted_iota(jnp.int32, (CHUNK, 1), 0)
    exps = []
    for li, n in enumerate(GLA_LEVELS):
        if n >= SUBLANES // 2:
            per_blk = max(2 * n // SUBLANES, 1)
            b4 = b.reshape(CHUNK // (per_blk * SUBLANES), per_blk, SUBLANES, MIX_W)
            r_reg, r_sub = divmod(n - 1, SUBLANES)
            src = b4[:, r_reg:r_reg + 1, r_sub:r_sub + 1, :]
            b_r = jnp.broadcast_to(src, b4.shape).reshape(CHUNK, MIX_W)
            exps.append(jnp.where(right_ref[li] > 0.0, b - b_r, b_r - b))
        elif n == 2:
            pos = row % 4
            nxt = pltpu.roll(log_a, CHUNK - 1, 0)
            prv = pltpu.roll(log_a, 1, 0)
            exps.append(jnp.where(pos == 0, nxt, jnp.where(pos == 1, 0.0,
                                                            jnp.where(pos == 2, log_a, log_a + prv))))
        else:
            exps.append(jnp.where(right_ref[li] > 0.0, log_a, 0.0))
    return exps


def _pair_blocks(z, low_half):
    zero = jnp.zeros_like(z)
    return jnp.concatenate([jnp.where(low_half, z, zero), jnp.where(low_half, zero, z)], axis=0)


def _gla_kernel(y_ref, wa2_ref, ba_ref, g_ref, tri_ref, mask_ref, right_ref, o_ref, st_ref, la_ref, *, ts,
                unroll):
    n_lv = len(GLA_LEVELS)
    pairs = MIX_HEADS // 2
    low_half = lax.broadcasted_iota(jnp.int32, (CHUNK, LANES), 1) < HEAD_D
    same_head = (lax.broadcasted_iota(jnp.int32, (LANES, LANES), 0) < HEAD_D) == (
        lax.broadcasted_iota(jnp.int32, (LANES, LANES), 1) < HEAD_D)

    @pl.when(pl.program_id(1) == 0)
    def _():
        st_ref[...] = jnp.zeros_like(st_ref)

    z = _dot(y_ref[:, SEG_TAIL:SEG_W].astype(BF16), wa2_ref[...]) + ba_ref[...]
    la_ref[...] = _log_sigmoid(z) * (1.0 / GLA_TAU)

    def chunk(c, carry):
        rows = pl.ds(pl.multiple_of(c * CHUNK, CHUNK), CHUNK)
        q = y_ref[rows, 0:MIX_W] * (HEAD_D ** -0.5)
        k = y_ref[rows, MIX_W:SEG_V]
        log_a = la_ref[rows, :]
        b = _cumsum_rows(tri_ref[...], log_a)
        b_end = b[CHUNK - 1:CHUNK, :]
        scores = [jnp.zeros((CHUNK, LANES), F32) for _ in range(pairs)]
        for li, e in enumerate(_gla_level_exponents(b, log_a, right_ref)):
            x = (jnp.where(right_ref[li] > 0.0, q, k) * jnp.exp(e)).astype(BF16)
            for p in range(pairs):
                x2 = x[:, p * LANES:(p + 1) * LANES]
                scores[p] = scores[p] + _dot_nt(x2, _pair_blocks(x2, low_half)) * mask_ref[li]
        qb = q.astype(BF16)
        kb = k.astype(BF16)
        q_in = (q * jnp.exp(b)).astype(BF16)
        k_out = (k * jnp.exp(b_end - b)).astype(BF16)
        dec_end = jnp.exp(b_end)
        for p in range(pairs):
            pl_ = slice(p * LANES, (p + 1) * LANES)
            a = scores[p] + _dot_nt(qb[:, pl_], _pair_blocks(kb[:, pl_], low_half)) * mask_ref[n_lv]
            v2 = y_ref[rows, SEG_V + p * LANES:SEG_V + (p + 1) * LANES].astype(BF16)
            st = st_ref[p]
            o = _dot(a.astype(BF16), _pair_blocks(v2, low_half)) + _dot_nt(q_in[:, pl_], st.astype(BF16))
            st_ref[p] = st * dec_end[:, pl_] + jnp.where(same_head, _dot_tn(v2, k_out[:, pl_]), 0.0)
            normed = jnp.zeros((CHUNK, LANES), F32)
            for mine in (low_half, jnp.logical_not(low_half)):
                mu = jnp.sum(jnp.where(mine, o, 0.0), -1, keepdims=True) * (1.0 / HEAD_D)
                cen = jnp.where(mine, o - mu, 0.0)
                var = jnp.sum(cen * cen, -1, keepdims=True) * (1.0 / HEAD_D)
                normed = normed + cen * lax.rsqrt(var + LN_EPS)
            r_gate = y_ref[rows, SEG_GATE + p * LANES:SEG_GATE + (p + 1) * LANES]
            o_ref[rows, pl_] = (normed * g_ref[:, pl_] * (r_gate * jax.nn.sigmoid(r_gate))).astype(o_ref.dtype)
        return carry

    lax.fori_loop(0, ts // CHUNK, chunk, 0, unroll=unroll)


def _gla(yg, wa2, ba, g, batch, seq, ts):
    n = yg.shape[0]
    nt = seq // ts
    masks, right = _gla_constants()
    masks = np.concatenate([masks, masks], axis=-1)
    tri = jnp.asarray(np.tril(np.ones((CHUNK, CHUNK), np.float32)), BF16)
    return pl.pallas_call(
        functools.partial(_gla_kernel, ts=ts, unroll=MIXER_UNROLL),
        grid=(batch, nt),
        in_specs=[pl.BlockSpec((ts, SEG_W), lambda b, i: (b * nt + i, 0)),
                  _full(wa2.shape), _full(ba.shape), _full(g.shape),
                  _full(tri.shape), _full(masks.shape), _full(right.shape)],
        out_specs=pl.BlockSpec((ts, MIX_W), lambda b, i: (b * nt + i, 0)),
        out_shape=jax.ShapeDtypeStruct((n, MIX_W), BF16),
        scratch_shapes=[pltpu.VMEM((MIX_HEADS // 2, LANES, LANES), F32),
                        pltpu.VMEM((ts, MIX_W), F32)],
        compiler_params=_params(("parallel", "arbitrary")),
        name="gla_mixer",
    )(yg, wa2, ba, g, tri, jnp.asarray(masks), jnp.asarray(right))


def _mlstm_kernel(y_ref, gr_ref, cw_ref, bcol_ref, brow_ref, g_ref, tri_ref, eb_ref, eye_ref, o_ref,
                  xe_ref, qk_ref, vx_ref, fcl_ref, wsl_ref, fcb_ref, wsb_ref, rv_ref, dec_ref,
                  w_ref, em_ref, upd_ref, c_ref, m_ref, *, ts):
    first = pl.program_id(1) == 0

    @pl.when(first)
    def _():
        xe_ref[0:8, :] = jnp.zeros((8, 2 * MIX_W), F32)
        c_ref[...] = jnp.zeros_like(c_ref)
        m_ref[...] = jnp.zeros_like(m_ref)

    @pl.when(jnp.logical_not(first))
    def _():
        xe_ref[0:8, :] = xe_ref[ts:ts + 8, :]

    xe_ref[8:ts + 8, :] = y_ref[:, 0:2 * MIX_W]
    for c in range(ts // CHUNK):
        r0 = 8 + c * CHUNK
        conv = cw_ref[MLSTM_CONV - 1:MLSTM_CONV, :] * xe_ref[r0:r0 + CHUNK, :]
        for j in range(MLSTM_CONV - 1):
            lo = r0 - (MLSTM_CONV - 1) + j
            conv = conv + cw_ref[j:j + 1, :] * xe_ref[lo:lo + CHUNK, :]
        qk_ref[c * CHUNK:(c + 1) * CHUNK, :] = conv * jax.nn.sigmoid(conv)

    ones_col = (lax.broadcasted_iota(jnp.int32, (ts, HEAD_D), 1) == 0).astype(BF16)
    for h in range(MIX_HEADS):
        v_at, one_at = (0, HEAD_D) if h % 2 == 0 else (HEAD_D, 0)
        vx_ref[:, h * LANES + v_at:h * LANES + v_at + HEAD_D] = (
            y_ref[:, SEG_V + h * HEAD_D:SEG_V + (h + 1) * HEAD_D].astype(BF16))
        vx_ref[:, h * LANES + one_at:h * LANES + one_at + HEAD_D] = ones_col

    tri = tri_ref[...]
    expand = eb_ref[...]
    nck = ts // CHUNK
    lane = lax.broadcasted_iota(jnp.int32, (1, LANES), 1)
    f_lanes = (lane >= MIX_HEADS) & (lane < 2 * MIX_HEADS)
    head_lane = (lax.broadcasted_iota(jnp.int32, (MIX_HEADS, LANES), 1)
                 == lax.broadcasted_iota(jnp.int32, (MIX_HEADS, LANES), 0) + MIX_HEADS)

    def to_lanes(col):
        return jnp.sum(jnp.where(head_lane, col, 0.0), 0, keepdims=True)

    g_rows = gr_ref[...] + brow_ref[...]
    ls_rows = _log_sigmoid(g_rows).reshape(nck * SUBLANES, CHUNK)
    fcum_rows = sum(_dot_nt(part, tri) for part in _split3(ls_rows)).reshape(nck, SUBLANES, CHUNK)
    m_col = m_ref[0:MIX_HEADS, 0:1]
    for c in range(nck):
        g_col = y_ref[c * CHUNK:(c + 1) * CHUNK, SEG_TAIL:SEG_W] + bcol_ref[...]
        fcum_col = jnp.where(f_lanes, _cumsum_rows(tri, _log_sigmoid(g_col)), 0.0)
        fcl_ref[c * CHUNK:(c + 1) * CHUNK, :] = fcum_col
        g_row = g_rows[c]
        fcum_row = fcum_rows[c]
        f_row = fcum_row[MIX_HEADS:2 * MIX_HEADS, :]
        i_row = g_row[0:MIX_HEADS, :]
        f_end = f_row[:, CHUNK - 1:CHUNK]
        rv_ref[c, 0:MIX_HEADS, :] = jnp.concatenate(
            [i_row - f_row, jnp.broadcast_to(m_col, (MIX_HEADS, HEAD_D))], axis=1)
        m_new = jnp.maximum(f_end + m_col, jnp.max(f_end - f_row + i_row, -1, keepdims=True))
        m_new_l = to_lanes(m_new)
        f_end_l = fcum_col[CHUNK - 1:CHUNK, :]
        i_shift = pltpu.roll(g_col, MIX_HEADS, 1)
        wsl_ref[c * CHUNK:(c + 1) * CHUNK, :] = jnp.where(
            f_lanes, jnp.exp(f_end_l - fcum_col + i_shift - m_new_l), 0.0)
        dec_ref[c] = jnp.broadcast_to(jnp.exp(f_end_l + to_lanes(m_col) - m_new_l), (SUBLANES, LANES))
        m_col = m_new
    m_ref[0:MIX_HEADS, :] = jnp.broadcast_to(m_col, (MIX_HEADS, LANES))
    slab = 2 * CHUNK
    for i in range(ts // slab):
        sr = slice(i * slab, (i + 1) * slab)
        fcb_ref[sr, :] = sum(_dot(part, expand) for part in _split3(fcl_ref[sr, :]))
        ws_hi, ws_mid, _ = _split3(wsl_ref[sr, :])
        wsb_ref[sr, :] = _dot(ws_hi, expand) + _dot(ws_mid, expand)

    t_idx = lax.broadcasted_iota(jnp.int32, (CHUNK, LANES), 0)
    s_idx = lax.broadcasted_iota(jnp.int32, (CHUNK, LANES), 1)
    keep = (s_idx <= t_idx) | (s_idx >= CHUNK)
    low_half = s_idx < HEAD_D

    def weights(c, carry):
        rows = pl.ds(pl.multiple_of(c * CHUNK, CHUNK), CHUNK)
        rv = rv_ref[c, 0:MIX_HEADS, :]
        for p in range(MIX_HEADS // 2):
            q2 = (qk_ref[rows, p * LANES:(p + 1) * LANES] * (HEAD_D ** -0.5)).astype(BF16)
            k2 = qk_ref[rows, MIX_W + p * LANES:MIX_W + (p + 1) * LANES]
            for h in (2 * p, 2 * p + 1):
                mine = low_half if h % 2 == 0 else jnp.logical_not(low_half)
                hb = slice(h * LANES, (h + 1) * LANES)
                logw = jnp.where(keep, fcb_ref[rows, hb] + rv[h:h + 1, :], -jnp.inf)
                m_t = jnp.max(logw, -1, keepdims=True)
                k_h = jnp.where(mine, k2, 0.0)
                qk = _dot_nt(q2, jnp.concatenate([k_h.astype(BF16), eye_ref[h % 2]], axis=0))
                w_ref[c * MIX_HEADS + h] = (jnp.exp(logw - m_t) * qk).astype(BF16)
                em_ref[c * MIX_HEADS + h] = jnp.exp(-m_t)
                upd = _dot_tn((k_h * wsb_ref[rows, hb]).astype(BF16), vx_ref[rows, hb])
                off = (h % 2) * HEAD_D
                upd_ref[c * MIX_HEADS + h] = upd[off:off + HEAD_D, :]
        return carry

    lax.fori_loop(0, nck, weights, 0, unroll=MIXER_UNROLL)

    def chunk(c, carry):
        rows = pl.ds(pl.multiple_of(c * CHUNK, CHUNK), CHUNK)
        dec_l = dec_ref[c]
        for p in range(MIX_HEADS // 2):
            pl_ = slice(p * LANES, (p + 1) * LANES)
            pair = jnp.zeros((CHUNK, LANES), F32)
            for h in (2 * p, 2 * p + 1):
                mine = low_half if h % 2 == 0 else jnp.logical_not(low_half)
                c_st = c_ref[h]
                num = _dot(w_ref[c * MIX_HEADS + h],
                           jnp.concatenate([vx_ref[rows, h * LANES:(h + 1) * LANES], c_st.astype(BF16)], axis=0))
                c_ref[h] = dec_l[0:1, MIX_HEADS + h:MIX_HEADS + h + 1] * c_st + upd_ref[c * MIX_HEADS + h]
                den = num[:, HEAD_D:HEAD_D + 1] if h % 2 == 0 else num[:, 0:1]
                r = 1.0 / jnp.maximum(jnp.abs(den), em_ref[c * MIX_HEADS + h])
                mu = jnp.sum(jnp.where(mine, num, 0.0), -1, keepdims=True) * (1.0 / HEAD_D)
                cen = jnp.where(mine, num - mu, 0.0)
                var = jnp.sum(cen * cen, -1, keepdims=True) * (1.0 / HEAD_D)
                pair = pair + cen * (r * lax.rsqrt(r * r * var + LN_EPS))
            o_gate = y_ref[rows, SEG_GATE + p * LANES:SEG_GATE + (p + 1) * LANES]
            o_ref[rows, pl_] = (pair * g_ref[:, pl_] * jax.nn.sigmoid(o_gate)).astype(o_ref.dtype)
        return carry

    lax.fori_loop(0, nck, chunk, 0, unroll=MIXER_UNROLL)


def _mlstm(ym, gates_row, conv_w, b_col, b_row, g, batch, seq, ts):
    n = ym.shape[0]
    nt = seq // ts
    nck = ts // CHUNK
    tri = jnp.asarray(np.tril(np.ones((CHUNK, CHUNK), np.float32)), BF16)
    expand = np.zeros((LANES, MIX_HEADS * LANES), np.float32)
    for h in range(MIX_HEADS):
        expand[MIX_HEADS + h, h * LANES:(h + 1) * LANES] = 1.0
    eye = np.zeros((2, HEAD_D, LANES), np.float32)
    eye[0, :, 0:HEAD_D] = np.eye(HEAD_D)
    eye[1, :, HEAD_D:] = np.eye(HEAD_D)
    expand = jnp.asarray(expand, BF16)
    eye = jnp.asarray(eye, BF16)
    return pl.pallas_call(
        functools.partial(_mlstm_kernel, ts=ts),
        grid=(batch, nt),
        in_specs=[pl.BlockSpec((ts, SEG_W), lambda b, i: (b * nt + i, 0)),
                  pl.BlockSpec((nck, 8, CHUNK), lambda b, i: (b * nt + i, 0, 0)),
                  _full(conv_w.shape), _full(b_col.shape), _full(b_row.shape), _full(g.shape),
                  _full(tri.shape), _full(expand.shape), _full(eye.shape)],
        out_specs=pl.BlockSpec((ts, MIX_W), lambda b, i: (b * nt + i, 0)),
        out_shape=jax.ShapeDtypeStruct((n, MIX_W), BF16),
        scratch_shapes=[pltpu.VMEM((ts + 8, 2 * MIX_W), F32),
                        pltpu.VMEM((ts, 2 * MIX_W), F32),
                        pltpu.VMEM((ts, MIX_HEADS * LANES), BF16),
                        pltpu.VMEM((ts, LANES), F32),
                        pltpu.VMEM((ts, LANES), F32),
                        pltpu.VMEM((ts, MIX_HEADS * LANES), F32),
                        pltpu.VMEM((ts, MIX_HEADS * LANES), F32),
                        pltpu.VMEM((nck, SUBLANES, LANES), F32),
                        pltpu.VMEM((nck, SUBLANES, LANES), F32),
                        pltpu.VMEM((nck * MIX_HEADS, CHUNK, LANES), BF16),
                        pltpu.VMEM((nck * MIX_HEADS, CHUNK, 1), F32),
                        pltpu.VMEM((nck * MIX_HEADS, HEAD_D, LANES), F32),
                        pltpu.VMEM((MIX_HEADS, HEAD_D, LANES), F32),
                        pltpu.VMEM((SUBLANES, LANES), F32)],
        compiler_params=_params(("parallel", "arbitrary")),
        name="mlstm_mixer",
    )(ym, gates_row, conv_w, b_col, b_row, g, tri, expand, eye)


def _rope_table_kernel(pos_ref, inv_ref, cos_ref, sin_ref):
    ang = pos_ref[...].astype(F32) * inv_ref[...]
    lane = lax.broadcasted_iota(jnp.int32, ang.shape, 1)
    rot = (lane >= MLA_NOPE) & (lane < MLA_NOPE + MLA_ROPE)
    first_half = lane < MLA_NOPE + MLA_ROPE // 2
    cos_ref[...] = jnp.where(lane < MLA_NOPE, 1.0, jnp.where(rot, jnp.cos(ang), 0.0))
    s = jnp.sin(ang)
    sin_ref[...] = jnp.where(rot, jnp.where(first_half, -s, s), 0.0)


def _rope_tables(pos_col, tm):
    n = pos_col.shape[0]
    half = MLA_ROPE // 2
    inv = ROPE_BASE ** (-np.arange(half, dtype=np.float32) / half)
    inv_row = np.zeros((1, LANES), np.float32)
    inv_row[0, MLA_NOPE:MLA_NOPE + half] = inv
    inv_row[0, MLA_NOPE + half:MLA_NOPE + MLA_ROPE] = inv
    return pl.pallas_call(
        _rope_table_kernel,
        grid=(n // tm,),
        in_specs=[pl.BlockSpec((tm, 1), lambda i: (i, 0)), _full((1, LANES))],
        out_specs=[pl.BlockSpec((tm, LANES), lambda i: (i, 0))] * 2,
        out_shape=[jax.ShapeDtypeStruct((n, LANES), F32)] * 2,
        compiler_params=_params(("parallel",)),
        name="rope_tables",
    )(pos_col, jnp.asarray(inv_row))


def _mla_prep_tile(y_ref, cos_ref, sin_ref, gq_ref, gkv_ref, wqa_ref, wqb_ref, wkn_ref, wv_ref, ones_ref,
                   q_ref, k_ref, v_ref):
    def rms(x, g):
        return x * lax.rsqrt(jnp.mean(x * x, -1, keepdims=True) + LN_EPS) * g

    cos = cos_ref[...]
    sin = sin_ref[...]
    cq = rms(y_ref[:, 0:MLA_Q_RANK], gq_ref[...]).astype(BF16)
    ckv = rms(y_ref[:, MLA_Q_RANK:MLA_Q_RANK + MLA_KV_RANK], gkv_ref[...]).astype(BF16)
    kr = MLA_Q_RANK + MLA_KV_RANK
    k_rope = y_ref[:, kr:kr + LANES] * cos + y_ref[:, kr + LANES:MLA_SEG_W] * sin
    qa = _dot(cq, wqa_ref[...])
    qb = _dot(cq, wqb_ref[...])
    kn = _dot(ckv, wkn_ref[...])
    scale = (MLA_NOPE + MLA_ROPE) ** -0.5 * LOG2_E
    for h in range(MLA_HEADS):
        sl = slice(h * MLA_QK_PAD, (h + 1) * MLA_QK_PAD)
        q_ref[:, sl] = ((qa[:, sl] * cos + qb[:, sl] * sin) * scale).astype(q_ref.dtype)
        k_ref[:, sl] = (kn[:, sl] + k_rope).astype(k_ref.dtype)
    v_ref[...] = (_dot(ckv, wv_ref[...]) + ones_ref[...]).astype(v_ref.dtype)


def _mla_attn_kernel(q_ref, k_ref, v_ref, o_ref, *, seq, tq):
    t_chunk = lax.broadcasted_iota(jnp.int32, (tq, tq), 0) // CHUNK
    s_chunk = lax.broadcasted_iota(jnp.int32, (tq, tq), 1) // CHUNK
    diag_mask = s_chunk <= t_chunk
    low_half = lax.broadcasted_iota(jnp.int32, (tq, LANES), 1) < MLA_V
    for i in range(seq // tq):
        rows = slice(i * tq, (i + 1) * tq)
        res = []
        for hh in range(MLA_HEAD_PAIR):
            ql = slice(hh * MLA_QK_PAD, (hh + 1) * MLA_QK_PAD)
            vl = slice(hh * LANES, (hh + 1) * LANES)
            q = q_ref[0, rows, ql]
            s_d = jnp.where(diag_mask, _dot_nt(q, k_ref[0, rows, ql]), -jnp.inf)
            m = jnp.max(s_d, -1, keepdims=True)
            if i > 0:
                s_o = _dot_nt(q, k_ref[0, 0:i * tq, ql])
                m = jnp.maximum(m, jnp.max(s_o, -1, keepdims=True))
            o = _dot(jnp.exp2(s_d - m).astype(BF16), v_ref[0, rows, vl])
            if i > 0:
                o = o + _dot(jnp.exp2(s_o - m).astype(BF16), v_ref[0, 0:i * tq, vl])
            l = o[:, MLA_V:MLA_V + 1] if hh == 0 else o[:, 0:1]
            res.append(o / l)
        o_ref[0, rows, :] = jnp.where(low_half, res[0], res[1]).astype(o_ref.dtype)


def _mla_attn(q, k, v, batch, seq, tq):
    n = q.shape[0]
    qk_w = MLA_HEADS * MLA_QK_PAD
    v_w = MLA_HEADS * MLA_V
    pair = lambda w: pl.BlockSpec((1, seq, MLA_HEAD_PAIR * w), lambda b, h: (b, 0, h))
    out = pl.pallas_call(
        functools.partial(_mla_attn_kernel, seq=seq, tq=tq),
        grid=(batch, MLA_HEADS // MLA_HEAD_PAIR),
        in_specs=[pair(MLA_QK_PAD), pair(MLA_QK_PAD), pair(LANES)],
        out_specs=pair(MLA_V),
        out_shape=jax.ShapeDtypeStruct((batch, seq, v_w), BF16),
        compiler_params=_params(("parallel", "parallel")),
        name="mla_attention",
    )(q.reshape(batch, seq, qk_w), k.reshape(batch, seq, qk_w), v.reshape(batch, seq, MLA_HEADS * LANES))
    return out.reshape(n, v_w)


def _xa_kv_kernel(mem_ref, w_ref, k_ref, v_ref):
    kv = _dot(mem_ref[...].astype(BF16), w_ref[...])
    k_ref[...] = kv[:, 0:D_MODEL].astype(k_ref.dtype)
    v_ref[...] = kv[:, D_MODEL:2 * D_MODEL].astype(v_ref.dtype)


def _xa_kv(mem2d, w_kv, mem_len):
    n = mem2d.shape[0]
    row = pl.BlockSpec((mem_len, D_MODEL), lambda i: (i, 0))
    return pl.pallas_call(
        _xa_kv_kernel,
        grid=(n // mem_len,),
        in_specs=[row, _full(w_kv.shape)],
        out_specs=[row, row],
        out_shape=[jax.ShapeDtypeStruct((n, D_MODEL), BF16)] * 2,
        compiler_params=_params(("parallel",)),
        name="xattn_kv",
    )(mem2d, w_kv)


def _route_tile(x, wh_ref, wm_ref, b_ref, tri_ref, cnt_ref):
    xh = x.astype(BF16)
    xm = (x - xh.astype(F32)).astype(BF16)
    logits = _dot(xh, wh_ref[...]) + _dot(xh, wm_ref[...]) + _dot(xm, wh_ref[...]) + b_ref[...]
    lane = lax.broadcasted_iota(jnp.int32, logits.shape, 1).astype(F32)
    is_group = (lane >= N_EXPERTS) & (lane < N_EXPERTS + N_GROUPS)
    g_max = jnp.max(jnp.where(is_group, logits, -jnp.inf), -1, keepdims=True)
    g_sum = jnp.sum(jnp.where(is_group, jnp.exp(logits - g_max), 0.0), -1, keepdims=True)
    g_p = 1.0 / g_sum
    g_idx = jnp.min(jnp.where(is_group & (logits == g_max), lane - N_EXPERTS, float(LANES)), -1, keepdims=True)
    in_group = (lane < N_EXPERTS) & (jnp.floor(lane * (1.0 / EXPERTS_PER_GROUP)) == g_idx)
    e_max = jnp.max(jnp.where(in_group, logits, -jnp.inf), -1, keepdims=True)
    e_exp = jnp.where(in_group, jnp.exp(logits - e_max), 0.0)
    prob = e_exp / jnp.sum(e_exp, -1, keepdims=True)
    cand = jnp.where(in_group, prob, -1.0)
    p1 = jnp.max(cand, -1, keepdims=True)
    i1 = jnp.min(jnp.where(cand == p1, lane, float(LANES)), -1, keepdims=True)
    cand2 = jnp.where(lane == i1, -1.0, cand)
    p2 = jnp.max(cand2, -1, keepdims=True)
    i2 = jnp.min(jnp.where(cand2 == p2, lane, float(LANES)), -1, keepdims=True)
    p_sum = p1 + p2
    gates = jnp.where(lane == i1, g_p * (p1 / p_sum), 0.0) + jnp.where(lane == i2, g_p * (p2 / p_sum), 0.0)
    onehot = jnp.where(lane == g_idx, 1.0, 0.0)
    before = _dot(tri_ref[...], onehot.astype(BF16)) + cnt_ref[...]
    rank = jnp.sum(onehot * before, -1, keepdims=True).astype(jnp.int32)
    cnt_ref[...] += jnp.sum(onehot, 0, keepdims=True)
    return gates, rank, g_idx.astype(jnp.int32)


def _xattn_kernel(og_ref, om_ref, oc_ref, x0_ref, wog_ref, wom_ref, woc_ref, g1_ref, b1_ref,
                  k_ref, v_ref, wq_ref, wo_ref, g_ref, b_ref, wrh_ref, wrm_ref, br_ref, tri_ref,
                  xg_ref, rank_ref, grp_ref, cnt_out_ref, cnt_ref):
    @pl.when((pl.program_id(0) == 0) & (pl.program_id(1) == 0))
    def _():
        cnt_ref[...] = jnp.zeros_like(cnt_ref)

    mix = _dot(og_ref[...], wog_ref[...]) + _dot(om_ref[...], wom_ref[...]) + _dot(oc_ref[...], woc_ref[...])
    x = _layer_norm(ALPHA * x0_ref[...] + mix, g1_ref[...], b1_ref[...])
    q =(_dot(x.astype(BF16), wq_ref[...]) * (XA_DH ** -0.5)).astype(BF16)
    heads = []
    for h in range(XA_HEADS):
        sl = slice(h * XA_DH, (h + 1) * XA_DH)
        s = _dot_nt(q[:, sl], k_ref[:, sl])
        p = jnp.exp(s - jnp.max(s, -1, keepdims=True))
        p = p / jnp.sum(p, -1, keepdims=True)
        heads.append(_dot(p.astype(BF16), v_ref[:, sl]).astype(BF16))
    out = _dot(jnp.concatenate(heads, axis=1), wo_ref[...])
    x2 = _layer_norm(ALPHA * x + out, g_ref[...], b_ref[...])
    gates, rank, grp = _route_tile(x2, wrh_ref, wrm_ref, br_ref, tri_ref, cnt_ref)
    xg_ref[:, 0:D_MODEL] = x2
    xg_ref[:, D_MODEL:] = gates
    rank_ref[...] = rank
    grp_ref[...] = grp
    cnt_out_ref[...] = cnt_ref[...]


def _out_xattn_route(og, om, oc, x0, wog, wom, woc, g1, b1, xk, xv, wq, wo, g, b, w_route_hi, w_route_mid, b_route,
                     batch, seq, mem_len, tm):
    n = x0.shape[0]
    nt = seq // tm
    tri = jnp.asarray(np.tril(np.ones((tm, tm), np.float32), -1), BF16)
    tile = lambda w: pl.BlockSpec((tm, w), lambda bi, i: (bi * nt + i, 0))
    mem = pl.BlockSpec((mem_len, D_MODEL), lambda bi, i: (bi, 0))
    return pl.pallas_call(
        _xattn_kernel,
        grid=(batch, nt),
        in_specs=[tile(MIX_W), tile(MIX_W), tile(MLA_HEADS * MLA_V), tile(D_MODEL),
                  _full(wog.shape), _full(wom.shape), _full(woc.shape), _full(g1.shape), _full(b1.shape),
                  mem, mem, _full(wq.shape), _full(wo.shape), _full(g.shape), _full(b.shape),
                  _full(w_route_hi.shape), _full(w_route_mid.shape), _full(b_route.shape), _full(tri.shape)],
        out_specs=[tile(XG_W), tile(1), tile(1), _full((1, LANES))],
        out_shape=[jax.ShapeDtypeStruct((n, XG_W), F32), jax.ShapeDtypeStruct((n, 1), jnp.int32),
                   jax.ShapeDtypeStruct((n, 1), jnp.int32), jax.ShapeDtypeStruct((1, LANES), F32)],
        scratch_shapes=[pltpu.VMEM((1, LANES), F32)],
        compiler_params=_params(("arbitrary", "arbitrary")),
        name="out_proj_xattn_route",
    )(og, om, oc, x0, wog, wom, woc, g1, b1, xk, xv, wq, wo, g, b, w_route_hi, w_route_mid, b_route, tri)


def _dispatch_kernel(pos_ref, x_ref, xs_hbm, sem, *, tm):
    def issue(t, carry):
        pltpu.make_async_copy(x_ref.at[pl.ds(t, 1)], xs_hbm.at[pl.ds(pos_ref[t], 1)], sem).start()
        return carry

    lax.fori_loop(0, tm, issue, 0, unroll=DMA_UNROLL)
    pltpu.make_async_copy(x_ref, xs_hbm.at[pl.ds(0, tm)], sem).wait()


def _dispatch(xg, pos, tm):
    n = xg.shape[0]
    return pl.pallas_call(
        functools.partial(_dispatch_kernel, tm=tm),
        grid=(n // tm,),
        in_specs=[pl.BlockSpec((tm,), lambda i: (i,), memory_space=pltpu.SMEM),
                  pl.BlockSpec((tm, XG_W), lambda i: (i, 0))],
        out_specs=pl.BlockSpec(memory_space=pl.ANY),
        out_shape=jax.ShapeDtypeStruct(xg.shape, xg.dtype),
        scratch_shapes=[pltpu.SemaphoreType.DMA(())],
        compiler_params=_params(("arbitrary",)),
        name="moe_dispatch",
    )(pos, xg)


def _moe_ffn_kernel(blk_ref, grp_ref, first_ref, valid_ref, xs_ref, wg_ref, wu_ref, wd_ref, o_ref, xb_ref):
    w = pl.program_id(0)
    e = pl.program_id(1)

    @pl.when((first_ref[w] == 1) & (e == 0))
    def _():
        xb_ref[...] = xs_ref[:, 0:D_MODEL].astype(BF16)
        o_ref[...] = jnp.zeros_like(o_ref)

    @pl.when(valid_ref[w] == 1)
    def _():
        xb = xb_ref[...]
        gates = xs_ref[:, D_MODEL:]
        lane = lax.broadcasted_iota(jnp.int32, gates.shape, 1)
        hidden = []
        for j in range(EXPERTS_PER_STEP):
            expert = grp_ref[w] * EXPERTS_PER_GROUP + e * EXPERTS_PER_STEP + j
            w_tok = jnp.sum(jnp.where(lane == expert, gates, 0.0), -1, keepdims=True)
            hg = _dot(xb, wg_ref[j].astype(BF16))
            hu = _dot(xb, wu_ref[j].astype(BF16))
            hidden.append((hg * jax.nn.sigmoid(hg) * hu * w_tok).astype(BF16))
        wd = wd_ref[...].astype(BF16).reshape(EXPERTS_PER_STEP * D_EXPERT, D_MODEL)
        o_ref[...] += _dot(jnp.concatenate(hidden, axis=1), wd)


def _moe_ffn(xs, items, w_gate, w_up, w_down, rb, e_base):
    n = xs.shape[0]
    blk, grp, first, valid = items

    steps = EXPERTS_PER_GROUP // EXPERTS_PER_STEP

    def expert(w, e, blk, grp, first, valid):
        return ((e_base + grp[w] * EXPERTS_PER_GROUP) // EXPERTS_PER_STEP
                + jnp.where(valid[w] == 1, e, steps - 1), 0, 0)

    grid_spec = pltpu.PrefetchScalarGridSpec(
        num_scalar_prefetch=4,
        grid=(blk.shape[0], steps),
        in_specs=[pl.BlockSpec((rb, XG_W), lambda w, e, blk, grp, first, valid: (blk[w], 0)),
                  pl.BlockSpec((EXPERTS_PER_STEP, D_MODEL, D_EXPERT), expert),
                  pl.BlockSpec((EXPERTS_PER_STEP, D_MODEL, D_EXPERT), expert),
                  pl.BlockSpec((EXPERTS_PER_STEP, D_EXPERT, D_MODEL), expert)],
        out_specs=pl.BlockSpec((rb, D_MODEL), lambda w, e, blk, grp, first, valid: (blk[w], 0)),
        scratch_shapes=[pltpu.VMEM((rb, D_MODEL), BF16)],
    )
    return pl.pallas_call(
        _moe_ffn_kernel,
        grid_spec=grid_spec,
        out_shape=jax.ShapeDtypeStruct((n, D_MODEL), F32),
        compiler_params=_params(("arbitrary", "arbitrary")),
        name="moe_experts",
    )(blk, grp, first, valid, xs, w_gate, w_up, w_down)


def _combine_kernel(pos_ref, pos_next_ref, xg_ref, ys_hbm, g_ref, b_ref, o_ref, buf_ref, sem, *, tm):
    i = pl.program_id(0)
    slot = i % 2

    def gather(p_ref, s):
        def issue(t, carry):
            pltpu.make_async_copy(ys_hbm.at[pl.ds(p_ref[t], 1)], buf_ref.at[s, pl.ds(t, 1)], sem.at[s]).start()
            return carry
        lax.fori_loop(0, tm, issue, 0, unroll=DMA_UNROLL)

    @pl.when(i == 0)
    def _():
        gather(pos_ref, slot)

    @pl.when(i + 1 < pl.num_programs(0))
    def _():
        gather(pos_next_ref, 1 - slot)

    pltpu.make_async_copy(ys_hbm.at[pl.ds(0, tm)], buf_ref.at[slot], sem.at[slot]).wait()
    o_ref[...] = _layer_norm(ALPHA * xg_ref[:, 0:D_MODEL] + buf_ref[slot], g_ref[...], b_ref[...])


def _combine(pos, xg, ys, g, b, tm):
    n = xg.shape[0]
    last = n // tm - 1
    return pl.pallas_call(
        functools.partial(_combine_kernel, tm=tm),
        grid=(n // tm,),
        in_specs=[pl.BlockSpec((tm,), lambda i: (i,), memory_space=pltpu.SMEM),
                  pl.BlockSpec((tm,), lambda i: (jnp.minimum(i + 1, last),), memory_space=pltpu.SMEM),
                  pl.BlockSpec((tm, XG_W), lambda i: (i, 0)),
                  pl.BlockSpec(memory_space=pl.ANY), _full(g.shape), _full(b.shape)],
        out_specs=pl.BlockSpec((tm, D_MODEL), lambda i: (i, 0)),
        out_shape=jax.ShapeDtypeStruct((n, D_MODEL), F32),
        scratch_shapes=[pltpu.VMEM((2, tm, D_MODEL), F32), pltpu.SemaphoreType.DMA((2,))],
        compiler_params=_params(("arbitrary",)),
        name="moe_combine_ln3",
    )(pos, pos, xg, ys, g, b)


def _moe_work_items(counts, n, rb):
    nb = n // rb
    n_items = nb + N_GROUPS - 1
    ends = jnp.cumsum(counts)
    start = jnp.arange(nb, dtype=jnp.int32) * rb
    g_lo = jnp.sum(ends[None, :] <= start[:, None], axis=1).astype(jnp.int32)
    g_hi = jnp.sum(ends[None, :] <= (start + rb - 1)[:, None], axis=1).astype(jnp.int32)
    per_blk = g_hi - g_lo + 1
    item0 = jnp.cumsum(per_blk) - per_blk
    w = jnp.arange(n_items, dtype=jnp.int32)
    valid = w < jnp.sum(per_blk)
    blk = jnp.clip(jnp.sum(item0[None, :] <= w[:, None], axis=1) - 1, 0, nb - 1).astype(jnp.int32)
    grp = jnp.where(valid, g_lo[blk] + (w - item0[blk]), g_hi[nb - 1]).astype(jnp.int32)
    first = (valid & (w == item0[blk])).astype(jnp.int32)
    return blk, grp, first, valid.astype(jnp.int32)


def _moe(xg, rank, grp, counts, w_gate, w_up, w_down, e_base, g, b, t_dma, rb):
    n = xg.shape[0]
    counts = counts[0, :N_GROUPS].astype(jnp.int32)
    offsets = jnp.cumsum(counts) - counts
    pos = (offsets[grp[:, 0]] + rank[:, 0]).astype(jnp.int32)
    xs = _dispatch(xg, pos, t_dma)
    ys = _moe_ffn(xs, _moe_work_items(counts, n, rb), w_gate, w_up, w_down, rb, e_base)
    return _combine(pos, xg, ys, g, b, t_dma)


def _pad_cols(w, width):
    return jnp.pad(w, ((0, 0), (0, width - w.shape[1])))


def _layer_weights(w_in, w_out, gla_w_a2, ml_b_i, ml_b_f, mla_w_uq, mla_w_ukv, moe_w_group, moe_b_group,
                   moe_w_router, moe_b_router):
    o = _IN_OFF
    half = MLA_ROPE // 2
    wg = _pad_cols(w_in[:, o[0]:o[5]], SEG_W).astype(BF16)
    wm = _pad_cols(w_in[:, o[5]:o[10]], SEG_W).astype(BF16)
    kr = w_in[:, o[12]:o[13]]
    zeros = lambda w: jnp.zeros((D_MODEL, w), F32)
    kra = jnp.concatenate([zeros(MLA_NOPE), kr, zeros(LANES - MLA_NOPE - MLA_ROPE)], 1)
    krb = jnp.concatenate([zeros(MLA_NOPE), kr[:, half:], kr[:, :half], zeros(LANES - MLA_NOPE - MLA_ROPE)], 1)
    wc = jnp.concatenate([w_in[:, o[10]:o[12]], kra, krb], 1).astype(BF16)
    wift = w_in[:, o[8]:o[10]].T.astype(BF16)
    wa2 = jnp.pad(gla_w_a2, ((0, LANES - GLA_GATE_RANK), (0, 0))).astype(BF16)
    b_gate = jnp.concatenate([ml_b_i, ml_b_f])
    b_col = jnp.pad(b_gate, (0, LANES - 2 * MIX_HEADS)).reshape(1, LANES)
    b_row = b_gate.reshape(2 * MIX_HEADS, 1)
    uq = mla_w_uq.reshape(MLA_Q_RANK, MLA_HEADS, MLA_NOPE + MLA_ROPE)
    zq = jnp.zeros((MLA_Q_RANK, MLA_HEADS, LANES - MLA_NOPE - MLA_ROPE), F32)
    wqa = jnp.concatenate([uq, zq], -1).reshape(MLA_Q_RANK, -1).astype(BF16)
    wqb = jnp.concatenate([jnp.zeros((MLA_Q_RANK, MLA_HEADS, MLA_NOPE), F32), uq[..., MLA_NOPE + half:],
                           uq[..., MLA_NOPE:MLA_NOPE + half], zq], -1).reshape(MLA_Q_RANK, -1).astype(BF16)
    ukv = mla_w_ukv.reshape(MLA_KV_RANK, MLA_HEADS, MLA_NOPE + MLA_V)
    wkn = jnp.concatenate([ukv[..., :MLA_NOPE], jnp.zeros((MLA_KV_RANK, MLA_HEADS, LANES - MLA_NOPE), F32)],
                          -1).reshape(MLA_KV_RANK, -1).astype(BF16)
    uv = ukv[..., MLA_NOPE:]
    zv = jnp.zeros_like(uv)
    odd = (jnp.arange(MLA_HEADS) % 2 == 1)[None, :, None]
    wv = jnp.concatenate([jnp.where(odd, zv, uv), jnp.where(odd, uv, zv)], -1).reshape(MLA_KV_RANK, -1).astype(BF16)
    wo = w_out.astype(BF16)
    w_route = _pad_cols(jnp.concatenate([moe_w_router, moe_w_group], 1), LANES)
    w_route_hi = w_route.astype(BF16)
    w_route_mid = (w_route - w_route_hi.astype(F32)).astype(BF16)
    b_route = jnp.pad(jnp.concatenate([moe_b_router, moe_b_group]), (0, LANES - N_EXPERTS - N_GROUPS)).reshape(1, LANES)
    return dict(wg=wg, wm=wm, wc=wc, wift=wift, wa2=wa2, b_col=b_col, b_row=b_row, wqa=wqa, wqb=wqb, wkn=wkn,
                wv=wv, wo_g=wo[0:MIX_W], wo_m=wo[MIX_W:2 * MIX_W], wo_c=wo[2 * MIX_W:], w_route_hi=w_route_hi, w_route_mid=w_route_mid,
                b_route=b_route)


def _tile(total, want):
    t = min(total, want)
    assert total % t == 0
    return t


def kernel(x, mem, positions, w_in, w_out, gla_w_a2, gla_b_a, gla_norm_g, ml_conv_w, ml_b_i, ml_b_f, ml_norm_g, mla_q_norm_g, mla_w_uq, mla_kv_norm_g, mla_w_ukv, xa_w_q, xa_w_kv, xa_w_o, moe_w_group, moe_b_group, moe_w_router, moe_b_router, moe_w_gate, moe_w_up, moe_w_down, ln1_g, ln1_b, ln2_g, ln2_b, ln3_g, ln3_b):
    batch, seq, _ = x.shape
    mem_len = mem.shape[1]
    n = batch * seq
    depth = w_in.shape[0]
    assert seq % CHUNK == 0
    t_in = _tile(n, 512)
    tm = _tile(n, 1024)
    ts = _tile(seq, 512)
    tq = _tile(seq, 512)
    t_moe = _tile(n, 1024)
    row = lambda a: a.reshape(1, -1)

    cos_t, sin_t = _rope_tables(positions.reshape(n, 1), _tile(n, 2048))
    mem2d = mem.reshape(batch * mem_len, D_MODEL)
    experts_gate = moe_w_gate.reshape(-1, D_MODEL, D_EXPERT)
    experts_up = moe_w_up.reshape(-1, D_MODEL, D_EXPERT)
    experts_down = moe_w_down.reshape(-1, D_EXPERT, D_MODEL)
    h = x.reshape(n, D_MODEL)
    for l in range(depth):
        w = _layer_weights(w_in[l], w_out[l], gla_w_a2[l], ml_b_i[l], ml_b_f[l], mla_w_uq[l], mla_w_ukv[l],
                           moe_w_group[l], moe_b_group[l], moe_w_router[l], moe_b_router[l])
        yg, ym, yift, q, k, v = _in_proj(h, w["wg"], w["wm"], w["wc"], w["wift"], cos_t, sin_t,
                                         row(mla_q_norm_g[l]), row(mla_kv_norm_g[l]),
                                         w["wqa"], w["wqb"], w["wkn"], w["wv"], t_in)
        gates_row = yift.reshape(2 * MIX_HEADS, n // CHUNK, CHUNK).transpose(1, 0, 2)
        og = _gla(yg, w["wa2"], row(gla_b_a[l]), row(gla_norm_g[l]), batch, seq, ts)
        om = _mlstm(ym, gates_row, ml_conv_w[l], w["b_col"], w["b_row"], row(ml_norm_g[l]), batch, seq, ts)
        oc = _mla_attn(q, k, v, batch, seq, tq)
        xk, xv = _xa_kv(mem2d, xa_w_kv[l].astype(BF16), mem_len)
        xg, rank, grp, counts = _out_xattn_route(
            og, om, oc, h, w["wo_g"], w["wo_m"], w["wo_c"], row(ln1_g[l]), row(ln1_b[l]),
            xk, xv, xa_w_q[l].astype(BF16), xa_w_o[l].astype(BF16), row(ln2_g[l]), row(ln2_b[l]),
            w["w_route_hi"], w["w_route_mid"], w["b_route"], batch, seq, mem_len, tm)
        h = _moe(xg, rank, grp, counts, experts_gate, experts_up, experts_down, l * N_EXPERTS,
                 row(ln3_g[l]), row(ln3_b[l]), t_moe, t_moe)
    return h.reshape(batch, seq, D_MODEL)
```
